```python
import jax, jax.numpy as jnp
from jax import lax
import numpy as np

D_MODEL = 1024
BATCH = 4
SEQ = 4096
DEPTH = 1

N_META = 16
ATTN_BLOCK = 128
META_PAD = ATTN_BLOCK - N_META
PREFIX = ATTN_BLOCK
FOX_HEADS = 8
FOX_HEAD_DIM = 64
FOX_WIDTH = FOX_HEADS * FOX_HEAD_DIM
DN_HEADS = 4
DN_HEAD_DIM = 128
DN_WIDTH = DN_HEADS * DN_HEAD_DIM
DN_CHUNK = 64
CONV_WIDTH = 4
D_FF = -(-8 * D_MODEL // (3 * 256)) * 256
EPS = 1e-6
NEG_INF = -1e30
IN_SPLITS = (FOX_WIDTH, FOX_WIDTH, FOX_WIDTH, FOX_HEADS,
             DN_WIDTH, DN_WIDTH, DN_WIDTH, DN_HEADS, DN_HEADS,
             DN_WIDTH,
             D_MODEL, D_MODEL)
D_IN = sum(IN_SPLITS)

kernel_name = "fox_gdn_gated_hybrid_block"


def rmsnorm(x, w):
    xf = x.astype(jnp.float32)
    y = xf * lax.rsqrt(jnp.mean(xf * xf, axis=-1, keepdims=True) + EPS)
    return (y * w.astype(jnp.float32)).astype(x.dtype)


def l2norm(x):
    return x * lax.rsqrt(jnp.sum(x * x, axis=-1, keepdims=True) + EPS)


def split_columns(t, sizes):
    out, start = [], 0
    for s in sizes:
        out.append(t[..., start:start + s])
        start += s
    return out


def causal_short_conv(u, w):
    K = w.shape[0]
    L = u.shape[1]
    up = jnp.pad(u, ((0, 0), (K - 1, 0), (0, 0)))
    return sum(up[:, i:i + L, :] * w[i] for i in range(K))


def fox_attention(q, k, v, log_f, valid):
    B, L, H, dh = q.shape
    nb = L // ATTN_BLOCK
    qf = q.astype(jnp.float32) * (dh ** -0.5)
    kf = k.astype(jnp.float32)
    vf = v.astype(jnp.float32)
    c = jnp.cumsum(log_f, axis=1)
    c_keys = c.transpose(0, 2, 1)
    kpos = jnp.arange(L)
    q_blocks = qf.reshape(B, nb, ATTN_BLOCK, H, dh).transpose(1, 0, 2, 3, 4)
    c_blocks = c.reshape(B, nb, ATTN_BLOCK, H).transpose(1, 0, 3, 2)

    def one_block(args):
        qb, cb, bi = args
        qpos = bi * ATTN_BLOCK + jnp.arange(ATTN_BLOCK)
        s = jnp.einsum('bqhd,bkhd->bhqk', qb, kf)
        s = s + cb[..., :, None] - c_keys[:, :, None, :]
        mask = (kpos[None, :] <= qpos[:, None]) & valid[None, :]
        s = jnp.where(mask, s, NEG_INF)
        p = jax.nn.softmax(s, axis=-1)
        return jnp.einsum('bhqk,bkhd->bqhd', p, vf)

    o = lax.map(one_block, (q_blocks, c_blocks, jnp.arange(nb)))
    return o.transpose(1, 0, 2, 3, 4).reshape(B, L, H * dh)


def gated_delta_rule(q, k, v, g, beta):
    B, H, L, dk = q.shape
    dv = v.shape[-1]
    C = DN_CHUNK
    N = L // C
    q = q * (dk ** -0.5)
    q, k, v = (t.reshape(B, H, N, C, t.shape[-1]) for t in (q, k, v))
    g, beta = (t.reshape(B, H, N, C) for t in (g, beta))
    gc = jnp.cumsum(g, axis=-1)
    tril = jnp.tril(jnp.ones((C, C), dtype=bool))
    strict = jnp.tril(jnp.ones((C, C), dtype=bool), -1)
    decay = jnp.exp(jnp.where(tril, gc[..., :, None] - gc[..., None, :], -jnp.inf))
    kb = k * beta[..., None]
    low = jnp.where(strict, jnp.einsum('bhncd,bhnsd->bhncs', kb, k) * decay, 0.0)
    a = low + jnp.eye(C, dtype=jnp.float32)
    rhs = jnp.concatenate([v * beta[..., None], kb * jnp.exp(gc)[..., None]], axis=-1)
    sol = lax.linalg.triangular_solve(a, rhs, left_side=True, lower=True, unit_diagonal=True)
    u, w = sol[..., :dv], sol[..., dv:]
    attn = jnp.einsum('bhncd,bhnsd->bhncs', q, k) * decay
    q_dec = q * jnp.exp(gc)[..., None]
    k_dec = k * jnp.exp(gc[..., -1:] - gc)[..., None]
    chunk_decay = jnp.exp(gc[..., -1])
    xs = tuple(jnp.moveaxis(t, 2, 0) for t in (q_dec, k_dec, u, w, attn, chunk_decay))

    def step(S, inp):
        qd, kd, uc, wc, ac, cd = inp
        v_new = uc - jnp.einsum('bhcd,bhde->bhce', wc, S)
        o = jnp.einsum('bhcd,bhde->bhce', qd, S) + jnp.einsum('bhcs,bhse->bhce', ac, v_new)
        S = S * cd[..., None, None] + jnp.einsum('bhcd,bhce->bhde', kd, v_new)
        return S, o

    S0 = jnp.zeros((B, H, dk, dv), jnp.float32)
    _, o = lax.scan(step, S0, xs)
    return jnp.moveaxis(o, 0, 2).reshape(B, H, L, dv)


def hybrid_mixer(h, valid, w_in, fox_forget_bias, dn_conv_w, dn_a_log, dn_dt_bias,
                 dn_out_norm_w, w_branch_fox, w_branch_dn, w_out):
    B, L, _ = h.shape
    proj = h @ w_in
    (fq, fk, fv, f_logit, dq, dk_, dv_, b_logit, a_logit, dz, ga, gb) = split_columns(proj, IN_SPLITS)
    vm = valid[None, :, None].astype(h.dtype)

    log_f = jax.nn.log_sigmoid(f_logit.astype(jnp.float32) + fox_forget_bias.astype(jnp.float32))
    rs_f = lambda t: t.reshape(B, L, FOX_HEADS, FOX_HEAD_DIM)
    o_fox = fox_attention(rs_f(fq), rs_f(fk), rs_f(fv), log_f, valid).astype(h.dtype)

    qkv = jax.nn.silu(causal_short_conv(jnp.concatenate([dq, dk_, dv_], axis=-1) * vm, dn_conv_w))
    qkv = qkv.astype(jnp.float32)
    rs_d = lambda t: t.reshape(B, L, DN_HEADS, DN_HEAD_DIM).transpose(0, 2, 1, 3)
    q_d = l2norm(rs_d(qkv[..., :DN_WIDTH]))
    k_d = l2norm(rs_d(qkv[..., DN_WIDTH:2 * DN_WIDTH]))
    v_d = rs_d(qkv[..., 2 * DN_WIDTH:])
    vmf = valid[None, None, :].astype(jnp.float32)
    beta = jax.nn.sigmoid(b_logit.astype(jnp.float32)).transpose(0, 2, 1) * vmf
    g = (-jnp.exp(dn_a_log.astype(jnp.float32))[None, :, None]
         * jax.nn.softplus(a_logit.astype(jnp.float32) + dn_dt_bias.astype(jnp.float32)).transpose(0, 2, 1)) * vmf
    o_dn = gated_delta_rule(q_d, k_d, v_d, g, beta).transpose(0, 2, 1, 3)
    z = dz.reshape(B, L, DN_HEADS, DN_HEAD_DIM)
    o_dn = (rmsnorm(o_dn, dn_out_norm_w) * jax.nn.silu(z.astype(jnp.float32))).reshape(B, L, DN_WIDTH)
    o_dn = o_dn.astype(h.dtype)

    y = jax.nn.sigmoid(ga) * (o_fox @ w_branch_fox) + jax.nn.sigmoid(gb) * (o_dn @ w_branch_dn)
    return y @ w_out


def swiglu(h, w_gate, w_up, w_down):
    return (jax.nn.silu(h @ w_gate) * (h @ w_up)) @ w_down


def setup_inputs(seed: int = 0) -> dict:
    key = jax.random.key(seed)
    ks = jax.random.split(key, 20)
    f32 = jnp.float32
    nrm = lambda k, shape, s: jax.random.normal(k, shape, f32) * s
    dt = jnp.exp(jax.random.uniform(ks[7], (DEPTH, DN_HEADS), f32)
                 * (np.log(0.1) - np.log(0.001)) + np.log(0.001))
    return {
        "x": nrm(ks[0], (BATCH, SEQ, D_MODEL), 1.0),
        "meta_tokens": nrm(ks[1], (N_META, D_MODEL), 1.0),
        "mix_norm_w": 1.0 + nrm(ks[2], (DEPTH, D_MODEL), 0.02),
        "w_in": nrm(ks[3], (DEPTH, D_MODEL, D_IN), D_MODEL ** -0.5),
        "fox_forget_bias": 3.0 + nrm(ks[4], (DEPTH, FOX_HEADS), 0.1),
        "dn_conv_w": nrm(ks[5], (DEPTH, CONV_WIDTH, 3 * DN_WIDTH), CONV_WIDTH ** -0.5),
        "dn_a_log": jnp.log(jax.random.uniform(ks[6], (DEPTH, DN_HEADS), f32, 1.0, 16.0)),
        "dn_dt_bias": dt + jnp.log(-jnp.expm1(-dt)),
        "dn_out_norm_w": 1.0 + nrm(ks[8], (DEPTH, DN_HEAD_DIM), 0.02),
        "w_branch_fox": nrm(ks[9], (DEPTH, FOX_WIDTH, D_MODEL), FOX_WIDTH ** -0.5),
        "w_branch_dn": nrm(ks[10], (DEPTH, DN_WIDTH, D_MODEL), DN_WIDTH ** -0.5),
        "w_out": nrm(ks[11], (DEPTH, D_MODEL, D_MODEL), D_MODEL ** -0.5),
        "ffn_norm_w": 1.0 + nrm(ks[12], (DEPTH, D_MODEL), 0.02),
        "w_ffn_gate": nrm(ks[13], (DEPTH, D_MODEL, D_FF), D_MODEL ** -0.5),
        "w_ffn_up": nrm(ks[14], (DEPTH, D_MODEL, D_FF), D_MODEL ** -0.5),
        "w_ffn_down": nrm(ks[15], (DEPTH, D_FF, D_MODEL), D_FF ** -0.5),
        "final_norm_w": 1.0 + nrm(ks[16], (D_MODEL,), 0.02),
    }


def reference(x, meta_tokens, mix_norm_w, w_in, fox_forget_bias, dn_conv_w, dn_a_log,
              dn_dt_bias, dn_out_norm_w, w_branch_fox, w_branch_dn, w_out, ffn_norm_w,
              w_ffn_gate, w_ffn_up, w_ffn_down, final_norm_w):
    B = x.shape[0]
    pad = jnp.zeros((B, META_PAD, D_MODEL), x.dtype)
    meta = jnp.broadcast_to(meta_tokens.astype(x.dtype)[None], (B, N_META, D_MODEL))
    h = jnp.concatenate([pad, meta, x], axis=1)
    L = h.shape[1]
    valid = jnp.arange(L) >= META_PAD
    for l in range(DEPTH):
        h = h + hybrid_mixer(rmsnorm(h, mix_norm_w[l]), valid, w_in[l], fox_forget_bias[l],
                             dn_conv_w[l], dn_a_log[l], dn_dt_bias[l], dn_out_norm_w[l],
                             w_branch_fox[l], w_branch_dn[l], w_out[l])
        h = h + swiglu(rmsnorm(h, ffn_norm_w[l]), w_ffn_gate[l], w_ffn_up[l], w_ffn_down[l])
    h = rmsnorm(h, final_norm_w)
    return h[:, PREFIX:, :]
```

```python
import functools

import jax
import jax.numpy as jnp
from jax import lax
from jax.experimental import pallas as pl
from jax.experimental.pallas import tpu as pltpu

F32 = jnp.float32
BF16 = jnp.bfloat16
HIGHEST = lax.Precision.HIGHEST

D_MODEL = 1024
N_META = 16
PREFIX = 128
N_PAD = PREFIX - N_META
FOX_HEADS = 8
FOX_HEAD_DIM = 64
FOX_WIDTH = FOX_HEADS * FOX_HEAD_DIM
DN_HEADS = 4
DN_HEAD_DIM = 128
DN_WIDTH = DN_HEADS * DN_HEAD_DIM
DN_CHUNK = 64
CONV_WIDTH = 4
D_FF = 2816
EPS = 1e-6
NEG_INF = -1e30

LANES = 128
SUBLANES = 8
N_MAIN = 3 * FOX_WIDTH + 3 * DN_WIDTH + DN_WIDTH + 2 * D_MODEL
COL_CHUNK = 512
LOGF_LANE = 0
BETA_LANE = FOX_HEADS
G_LANE = FOX_HEADS + DN_HEADS
N_SMALL = FOX_HEADS + 2 * DN_HEADS
VMEM_LIMIT = 56 * 1024 * 1024


def _const_spec(shape):
    nd = len(shape)
    return pl.BlockSpec(shape, lambda *_: (0,) * nd, pipeline_mode=pl.Buffered(1))


def _dot(a, b, **kw):
    return jnp.dot(a, b, preferred_element_type=F32, **kw)


def _dot_nt(a, b, **kw):
    return lax.dot_general(a, b, (((1,), (1,)), ((), ())), preferred_element_type=F32, **kw)


def _dot_tn(a, b, **kw):
    return lax.dot_general(a, b, (((0,), (0,)), ((), ())), preferred_element_type=F32, **kw)


def _inproj_kernel(x_ref, nw_ref, wmain_ref, wsmall_ref, bias_ref, carry0_ref,
                   fqkv_ref, dqkv_ref, dzs_ref, gates_ref, scol_ref, srow_ref,
                   carry_ref, *, tm, tiles_per_batch, n_pad):
    i = pl.program_id(0)

    @pl.when(i % tiles_per_batch == 0)
    def _():
        carry_ref[...] = carry0_ref[...]

    x = x_ref[...]
    ms = jnp.mean(x * x, axis=-1, keepdims=True)
    hn = (x * lax.rsqrt(ms + EPS) * nw_ref[...]).astype(BF16)

    for c in range(N_MAIN // COL_CHUNK):
        acc = _dot(hn, wmain_ref[:, c * COL_CHUNK:(c + 1) * COL_CHUNK])
        if c < 3:
            if c == 0:
                acc = acc * (FOX_HEAD_DIM ** -0.5)
            fqkv_ref[:, c * COL_CHUNK:(c + 1) * COL_CHUNK] = acc.astype(BF16)
        elif c < 6:
            dqkv_ref[:, (c - 3) * COL_CHUNK:(c - 2) * COL_CHUNK] = acc
        elif c == 6:
            dzs_ref[...] = acc * (1.0 / (1.0 + jnp.exp(-acc)))
        else:
            gates_ref[:, (c - 7) * COL_CHUNK:(c - 6) * COL_CHUNK] = 1.0 / (1.0 + jnp.exp(-acc))

    z = _dot(hn, wsmall_ref[...]) + bias_ref[0:1, :]
    lane = lax.broadcasted_iota(jnp.int32, (tm, LANES), 1)
    e = jnp.exp(-jnp.abs(z))
    l1p = jnp.log1p(e)
    logf = jnp.minimum(z, 0.0) - l1p
    softplus = jnp.maximum(z, 0.0) + l1p
    sig = jnp.where(z >= 0.0, 1.0, e) / (1.0 + e)
    g = -jnp.exp(bias_ref[1:2, :]) * softplus
    if n_pad:
        row = lax.broadcasted_iota(jnp.int32, (tm, LANES), 0) + (i % tiles_per_batch) * tm
        vm = (row >= n_pad).astype(F32)
        sig = sig * vm
        g = g * vm
    val = jnp.where(lane < BETA_LANE, logf,
                    jnp.where(lane < G_LANE, sig, jnp.where(lane < N_SMALL, g, 0.0)))
    r = lax.broadcasted_iota(jnp.int32, (tm, tm), 0)
    cidx = lax.broadcasted_iota(jnp.int32, (tm, tm), 1)
    tril = (r >= cidx).astype(F32)
    cs = _dot(tril, val, precision=HIGHEST) + carry_ref[...]
    carry_ref[...] = cs[tm - 1:tm, :]
    out = jnp.where(lane < BETA_LANE, cs, val)
    scol_ref[...] = out
    srow_ref[...] = out.T[:2 * SUBLANES, :]


def _inproj(x2d, nw, wmain, wsmall, bias, carry0, *, tm, tiles_per_batch, n_pad):
    m = x2d.shape[0]
    row = lambda w: pl.BlockSpec((tm, w), lambda i: (i, 0))
    return pl.pallas_call(
        functools.partial(_inproj_kernel, tm=tm, tiles_per_batch=tiles_per_batch, n_pad=n_pad),
        grid=(m // tm,),
        in_specs=[row(D_MODEL), _const_spec((1, D_MODEL)), _const_spec((D_MODEL, N_MAIN)),
                  _const_spec((D_MODEL, LANES)), _const_spec((SUBLANES, LANES)),
                  _const_spec((1, LANES))],
        out_specs=[row(3 * FOX_WIDTH), row(3 * DN_WIDTH), row(DN_WIDTH), row(2 * D_MODEL),
                   row(LANES),
                   pl.BlockSpec((None, 2 * SUBLANES, tm),
                                lambda i: (i // tiles_per_batch, 0, i % tiles_per_batch))],
        out_shape=[jax.ShapeDtypeStruct((m, 3 * FOX_WIDTH), BF16),
                   jax.ShapeDtypeStruct((m, 3 * DN_WIDTH), F32),
                   jax.ShapeDtypeStruct((m, DN_WIDTH), F32),
                   jax.ShapeDtypeStruct((m, 2 * D_MODEL), F32),
                   jax.ShapeDtypeStruct((m, LANES), F32),
                   jax.ShapeDtypeStruct((m // (tm * tiles_per_batch), 2 * SUBLANES,
                                         tm * tiles_per_batch), F32)],
        scratch_shapes=[pltpu.VMEM((1, LANES), F32)],
        compiler_params=pltpu.CompilerParams(dimension_semantics=("arbitrary",),
                                             vmem_limit_bytes=VMEM_LIMIT),
        name="inproj",
    )(x2d, nw, wmain, wsmall, bias, carry0)


def _fox_kernel(q_ref, k_ref, v_ref, kp_ref, vp_ref, ccol_ref, crow_ref, crowp_ref,
                o_ref, m_ref, l_ref, acc_ref, *, tq):
    p = pl.program_id(1)
    i = pl.program_id(2)
    lane_row = lax.broadcasted_iota(jnp.int32, (1, LANES), 1)
    head0 = lane_row < FOX_HEAD_DIM
    head_sel = (head0, jnp.logical_not(head0))

    q = q_ref[...]
    zero = jnp.zeros_like(q)
    qh = [jnp.where(head_sel[h], q, zero) for h in range(2)]
    ccol = ccol_ref[...]
    lane_q = lax.broadcasted_iota(jnp.int32, (tq, LANES), 1)
    cq = [jnp.sum(jnp.where(lane_q == 2 * p + h, ccol, 0.0), axis=1, keepdims=True)
          for h in range(2)]
    sub16 = lax.broadcasted_iota(jnp.int32, (2 * SUBLANES, 1), 0)

    def key_bias(crow_blk, h):
        return jnp.sum(jnp.where(sub16 == 2 * p + h, crow_blk, 0.0), axis=0, keepdims=True)

    def block(kt, vt, crow_blk, mask, first):
        alphas, pvs = [], []
        for h in range(2):
            s = _dot_nt(qh[h], kt) + (cq[h] - key_bias(crow_blk, h))
            if mask is not None:
                s = jnp.where(mask, s, NEG_INF)
            s_max = jnp.max(s, axis=1, keepdims=True)
            if first:
                m_new = s_max
            else:
                m_prev = m_ref[h]
                m_new = jnp.maximum(m_prev, s_max)
            pm = jnp.exp(s - m_new)
            psum = jnp.sum(pm, axis=1, keepdims=True)
            vmask = jnp.where(head_sel[h], vt, jnp.zeros_like(vt))
            pvs.append(_dot(pm.astype(BF16), vmask))
            if first:
                l_ref[h] = psum
            else:
                alpha = jnp.exp(m_prev - m_new)
                l_ref[h] = alpha * l_ref[h] + psum
                alphas.append(alpha)
            m_ref[h] = m_new
        if first:
            acc_ref[...] = pvs[0] + pvs[1]
        else:
            acc_ref[...] = jnp.where(head0, alphas[0], alphas[1]) * acc_ref[...] + pvs[0] + pvs[1]

    kcol_p = lax.broadcasted_iota(jnp.int32, (tq, PREFIX), 1)
    block(kp_ref[...], vp_ref[...], crowp_ref[...], kcol_p >= N_PAD, True)

    def body(j, carry):
        off = pl.multiple_of(j * tq, tq)
        block(k_ref[pl.ds(off, tq), :], v_ref[pl.ds(off, tq), :], crow_ref[:, pl.ds(off, tq)],
              None, False)
        return carry

    lax.fori_loop(0, i, body, 0)

    off = pl.multiple_of(i * tq, tq)
    rr = lax.broadcasted_iota(jnp.int32, (tq, tq), 0)
    cc = lax.broadcasted_iota(jnp.int32, (tq, tq), 1)
    block(k_ref[pl.ds(off, tq), :], v_ref[pl.ds(off, tq), :], crow_ref[:, pl.ds(off, tq)],
          cc <= rr, False)

    l_comb = jnp.where(head0, l_ref[0], l_ref[1])
    o_ref[...] = (acc_ref[...] / l_comb).astype(BF16)


def _fox(fqkv, fqkv_p, scol, srow, srow_p, *, tq):
    b, s, _ = fqkv.shape
    npairs = FOX_HEADS // 2
    qblk = FOX_WIDTH // LANES
    return pl.pallas_call(
        functools.partial(_fox_kernel, tq=tq),
        grid=(b, npairs, s // tq),
        in_specs=[
            pl.BlockSpec((None, tq, LANES), lambda bi, p, i: (bi, i, p)),
            pl.BlockSpec((None, s, LANES), lambda bi, p, i: (bi, 0, qblk + p)),
            pl.BlockSpec((None, s, LANES), lambda bi, p, i: (bi, 0, 2 * qblk + p)),
            pl.BlockSpec((PREFIX, LANES), lambda bi, p, i: (0, qblk + p)),
            pl.BlockSpec((PREFIX, LANES), lambda bi, p, i: (0, 2 * qblk + p)),
            pl.BlockSpec((None, tq, LANES), lambda bi, p, i: (bi, i, 0)),
            pl.BlockSpec((None, 2 * SUBLANES, s), lambda bi, p, i: (bi, 0, 0)),
            pl.BlockSpec((2 * SUBLANES, PREFIX), lambda bi, p, i: (0, 0)),
        ],
        out_specs=pl.BlockSpec((None, tq, LANES), lambda bi, p, i: (bi, i, p)),
        out_shape=jax.ShapeDtypeStruct((b, s, FOX_WIDTH), BF16),
        scratch_shapes=[pltpu.VMEM((2, tq, 1), F32), pltpu.VMEM((2, tq, 1), F32),
                        pltpu.VMEM((tq, LANES), F32)],
        compiler_params=pltpu.CompilerParams(
            dimension_semantics=("arbitrary", "arbitrary", "arbitrary"),
            vmem_limit_bytes=VMEM_LIMIT),
        name="fox_attention",
    )(fqkv, fqkv, fqkv, fqkv_p, fqkv_p, scol, srow, srow_p)


def _dn_kernel(dqkv_ref, halo_ref, halo0_ref, convw_ref, scol_ref, srow_ref, dzs_ref, onw_ref,
               s0_ref, odn_ref, sfin_ref, state_ref, conv_ref, *, tt, n_pad):
    i = pl.program_id(1)
    nchunk = tt // DN_CHUNK

    @pl.when(i == 0)
    def _():
        state_ref[...] = s0_ref[...]

    u = dqkv_ref[...]
    if n_pad:
        row = lax.broadcasted_iota(jnp.int32, (tt, 1), 0) + i * tt
        u = u * (row >= n_pad).astype(F32)
    conv_ref[0:SUBLANES, :] = jnp.where(i == 0, halo0_ref[...], halo_ref[...])
    conv_ref[SUBLANES:, :] = u

    def conv_silu(c0):
        a = convw_ref[CONV_WIDTH - 1:CONV_WIDTH, c0:c0 + LANES] * conv_ref[SUBLANES:, c0:c0 + LANES]
        for t in range(CONV_WIDTH - 1):
            start = SUBLANES - (CONV_WIDTH - 1) + t
            a = a + convw_ref[t:t + 1, c0:c0 + LANES] * conv_ref[start:start + tt, c0:c0 + LANES]
        return a * (1.0 / (1.0 + jnp.exp(-a)))

    sc = scol_ref[...]
    sr = srow_ref[...]
    rr = lax.broadcasted_iota(jnp.int32, (tt, tt), 0)
    cc = lax.broadcasted_iota(jnp.int32, (tt, tt), 1)
    same = (rr // DN_CHUNK) == (cc // DN_CHUNK)
    tril_m = jnp.logical_and(same, rr >= cc)
    strict_m = jnp.logical_and(same, rr > cc)
    eye = (rr == cc).astype(F32)
    gc_cols = _dot(tril_m.astype(F32), sc, precision=HIGHEST)
    gl_cols = _dot(same.astype(F32), sc, precision=HIGHEST)
    gc_rows = _dot(sr, jnp.logical_and(same, rr <= cc).astype(F32), precision=HIGHEST)

    for h in range(DN_HEADS):
        qh = conv_silu(h * LANES)
        kh = conv_silu(DN_WIDTH + h * LANES)
        vh = conv_silu(2 * DN_WIDTH + h * LANES)
        qh = qh * lax.rsqrt(jnp.sum(qh * qh, axis=-1, keepdims=True) + EPS) * (DN_HEAD_DIM ** -0.5)
        kh = kh * lax.rsqrt(jnp.sum(kh * kh, axis=-1, keepdims=True) + EPS)
        beta = sc[:, BETA_LANE + h:BETA_LANE + h + 1]
        gc_c = gc_cols[:, G_LANE + h:G_LANE + h + 1]
        gl_c = gl_cols[:, G_LANE + h:G_LANE + h + 1]
        gc_r = gc_rows[G_LANE + h:G_LANE + h + 1, :]
        decay = jnp.exp(jnp.where(tril_m, gc_c - gc_r, NEG_INF))
        egc = jnp.exp(gc_c)
        kb = kh * beta
        xp = jnp.where(strict_m, -(_dot_nt(kb, kh, precision=HIGHEST) * decay), 0.0)
        ainv = eye + xp
        for _ in range(5):
            xp = _dot(xp, xp, precision=HIGHEST)
            ainv = ainv + _dot(ainv, xp, precision=HIGHEST)
        rhs = jnp.concatenate([vh * beta, kb * egc], axis=1)
        sol = _dot(ainv, rhs, precision=HIGHEST)
        uu = sol[:, :DN_HEAD_DIM]
        ww = sol[:, DN_HEAD_DIM:]
        attn = _dot_nt(qh, kh, precision=HIGHEST) * decay
        qd = qh * egc
        kd = kh * jnp.exp(gl_c - gc_c)

        state = state_ref[h]
        vnews, qss = [], []
        for c in range(nchunk):
            lo, hi = c * DN_CHUNK, (c + 1) * DN_CHUNK
            wq = jnp.concatenate([ww[lo:hi], qd[lo:hi]], axis=0)
            r = _dot(wq, state, precision=HIGHEST)
            vn = uu[lo:hi] - r[:DN_CHUNK]
            qss.append(r[DN_CHUNK:])
            state = jnp.exp(gl_c[lo:lo + 1, :]) * state + _dot_tn(kd[lo:hi], vn, precision=HIGHEST)
            vnews.append(vn)
        state_ref[h] = state
        sfin_ref[h] = state
        vn_all = jnp.concatenate(vnews, axis=0)
        o = jnp.concatenate(qss, axis=0) + _dot(attn, vn_all, precision=HIGHEST)
        o = o * lax.rsqrt(jnp.mean(o * o, axis=-1, keepdims=True) + EPS) * onw_ref[...]
        o = o * dzs_ref[:, h * LANES:(h + 1) * LANES]
        odn_ref[:, h * LANES:(h + 1) * LANES] = o.astype(BF16)


def _deltanet(dqkv, halo0, convw, scol, srow, dzs, onw, s0, *, tt, n_pad):
    b, s, _ = dqkv.shape
    nt = s // tt
    hb = tt // SUBLANES
    return pl.pallas_call(
        functools.partial(_dn_kernel, tt=tt, n_pad=n_pad),
        grid=(b, nt),
        in_specs=[
            pl.BlockSpec((None, tt, 3 * DN_WIDTH), lambda bi, i: (bi, i, 0)),
            pl.BlockSpec((None, SUBLANES, 3 * DN_WIDTH),
                         lambda bi, i: (bi, jnp.maximum(i * hb - 1, 0), 0)),
            pl.BlockSpec((SUBLANES, 3 * DN_WIDTH), lambda bi, i: (0, 0)),
            pl.BlockSpec((CONV_WIDTH, 3 * DN_WIDTH), lambda bi, i: (0, 0)),
            pl.BlockSpec((None, tt, LANES), lambda bi, i: (bi, i, 0)),
            pl.BlockSpec((None, 2 * SUBLANES, tt), lambda bi, i: (bi, 0, i)),
            pl.BlockSpec((None, tt, DN_WIDTH), lambda bi, i: (bi, i, 0)),
            pl.BlockSpec((1, DN_HEAD_DIM), lambda bi, i: (0, 0)),
            pl.BlockSpec((DN_HEADS, DN_HEAD_DIM, DN_HEAD_DIM), lambda bi, i: (0, 0, 0)),
        ],
        out_specs=[pl.BlockSpec((None, tt, DN_WIDTH), lambda bi, i: (bi, i, 0)),
                   pl.BlockSpec((DN_HEADS, DN_HEAD_DIM, DN_HEAD_DIM), lambda bi, i: (0, 0, 0))],
        out_shape=[jax.ShapeDtypeStruct((b, s, DN_WIDTH), BF16),
                   jax.ShapeDtypeStruct((DN_HEADS, DN_HEAD_DIM, DN_HEAD_DIM), F32)],
        scratch_shapes=[pltpu.VMEM((DN_HEADS, DN_HEAD_DIM, DN_HEAD_DIM), F32),
                        pltpu.VMEM((tt + SUBLANES, 3 * DN_WIDTH), F32)],
        compiler_params=pltpu.CompilerParams(dimension_semantics=("arbitrary", "arbitrary"),
                                             vmem_limit_bytes=VMEM_LIMIT),
        name="deltanet",
    )(dqkv, dqkv, halo0, convw, scol, srow, dzs, onw, s0)


def _tail_kernel(x_ref, ofox_ref, odn_ref, gates_ref, wbf_ref, wbd_ref, wout_ref, fnw_ref,
                 wg_ref, wu_ref, wd_ref, finw_ref, o_ref, *, ff_chunk):
    a = _dot(ofox_ref[...], wbf_ref[...])
    bb = _dot(odn_ref[...], wbd_ref[...])
    y = gates_ref[:, :D_MODEL] * a + gates_ref[:, D_MODEL:] * bb
    h1 = x_ref[...] + _dot(y.astype(BF16), wout_ref[...])
    n = (h1 * lax.rsqrt(jnp.mean(h1 * h1, axis=-1, keepdims=True) + EPS) * fnw_ref[...]).astype(BF16)
    acc = h1
    for c in range(D_FF // ff_chunk):
        lo, hi = c * ff_chunk, (c + 1) * ff_chunk
        gt = _dot(n, wg_ref[:, lo:hi])
        up = _dot(n, wu_ref[:, lo:hi])
        act = (gt * (1.0 / (1.0 + jnp.exp(-gt))) * up).astype(BF16)
        acc = acc + _dot(act, wd_ref[lo:hi, :])
    o_ref[...] = acc * lax.rsqrt(jnp.mean(acc * acc, axis=-1, keepdims=True) + EPS) * finw_ref[...]


def _tail(x2d, ofox, odn, gates, wbf, wbd, wout, fnw, wg, wu, wd, finw, *, tm, ff_chunk):
    m = x2d.shape[0]
    row = lambda w: pl.BlockSpec((tm, w), lambda i: (i, 0))
    return pl.pallas_call(
        functools.partial(_tail_kernel, ff_chunk=ff_chunk),
        grid=(m // tm,),
        in_specs=[row(D_MODEL), row(FOX_WIDTH), row(DN_WIDTH), row(2 * D_MODEL),
                  _const_spec((FOX_WIDTH, D_MODEL)), _const_spec((DN_WIDTH, D_MODEL)),
                  _const_spec((D_MODEL, D_MODEL)), _const_spec((1, D_MODEL)),
                  _const_spec((D_MODEL, D_FF)), _const_spec((D_MODEL, D_FF)),
                  _const_spec((D_FF, D_MODEL)), _const_spec((1, D_MODEL))],
        out_specs=row(D_MODEL),
        out_shape=jax.ShapeDtypeStruct((m, D_MODEL), F32),
        compiler_params=pltpu.CompilerParams(dimension_semantics=("arbitrary",),
                                             vmem_limit_bytes=VMEM_LIMIT),
        name="merge_ffn",
    )(x2d, ofox, odn, gates, wbf, wbd, wout, fnw, wg, wu, wd, finw)


def _pick_tile(n, pref):
    t = min(pref, n)
    while n % t:
        t //= 2
    return t


def kernel(x, meta_tokens, mix_norm_w, w_in, fox_forget_bias, dn_conv_w, dn_a_log, dn_dt_bias,
           dn_out_norm_w, w_branch_fox, w_branch_dn, w_out, ffn_norm_w, w_ffn_gate, w_ffn_up,
           w_ffn_down, final_norm_w):
    b, s, _ = x.shape
    assert mix_norm_w.shape[0] == 1, "single layer only"
    assert s % PREFIX == 0
    m = b * s

    wi = w_in[0]
    o_small0 = 3 * FOX_WIDTH
    o_dn = o_small0 + FOX_HEADS
    o_small1 = o_dn + 3 * DN_WIDTH
    o_rest = o_small1 + 2 * DN_HEADS
    wmain = jnp.concatenate([wi[:, :o_small0], wi[:, o_dn:o_small1], wi[:, o_rest:]], axis=1).astype(BF16)
    wsmall = jnp.concatenate([wi[:, o_small0:o_dn], wi[:, o_small1:o_rest]], axis=1)
    wsmall = jnp.pad(wsmall, ((0, 0), (0, LANES - N_SMALL))).astype(BF16)
    bias = jnp.zeros((SUBLANES, LANES), F32)
    bias = bias.at[0, LOGF_LANE:LOGF_LANE + FOX_HEADS].set(fox_forget_bias[0].astype(F32))
    bias = bias.at[0, G_LANE:G_LANE + DN_HEADS].set(dn_dt_bias[0].astype(F32))
    bias = bias.at[1, G_LANE:G_LANE + DN_HEADS].set(dn_a_log[0].astype(F32))
    nw = mix_norm_w[0].reshape(1, D_MODEL).astype(F32)

    x_p = jnp.concatenate([jnp.zeros((N_PAD, D_MODEL), F32), meta_tokens.astype(F32)], axis=0)
    fqkv_p, dqkv_p, dzs_p, _, scol_p, srow_p = _inproj(
        x_p, nw, wmain, wsmall, bias, jnp.zeros((1, LANES), F32),
        tm=PREFIX, tiles_per_batch=1, n_pad=N_PAD)

    tm = _pick_tile(s, 512)
    fqkv, dqkv, dzs, gates, scol, srow3 = _inproj(
        x.reshape(m, D_MODEL), nw, wmain, wsmall, bias, scol_p[PREFIX - 1:PREFIX, :],
        tm=tm, tiles_per_batch=s // tm, n_pad=0)

    fqkv3 = fqkv.reshape(b, s, 3 * FOX_WIDTH)
    scol3 = scol.reshape(b, s, LANES)
    ofox = _fox(fqkv3, fqkv_p, scol3, srow3, srow_p[0], tq=_pick_tile(s, 512))

    convw = dn_conv_w[0].astype(F32)
    onw = dn_out_norm_w[0].reshape(1, DN_HEAD_DIM).astype(F32)
    _, s_prefix = _deltanet(
        dqkv_p[None], jnp.zeros((SUBLANES, 3 * DN_WIDTH), F32), convw, scol_p[None], srow_p,
        dzs_p[None], onw, jnp.zeros((DN_HEADS, DN_HEAD_DIM, DN_HEAD_DIM), F32),
        tt=PREFIX, n_pad=N_PAD)
    odn, _ = _deltanet(
        dqkv.reshape(b, s, 3 * DN_WIDTH), dqkv_p[PREFIX - SUBLANES:], convw, scol3, srow3,
        dzs.reshape(b, s, DN_WIDTH), onw, s_prefix, tt=_pick_tile(s, 256), n_pad=0)

    out = _tail(
        x.reshape(m, D_MODEL), ofox.reshape(m, FOX_WIDTH), odn.reshape(m, DN_WIDTH), gates,
        w_branch_fox[0].astype(BF16), w_branch_dn[0].astype(BF16), w_out[0].astype(BF16),
        ffn_norm_w[0].reshape(1, D_MODEL).astype(F32),
        w_ffn_gate[0].astype(BF16), w_ffn_up[0].astype(BF16), w_ffn_down[0].astype(BF16),
        final_norm_w.reshape(1, D_MODEL).astype(F32),
        tm=_pick_tile(m, 512), ff_chunk=D_FF // 2)
    return out.reshape(b, s, D_MODEL)
```

```python
import functools

import jax
import jax.numpy as jnp
from jax import lax
from jax.experimental import pallas as pl
from jax.experimental.pallas import tpu as pltpu

F32 = jnp.float32
BF16 = jnp.bfloat16
HIGHEST = lax.Precision.HIGHEST

D_MODEL = 1024
N_META = 16
PREFIX = 128
N_PAD = PREFIX - N_META
FOX_HEADS = 8
FOX_HEAD_DIM = 64
FOX_WIDTH = FOX_HEADS * FOX_HEAD_DIM
DN_HEADS = 4
DN_HEAD_DIM = 128
DN_WIDTH = DN_HEADS * DN_HEAD_DIM
DN_CHUNK = 64
CONV_WIDTH = 4
D_FF = 2816
EPS = 1e-6
NEG_INF = -1e30

LANES = 128
SUBLANES = 8
N_MAIN = 3 * FOX_WIDTH + 3 * DN_WIDTH + DN_WIDTH + 2 * D_MODEL
COL_CHUNK = 512
LOGF_LANE = 0
BETA_LANE = FOX_HEADS
G_LANE = FOX_HEADS + DN_HEADS
N_SMALL = FOX_HEADS + 2 * DN_HEADS
VMEM_LIMIT = 56 * 1024 * 1024


def _const_spec(shape):
    nd = len(shape)
    return pl.BlockSpec(shape, lambda *_: (0,) * nd, pipeline_mode=pl.Buffered(1))


def _dot(a, b, **kw):
    return jnp.dot(a, b, preferred_element_type=F32, **kw)


def _dot_nt(a, b, **kw):
    return lax.dot_general(a, b, (((1,), (1,)), ((), ())), preferred_element_type=F32, **kw)


def _dot_tn(a, b, **kw):
    return lax.dot_general(a, b, (((0,), (0,)), ((), ())), preferred_element_type=F32, **kw)


def _inproj_kernel(x_ref, nw_ref, wmain_ref, wsmall_ref, bias_ref, carry0_ref,
                   fqkv_ref, dqkv_ref, dzs_ref, gates_ref, scol_ref, srow_ref,
                   carry_ref, *, tm, tiles_per_batch, n_pad):
    i = pl.program_id(0)

    @pl.when(i % tiles_per_batch == 0)
    def _():
        carry_ref[...] = carry0_ref[...]

    x = x_ref[...]
    ms = jnp.mean(x * x, axis=-1, keepdims=True)
    hn = (x * lax.rsqrt(ms + EPS) * nw_ref[...]).astype(BF16)

    for c in range(N_MAIN // COL_CHUNK):
        acc = _dot(hn, wmain_ref[:, c * COL_CHUNK:(c + 1) * COL_CHUNK])
        if c < 3:
            if c == 0:
                acc = acc * (FOX_HEAD_DIM ** -0.5)
            fqkv_ref[:, c * COL_CHUNK:(c + 1) * COL_CHUNK] = acc.astype(BF16)
        elif c < 6:
            dqkv_ref[:, (c - 3) * COL_CHUNK:(c - 2) * COL_CHUNK] = acc
        elif c == 6:
            dzs_ref[...] = acc * (1.0 / (1.0 + jnp.exp(-acc)))
        else:
            gates_ref[:, (c - 7) * COL_CHUNK:(c - 6) * COL_CHUNK] = 1.0 / (1.0 + jnp.exp(-acc))

    z = _dot(hn, wsmall_ref[...]) + bias_ref[0:1, :]
    lane = lax.broadcasted_iota(jnp.int32, (tm, LANES), 1)
    e = jnp.exp(-jnp.abs(z))
    l1p = jnp.log1p(e)
    logf = jnp.minimum(z, 0.0) - l1p
    softplus = jnp.maximum(z, 0.0) + l1p
    sig = jnp.where(z >= 0.0, 1.0, e) / (1.0 + e)
    g = -jnp.exp(bias_ref[1:2, :]) * softplus
    if n_pad:
        row = lax.broadcasted_iota(jnp.int32, (tm, LANES), 0) + (i % tiles_per_batch) * tm
        vm = (row >= n_pad).astype(F32)
        sig = sig * vm
        g = g * vm
    val = jnp.where(lane < BETA_LANE, logf,
                    jnp.where(lane < G_LANE, sig, jnp.where(lane < N_SMALL, g, 0.0)))
    r = lax.broadcasted_iota(jnp.int32, (tm, tm), 0)
    cidx = lax.broadcasted_iota(jnp.int32, (tm, tm), 1)
    tril = (r >= cidx).astype(F32)
    cs = _dot(tril, val, precision=HIGHEST) + carry_ref[...]
    carry_ref[...] = cs[tm - 1:tm, :]
    out = jnp.where(lane < BETA_LANE, cs, val)
    scol_ref[...] = out
    srow_ref[...] = out.T[:2 * SUBLANES, :]


def _inproj(x2d, nw, wmain, wsmall, bias, carry0, *, tm, tiles_per_batch, n_pad):
    m = x2d.shape[0]
    row = lambda w: pl.BlockSpec((tm, w), lambda i: (i, 0))
    return pl.pallas_call(
        functools.partial(_inproj_kernel, tm=tm, tiles_per_batch=tiles_per_batch, n_pad=n_pad),
        grid=(m // tm,),
        in_specs=[row(D_MODEL), _const_spec((1, D_MODEL)), _const_spec((D_MODEL, N_MAIN)),
                  _const_spec((D_MODEL, LANES)), _const_spec((SUBLANES, LANES)),
                  _const_spec((1, LANES))],
        out_specs=[row(3 * FOX_WIDTH), row(3 * DN_WIDTH), row(DN_WIDTH), row(2 * D_MODEL),
                   row(LANES),
                   pl.BlockSpec((None, 2 * SUBLANES, tm),
                                lambda i: (i // tiles_per_batch, 0, i % tiles_per_batch))],
        out_shape=[jax.ShapeDtypeStruct((m, 3 * FOX_WIDTH), BF16),
                   jax.ShapeDtypeStruct((m, 3 * DN_WIDTH), F32),
                   jax.ShapeDtypeStruct((m, DN_WIDTH), F32),
                   jax.ShapeDtypeStruct((m, 2 * D_MODEL), F32),
                   jax.ShapeDtypeStruct((m, LANES), F32),
                   jax.ShapeDtypeStruct((m // (tm * tiles_per_batch), 2 * SUBLANES,
                                         tm * tiles_per_batch), F32)],
        scratch_shapes=[pltpu.VMEM((1, LANES), F32)],
        compiler_params=pltpu.CompilerParams(dimension_semantics=("arbitrary",),
                                             vmem_limit_bytes=VMEM_LIMIT),
        name="inproj",
    )(x2d, nw, wmain, wsmall, bias, carry0)


def _fox_kernel(q_ref, k_ref, v_ref, kp_ref, vp_ref, ccol_ref, crow_ref, crowp_ref,
                o_ref, m_ref, l_ref, acc_ref, *, tq):
    p = pl.program_id(1)
    i = pl.program_id(2)
    lane_row = lax.broadcasted_iota(jnp.int32, (1, LANES), 1)
    head0 = lane_row < FOX_HEAD_DIM
    head_sel = (head0, jnp.logical_not(head0))

    q = q_ref[...]
    zero = jnp.zeros_like(q)
    qh = [jnp.where(head_sel[h], q, zero) for h in range(2)]
    ccol = ccol_ref[...]
    lane_q = lax.broadcasted_iota(jnp.int32, (tq, LANES), 1)
    cq = [jnp.sum(jnp.where(lane_q == 2 * p + h, ccol, 0.0), axis=1, keepdims=True)
          for h in range(2)]
    sub16 = lax.broadcasted_iota(jnp.int32, (2 * SUBLANES, 1), 0)

    def key_bias(crow_blk, h):
        return jnp.sum(jnp.where(sub16 == 2 * p + h, crow_blk, 0.0), axis=0, keepdims=True)

    def block(kt, vt, crow_blk, mask, first):
        alphas, pvs = [], []
        for h in range(2):
            s = _dot_nt(qh[h], kt) + (cq[h] - key_bias(crow_blk, h))
            if mask is not None:
                s = jnp.where(mask, s, NEG_INF)
            s_max = jnp.max(s, axis=1, keepdims=True)
            if first:
                m_new = s_max
            else:
                m_prev = m_ref[h]
                m_new = jnp.maximum(m_prev, s_max)
            pm = jnp.exp(s - m_new)
            psum = jnp.sum(pm, axis=1, keepdims=True)
            vmask = jnp.where(head_sel[h], vt, jnp.zeros_like(vt))
            pvs.append(_dot(pm.astype(BF16), vmask))
            if first:
                l_ref[h] = psum
            else:
                alpha = jnp.exp(m_prev - m_new)
                l_ref[h] = alpha * l_ref[h] + psum
                alphas.append(alpha)
            m_ref[h] = m_new
        if first:
            acc_ref[...] = pvs[0] + pvs[1]
        else:
            acc_ref[...] = jnp.where(head0, alphas[0], alphas[1]) * acc_ref[...] + pvs[0] + pvs[1]

    kcol_p = lax.broadcasted_iota(jnp.int32, (tq, PREFIX), 1)
    block(kp_ref[...], vp_ref[...], crowp_ref[...], kcol_p >= N_PAD, True)

    def body(j, carry):
        off = pl.multiple_of(j * tq, tq)
        block(k_ref[pl.ds(off, tq), :], v_ref[pl.ds(off, tq), :], crow_ref[:, pl.ds(off, tq)],
              None, False)
        return carry

    lax.fori_loop(0, i, body, 0)

    off = pl.multiple_of(i * tq, tq)
    rr = lax.broadcasted_iota(jnp.int32, (tq, tq), 0)
    cc = lax.broadcasted_iota(jnp.int32, (tq, tq), 1)
    block(k_ref[pl.ds(off, tq), :], v_ref[pl.ds(off, tq), :], crow_ref[:, pl.ds(off, tq)],
          cc <= rr, False)

    l_comb = jnp.where(head0, l_ref[0], l_ref[1])
    o_ref[...] = (acc_ref[...] / l_comb).astype(BF16)


def _fox(fqkv, fqkv_p, scol, srow, srow_p, *, tq):
    b, s, _ = fqkv.shape
    npairs = FOX_HEADS // 2
    qblk = FOX_WIDTH // LANES
    return pl.pallas_call(
        functools.partial(_fox_kernel, tq=tq),
        grid=(b, npairs, s // tq),
        in_specs=[
            pl.BlockSpec((None, tq, LANES), lambda bi, p, i: (bi, i, p)),
            pl.BlockSpec((None, s, LANES), lambda bi, p, i: (bi, 0, qblk + p)),
            pl.BlockSpec((None, s, LANES), lambda bi, p, i: (bi, 0, 2 * qblk + p)),
            pl.BlockSpec((PREFIX, LANES), lambda bi, p, i: (0, qblk + p)),
            pl.BlockSpec((PREFIX, LANES), lambda bi, p, i: (0, 2 * qblk + p)),
            pl.BlockSpec((None, tq, LANES), lambda bi, p, i: (bi, i, 0)),
            pl.BlockSpec((None, 2 * SUBLANES, s), lambda bi, p, i: (bi, 0, 0)),
            pl.BlockSpec((2 * SUBLANES, PREFIX), lambda bi, p, i: (0, 0)),
        ],
        out_specs=pl.BlockSpec((None, tq, LANES), lambda bi, p, i: (bi, i, p)),
        out_shape=jax.ShapeDtypeStruct((b, s, FOX_WIDTH), BF16),
        scratch_shapes=[pltpu.VMEM((2, tq, 1), F32), pltpu.VMEM((2, tq, 1), F32),
                        pltpu.VMEM((tq, LANES), F32)],
        compiler_params=pltpu.CompilerParams(
            dimension_semantics=("arbitrary", "arbitrary", "arbitrary"),
            vmem_limit_bytes=VMEM_LIMIT),
        name="fox_attention",
    )(fqkv, fqkv, fqkv, fqkv_p, fqkv_p, scol, srow, srow_p)


def _dn_kernel(dqkv_ref, halo_ref, halo0_ref, convw_ref, scol_ref, srow_ref, dzs_ref, onw_ref,
               s0_ref, odn_ref, sfin_ref, state_ref, conv_ref, *, tt, n_pad):
    i = pl.program_id(1)
    nchunk = tt // DN_CHUNK

    @pl.when(i == 0)
    def _():
        state_ref[...] = s0_ref[...]

    u = dqkv_ref[...]
    if n_pad:
        row = lax.broadcasted_iota(jnp.int32, (tt, 1), 0) + i * tt
        u = u * (row >= n_pad).astype(F32)
    conv_ref[0:SUBLANES, :] = jnp.where(i == 0, halo0_ref[...], halo_ref[...])
    conv_ref[SUBLANES:, :] = u

    def conv_silu(c0):
        a = convw_ref[CONV_WIDTH - 1:CONV_WIDTH, c0:c0 + LANES] * conv_ref[SUBLANES:, c0:c0 + LANES]
        for t in range(CONV_WIDTH - 1):
            start = SUBLANES - (CONV_WIDTH - 1) + t
            a = a + convw_ref[t:t + 1, c0:c0 + LANES] * conv_ref[start:start + tt, c0:c0 + LANES]
        return a * (1.0 / (1.0 + jnp.exp(-a)))

    sc = scol_ref[...]
    sr = srow_ref[...]
    rr = lax.broadcasted_iota(jnp.int32, (tt, tt), 0)
    cc = lax.broadcasted_iota(jnp.int32, (tt, tt), 1)
    rc_xor = jnp.bitwise_xor(rr, cc)
    same = rc_xor < DN_CHUNK
    tril_m = jnp.logical_and(same, rr >= cc)
    strict_m = jnp.logical_and(same, rr > cc)
    eye = (rr == cc).astype(F32)
    gc_cols = _dot(tril_m.astype(F32), sc, precision=HIGHEST)
    gl_cols = _dot(same.astype(F32), sc, precision=HIGHEST)
    gc_rows = _dot(sr, jnp.logical_and(same, rr <= cc).astype(F32), precision=HIGHEST)

    heads = range(DN_HEADS)
    qs = [conv_silu(h * LANES) for h in heads]
    ks = [conv_silu(DN_WIDTH + h * LANES) for h in heads]
    vs = [conv_silu(2 * DN_WIDTH + h * LANES) for h in heads]
    states = [state_ref[h] for h in heads]
    zs = [dzs_ref[:, h * LANES:(h + 1) * LANES] for h in heads]
    qs = [q * lax.rsqrt(jnp.sum(q * q, axis=-1, keepdims=True) + EPS) * (DN_HEAD_DIM ** -0.5)
          for q in qs]
    ks = [k * lax.rsqrt(jnp.sum(k * k, axis=-1, keepdims=True) + EPS) for k in ks]
    betas = [sc[:, BETA_LANE + h:BETA_LANE + h + 1] for h in heads]
    gcs = [gc_cols[:, G_LANE + h:G_LANE + h + 1] for h in heads]
    gls = [gl_cols[:, G_LANE + h:G_LANE + h + 1] for h in heads]
    decays = [jnp.exp(jnp.where(tril_m, gcs[h] - gc_rows[G_LANE + h:G_LANE + h + 1, :], NEG_INF))
              for h in heads]
    egcs = [jnp.exp(g) for g in gcs]
    kbs = [ks[h] * betas[h] for h in heads]
    ks_b = [k.astype(BF16) for k in ks]
    lmats = [jnp.where(strict_m, _dot_nt(kbs[h].astype(BF16), ks_b[h]) * decays[h], 0.0)
             for h in heads]
    attns = [_dot_nt(qs[h].astype(BF16), ks_b[h]) * decays[h] for h in heads]
    ainvs = [eye - jnp.where(rc_xor == 1, lm, 0.0) for lm in lmats]
    for lvl in range(1, 6):
        blk = 2 ** lvl
        lvl_m = jnp.logical_and(rc_xor >= blk, rc_xor < 2 * blk)
        ainvs_b = [a.astype(BF16) for a in ainvs]
        mids = [_dot(jnp.where(lvl_m, lmats[h], 0.0).astype(BF16), ainvs_b[h]).astype(BF16)
                for h in heads]
        ainvs = [ainvs[h] - _dot(ainvs_b[h], mids[h]) for h in heads]
    sols = [_dot(ainvs[h].astype(BF16),
                 jnp.concatenate([vs[h] * betas[h], kbs[h] * egcs[h]], axis=1).astype(BF16))
            for h in heads]
    uus = [sol[:, :DN_HEAD_DIM] for sol in sols]
    wws = [sol[:, DN_HEAD_DIM:] for sol in sols]
    qds = [qs[h] * egcs[h] for h in heads]
    kds = [(ks[h] * jnp.exp(gls[h] - gcs[h])).astype(BF16) for h in heads]

    vnews = [[] for _ in heads]
    qss = [[] for _ in heads]
    for c in range(nchunk):
        lo, hi = c * DN_CHUNK, (c + 1) * DN_CHUNK
        rs = [_dot(jnp.concatenate([wws[h][lo:hi], qds[h][lo:hi]], axis=0).astype(BF16),
                   states[h].astype(BF16)) for h in heads]
        vns = [uus[h][lo:hi] - rs[h][:DN_CHUNK] for h in heads]
        states = [jnp.exp(gls[h][lo:lo + 1, :]) * states[h]
                  + _dot_tn(kds[h][lo:hi], vns[h].astype(BF16)) for h in heads]
        for h in heads:
            qss[h].append(rs[h][DN_CHUNK:])
            vnews[h].append(vns[h])
    outs = []
    for h in heads:
        o = (jnp.concatenate(qss[h], axis=0)
             + _dot(attns[h].astype(BF16), jnp.concatenate(vnews[h], axis=0).astype(BF16)))
        o = o * lax.rsqrt(jnp.mean(o * o, axis=-1, keepdims=True) + EPS) * onw_ref[...]
        outs.append((o * zs[h]).astype(BF16))
    for h in heads:
        state_ref[h] = states[h]
        sfin_ref[h] = states[h]
        odn_ref[:, h * LANES:(h + 1) * LANES] = outs[h]


def _deltanet(dqkv, halo0, convw, scol, srow, dzs, onw, s0, *, tt, n_pad):
    b, s, _ = dqkv.shape
    nt = s // tt
    hb = tt // SUBLANES
    return pl.pallas_call(
        functools.partial(_dn_kernel, tt=tt, n_pad=n_pad),
        grid=(b, nt),
        in_specs=[
            pl.BlockSpec((None, tt, 3 * DN_WIDTH), lambda bi, i: (bi, i, 0)),
            pl.BlockSpec((None, SUBLANES, 3 * DN_WIDTH),
                         lambda bi, i: (bi, jnp.maximum(i * hb - 1, 0), 0)),
            pl.BlockSpec((SUBLANES, 3 * DN_WIDTH), lambda bi, i: (0, 0)),
            pl.BlockSpec((CONV_WIDTH, 3 * DN_WIDTH), lambda bi, i: (0, 0)),
            pl.BlockSpec((None, tt, LANES), lambda bi, i: (bi, i, 0)),
            pl.BlockSpec((None, 2 * SUBLANES, tt), lambda bi, i: (bi, 0, i)),
            pl.BlockSpec((None, tt, DN_WIDTH), lambda bi, i: (bi, i, 0)),
            pl.BlockSpec((1, DN_HEAD_DIM), lambda bi, i: (0, 0)),
            pl.BlockSpec((DN_HEADS, DN_HEAD_DIM, DN_HEAD_DIM), lambda bi, i: (0, 0, 0)),
        ],
        out_specs=[pl.BlockSpec((None, tt, DN_WIDTH), lambda bi, i: (bi, i, 0)),
                   pl.BlockSpec((DN_HEADS, DN_HEAD_DIM, DN_HEAD_DIM), lambda bi, i: (0, 0, 0))],
        out_shape=[jax.ShapeDtypeStruct((b, s, DN_WIDTH), BF16),
                   jax.ShapeDtypeStruct((DN_HEADS, DN_HEAD_DIM, DN_HEAD_DIM), F32)],
        scratch_shapes=[pltpu.VMEM((DN_HEADS, DN_HEAD_DIM, DN_HEAD_DIM), F32),
                        pltpu.VMEM((tt + SUBLANES, 3 * DN_WIDTH), F32)],
        compiler_params=pltpu.CompilerParams(dimension_semantics=("arbitrary", "arbitrary"),
                                             vmem_limit_bytes=VMEM_LIMIT),
        name="deltanet",
    )(dqkv, dqkv, halo0, convw, scol, srow, dzs, onw, s0)


def _tail_kernel(x_ref, ofox_ref, odn_ref, gates_ref, wbf_ref, wbd_ref, wout_ref, fnw_ref,
                 wg_ref, wu_ref, wd_ref, finw_ref, o_ref, *, ff_chunk):
    a = _dot(ofox_ref[...], wbf_ref[...])
    bb = _dot(odn_ref[...], wbd_ref[...])
    y = gates_ref[:, :D_MODEL] * a + gates_ref[:, D_MODEL:] * bb
    h1 = x_ref[...] + _dot(y.astype(BF16), wout_ref[...])
    n = (h1 * lax.rsqrt(jnp.mean(h1 * h1, axis=-1, keepdims=True) + EPS) * fnw_ref[...]).astype(BF16)
    acc = h1
    for c in range(D_FF // ff_chunk):
        lo, hi = c * ff_chunk, (c + 1) * ff_chunk
        gt = _dot(n, wg_ref[:, lo:hi])
        up = _dot(n, wu_ref[:, lo:hi])
        act = (gt * (1.0 / (1.0 + jnp.exp(-gt))) * up).astype(BF16)
        acc = acc + _dot(act, wd_ref[lo:hi, :])
    o_ref[...] = acc * lax.rsqrt(jnp.mean(acc * acc, axis=-1, keepdims=True) + EPS) * finw_ref[...]


def _tail(x2d, ofox, odn, gates, wbf, wbd, wout, fnw, wg, wu, wd, finw, *, tm, ff_chunk):
    m = x2d.shape[0]
    row = lambda w: pl.BlockSpec((tm, w), lambda i: (i, 0))
    return pl.pallas_call(
        functools.partial(_tail_kernel, ff_chunk=ff_chunk),
        grid=(m // tm,),
        in_specs=[row(D_MODEL), row(FOX_WIDTH), row(DN_WIDTH), row(2 * D_MODEL),
                  _const_spec((FOX_WIDTH, D_MODEL)), _const_spec((DN_WIDTH, D_MODEL)),
                  _const_spec((D_MODEL, D_MODEL)), _const_spec((1, D_MODEL)),
                  _const_spec((D_MODEL, D_FF)), _const_spec((D_MODEL, D_FF)),
                  _const_spec((D_FF, D_MODEL)), _const_spec((1, D_MODEL))],
        out_specs=row(D_MODEL),
        out_shape=jax.ShapeDtypeStruct((m, D_MODEL), F32),
        compiler_params=pltpu.CompilerParams(dimension_semantics=("arbitrary",),
                                             vmem_limit_bytes=VMEM_LIMIT),
        name="merge_ffn",
    )(x2d, ofox, odn, gates, wbf, wbd, wout, fnw, wg, wu, wd, finw)


def _pick_tile(n, pref):
    t = min(pref, n)
    while n % t:
        t //= 2
    return t


def kernel(x, meta_tokens, mix_norm_w, w_in, fox_forget_bias, dn_conv_w, dn_a_log, dn_dt_bias,
           dn_out_norm_w, w_branch_fox, w_branch_dn, w_out, ffn_norm_w, w_ffn_gate, w_ffn_up,
           w_ffn_down, final_norm_w):
    b, s, _ = x.shape
    assert mix_norm_w.shape[0] == 1, "single layer only"
    assert s % PREFIX == 0
    m = b * s

    wi = w_in[0]
    o_small0 = 3 * FOX_WIDTH
    o_dn = o_small0 + FOX_HEADS
    o_small1 = o_dn + 3 * DN_WIDTH
    o_rest = o_small1 + 2 * DN_HEADS
    wmain = jnp.concatenate([wi[:, :o_small0], wi[:, o_dn:o_small1], wi[:, o_rest:]], axis=1).astype(BF16)
    wsmall = jnp.concatenate([wi[:, o_small0:o_dn], wi[:, o_small1:o_rest]], axis=1)
    wsmall = jnp.pad(wsmall, ((0, 0), (0, LANES - N_SMALL))).astype(BF16)
    bias = jnp.zeros((SUBLANES, LANES), F32)
    bias = bias.at[0, LOGF_LANE:LOGF_LANE + FOX_HEADS].set(fox_forget_bias[0].astype(F32))
    bias = bias.at[0, G_LANE:G_LANE + DN_HEADS].set(dn_dt_bias[0].astype(F32))
    bias = bias.at[1, G_LANE:G_LANE + DN_HEADS].set(dn_a_log[0].astype(F32))
    nw = mix_norm_w[0].reshape(1, D_MODEL).astype(F32)

    x_p = jnp.concatenate([jnp.zeros((N_PAD, D_MODEL), F32), meta_tokens.astype(F32)], axis=0)
    fqkv_p, dqkv_p, dzs_p, _, scol_p, srow_p = _inproj(
        x_p, nw, wmain, wsmall, bias, jnp.zeros((1, LANES), F32),
        tm=PREFIX, tiles_per_batch=1, n_pad=N_PAD)

    tm = _pick_tile(s, 512)
    fqkv, dqkv, dzs, gates, scol, srow3 = _inproj(
        x.reshape(m, D_MODEL), nw, wmain, wsmall, bias, scol_p[PREFIX - 1:PREFIX, :],
        tm=tm, tiles_per_batch=s // tm, n_pad=0)

    fqkv3 = fqkv.reshape(b, s, 3 * FOX_WIDTH)
    scol3 = scol.reshape(b, s, LANES)
    ofox = _fox(fqkv3, fqkv_p, scol3, srow3, srow_p[0], tq=_pick_tile(s, 512))

    convw = dn_conv_w[0].astype(F32)
    onw = dn_out_norm_w[0].reshape(1, DN_HEAD_DIM).astype(F32)
    _, s_prefix = _deltanet(
        dqkv_p[None], jnp.zeros((SUBLANES, 3 * DN_WIDTH), F32), convw, scol_p[None], srow_p,
        dzs_p[None], onw, jnp.zeros((DN_HEADS, DN_HEAD_DIM, DN_HEAD_DIM), F32),
        tt=PREFIX, n_pad=N_PAD)
    odn, _ = _deltanet(
        dqkv.reshape(b, s, 3 * DN_WIDTH), dqkv_p[PREFIX - SUBLANES:], convw, scol3, srow3,
        dzs.reshape(b, s, DN_WIDTH), onw, s_prefix, tt=_pick_tile(s, 128), n_pad=0)

    out = _tail(
        x.reshape(m, D_MODEL), ofox.reshape(m, FOX_WIDTH), odn.reshape(m, DN_WIDTH), gates,
        w_branch_fox[0].astype(BF16), w_branch_dn[0].astype(BF16), w_out[0].astype(BF16),
        ffn_norm_w[0].reshape(1, D_MODEL).astype(F32),
        w_ffn_gate[0].astype(BF16), w_ffn_up[0].astype(BF16), w_ffn_down[0].astype(BF16),
        final_norm_w.reshape(1, D_MODEL).astype(F32),
        tm=_pick_tile(m, 512), ff_chunk=D_FF // 2)
    return out.reshape(b, s, D_MODEL)
```

```python
import functools

import jax
import jax.numpy as jnp
from jax import lax
from jax.experimental import pallas as pl
from jax.experimental.pallas import tpu as pltpu

F32 = jnp.float32
BF16 = jnp.bfloat16
HIGHEST = lax.Precision.HIGHEST

D_MODEL = 1024
N_META = 16
PREFIX = 128
N_PAD = PREFIX - N_META
FOX_HEADS = 8
FOX_HEAD_DIM = 64
FOX_WIDTH = FOX_HEADS * FOX_HEAD_DIM
DN_HEADS = 4
DN_HEAD_DIM = 128
DN_WIDTH = DN_HEADS * DN_HEAD_DIM
DN_CHUNK = 64
CONV_WIDTH = 4
D_FF = 2816
EPS = 1e-6
NEG_INF = -1e30
LOG2E = 1.4426950408889634

LANES = 128
SUBLANES = 8
N_MAIN = FOX_WIDTH + 3 * DN_WIDTH + DN_WIDTH + 2 * D_MODEL
COL_CHUNK = 512
LOGF_LANE = 0
BETA_LANE = FOX_HEADS
G_LANE = FOX_HEADS + DN_HEADS
N_SMALL = FOX_HEADS + 2 * DN_HEADS
VMEM_LIMIT = 56 * 1024 * 1024


def _const_spec(shape):
    nd = len(shape)
    return pl.BlockSpec(shape, lambda *_: (0,) * nd, pipeline_mode=pl.Buffered(1))


def _dot(a, b, **kw):
    return jnp.dot(a, b, preferred_element_type=F32, **kw)


def _dot_nt(a, b, **kw):
    return lax.dot_general(a, b, (((1,), (1,)), ((), ())), preferred_element_type=F32, **kw)


def _dot_tn(a, b, **kw):
    return lax.dot_general(a, b, (((0,), (0,)), ((), ())), preferred_element_type=F32, **kw)


def _inproj_kernel(x_ref, nw_ref, wqv_t_ref, wmain_ref, wsmall_ref, bias_ref, carry0_ref,
                   q_t_ref, k_ref, v_t_ref, kbias_ref, dqkv_ref, dzs_ref, gates_ref, scol_ref,
                   srow_ref, carry_ref, *, tm, tiles_per_batch, n_pad):
    i = pl.program_id(0)

    @pl.when(i % tiles_per_batch == 0)
    def _():
        carry_ref[...] = carry0_ref[...]

    x = x_ref[...]
    ms = jnp.mean(x * x, axis=-1, keepdims=True)
    hn = (x * lax.rsqrt(ms + EPS) * nw_ref[...]).astype(BF16)

    q_t = _dot_nt(wqv_t_ref[:FOX_WIDTH, :], hn) * (FOX_HEAD_DIM ** -0.5 * LOG2E)
    q_t_ref[...] = q_t.astype(BF16)
    v_t_ref[...] = _dot_nt(wqv_t_ref[FOX_WIDTH:, :], hn).astype(BF16)

    for c in range(N_MAIN // COL_CHUNK):
        acc = _dot(hn, wmain_ref[:, c * COL_CHUNK:(c + 1) * COL_CHUNK])
        if c == 0:
            k_ref[...] = acc.astype(BF16)
        elif c < 4:
            dqkv_ref[:, (c - 1) * COL_CHUNK:c * COL_CHUNK] = acc
        elif c == 4:
            dzs_ref[...] = acc * (1.0 / (1.0 + jnp.exp(-acc)))
        else:
            gates_ref[:, (c - 5) * COL_CHUNK:(c - 4) * COL_CHUNK] = 1.0 / (1.0 + jnp.exp(-acc))

    z = _dot(hn, wsmall_ref[...]) + bias_ref[0:1, :]
    lane = lax.broadcasted_iota(jnp.int32, (tm, LANES), 1)
    e = jnp.exp(-jnp.abs(z))
    l1p = jnp.log1p(e)
    logf = jnp.minimum(z, 0.0) - l1p
    softplus = jnp.maximum(z, 0.0) + l1p
    sig = jnp.where(z >= 0.0, 1.0, e) / (1.0 + e)
    g = -jnp.exp(bias_ref[1:2, :]) * softplus
    if n_pad:
        row = lax.broadcasted_iota(jnp.int32, (tm, LANES), 0) + (i % tiles_per_batch) * tm
        vm = (row >= n_pad).astype(F32)
        sig = sig * vm
        g = g * vm
    val = jnp.where(lane < BETA_LANE, logf,
                    jnp.where(lane < G_LANE, sig, jnp.where(lane < N_SMALL, g, 0.0)))
    r = lax.broadcasted_iota(jnp.int32, (tm, tm), 0)
    cidx = lax.broadcasted_iota(jnp.int32, (tm, tm), 1)
    tril = (r >= cidx).astype(F32)
    cs = _dot(tril, val, precision=HIGHEST) + carry_ref[...]
    carry_ref[...] = cs[tm - 1:tm, :]
    out = jnp.where(lane < BETA_LANE, cs, val)
    scol_ref[...] = out
    srow_ref[...] = out.T[:2 * SUBLANES, :]
    nc = jnp.where(lane < FOX_HEADS, cs * -LOG2E, 0.0)
    hi = nc.astype(BF16).astype(F32)
    mid = (nc - hi).astype(BF16).astype(F32)
    lo = nc - hi - mid
    kbias = hi + pltpu.roll(mid, FOX_HEADS, axis=1) + pltpu.roll(lo, 2 * FOX_HEADS, axis=1)
    kbias_ref[...] = kbias.astype(BF16)


def _inproj(x2d, nw, wqv_t, wmain, wsmall, bias, carry0, *, tm, tiles_per_batch, n_pad):
    m = x2d.shape[0]
    row = lambda w: pl.BlockSpec((tm, w), lambda i: (i, 0))
    col = lambda h: pl.BlockSpec((h, tm), lambda i: (0, i))
    return pl.pallas_call(
        functools.partial(_inproj_kernel, tm=tm, tiles_per_batch=tiles_per_batch, n_pad=n_pad),
        grid=(m // tm,),
        in_specs=[row(D_MODEL), _const_spec((1, D_MODEL)), _const_spec((2 * FOX_WIDTH, D_MODEL)),
                  _const_spec((D_MODEL, N_MAIN)),
                  _const_spec((D_MODEL, LANES)), _const_spec((SUBLANES, LANES)),
                  _const_spec((1, LANES))],
        out_specs=[col(FOX_WIDTH), row(FOX_WIDTH), col(FOX_WIDTH), row(LANES),
                   row(3 * DN_WIDTH), row(DN_WIDTH), row(2 * D_MODEL),
                   row(LANES),
                   pl.BlockSpec((None, 2 * SUBLANES, tm),
                                lambda i: (i // tiles_per_batch, 0, i % tiles_per_batch))],
        out_shape=[jax.ShapeDtypeStruct((FOX_WIDTH, m), BF16),
                   jax.ShapeDtypeStruct((m, FOX_WIDTH), BF16),
                   jax.ShapeDtypeStruct((FOX_WIDTH, m), BF16),
                   jax.ShapeDtypeStruct((m, LANES), BF16),
                   jax.ShapeDtypeStruct((m, 3 * DN_WIDTH), F32),
                   jax.ShapeDtypeStruct((m, DN_WIDTH), F32),
                   jax.ShapeDtypeStruct((m, 2 * D_MODEL), F32),
                   jax.ShapeDtypeStruct((m, LANES), F32),
                   jax.ShapeDtypeStruct((m // (tm * tiles_per_batch), 2 * SUBLANES,
                                         tm * tiles_per_batch), F32)],
        scratch_shapes=[pltpu.VMEM((1, LANES), F32)],
        compiler_params=pltpu.CompilerParams(dimension_semantics=("arbitrary",),
                                             vmem_limit_bytes=VMEM_LIMIT),
        name="inproj",
    )(x2d, nw, wqv_t, wmain, wsmall, bias, carry0)


def _fox_kernel(q_t_ref, k_ref, v_t_ref, kb_ref, kp_ref, v_tp_ref, kbp_ref,
                o_ref, m_ref, l_ref, acc_ref, *, tq, tk):
    p = pl.program_id(1)
    i = pl.program_id(2)
    sub = lax.broadcasted_iota(jnp.int32, (LANES, 1), 0)
    q_t = q_t_ref[...]
    zero = jnp.zeros_like(q_t)
    q_th = [jnp.where(sub < FOX_HEAD_DIM, q_t, zero), jnp.where(sub >= FOX_HEAD_DIM, q_t, zero)]
    sel = [jnp.where(jnp.logical_and(sub < 3 * FOX_HEADS, (sub % FOX_HEADS) == 2 * p + h),
                     1.0, 0.0).astype(BF16) * jnp.ones((1, tq), BF16) for h in range(2)]
    rhs = [jnp.concatenate([q_th[h], sel[h]], axis=0) for h in range(2)]

    def tile(kt, kbt, v_tt, mask, first):
        lhs = jnp.concatenate([kt, kbt], axis=1)
        ts = [_dot(lhs, rhs[h]) for h in range(2)]
        pvs, alphas = [], []
        for h in range(2):
            t = ts[h]
            if mask is not None:
                t = jnp.where(mask, t, NEG_INF)
            t_max = jnp.max(t, axis=0, keepdims=True)
            if first:
                m_new = t_max
            else:
                m_prev = m_ref[h]
                m_new = jnp.maximum(m_prev, t_max)
            pm = jnp.exp2(t - m_new)
            psum = jnp.sum(pm, axis=0, keepdims=True)
            pvs.append(_dot(v_tt[h * FOX_HEAD_DIM:(h + 1) * FOX_HEAD_DIM, :], pm.astype(BF16)))
            if first:
                l_ref[h] = psum
            else:
                alpha = jnp.exp2(m_prev - m_new)
                l_ref[h] = alpha * l_ref[h] + psum
                alphas.append(jnp.broadcast_to(alpha, (FOX_HEAD_DIM, tq)))
            m_ref[h] = m_new
        pv = jnp.concatenate(pvs, axis=0)
        if first:
            acc_ref[...] = pv
        else:
            acc_ref[...] = jnp.concatenate(alphas, axis=0) * acc_ref[...] + pv

    krow_p = lax.broadcasted_iota(jnp.int32, (PREFIX, tq), 0)
    tile(kp_ref[...], kbp_ref[...], v_tp_ref[...], krow_p >= N_PAD, True)

    def body(j, carry):
        off = pl.multiple_of(j * tk, tk)
        tile(k_ref[pl.ds(off, tk), :], kb_ref[pl.ds(off, tk), :], v_t_ref[:, pl.ds(off, tk)],
             None, False)
        return carry

    lax.fori_loop(0, i * (tq // tk), body, 0)

    rr = lax.broadcasted_iota(jnp.int32, (tk, tq), 0)
    cc = lax.broadcasted_iota(jnp.int32, (tk, tq), 1)
    for r in range(tq // tk):
        off = pl.multiple_of(i * tq + r * tk, tk)
        tile(k_ref[pl.ds(off, tk), :], kb_ref[pl.ds(off, tk), :], v_t_ref[:, pl.ds(off, tk)],
             rr + r * tk <= cc, False)

    l_all = jnp.concatenate([jnp.broadcast_to(l_ref[h], (FOX_HEAD_DIM, tq)) for h in range(2)], axis=0)
    o_ref[...] = (acc_ref[...] / l_all).T.astype(BF16)


def _fox(q_t, k, v_t, kbias, k_p, v_tp, kbias_p, *, tq, tk):
    b, s, _ = k.shape
    npairs = FOX_HEADS // 2
    nq = s // tq
    return pl.pallas_call(
        functools.partial(_fox_kernel, tq=tq, tk=tk),
        grid=(b, npairs, nq),
        in_specs=[
            pl.BlockSpec((LANES, tq), lambda bi, p, i: (p, bi * nq + i)),
            pl.BlockSpec((None, s, LANES), lambda bi, p, i: (bi, 0, p)),
            pl.BlockSpec((LANES, s), lambda bi, p, i: (p, bi)),
            pl.BlockSpec((None, s, LANES), lambda bi, p, i: (bi, 0, 0)),
            pl.BlockSpec((PREFIX, LANES), lambda bi, p, i: (0, p)),
            pl.BlockSpec((LANES, PREFIX), lambda bi, p, i: (p, 0)),
            pl.BlockSpec((PREFIX, LANES), lambda bi, p, i: (0, 0)),
        ],
        out_specs=pl.BlockSpec((None, tq, LANES), lambda bi, p, i: (bi, i, p)),
        out_shape=jax.ShapeDtypeStruct((b, s, FOX_WIDTH), BF16),
        scratch_shapes=[pltpu.VMEM((2, 1, tq), F32), pltpu.VMEM((2, 1, tq), F32),
                        pltpu.VMEM((LANES, tq), F32)],
        compiler_params=pltpu.CompilerParams(
            dimension_semantics=("arbitrary", "arbitrary", "arbitrary"),
            vmem_limit_bytes=VMEM_LIMIT),
        name="fox_attention",
    )(q_t, k, v_t, kbias, k_p, v_tp, kbias_p)


def _dn_kernel(dqkv_ref, halo_ref, halo0_ref, convw_ref, scol_ref, srow_ref, dzs_ref, onw_ref,
               s0_ref, odn_ref, sfin_ref, state_ref, conv_ref, *, tt, n_pad):
    i = pl.program_id(1)
    nchunk = tt // DN_CHUNK

    @pl.when(i == 0)
    def _():
        state_ref[...] = s0_ref[...]

    u = dqkv_ref[...]
    if n_pad:
        row = lax.broadcasted_iota(jnp.int32, (tt, 1), 0) + i * tt
        u = u * (row >= n_pad).astype(F32)
    conv_ref[0:SUBLANES, :] = jnp.where(i == 0, halo0_ref[...], halo_ref[...])
    conv_ref[SUBLANES:, :] = u

    def conv_silu(c0):
        a = convw_ref[CONV_WIDTH - 1:CONV_WIDTH, c0:c0 + LANES] * conv_ref[SUBLANES:, c0:c0 + LANES]
        for t in range(CONV_WIDTH - 1):
            start = SUBLANES - (CONV_WIDTH - 1) + t
            a = a + convw_ref[t:t + 1, c0:c0 + LANES] * conv_ref[start:start + tt, c0:c0 + LANES]
        return a * (1.0 / (1.0 + jnp.exp(-a)))

    sc = scol_ref[...]
    sr = srow_ref[...]
    rr = lax.broadcasted_iota(jnp.int32, (tt, tt), 0)
    cc = lax.broadcasted_iota(jnp.int32, (tt, tt), 1)
    rc_xor = jnp.bitwise_xor(rr, cc)
    same = rc_xor < DN_CHUNK
    tril_m = jnp.logical_and(same, rr >= cc)
    strict_m = jnp.logical_and(same, rr > cc)
    eye = (rr == cc).astype(F32)
    gc_cols = _dot(tril_m.astype(F32), sc, precision=HIGHEST)
    gl_cols = _dot(same.astype(F32), sc, precision=HIGHEST)
    gc_rows = _dot(sr, jnp.logical_and(same, rr <= cc).astype(F32), precision=HIGHEST)

    heads = range(DN_HEADS)
    qs = [conv_silu(h * LANES) for h in heads]
    ks = [conv_silu(DN_WIDTH + h * LANES) for h in heads]
    vs = [conv_silu(2 * DN_WIDTH + h * LANES) for h in heads]
    states = [state_ref[h] for h in heads]
    zs = [dzs_ref[:, h * LANES:(h + 1) * LANES] for h in heads]
    qs = [q * lax.rsqrt(jnp.sum(q * q, axis=-1, keepdims=True) + EPS) * (DN_HEAD_DIM ** -0.5)
          for q in qs]
    ks = [k * lax.rsqrt(jnp.sum(k * k, axis=-1, keepdims=True) + EPS) for k in ks]
    betas = [sc[:, BETA_LANE + h:BETA_LANE + h + 1] for h in heads]
    gcs = [gc_cols[:, G_LANE + h:G_LANE + h + 1] for h in heads]
    gls = [gl_cols[:, G_LANE + h:G_LANE + h + 1] for h in heads]
    decays = [jnp.exp(jnp.where(tril_m, gcs[h] - gc_rows[G_LANE + h:G_LANE + h + 1, :], NEG_INF))
              for h in heads]
    egcs = [jnp.exp(g) for g in gcs]
    kbs = [ks[h] * betas[h] for h in heads]
    ks_b = [k.astype(BF16) for k in ks]
    lmats = [jnp.where(strict_m, _dot_nt(kbs[h].astype(BF16), ks_b[h]) * decays[h], 0.0)
             for h in heads]
    attns = [_dot_nt(qs[h].astype(BF16), ks_b[h]) * decays[h] for h in heads]
    ainvs = [eye - jnp.where(rc_xor == 1, lm, 0.0) for lm in lmats]
    for lvl in range(1, 6):
        blk = 2 ** lvl
        lvl_m = jnp.logical_and(rc_xor >= blk, rc_xor < 2 * blk)
        ainvs_b = [a.astype(BF16) for a in ainvs]
        mids = [_dot(jnp.where(lvl_m, lmats[h], 0.0).astype(BF16), ainvs_b[h]).astype(BF16)
                for h in heads]
        ainvs = [ainvs[h] - _dot(ainvs_b[h], mids[h]) for h in heads]
    sols = [_dot(ainvs[h].astype(BF16),
                 jnp.concatenate([vs[h] * betas[h], kbs[h] * egcs[h]], axis=1).astype(BF16))
            for h in heads]
    uus = [sol[:, :DN_HEAD_DIM] for sol in sols]
    wws = [sol[:, DN_HEAD_DIM:] for sol in sols]
    qds = [qs[h] * egcs[h] for h in heads]
    kds = [(ks[h] * jnp.exp(gls[h] - gcs[h])).astype(BF16) for h in heads]

    vnews = [[] for _ in heads]
    qss = [[] for _ in heads]
    for c in range(nchunk):
        lo, hi = c * DN_CHUNK, (c + 1) * DN_CHUNK
        rs = [_dot(jnp.concatenate([wws[h][lo:hi], qds[h][lo:hi]], axis=0).astype(BF16),
                   states[h].astype(BF16)) for h in heads]
        vns = [uus[h][lo:hi] - rs[h][:DN_CHUNK] for h in heads]
        states = [jnp.exp(gls[h][lo:lo + 1, :]) * states[h]
                  + _dot_tn(kds[h][lo:hi], vns[h].astype(BF16)) for h in heads]
        for h in heads:
            qss[h].append(rs[h][DN_CHUNK:])
            vnews[h].append(vns[h])
    outs = []
    for h in heads:
        o = (jnp.concatenate(qss[h], axis=0)
             + _dot(attns[h].astype(BF16), jnp.concatenate(vnews[h], axis=0).astype(BF16)))
        o = o * lax.rsqrt(jnp.mean(o * o, axis=-1, keepdims=True) + EPS) * onw_ref[...]
        outs.append((o * zs[h]).astype(BF16))
    for h in heads:
        state_ref[h] = states[h]
        sfin_ref[h] = states[h]
        odn_ref[:, h * LANES:(h + 1) * LANES] = outs[h]


def _deltanet(dqkv, halo0, convw, scol, srow, dzs, onw, s0, *, tt, n_pad):
    b, s, _ = dqkv.shape
    nt = s // tt
    hb = tt // SUBLANES
    return pl.pallas_call(
        functools.partial(_dn_kernel, tt=tt, n_pad=n_pad),
        grid=(b, nt),
        in_specs=[
            pl.BlockSpec((None, tt, 3 * DN_WIDTH), lambda bi, i: (bi, i, 0)),
            pl.BlockSpec((None, SUBLANES, 3 * DN_WIDTH),
                         lambda bi, i: (bi, jnp.maximum(i * hb - 1, 0), 0)),
            pl.BlockSpec((SUBLANES, 3 * DN_WIDTH), lambda bi, i: (0, 0)),
            pl.BlockSpec((CONV_WIDTH, 3 * DN_WIDTH), lambda bi, i: (0, 0)),
            pl.BlockSpec((None, tt, LANES), lambda bi, i: (bi, i, 0)),
            pl.BlockSpec((None, 2 * SUBLANES, tt), lambda bi, i: (bi, 0, i)),
            pl.BlockSpec((None, tt, DN_WIDTH), lambda bi, i: (bi, i, 0)),
            pl.BlockSpec((1, DN_HEAD_DIM), lambda bi, i: (0, 0)),
            pl.BlockSpec((DN_HEADS, DN_HEAD_DIM, DN_HEAD_DIM), lambda bi, i: (0, 0, 0)),
        ],
        out_specs=[pl.BlockSpec((None, tt, DN_WIDTH), lambda bi, i: (bi, i, 0)),
                   pl.BlockSpec((DN_HEADS, DN_HEAD_DIM, DN_HEAD_DIM), lambda bi, i: (0, 0, 0))],
        out_shape=[jax.ShapeDtypeStruct((b, s, DN_WIDTH), BF16),
                   jax.ShapeDtypeStruct((DN_HEADS, DN_HEAD_DIM, DN_HEAD_DIM), F32)],
        scratch_shapes=[pltpu.VMEM((DN_HEADS, DN_HEAD_DIM, DN_HEAD_DIM), F32),
                        pltpu.VMEM((tt + SUBLANES, 3 * DN_WIDTH), F32)],
        compiler_params=pltpu.CompilerParams(dimension_semantics=("arbitrary", "arbitrary"),
                                             vmem_limit_bytes=VMEM_LIMIT),
        name="deltanet",
    )(dqkv, dqkv, halo0, convw, scol, srow, dzs, onw, s0)


def _tail_kernel(x_ref, ofox_ref, odn_ref, gates_ref, wbf_ref, wbd_ref, wout_ref, fnw_ref,
                 wg_ref, wu_ref, wd_ref, finw_ref, o_ref, *, ff_chunk):
    a = _dot(ofox_ref[...], wbf_ref[...])
    bb = _dot(odn_ref[...], wbd_ref[...])
    y = gates_ref[:, :D_MODEL] * a + gates_ref[:, D_MODEL:] * bb
    h1 = x_ref[...] + _dot(y.astype(BF16), wout_ref[...])
    n = (h1 * lax.rsqrt(jnp.mean(h1 * h1, axis=-1, keepdims=True) + EPS) * fnw_ref[...]).astype(BF16)
    acc = h1
    for c in range(D_FF // ff_chunk):
        lo, hi = c * ff_chunk, (c + 1) * ff_chunk
        gt = _dot(n, wg_ref[:, lo:hi])
        up = _dot(n, wu_ref[:, lo:hi])
        act = (gt * (1.0 / (1.0 + jnp.exp(-gt))) * up).astype(BF16)
        acc = acc + _dot(act, wd_ref[lo:hi, :])
    o_ref[...] = acc * lax.rsqrt(jnp.mean(acc * acc, axis=-1, keepdims=True) + EPS) * finw_ref[...]


def _tail(x2d, ofox, odn, gates, wbf, wbd, wout, fnw, wg, wu, wd, finw, *, tm, ff_chunk):
    m = x2d.shape[0]
    row = lambda w: pl.BlockSpec((tm, w), lambda i: (i, 0))
    return pl.pallas_call(
        functools.partial(_tail_kernel, ff_chunk=ff_chunk),
        grid=(m // tm,),
        in_specs=[row(D_MODEL), row(FOX_WIDTH), row(DN_WIDTH), row(2 * D_MODEL),
                  _const_spec((FOX_WIDTH, D_MODEL)), _const_spec((DN_WIDTH, D_MODEL)),
                  _const_spec((D_MODEL, D_MODEL)), _const_spec((1, D_MODEL)),
                  _const_spec((D_MODEL, D_FF)), _const_spec((D_MODEL, D_FF)),
                  _const_spec((D_FF, D_MODEL)), _const_spec((1, D_MODEL))],
        out_specs=row(D_MODEL),
        out_shape=jax.ShapeDtypeStruct((m, D_MODEL), F32),
        compiler_params=pltpu.CompilerParams(dimension_semantics=("arbitrary",),
                                             vmem_limit_bytes=VMEM_LIMIT),
        name="merge_ffn",
    )(x2d, ofox, odn, gates, wbf, wbd, wout, fnw, wg, wu, wd, finw)


def _pick_tile(n, pref):
    t = min(pref, n)
    while n % t:
        t //= 2
    return t


def kernel(x, meta_tokens, mix_norm_w, w_in, fox_forget_bias, dn_conv_w, dn_a_log, dn_dt_bias,
           dn_out_norm_w, w_branch_fox, w_branch_dn, w_out, ffn_norm_w, w_ffn_gate, w_ffn_up,
           w_ffn_down, final_norm_w):
    b, s, _ = x.shape
    assert mix_norm_w.shape[0] == 1, "single layer only"
    assert s % PREFIX == 0
    m = b * s

    wi = w_in[0]
    o_small0 = 3 * FOX_WIDTH
    o_dn = o_small0 + FOX_HEADS
    o_small1 = o_dn + 3 * DN_WIDTH
    o_rest = o_small1 + 2 * DN_HEADS
    wqv_t = jnp.concatenate([wi[:, :FOX_WIDTH], wi[:, 2 * FOX_WIDTH:o_small0]], axis=1).T.astype(BF16)
    wmain = jnp.concatenate([wi[:, FOX_WIDTH:2 * FOX_WIDTH], wi[:, o_dn:o_small1], wi[:, o_rest:]],
                            axis=1).astype(BF16)
    wsmall = jnp.concatenate([wi[:, o_small0:o_dn], wi[:, o_small1:o_rest]], axis=1)
    wsmall = jnp.pad(wsmall, ((0, 0), (0, LANES - N_SMALL))).astype(BF16)
    bias = jnp.zeros((SUBLANES, LANES), F32)
    bias = bias.at[0, LOGF_LANE:LOGF_LANE + FOX_HEADS].set(fox_forget_bias[0].astype(F32))
    bias = bias.at[0, G_LANE:G_LANE + DN_HEADS].set(dn_dt_bias[0].astype(F32))
    bias = bias.at[1, G_LANE:G_LANE + DN_HEADS].set(dn_a_log[0].astype(F32))
    nw = mix_norm_w[0].reshape(1, D_MODEL).astype(F32)

    x_p = jnp.concatenate([jnp.zeros((N_PAD, D_MODEL), F32), meta_tokens.astype(F32)], axis=0)
    _, k_p, v_tp, kbias_p, dqkv_p, dzs_p, _, scol_p, srow_p = _inproj(
        x_p, nw, wqv_t, wmain, wsmall, bias, jnp.zeros((1, LANES), F32),
        tm=PREFIX, tiles_per_batch=1, n_pad=N_PAD)

    tm = _pick_tile(s, 512)
    q_t, k, v_t, kbias, dqkv, dzs, gates, scol, srow3 = _inproj(
        x.reshape(m, D_MODEL), nw, wqv_t, wmain, wsmall, bias, scol_p[PREFIX - 1:PREFIX, :],
        tm=tm, tiles_per_batch=s // tm, n_pad=0)

    scol3 = scol.reshape(b, s, LANES)
    tq = _pick_tile(s, 512)
    ofox = _fox(q_t, k.reshape(b, s, FOX_WIDTH), v_t, kbias.reshape(b, s, LANES), k_p, v_tp, kbias_p,
                tq=tq, tk=_pick_tile(tq, 512))

    convw = dn_conv_w[0].astype(F32)
    onw = dn_out_norm_w[0].reshape(1, DN_HEAD_DIM).astype(F32)
    _, s_prefix = _deltanet(
        dqkv_p[None], jnp.zeros((SUBLANES, 3 * DN_WIDTH), F32), convw, scol_p[None], srow_p,
        dzs_p[None], onw, jnp.zeros((DN_HEADS, DN_HEAD_DIM, DN_HEAD_DIM), F32),
        tt=PREFIX, n_pad=N_PAD)
    odn, _ = _deltanet(
        dqkv.reshape(b, s, 3 * DN_WIDTH), dqkv_p[PREFIX - SUBLANES:], convw, scol3, srow3,
        dzs.reshape(b, s, DN_WIDTH), onw, s_prefix, tt=_pick_tile(s, 128), n_pad=0)

    out = _tail(
        x.reshape(m, D_MODEL), ofox.reshape(m, FOX_WIDTH), odn.reshape(m, DN_WIDTH), gates,
        w_branch_fox[0].astype(BF16), w_branch_dn[0].astype(BF16), w_out[0].astype(BF16),
        ffn_norm_w[0].reshape(1, D_MODEL).astype(F32),
        w_ffn_gate[0].astype(BF16), w_ffn_up[0].astype(BF16), w_ffn_down[0].astype(BF16),
        final_norm_w.reshape(1, D_MODEL).astype(F32),
        tm=_pick_tile(m, 512), ff_chunk=D_FF // 2)
    return out.reshape(b, s, D_MODEL)
```

```python
import functools

import jax
import jax.numpy as jnp
from jax import lax
from jax.experimental import pallas as pl
from jax.experimental.pallas import tpu as pltpu

F32 = jnp.float32
BF16 = jnp.bfloat16
HIGHEST = lax.Precision.HIGHEST

D_MODEL = 1024
N_META = 16
PREFIX = 128
N_PAD = PREFIX - N_META
FOX_HEADS = 8
FOX_HEAD_DIM = 64
FOX_WIDTH = FOX_HEADS * FOX_HEAD_DIM
DN_HEADS = 4
DN_HEAD_DIM = 128
DN_WIDTH = DN_HEADS * DN_HEAD_DIM
DN_CHUNK = 64
CONV_WIDTH = 4
D_FF = 2816
EPS = 1e-6
NEG_INF = -1e30
LOG2E = 1.4426950408889634

LANES = 128
SUBLANES = 8
N_MAIN = FOX_WIDTH + 3 * DN_WIDTH + DN_WIDTH + 2 * D_MODEL
COL_CHUNK = 512
LOGF_LANE = 0
BETA_LANE = FOX_HEADS
G_LANE = FOX_HEADS + DN_HEADS
N_SMALL = FOX_HEADS + 2 * DN_HEADS
VMEM_LIMIT = 56 * 1024 * 1024


def _const_spec(shape):
    nd = len(shape)
    return pl.BlockSpec(shape, lambda *_: (0,) * nd, pipeline_mode=pl.Buffered(1))


def _dot(a, b, **kw):
    return jnp.dot(a, b, preferred_element_type=F32, **kw)


def _dot_nt(a, b, **kw):
    return lax.dot_general(a, b, (((1,), (1,)), ((), ())), preferred_element_type=F32, **kw)


def _dot_tn(a, b, **kw):
    return lax.dot_general(a, b, (((0,), (0,)), ((), ())), preferred_element_type=F32, **kw)


def _inproj_kernel(x_ref, nw_ref, wqv_t_ref, wmain_ref, wsmall_ref, bias_ref, carry0_ref,
                   q_t_ref, k_ref, v_t_ref, kbias_ref, dqkv_ref, dzs_ref, gates_ref, scol_ref,
                   srow_ref, carry_ref, *, tm, tiles_per_batch, n_pad):
    i = pl.program_id(0)

    @pl.when(i % tiles_per_batch == 0)
    def _():
        carry_ref[...] = carry0_ref[...]

    x = x_ref[...]
    ms = jnp.mean(x * x, axis=-1, keepdims=True)
    hn = (x * lax.rsqrt(ms + EPS) * nw_ref[...]).astype(BF16)

    q_t = _dot_nt(wqv_t_ref[:FOX_WIDTH, :], hn) * (FOX_HEAD_DIM ** -0.5 * LOG2E)
    q_t_ref[...] = q_t.astype(BF16)
    v_t_ref[...] = _dot_nt(wqv_t_ref[FOX_WIDTH:, :], hn).astype(BF16)

    for c in range(N_MAIN // COL_CHUNK):
        acc = _dot(hn, wmain_ref[:, c * COL_CHUNK:(c + 1) * COL_CHUNK])
        if c == 0:
            k_ref[...] = acc.astype(BF16)
        elif c < 4:
            dqkv_ref[:, (c - 1) * COL_CHUNK:c * COL_CHUNK] = acc
        elif c == 4:
            dzs_ref[...] = acc * (1.0 / (1.0 + jnp.exp(-acc)))
        else:
            gates_ref[:, (c - 5) * COL_CHUNK:(c - 4) * COL_CHUNK] = 1.0 / (1.0 + jnp.exp(-acc))

    z = _dot(hn, wsmall_ref[...]) + bias_ref[0:1, :]
    lane = lax.broadcasted_iota(jnp.int32, (tm, LANES), 1)
    e = jnp.exp(-jnp.abs(z))
    l1p = jnp.log1p(e)
    logf = jnp.minimum(z, 0.0) - l1p
    softplus = jnp.maximum(z, 0.0) + l1p
    sig = jnp.where(z >= 0.0, 1.0, e) / (1.0 + e)
    g = -jnp.exp(bias_ref[1:2, :]) * softplus
    if n_pad:
        row = lax.broadcasted_iota(jnp.int32, (tm, LANES), 0) + (i % tiles_per_batch) * tm
        vm = (row >= n_pad).astype(F32)
        sig = sig * vm
        g = g * vm
    val = jnp.where(lane < BETA_LANE, logf,
                    jnp.where(lane < G_LANE, sig, jnp.where(lane < N_SMALL, g, 0.0)))
    r = lax.broadcasted_iota(jnp.int32, (tm, tm), 0)
    cidx = lax.broadcasted_iota(jnp.int32, (tm, tm), 1)
    tril = (r >= cidx).astype(F32)
    cs = _dot(tril, val, precision=HIGHEST) + carry_ref[...]
    carry_ref[...] = cs[tm - 1:tm, :]
    out = jnp.where(lane < BETA_LANE, cs, val)
    scol_ref[...] = out
    srow_ref[...] = out.T[:2 * SUBLANES, :]
    nc = jnp.where(lane < FOX_HEADS, cs * -LOG2E, 0.0)
    hi = nc.astype(BF16).astype(F32)
    mid = (nc - hi).astype(BF16).astype(F32)
    lo = nc - hi - mid
    kbias = hi + pltpu.roll(mid, FOX_HEADS, axis=1) + pltpu.roll(lo, 2 * FOX_HEADS, axis=1)
    kbias_ref[...] = kbias.astype(BF16)


def _inproj(x2d, nw, wqv_t, wmain, wsmall, bias, carry0, *, tm, tiles_per_batch, n_pad):
    m = x2d.shape[0]
    row = lambda w: pl.BlockSpec((tm, w), lambda i: (i, 0))
    col = lambda h: pl.BlockSpec((h, tm), lambda i: (0, i))
    return pl.pallas_call(
        functools.partial(_inproj_kernel, tm=tm, tiles_per_batch=tiles_per_batch, n_pad=n_pad),
        grid=(m // tm,),
        in_specs=[row(D_MODEL), _const_spec((1, D_MODEL)), _const_spec((2 * FOX_WIDTH, D_MODEL)),
                  _const_spec((D_MODEL, N_MAIN)),
                  _const_spec((D_MODEL, LANES)), _const_spec((SUBLANES, LANES)),
                  _const_spec((1, LANES))],
        out_specs=[col(FOX_WIDTH), row(FOX_WIDTH), col(FOX_WIDTH), row(LANES),
                   row(3 * DN_WIDTH), row(DN_WIDTH), row(2 * D_MODEL),
                   row(LANES),
                   pl.BlockSpec((None, 2 * SUBLANES, tm),
                                lambda i: (i // tiles_per_batch, 0, i % tiles_per_batch))],
        out_shape=[jax.ShapeDtypeStruct((FOX_WIDTH, m), BF16),
                   jax.ShapeDtypeStruct((m, FOX_WIDTH), BF16),
                   jax.ShapeDtypeStruct((FOX_WIDTH, m), BF16),
                   jax.ShapeDtypeStruct((m, LANES), BF16),
                   jax.ShapeDtypeStruct((m, 3 * DN_WIDTH), F32),
                   jax.ShapeDtypeStruct((m, DN_WIDTH), F32),
                   jax.ShapeDtypeStruct((m, 2 * D_MODEL), F32),
                   jax.ShapeDtypeStruct((m, LANES), F32),
                   jax.ShapeDtypeStruct((m // (tm * tiles_per_batch), 2 * SUBLANES,
                                         tm * tiles_per_batch), F32)],
        scratch_shapes=[pltpu.VMEM((1, LANES), F32)],
        compiler_params=pltpu.CompilerParams(dimension_semantics=("arbitrary",),
                                             vmem_limit_bytes=VMEM_LIMIT),
        name="inproj",
    )(x2d, nw, wqv_t, wmain, wsmall, bias, carry0)


def _fox_kernel(q_t_ref, k_ref, v_t_ref, kb_ref, kp_ref, v_tp_ref, kbp_ref,
                o_ref, m_ref, l_ref, acc_ref, t_ref, *, tq, tk):
    p = pl.program_id(1)
    i = pl.program_id(2)
    sub = lax.broadcasted_iota(jnp.int32, (LANES, 1), 0)
    q_t = q_t_ref[...]
    zero = jnp.zeros_like(q_t)
    q_th = [jnp.where(sub < FOX_HEAD_DIM, q_t, zero), jnp.where(sub >= FOX_HEAD_DIM, q_t, zero)]
    sel = [jnp.where(jnp.logical_and(sub < 3 * FOX_HEADS, (sub % FOX_HEADS) == 2 * p + h),
                     1.0, 0.0).astype(BF16) * jnp.ones((1, tq), BF16) for h in range(2)]
    rhs = [jnp.concatenate([q_th[h], sel[h]], axis=0) for h in range(2)]

    def scores(kt, kbt):
        lhs = jnp.concatenate([kt, kbt], axis=1)
        return tuple(_dot(lhs, rhs[h]) for h in range(2))

    def scores_at(off):
        return scores(k_ref[pl.ds(off, tk), :], kb_ref[pl.ds(off, tk), :])

    def update(ts, v_tt, mask, first):
        pvs, alphas = [], []
        for h in range(2):
            t = ts[h]
            if mask is not None:
                t = jnp.where(mask, t, NEG_INF)
            t_max = jnp.max(t, axis=0, keepdims=True)
            if first:
                m_new = t_max
            else:
                m_prev = m_ref[h]
                m_new = jnp.maximum(m_prev, t_max)
            pm = jnp.exp2(t - m_new)
            psum = jnp.sum(pm, axis=0, keepdims=True)
            pvs.append(_dot(v_tt[h * FOX_HEAD_DIM:(h + 1) * FOX_HEAD_DIM, :], pm.astype(BF16)))
            if first:
                l_ref[h] = psum
            else:
                alpha = jnp.exp2(m_prev - m_new)
                l_ref[h] = alpha * l_ref[h] + psum
                alphas.append(jnp.broadcast_to(alpha, (FOX_HEAD_DIM, tq)))
            m_ref[h] = m_new
        pv = jnp.concatenate(pvs, axis=0)
        if first:
            acc_ref[...] = pv
        else:
            acc_ref[...] = jnp.concatenate(alphas, axis=0) * acc_ref[...] + pv

    krow_p = lax.broadcasted_iota(jnp.int32, (PREFIX, tq), 0)
    update(scores(kp_ref[...], kbp_ref[...]), v_tp_ref[...], krow_p >= N_PAD, True)

    def stash(ts):
        for h in range(2):
            t_ref[h] = ts[h]

    def stashed():
        return tuple(t_ref[h] for h in range(2))

    def v_at(off):
        return v_t_ref[:, pl.ds(off, tk)]

    stash(scores_at(0))

    def body(jj, carry):
        off = pl.multiple_of(jj * tq, tq)
        ts_odd = scores_at(off + tk)
        update(stashed(), v_at(off), None, False)
        ts_even = scores_at(off + tq)
        update(ts_odd, v_at(off + tk), None, False)
        stash(ts_even)
        return carry

    lax.fori_loop(0, i, body, 0)

    off = pl.multiple_of(i * tq, tq)
    rr = lax.broadcasted_iota(jnp.int32, (tk, tq), 0)
    cc = lax.broadcasted_iota(jnp.int32, (tk, tq), 1)
    ts_odd = scores_at(off + tk)
    update(stashed(), v_at(off), rr <= cc, False)
    update(ts_odd, v_at(off + tk), rr + tk <= cc, False)

    l_all = jnp.concatenate([jnp.broadcast_to(l_ref[h], (FOX_HEAD_DIM, tq)) for h in range(2)], axis=0)
    o_ref[...] = (acc_ref[...] / l_all).T.astype(BF16)


def _fox(q_t, k, v_t, kbias, k_p, v_tp, kbias_p, *, tq, tk):
    b, s, _ = k.shape
    npairs = FOX_HEADS // 2
    nq = s // tq
    return pl.pallas_call(
        functools.partial(_fox_kernel, tq=tq, tk=tk),
        grid=(b, npairs, nq),
        in_specs=[
            pl.BlockSpec((LANES, tq), lambda bi, p, i: (p, bi * nq + i)),
            pl.BlockSpec((None, s, LANES), lambda bi, p, i: (bi, 0, p)),
            pl.BlockSpec((LANES, s), lambda bi, p, i: (p, bi)),
            pl.BlockSpec((None, s, LANES), lambda bi, p, i: (bi, 0, 0)),
            pl.BlockSpec((PREFIX, LANES), lambda bi, p, i: (0, p)),
            pl.BlockSpec((LANES, PREFIX), lambda bi, p, i: (p, 0)),
            pl.BlockSpec((PREFIX, LANES), lambda bi, p, i: (0, 0)),
        ],
        out_specs=pl.BlockSpec((None, tq, LANES), lambda bi, p, i: (bi, i, p)),
        out_shape=jax.ShapeDtypeStruct((b, s, FOX_WIDTH), BF16),
        scratch_shapes=[pltpu.VMEM((2, 1, tq), F32), pltpu.VMEM((2, 1, tq), F32),
                        pltpu.VMEM((LANES, tq), F32), pltpu.VMEM((2, tk, tq), F32)],
        compiler_params=pltpu.CompilerParams(
            dimension_semantics=("arbitrary", "arbitrary", "arbitrary"),
            vmem_limit_bytes=VMEM_LIMIT),
        name="fox_attention",
    )(q_t, k, v_t, kbias, k_p, v_tp, kbias_p)


def _dn_kernel(dqkv_ref, halo_ref, halo0_ref, convw_ref, scol_ref, srow_ref, dzs_ref, onw_ref,
               s0_ref, odn_ref, sfin_ref, state_ref, conv_ref, *, tt, n_pad):
    i = pl.program_id(1)
    nchunk = tt // DN_CHUNK

    @pl.when(i == 0)
    def _():
        state_ref[...] = s0_ref[...]

    u = dqkv_ref[...]
    if n_pad:
        row = lax.broadcasted_iota(jnp.int32, (tt, 1), 0) + i * tt
        u = u * (row >= n_pad).astype(F32)
    conv_ref[0:SUBLANES, :] = jnp.where(i == 0, halo0_ref[...], halo_ref[...])
    conv_ref[SUBLANES:, :] = u

    def conv_silu(c0):
        a = convw_ref[CONV_WIDTH - 1:CONV_WIDTH, c0:c0 + LANES] * conv_ref[SUBLANES:, c0:c0 + LANES]
        for t in range(CONV_WIDTH - 1):
            start = SUBLANES - (CONV_WIDTH - 1) + t
            a = a + convw_ref[t:t + 1, c0:c0 + LANES] * conv_ref[start:start + tt, c0:c0 + LANES]
        return a * (1.0 / (1.0 + jnp.exp(-a)))

    sc = scol_ref[...]
    sr = srow_ref[...]
    rr = lax.broadcasted_iota(jnp.int32, (tt, tt), 0)
    cc = lax.broadcasted_iota(jnp.int32, (tt, tt), 1)
    rc_xor = jnp.bitwise_xor(rr, cc)
    same = rc_xor < DN_CHUNK
    tril_m = jnp.logical_and(same, rr >= cc)
    strict_m = jnp.logical_and(same, rr > cc)
    eye = (rr == cc).astype(F32)
    gc_cols = _dot(tril_m.astype(F32), sc, precision=HIGHEST)
    gl_cols = _dot(same.astype(F32), sc, precision=HIGHEST)
    gc_rows = _dot(sr, jnp.logical_and(same, rr <= cc).astype(F32), precision=HIGHEST)

    heads = range(DN_HEADS)
    qs = [conv_silu(h * LANES) for h in heads]
    ks = [conv_silu(DN_WIDTH + h * LANES) for h in heads]
    vs = [conv_silu(2 * DN_WIDTH + h * LANES) for h in heads]
    states = [state_ref[h] for h in heads]
    zs = [dzs_ref[:, h * LANES:(h + 1) * LANES] for h in heads]
    qs = [q * lax.rsqrt(jnp.sum(q * q, axis=-1, keepdims=True) + EPS) * (DN_HEAD_DIM ** -0.5)
          for q in qs]
    ks = [k * lax.rsqrt(jnp.sum(k * k, axis=-1, keepdims=True) + EPS) for k in ks]
    betas = [sc[:, BETA_LANE + h:BETA_LANE + h + 1] for h in heads]
    gcs = [gc_cols[:, G_LANE + h:G_LANE + h + 1] for h in heads]
    gls = [gl_cols[:, G_LANE + h:G_LANE + h + 1] for h in heads]
    decays = [jnp.exp(jnp.where(tril_m, gcs[h] - gc_rows[G_LANE + h:G_LANE + h + 1, :], NEG_INF))
              for h in heads]
    egcs = [jnp.exp(g) for g in gcs]
    kbs = [ks[h] * betas[h] for h in heads]
    ks_b = [k.astype(BF16) for k in ks]
    lmats = [jnp.where(strict_m, _dot_nt(kbs[h].astype(BF16), ks_b[h]) * decays[h], 0.0)
             for h in heads]
    attns = [_dot_nt(qs[h].astype(BF16), ks_b[h]) * decays[h] for h in heads]
    ainvs = [eye - jnp.where(rc_xor == 1, lm, 0.0) for lm in lmats]
    for lvl in range(1, 6):
        blk = 2 ** lvl
        lvl_m = jnp.logical_and(rc_xor >= blk, rc_xor < 2 * blk)
        ainvs_b = [a.astype(BF16) for a in ainvs]
        mids = [_dot(jnp.where(lvl_m, lmats[h], 0.0).astype(BF16), ainvs_b[h]).astype(BF16)
                for h in heads]
        ainvs = [ainvs[h] - _dot(ainvs_b[h], mids[h]) for h in heads]
    sols = [_dot(ainvs[h].astype(BF16),
                 jnp.concatenate([vs[h] * betas[h], kbs[h] * egcs[h]], axis=1).astype(BF16))
            for h in heads]
    uus = [sol[:, :DN_HEAD_DIM] for sol in sols]
    wws = [sol[:, DN_HEAD_DIM:] for sol in sols]
    qds = [qs[h] * egcs[h] for h in heads]
    kds = [(ks[h] * jnp.exp(gls[h] - gcs[h])).astype(BF16) for h in heads]

    vnews = [[] for _ in heads]
    qss = [[] for _ in heads]
    for c in range(nchunk):
        lo, hi = c * DN_CHUNK, (c + 1) * DN_CHUNK
        rs = [_dot(jnp.concatenate([wws[h][lo:hi], qds[h][lo:hi]], axis=0).astype(BF16),
                   states[h].astype(BF16)) for h in heads]
        vns = [uus[h][lo:hi] - rs[h][:DN_CHUNK] for h in heads]
        states = [jnp.exp(gls[h][lo:lo + 1, :]) * states[h]
                  + _dot_tn(kds[h][lo:hi], vns[h].astype(BF16)) for h in heads]
        for h in heads:
            qss[h].append(rs[h][DN_CHUNK:])
            vnews[h].append(vns[h])
    outs = []
    for h in heads:
        o = (jnp.concatenate(qss[h], axis=0)
             + _dot(attns[h].astype(BF16), jnp.concatenate(vnews[h], axis=0).astype(BF16)))
        o = o * lax.rsqrt(jnp.mean(o * o, axis=-1, keepdims=True) + EPS) * onw_ref[...]
        outs.append((o * zs[h]).astype(BF16))
    for h in heads:
        state_ref[h] = states[h]
        sfin_ref[h] = states[h]
        odn_ref[:, h * LANES:(h + 1) * LANES] = outs[h]


def _deltanet(dqkv, halo0, convw, scol, srow, dzs, onw, s0, *, tt, n_pad):
    b, s, _ = dqkv.shape
    nt = s // tt
    hb = tt // SUBLANES
    return pl.pallas_call(
        functools.partial(_dn_kernel, tt=tt, n_pad=n_pad),
        grid=(b, nt),
        in_specs=[
            pl.BlockSpec((None, tt, 3 * DN_WIDTH), lambda bi, i: (bi, i, 0)),
            pl.BlockSpec((None, SUBLANES, 3 * DN_WIDTH),
                         lambda bi, i: (bi, jnp.maximum(i * hb - 1, 0), 0)),
            pl.BlockSpec((SUBLANES, 3 * DN_WIDTH), lambda bi, i: (0, 0)),
            pl.BlockSpec((CONV_WIDTH, 3 * DN_WIDTH), lambda bi, i: (0, 0)),
            pl.BlockSpec((None, tt, LANES), lambda bi, i: (bi, i, 0)),
            pl.BlockSpec((None, 2 * SUBLANES, tt), lambda bi, i: (bi, 0, i)),
            pl.BlockSpec((None, tt, DN_WIDTH), lambda bi, i: (bi, i, 0)),
            pl.BlockSpec((1, DN_HEAD_DIM), lambda bi, i: (0, 0)),
            pl.BlockSpec((DN_HEADS, DN_HEAD_DIM, DN_HEAD_DIM), lambda bi, i: (0, 0, 0)),
        ],
        out_specs=[pl.BlockSpec((None, tt, DN_WIDTH), lambda bi, i: (bi, i, 0)),
                   pl.BlockSpec((DN_HEADS, DN_HEAD_DIM, DN_HEAD_DIM), lambda bi, i: (0, 0, 0))],
        out_shape=[jax.ShapeDtypeStruct((b, s, DN_WIDTH), BF16),
                   jax.ShapeDtypeStruct((DN_HEADS, DN_HEAD_DIM, DN_HEAD_DIM), F32)],
        scratch_shapes=[pltpu.VMEM((DN_HEADS, DN_HEAD_DIM, DN_HEAD_DIM), F32),
                        pltpu.VMEM((tt + SUBLANES, 3 * DN_WIDTH), F32)],
        compiler_params=pltpu.CompilerParams(dimension_semantics=("arbitrary", "arbitrary"),
                                             vmem_limit_bytes=VMEM_LIMIT),
        name="deltanet",
    )(dqkv, dqkv, halo0, convw, scol, srow, dzs, onw, s0)


def _tail_kernel(x_ref, ofox_ref, odn_ref, gates_ref, wbf_ref, wbd_ref, wout_ref, fnw_ref,
                 wg_ref, wu_ref, wd_ref, finw_ref, o_ref, *, ff_chunk):
    a = _dot(ofox_ref[...], wbf_ref[...])
    bb = _dot(odn_ref[...], wbd_ref[...])
    y = gates_ref[:, :D_MODEL] * a + gates_ref[:, D_MODEL:] * bb
    h1 = x_ref[...] + _dot(y.astype(BF16), wout_ref[...])
    n = (h1 * lax.rsqrt(jnp.mean(h1 * h1, axis=-1, keepdims=True) + EPS) * fnw_ref[...]).astype(BF16)
    acc = h1
    for c in range(D_FF // ff_chunk):
        lo, hi = c * ff_chunk, (c + 1) * ff_chunk
        gt = _dot(n, wg_ref[:, lo:hi])
        up = _dot(n, wu_ref[:, lo:hi])
        act = (gt * (1.0 / (1.0 + jnp.exp(-gt))) * up).astype(BF16)
        acc = acc + _dot(act, wd_ref[lo:hi, :])
    o_ref[...] = acc * lax.rsqrt(jnp.mean(acc * acc, axis=-1, keepdims=True) + EPS) * finw_ref[...]


def _tail(x2d, ofox, odn, gates, wbf, wbd, wout, fnw, wg, wu, wd, finw, *, tm, ff_chunk):
    m = x2d.shape[0]
    row = lambda w: pl.BlockSpec((tm, w), lambda i: (i, 0))
    return pl.pallas_call(
        functools.partial(_tail_kernel, ff_chunk=ff_chunk),
        grid=(m // tm,),
        in_specs=[row(D_MODEL), row(FOX_WIDTH), row(DN_WIDTH), row(2 * D_MODEL),
                  _const_spec((FOX_WIDTH, D_MODEL)), _const_spec((DN_WIDTH, D_MODEL)),
                  _const_spec((D_MODEL, D_MODEL)), _const_spec((1, D_MODEL)),
                  _const_spec((D_MODEL, D_FF)), _const_spec((D_MODEL, D_FF)),
                  _const_spec((D_FF, D_MODEL)), _const_spec((1, D_MODEL))],
        out_specs=row(D_MODEL),
        out_shape=jax.ShapeDtypeStruct((m, D_MODEL), F32),
        compiler_params=pltpu.CompilerParams(dimension_semantics=("arbitrary",),
                                             vmem_limit_bytes=VMEM_LIMIT),
        name="merge_ffn",
    )(x2d, ofox, odn, gates, wbf, wbd, wout, fnw, wg, wu, wd, finw)


def _pick_tile(n, pref):
    t = min(pref, n)
    while n % t:
        t //= 2
    return t


def kernel(x, meta_tokens, mix_norm_w, w_in, fox_forget_bias, dn_conv_w, dn_a_log, dn_dt_bias,
           dn_out_norm_w, w_branch_fox, w_branch_dn, w_out, ffn_norm_w, w_ffn_gate, w_ffn_up,
           w_ffn_down, final_norm_w):
    b, s, _ = x.shape
    assert mix_norm_w.shape[0] == 1, "single layer only"
    assert s % PREFIX == 0
    m = b * s

    wi = w_in[0]
    o_small0 = 3 * FOX_WIDTH
    o_dn = o_small0 + FOX_HEADS
    o_small1 = o_dn + 3 * DN_WIDTH
    o_rest = o_small1 + 2 * DN_HEADS
    wqv_t = jnp.concatenate([wi[:, :FOX_WIDTH], wi[:, 2 * FOX_WIDTH:o_small0]], axis=1).T.astype(BF16)
    wmain = jnp.concatenate([wi[:, FOX_WIDTH:2 * FOX_WIDTH], wi[:, o_dn:o_small1], wi[:, o_rest:]],
                            axis=1).astype(BF16)
    wsmall = jnp.concatenate([wi[:, o_small0:o_dn], wi[:, o_small1:o_rest]], axis=1)
    wsmall = jnp.pad(wsmall, ((0, 0), (0, LANES - N_SMALL))).astype(BF16)
    bias = jnp.zeros((SUBLANES, LANES), F32)
    bias = bias.at[0, LOGF_LANE:LOGF_LANE + FOX_HEADS].set(fox_forget_bias[0].astype(F32))
    bias = bias.at[0, G_LANE:G_LANE + DN_HEADS].set(dn_dt_bias[0].astype(F32))
    bias = bias.at[1, G_LANE:G_LANE + DN_HEADS].set(dn_a_log[0].astype(F32))
    nw = mix_norm_w[0].reshape(1, D_MODEL).astype(F32)

    x_p = jnp.concatenate([jnp.zeros((N_PAD, D_MODEL), F32), meta_tokens.astype(F32)], axis=0)
    _, k_p, v_tp, kbias_p, dqkv_p, dzs_p, _, scol_p, srow_p = _inproj(
        x_p, nw, wqv_t, wmain, wsmall, bias, jnp.zeros((1, LANES), F32),
        tm=PREFIX, tiles_per_batch=1, n_pad=N_PAD)

    tm = _pick_tile(s, 512)
    q_t, k, v_t, kbias, dqkv, dzs, gates, scol, srow3 = _inproj(
        x.reshape(m, D_MODEL), nw, wqv_t, wmain, wsmall, bias, scol_p[PREFIX - 1:PREFIX, :],
        tm=tm, tiles_per_batch=s // tm, n_pad=0)

    scol3 = scol.reshape(b, s, LANES)
    tq = _pick_tile(s, 512)
    ofox = _fox(q_t, k.reshape(b, s, FOX_WIDTH), v_t, kbias.reshape(b, s, LANES), k_p, v_tp, kbias_p,
                tq=tq, tk=tq // 2)

    convw = dn_conv_w[0].astype(F32)
    onw = dn_out_norm_w[0].reshape(1, DN_HEAD_DIM).astype(F32)
    _, s_prefix = _deltanet(
        dqkv_p[None], jnp.zeros((SUBLANES, 3 * DN_WIDTH), F32), convw, scol_p[None], srow_p,
        dzs_p[None], onw, jnp.zeros((DN_HEADS, DN_HEAD_DIM, DN_HEAD_DIM), F32),
        tt=PREFIX, n_pad=N_PAD)
    odn, _ = _deltanet(
        dqkv.reshape(b, s, 3 * DN_WIDTH), dqkv_p[PREFIX - SUBLANES:], convw, scol3, srow3,
        dzs.reshape(b, s, DN_WIDTH), onw, s_prefix, tt=_pick_tile(s, 128), n_pad=0)

    out = _tail(
        x.reshape(m, D_MODEL), ofox.reshape(m, FOX_WIDTH), odn.reshape(m, DN_WIDTH), gates,
        w_branch_fox[0].astype(BF16), w_branch_dn[0].astype(BF16), w_out[0].astype(BF16),
        ffn_norm_w[0].reshape(1, D_MODEL).astype(F32),
        w_ffn_gate[0].astype(BF16), w_ffn_up[0].astype(BF16), w_ffn_down[0].astype(BF16),
        final_norm_w.reshape(1, D_MODEL).astype(F32),
        tm=_pick_tile(m, 512), ff_chunk=D_FF // 2)
    return out.reshape(b, s, D_MODEL)
```

```python
import functools

import jax
import jax.numpy as jnp
from jax import lax
from jax.experimental import pallas as pl
from jax.experimental.pallas import tpu as pltpu

F32 = jnp.float32
BF16 = jnp.bfloat16
HIGHEST = lax.Precision.HIGHEST

D_MODEL = 1024
N_META = 16
PREFIX = 128
N_PAD = PREFIX - N_META
FOX_HEADS = 8
FOX_HEAD_DIM = 64
FOX_WIDTH = FOX_HEADS * FOX_HEAD_DIM
DN_HEADS = 4
DN_HEAD_DIM = 128
DN_WIDTH = DN_HEADS * DN_HEAD_DIM
DN_CHUNK = 64
DN_BLOCK = 2 * DN_CHUNK
CONV_WIDTH = 4
D_FF = 2816
EPS = 1e-6
NEG_INF = -1e30
LOG2E = 1.4426950408889634

LANES = 128
SUBLANES = 8
N_MAIN = FOX_WIDTH + 3 * DN_WIDTH + DN_WIDTH + 2 * D_MODEL
COL_CHUNK = 512
LOGF_LANE = 0
BETA_LANE = FOX_HEADS
G_LANE = BETA_LANE + DN_HEADS
GSUF_LANE = G_LANE + DN_HEADS
GRAW_LANE = GSUF_LANE + DN_HEADS
N_SMALL = GRAW_LANE + DN_HEADS
SROW = 24
VMEM_LIMIT = 56 * 1024 * 1024


def _const_spec(shape):
    nd = len(shape)
    return pl.BlockSpec(shape, lambda *_: (0,) * nd, pipeline_mode=pl.Buffered(1))


def _dot(a, b, **kw):
    return jnp.dot(a, b, preferred_element_type=F32, **kw)


def _dot_nt(a, b, **kw):
    return lax.dot_general(a, b, (((1,), (1,)), ((), ())), preferred_element_type=F32, **kw)


def _dot_tn(a, b, **kw):
    return lax.dot_general(a, b, (((0,), (0,)), ((), ())), preferred_element_type=F32, **kw)


def _inproj_kernel(x_ref, nw_ref, wqv_t_ref, wmain_ref, wsmall_ref, bias_ref, carry0_ref,
                   q_t_ref, k_ref, v_t_ref, kbias_ref, dqkv_ref, dzs_ref, gates_ref, scol_ref,
                   srow_ref, carry_ref, *, tm, tiles_per_batch, n_pad):
    i = pl.program_id(0)

    @pl.when(i % tiles_per_batch == 0)
    def _():
        carry_ref[...] = carry0_ref[...]

    x = x_ref[...]
    ms = jnp.mean(x * x, axis=-1, keepdims=True)
    hn = (x * lax.rsqrt(ms + EPS) * nw_ref[...]).astype(BF16)

    z = _dot(hn, wsmall_ref[...]) + bias_ref[0:1, :]
    lane = lax.broadcasted_iota(jnp.int32, (tm, LANES), 1)
    row = lax.broadcasted_iota(jnp.int32, (tm, LANES), 0)
    e = jnp.exp(-jnp.abs(z))
    l1p = jnp.log1p(e)
    logf = jnp.minimum(z, 0.0) - l1p
    softplus = jnp.maximum(z, 0.0) + l1p
    sig = jnp.where(z >= 0.0, 1.0, e) / (1.0 + e)
    g = -jnp.exp(bias_ref[1:2, :]) * softplus
    if n_pad:
        vm = (row + (i % tiles_per_batch) * tm >= n_pad).astype(F32)
        sig = sig * vm
        g = g * vm
    val = jnp.where(lane < BETA_LANE, logf,
                    jnp.where(lane < G_LANE, sig, jnp.where(lane < N_SMALL, g, 0.0)))
    in_chunk = row % DN_CHUNK
    is_logf = lane < BETA_LANE
    is_gpre = jnp.logical_and(lane >= G_LANE, lane < GSUF_LANE)
    is_gsuf = jnp.logical_and(lane >= GSUF_LANE, lane < GRAW_LANE)
    scan = val
    sh = 1
    while sh < tm:
        take_up = jnp.logical_and(is_logf, row >= sh)
        if sh < DN_CHUNK:
            take_up = jnp.logical_or(take_up, jnp.logical_and(is_gpre, in_chunk >= sh))
            take_down = jnp.logical_and(is_gsuf, in_chunk + sh < DN_CHUNK)
            below = jnp.where(take_down, pltpu.roll(scan, tm - sh, axis=0), 0.0)
        else:
            below = 0.0
        scan = scan + jnp.where(take_up, pltpu.roll(scan, sh, axis=0), below)
        sh *= 2
    out = scan + jnp.where(is_logf, carry_ref[...], 0.0)
    carry_ref[...] = out[tm - 1:tm, :]
    scol_ref[...] = out
    srow_ref[...] = out.T[:SROW, :]
    nc = jnp.where(is_logf, out * -LOG2E, 0.0)
    hi = nc.astype(BF16).astype(F32)
    mid = (nc - hi).astype(BF16).astype(F32)
    lo = nc - hi - mid
    kbias = hi + pltpu.roll(mid, FOX_HEADS, axis=1) + pltpu.roll(lo, 2 * FOX_HEADS, axis=1)
    kbias_ref[...] = kbias.astype(BF16)

    q_t = _dot_nt(wqv_t_ref[:FOX_WIDTH, :], hn) * (FOX_HEAD_DIM ** -0.5 * LOG2E)
    q_t_ref[...] = q_t.astype(BF16)
    v_t_ref[...] = _dot_nt(wqv_t_ref[FOX_WIDTH:, :], hn).astype(BF16)

    for c in range(N_MAIN // COL_CHUNK):
        acc = _dot(hn, wmain_ref[:, c * COL_CHUNK:(c + 1) * COL_CHUNK])
        if c == 0:
            k_ref[...] = acc.astype(BF16)
        elif c < 4:
            dqkv_ref[:, (c - 1) * COL_CHUNK:c * COL_CHUNK] = acc
        elif c == 4:
            dzs_ref[...] = acc * (1.0 / (1.0 + jnp.exp(-acc)))
        else:
            gates_ref[:, (c - 5) * COL_CHUNK:(c - 4) * COL_CHUNK] = 1.0 / (1.0 + jnp.exp(-acc))


def _inproj(x2d, nw, wqv_t, wmain, wsmall, bias, carry0, *, tm, tiles_per_batch, n_pad):
    m = x2d.shape[0]
    row = lambda w: pl.BlockSpec((tm, w), lambda i: (i, 0))
    col = lambda h: pl.BlockSpec((h, tm), lambda i: (0, i))
    return pl.pallas_call(
        functools.partial(_inproj_kernel, tm=tm, tiles_per_batch=tiles_per_batch, n_pad=n_pad),
        grid=(m // tm,),
        in_specs=[row(D_MODEL), _const_spec((1, D_MODEL)), _const_spec((2 * FOX_WIDTH, D_MODEL)),
                  _const_spec((D_MODEL, N_MAIN)),
                  _const_spec((D_MODEL, LANES)), _const_spec((SUBLANES, LANES)),
                  _const_spec((1, LANES))],
        out_specs=[col(FOX_WIDTH), row(FOX_WIDTH), col(FOX_WIDTH), row(LANES),
                   row(3 * DN_WIDTH), row(DN_WIDTH), row(2 * D_MODEL),
                   row(LANES),
                   pl.BlockSpec((None, SROW, tm),
                                lambda i: (i // tiles_per_batch, 0, i % tiles_per_batch))],
        out_shape=[jax.ShapeDtypeStruct((FOX_WIDTH, m), BF16),
                   jax.ShapeDtypeStruct((m, FOX_WIDTH), BF16),
                   jax.ShapeDtypeStruct((FOX_WIDTH, m), BF16),
                   jax.ShapeDtypeStruct((m, LANES), BF16),
                   jax.ShapeDtypeStruct((m, 3 * DN_WIDTH), F32),
                   jax.ShapeDtypeStruct((m, DN_WIDTH), F32),
                   jax.ShapeDtypeStruct((m, 2 * D_MODEL), F32),
                   jax.ShapeDtypeStruct((m, LANES), F32),
                   jax.ShapeDtypeStruct((m // (tm * tiles_per_batch), SROW,
                                         tm * tiles_per_batch), F32)],
        scratch_shapes=[pltpu.VMEM((1, LANES), F32)],
        compiler_params=pltpu.CompilerParams(dimension_semantics=("arbitrary",),
                                             vmem_limit_bytes=VMEM_LIMIT),
        name="inproj",
    )(x2d, nw, wqv_t, wmain, wsmall, bias, carry0)


def _fox_kernel(q_t_ref, k_ref, v_t_ref, kb_ref, kp_ref, v_tp_ref, kbp_ref,
                o_ref, m_ref, l_ref, acc_ref, t_ref, *, tq, tk):
    p = pl.program_id(1)
    i = pl.program_id(2)
    sub = lax.broadcasted_iota(jnp.int32, (LANES, 1), 0)
    q_t = q_t_ref[...]
    zero = jnp.zeros_like(q_t)
    q_th = [jnp.where(sub < FOX_HEAD_DIM, q_t, zero), jnp.where(sub >= FOX_HEAD_DIM, q_t, zero)]
    sel = [jnp.where(jnp.logical_and(sub < 3 * FOX_HEADS, (sub % FOX_HEADS) == 2 * p + h),
                     1.0, 0.0).astype(BF16) * jnp.ones((1, tq), BF16) for h in range(2)]
    rhs = [jnp.concatenate([q_th[h], sel[h]], axis=0) for h in range(2)]

    def scores(kt, kbt):
        lhs = jnp.concatenate([kt, kbt], axis=1)
        return tuple(_dot(lhs, rhs[h]) for h in range(2))

    def scores_at(off):
        return scores(k_ref[pl.ds(off, tk), :], kb_ref[pl.ds(off, tk), :])

    def update(ts, v_tt, mask, first):
        pvs, alphas = [], []
        for h in range(2):
            t = ts[h]
            if mask is not None:
                t = jnp.where(mask, t, NEG_INF)
            t_max = jnp.max(t, axis=0, keepdims=True)
            if first:
                m_new = t_max
            else:
                m_prev = m_ref[h]
                m_new = jnp.maximum(m_prev, t_max)
            pm = jnp.exp2(t - m_new)
            psum = jnp.sum(pm, axis=0, keepdims=True)
            pvs.append(_dot(v_tt[h * FOX_HEAD_DIM:(h + 1) * FOX_HEAD_DIM, :], pm.astype(BF16)))
            if first:
                l_ref[h] = psum
            else:
                alpha = jnp.exp2(m_prev - m_new)
                l_ref[h] = alpha * l_ref[h] + psum
                alphas.append(jnp.broadcast_to(alpha, (FOX_HEAD_DIM, tq)))
            m_ref[h] = m_new
        pv = jnp.concatenate(pvs, axis=0)
        if first:
            acc_ref[...] = pv
        else:
            acc_ref[...] = jnp.concatenate(alphas, axis=0) * acc_ref[...] + pv

    krow_p = lax.broadcasted_iota(jnp.int32, (PREFIX, tq), 0)
    update(scores(kp_ref[...], kbp_ref[...]), v_tp_ref[...], krow_p >= N_PAD, True)

    def stash(ts):
        for h in range(2):
            t_ref[h] = ts[h]

    def stashed():
        return tuple(t_ref[h] for h in range(2))

    def v_at(off):
        return v_t_ref[:, pl.ds(off, tk)]

    stash(scores_at(0))

    def body(jj, carry):
        off = pl.multiple_of(jj * tq, tq)
        ts_odd = scores_at(off + tk)
        update(stashed(), v_at(off), None, False)
        ts_even = scores_at(off + tq)
        update(ts_odd, v_at(off + tk), None, False)
        stash(ts_even)
        return carry

    lax.fori_loop(0, i, body, 0)

    off = pl.multiple_of(i * tq, tq)
    rr = lax.broadcasted_iota(jnp.int32, (tk, tq), 0)
    cc = lax.broadcasted_iota(jnp.int32, (tk, tq), 1)
    ts_odd = scores_at(off + tk)
    update(stashed(), v_at(off), rr <= cc, False)
    update(ts_odd, v_at(off + tk), rr + tk <= cc, False)

    l_all = jnp.concatenate([jnp.broadcast_to(l_ref[h], (FOX_HEAD_DIM, tq)) for h in range(2)], axis=0)
    o_ref[...] = (acc_ref[...] / l_all).T.astype(BF16)


def _fox(q_t, k, v_t, kbias, k_p, v_tp, kbias_p, *, tq, tk):
    b, s, _ = k.shape
    npairs = FOX_HEADS // 2
    nq = s // tq
    return pl.pallas_call(
        functools.partial(_fox_kernel, tq=tq, tk=tk),
        grid=(b, npairs, nq),
        in_specs=[
            pl.BlockSpec((LANES, tq), lambda bi, p, i: (p, bi * nq + i)),
            pl.BlockSpec((None, s, LANES), lambda bi, p, i: (bi, 0, p)),
            pl.BlockSpec((LANES, s), lambda bi, p, i: (p, bi)),
            pl.BlockSpec((None, s, LANES), lambda bi, p, i: (bi, 0, 0)),
            pl.BlockSpec((PREFIX, LANES), lambda bi, p, i: (0, p)),
            pl.BlockSpec((LANES, PREFIX), lambda bi, p, i: (p, 0)),
            pl.BlockSpec((PREFIX, LANES), lambda bi, p, i: (0, 0)),
        ],
        out_specs=pl.BlockSpec((None, tq, LANES), lambda bi, p, i: (bi, i, p)),
        out_shape=jax.ShapeDtypeStruct((b, s, FOX_WIDTH), BF16),
        scratch_shapes=[pltpu.VMEM((2, 1, tq), F32), pltpu.VMEM((2, 1, tq), F32),
                        pltpu.VMEM((LANES, tq), F32), pltpu.VMEM((2, tk, tq), F32)],
        compiler_params=pltpu.CompilerParams(
            dimension_semantics=("arbitrary", "arbitrary", "arbitrary"),
            vmem_limit_bytes=VMEM_LIMIT),
        name="fox_attention",
    )(q_t, k, v_t, kbias, k_p, v_tp, kbias_p)


def _dn_kernel(dqkv_ref, halo_ref, halo0_ref, convw_ref, scol_ref, srow_ref, dzs_ref, onw_ref,
               s0_ref, odn_ref, sfin_ref, state_ref, conv_ref, *, tt, n_pad):
    i = pl.program_id(1)
    nchunk = tt // DN_CHUNK

    @pl.when(i == 0)
    def _():
        state_ref[...] = s0_ref[...]

    u = dqkv_ref[...]
    if n_pad:
        row = lax.broadcasted_iota(jnp.int32, (tt, 1), 0) + i * tt
        u = u * (row >= n_pad).astype(F32)
    conv_ref[0:SUBLANES, :] = jnp.where(i == 0, halo0_ref[...], halo_ref[...])
    conv_ref[SUBLANES:, :] = u

    bs = DN_BLOCK
    heads = range(DN_HEADS)
    units = [(sb, h) for sb in range(tt // bs) for h in heads]

    def conv_silu(sb, c0):
        r0 = SUBLANES + sb * bs
        a = convw_ref[CONV_WIDTH - 1:CONV_WIDTH, c0:c0 + LANES] * conv_ref[r0:r0 + bs, c0:c0 + LANES]
        for t in range(CONV_WIDTH - 1):
            start = r0 - (CONV_WIDTH - 1) + t
            a = a + convw_ref[t:t + 1, c0:c0 + LANES] * conv_ref[start:start + bs, c0:c0 + LANES]
        return a * (1.0 / (1.0 + jnp.exp(-a)))

    def small_col(sb, lane):
        return scol_ref[sb * bs:(sb + 1) * bs, lane:lane + 1]

    rr = lax.broadcasted_iota(jnp.int32, (bs, bs), 0)
    cc = lax.broadcasted_iota(jnp.int32, (bs, bs), 1)
    rc_xor = jnp.bitwise_xor(rr, cc)
    same = rc_xor < DN_CHUNK
    tril_m = jnp.logical_and(same, rr >= cc)
    strict_m = jnp.logical_and(same, rr > cc)
    eye = (rr == cc).astype(F32)

    qs = [conv_silu(sb, h * LANES) for sb, h in units]
    ks = [conv_silu(sb, DN_WIDTH + h * LANES) for sb, h in units]
    vs = [conv_silu(sb, 2 * DN_WIDTH + h * LANES) for sb, h in units]
    states = [state_ref[h] for h in heads]
    qs = [q * lax.rsqrt(jnp.sum(q * q, axis=-1, keepdims=True) + EPS) * (DN_HEAD_DIM ** -0.5)
          for q in qs]
    ks = [k * lax.rsqrt(jnp.sum(k * k, axis=-1, keepdims=True) + EPS) for k in ks]
    betas = [small_col(sb, BETA_LANE + h) for sb, h in units]
    gcs = [small_col(sb, G_LANE + h) for sb, h in units]
    gls = [gcs[n] + small_col(sb, GSUF_LANE + h) - small_col(sb, GRAW_LANE + h)
           for n, (sb, h) in enumerate(units)]
    decays = [jnp.exp(jnp.where(
        tril_m, gcs[n] - srow_ref[G_LANE + h:G_LANE + h + 1, sb * bs:(sb + 1) * bs], NEG_INF))
        for n, (sb, h) in enumerate(units)]
    nu = range(len(units))
    egcs = [jnp.exp(g) for g in gcs]
    kbs = [ks[n] * betas[n] for n in nu]
    ks_b = [k.astype(BF16) for k in ks]
    lmats = [jnp.where(strict_m, _dot_nt(kbs[n].astype(BF16), ks_b[n]) * decays[n], 0.0) for n in nu]
    attns = [(_dot_nt(qs[n].astype(BF16), ks_b[n]) * decays[n]).astype(BF16) for n in nu]
    ainvs = [eye - jnp.where(rc_xor == 1, lm, 0.0) for lm in lmats]
    for lvl in range(1, 6):
        blk = 2 ** lvl
        lvl_m = jnp.logical_and(rc_xor >= blk, rc_xor < 2 * blk)
        ainvs_b = [a.astype(BF16) for a in ainvs]
        mids = [_dot(jnp.where(lvl_m, lmats[n], 0.0).astype(BF16), ainvs_b[n]).astype(BF16)
                for n in nu]
        ainvs = [ainvs[n] - _dot(ainvs_b[n], mids[n]) for n in nu]
    sols = [_dot(ainvs[n].astype(BF16),
                 jnp.concatenate([vs[n] * betas[n], kbs[n] * egcs[n]], axis=1).astype(BF16))
            for n in nu]
    uus = [sol[:, :DN_HEAD_DIM] for sol in sols]
    wws = [sol[:, DN_HEAD_DIM:] for sol in sols]
    qds = [qs[n] * egcs[n] for n in nu]
    kds = [(ks[n] * jnp.exp(gls[n] - gcs[n])).astype(BF16) for n in nu]

    vnews = [[] for _ in nu]
    qss = [[] for _ in nu]
    for c in range(nchunk):
        sb, lc = divmod(c, bs // DN_CHUNK)
        lo, hi = lc * DN_CHUNK, (lc + 1) * DN_CHUNK
        un = [sb * DN_HEADS + h for h in heads]
        rs = [_dot(jnp.concatenate([wws[n][lo:hi], qds[n][lo:hi]], axis=0).astype(BF16),
                   states[h].astype(BF16)) for h, n in enumerate(un)]
        vns = [uus[n][lo:hi] - rs[h][:DN_CHUNK] for h, n in enumerate(un)]
        states = [jnp.exp(gls[n][lo:lo + 1, :]) * states[h]
                  + _dot_tn(kds[n][lo:hi], vns[h].astype(BF16)) for h, n in enumerate(un)]
        for h, n in enumerate(un):
            qss[n].append(rs[h][DN_CHUNK:])
            vnews[n].append(vns[h])
    for h in heads:
        state_ref[h] = states[h]
        sfin_ref[h] = states[h]
    for n, (sb, h) in enumerate(units):
        o = (jnp.concatenate(qss[n], axis=0)
             + _dot(attns[n], jnp.concatenate(vnews[n], axis=0).astype(BF16)))
        o = o * lax.rsqrt(jnp.mean(o * o, axis=-1, keepdims=True) + EPS) * onw_ref[...]
        o = o * dzs_ref[sb * bs:(sb + 1) * bs, h * LANES:(h + 1) * LANES]
        odn_ref[sb * bs:(sb + 1) * bs, h * LANES:(h + 1) * LANES] = o.astype(BF16)


def _deltanet(dqkv, halo0, convw, scol, srow, dzs, onw, s0, *, tt, n_pad):
    b, s, _ = dqkv.shape
    nt = s // tt
    hb = tt // SUBLANES
    return pl.pallas_call(
        functools.partial(_dn_kernel, tt=tt, n_pad=n_pad),
        grid=(b, nt),
        in_specs=[
            pl.BlockSpec((None, tt, 3 * DN_WIDTH), lambda bi, i: (bi, i, 0)),
            pl.BlockSpec((None, SUBLANES, 3 * DN_WIDTH),
                         lambda bi, i: (bi, jnp.maximum(i * hb - 1, 0), 0)),
            pl.BlockSpec((SUBLANES, 3 * DN_WIDTH), lambda bi, i: (0, 0)),
            pl.BlockSpec((CONV_WIDTH, 3 * DN_WIDTH), lambda bi, i: (0, 0)),
            pl.BlockSpec((None, tt, LANES), lambda bi, i: (bi, i, 0)),
            pl.BlockSpec((None, SROW, tt), lambda bi, i: (bi, 0, i)),
            pl.BlockSpec((None, tt, DN_WIDTH), lambda bi, i: (bi, i, 0)),
            pl.BlockSpec((1, DN_HEAD_DIM), lambda bi, i: (0, 0)),
            pl.BlockSpec((DN_HEADS, DN_HEAD_DIM, DN_HEAD_DIM), lambda bi, i: (0, 0, 0)),
        ],
        out_specs=[pl.BlockSpec((None, tt, DN_WIDTH), lambda bi, i: (bi, i, 0)),
                   pl.BlockSpec((DN_HEADS, DN_HEAD_DIM, DN_HEAD_DIM), lambda bi, i: (0, 0, 0))],
        out_shape=[jax.ShapeDtypeStruct((b, s, DN_WIDTH), BF16),
                   jax.ShapeDtypeStruct((DN_HEADS, DN_HEAD_DIM, DN_HEAD_DIM), F32)],
        scratch_shapes=[pltpu.VMEM((DN_HEADS, DN_HEAD_DIM, DN_HEAD_DIM), F32),
                        pltpu.VMEM((tt + SUBLANES, 3 * DN_WIDTH), F32)],
        compiler_params=pltpu.CompilerParams(dimension_semantics=("arbitrary", "arbitrary"),
                                             vmem_limit_bytes=VMEM_LIMIT),
        name="deltanet",
    )(dqkv, dqkv, halo0, convw, scol, srow, dzs, onw, s0)


def _tail_kernel(x_ref, ofox_ref, odn_ref, gates_ref, wbf_ref, wbd_ref, wout_ref, fnw_ref,
                 wg_ref, wu_ref, wd_ref, finw_ref, o_ref, *, ff_chunk):
    a = _dot(ofox_ref[...], wbf_ref[...])
    bb = _dot(odn_ref[...], wbd_ref[...])
    y = gates_ref[:, :D_MODEL] * a + gates_ref[:, D_MODEL:] * bb
    h1 = x_ref[...] + _dot(y.astype(BF16), wout_ref[...])
    n = (h1 * lax.rsqrt(jnp.mean(h1 * h1, axis=-1, keepdims=True) + EPS) * fnw_ref[...]).astype(BF16)
    acc = h1
    for c in range(D_FF // ff_chunk):
        lo, hi = c * ff_chunk, (c + 1) * ff_chunk
        gt = _dot(n, wg_ref[:, lo:hi])
        up = _dot(n, wu_ref[:, lo:hi])
        act = (gt * (1.0 / (1.0 + jnp.exp(-gt))) * up).astype(BF16)
        acc = acc + _dot(act, wd_ref[lo:hi, :])
    o_ref[...] = acc * lax.rsqrt(jnp.mean(acc * acc, axis=-1, keepdims=True) + EPS) * finw_ref[...]


def _tail(x2d, ofox, odn, gates, wbf, wbd, wout, fnw, wg, wu, wd, finw, *, tm, ff_chunk):
    m = x2d.shape[0]
    row = lambda w: pl.BlockSpec((tm, w), lambda i: (i, 0))
    return pl.pallas_call(
        functools.partial(_tail_kernel, ff_chunk=ff_chunk),
        grid=(m // tm,),
        in_specs=[row(D_MODEL), row(FOX_WIDTH), row(DN_WIDTH), row(2 * D_MODEL),
                  _const_spec((FOX_WIDTH, D_MODEL)), _const_spec((DN_WIDTH, D_MODEL)),
                  _const_spec((D_MODEL, D_MODEL)), _const_spec((1, D_MODEL)),
                  _const_spec((D_MODEL, D_FF)), _const_spec((D_MODEL, D_FF)),
                  _const_spec((D_FF, D_MODEL)), _const_spec((1, D_MODEL))],
        out_specs=row(D_MODEL),
        out_shape=jax.ShapeDtypeStruct((m, D_MODEL), F32),
        compiler_params=pltpu.CompilerParams(dimension_semantics=("arbitrary",),
                                             vmem_limit_bytes=VMEM_LIMIT),
        name="merge_ffn",
    )(x2d, ofox, odn, gates, wbf, wbd, wout, fnw, wg, wu, wd, finw)


def _pick_tile(n, pref):
    t = min(pref, n)
    while n % t:
        t //= 2
    return t


def kernel(x, meta_tokens, mix_norm_w, w_in, fox_forget_bias, dn_conv_w, dn_a_log, dn_dt_bias,
           dn_out_norm_w, w_branch_fox, w_branch_dn, w_out, ffn_norm_w, w_ffn_gate, w_ffn_up,
           w_ffn_down, final_norm_w):
    b, s, _ = x.shape
    assert mix_norm_w.shape[0] == 1, "single layer only"
    assert s % PREFIX == 0
    m = b * s

    wi = w_in[0]
    o_small0 = 3 * FOX_WIDTH
    o_dn = o_small0 + FOX_HEADS
    o_small1 = o_dn + 3 * DN_WIDTH
    o_rest = o_small1 + 2 * DN_HEADS
    wqv_t = jnp.concatenate([wi[:, :FOX_WIDTH], wi[:, 2 * FOX_WIDTH:o_small0]], axis=1).T.astype(BF16)
    wmain = jnp.concatenate([wi[:, FOX_WIDTH:2 * FOX_WIDTH], wi[:, o_dn:o_small1], wi[:, o_rest:]],
                            axis=1).astype(BF16)
    w_alogit = wi[:, o_small1 + DN_HEADS:o_rest]
    wsmall = jnp.concatenate([wi[:, o_small0:o_dn], wi[:, o_small1:o_small1 + DN_HEADS],
                              w_alogit, w_alogit, w_alogit], axis=1)
    wsmall = jnp.pad(wsmall, ((0, 0), (0, LANES - N_SMALL))).astype(BF16)
    bias = jnp.zeros((SUBLANES, LANES), F32)
    bias = bias.at[0, LOGF_LANE:LOGF_LANE + FOX_HEADS].set(fox_forget_bias[0].astype(F32))
    bias = bias.at[0, G_LANE:N_SMALL].set(jnp.tile(dn_dt_bias[0].astype(F32), 3))
    bias = bias.at[1, G_LANE:N_SMALL].set(jnp.tile(dn_a_log[0].astype(F32), 3))
    nw = mix_norm_w[0].reshape(1, D_MODEL).astype(F32)

    x_p = jnp.concatenate([jnp.zeros((N_PAD, D_MODEL), F32), meta_tokens.astype(F32)], axis=0)
    _, k_p, v_tp, kbias_p, dqkv_p, dzs_p, _, scol_p, srow_p = _inproj(
        x_p, nw, wqv_t, wmain, wsmall, bias, jnp.zeros((1, LANES), F32),
        tm=PREFIX, tiles_per_batch=1, n_pad=N_PAD)

    tm = _pick_tile(s, 512)
    q_t, k, v_t, kbias, dqkv, dzs, gates, scol, srow3 = _inproj(
        x.reshape(m, D_MODEL), nw, wqv_t, wmain, wsmall, bias, scol_p[PREFIX - 1:PREFIX, :],
        tm=tm, tiles_per_batch=s // tm, n_pad=0)

    scol3 = scol.reshape(b, s, LANES)
    tq = _pick_tile(s, 512)
    ofox = _fox(q_t, k.reshape(b, s, FOX_WIDTH), v_t, kbias.reshape(b, s, LANES), k_p, v_tp, kbias_p,
                tq=tq, tk=tq // 2)

    convw = dn_conv_w[0].astype(F32)
    onw = dn_out_norm_w[0].reshape(1, DN_HEAD_DIM).astype(F32)
    _, s_prefix = _deltanet(
        dqkv_p[None], jnp.zeros((SUBLANES, 3 * DN_WIDTH), F32), convw, scol_p[None], srow_p,
        dzs_p[None], onw, jnp.zeros((DN_HEADS, DN_HEAD_DIM, DN_HEAD_DIM), F32),
        tt=PREFIX, n_pad=N_PAD)
    odn, _ = _deltanet(
        dqkv.reshape(b, s, 3 * DN_WIDTH), dqkv_p[PREFIX - SUBLANES:], convw, scol3, srow3,
        dzs.reshape(b, s, DN_WIDTH), onw, s_prefix, tt=_pick_tile(s, 512), n_pad=0)

    out = _tail(
        x.reshape(m, D_MODEL), ofox.reshape(m, FOX_WIDTH), odn.reshape(m, DN_WIDTH), gates,
        w_branch_fox[0].astype(BF16), w_branch_dn[0].astype(BF16), w_out[0].astype(BF16),
        ffn_norm_w[0].reshape(1, D_MODEL).astype(F32),
        w_ffn_gate[0].astype(BF16), w_ffn_up[0].astype(BF16), w_ffn_down[0].astype(BF16),
        final_norm_w.reshape(1, D_MODEL).astype(F32),
        tm=_pick_tile(m, 512), ff_chunk=D_FF // 2)
    return out.reshape(b, s, D_MODEL)
```

```python
import functools

import jax
import jax.numpy as jnp
from jax import lax
from jax.experimental import pallas as pl
from jax.experimental.pallas import tpu as pltpu

F32 = jnp.float32
BF16 = jnp.bfloat16
HIGHEST = lax.Precision.HIGHEST

D_MODEL = 1024
N_META = 16
PREFIX = 128
N_PAD = PREFIX - N_META
FOX_HEADS = 8
FOX_HEAD_DIM = 64
FOX_WIDTH = FOX_HEADS * FOX_HEAD_DIM
V_ROWS = FOX_HEAD_DIM + 8
DN_HEADS = 4
DN_HEAD_DIM = 128
DN_WIDTH = DN_HEADS * DN_HEAD_DIM
DN_CHUNK = 64
DN_BLOCK = 2 * DN_CHUNK
CONV_WIDTH = 4
D_FF = 2816
EPS = 1e-6
NEG_INF = -1e30
LOG2E = 1.4426950408889634

LANES = 128
SUBLANES = 8
N_MAIN = FOX_WIDTH + 3 * DN_WIDTH + DN_WIDTH + 2 * D_MODEL
COL_CHUNK = 512
LOGF_LANE = 0
BETA_LANE = FOX_HEADS
G_LANE = BETA_LANE + DN_HEADS
GSUF_LANE = G_LANE + DN_HEADS
GRAW_LANE = GSUF_LANE + DN_HEADS
N_SMALL = GRAW_LANE + DN_HEADS
SROW = 24
VMEM_LIMIT = 56 * 1024 * 1024


def _const_spec(shape):
    nd = len(shape)
    return pl.BlockSpec(shape, lambda *_: (0,) * nd, pipeline_mode=pl.Buffered(1))


def _dot(a, b, **kw):
    return jnp.dot(a, b, preferred_element_type=F32, **kw)


def _dot_nt(a, b, **kw):
    return lax.dot_general(a, b, (((1,), (1,)), ((), ())), preferred_element_type=F32, **kw)


def _dot_tn(a, b, **kw):
    return lax.dot_general(a, b, (((0,), (0,)), ((), ())), preferred_element_type=F32, **kw)


def _inproj_kernel(x_ref, nw_ref, wqv_t_ref, wmain_ref, wsmall_ref, bias_ref, carry0_ref,
                   q_t_ref, k_ref, v_t_ref, kbias_ref, dqkv_ref, dzs_ref, gates_ref, scol_ref,
                   srow_ref, carry_ref, *, tm, tiles_per_batch, n_pad):
    i = pl.program_id(0)

    @pl.when(i % tiles_per_batch == 0)
    def _():
        carry_ref[...] = carry0_ref[...]

    x = x_ref[...]
    ms = jnp.mean(x * x, axis=-1, keepdims=True)
    hn = (x * lax.rsqrt(ms + EPS) * nw_ref[...]).astype(BF16)

    z = _dot(hn, wsmall_ref[...]) + bias_ref[0:1, :]
    lane = lax.broadcasted_iota(jnp.int32, (tm, LANES), 1)
    row = lax.broadcasted_iota(jnp.int32, (tm, LANES), 0)
    e = jnp.exp(-jnp.abs(z))
    l1p = jnp.log1p(e)
    logf = jnp.minimum(z, 0.0) - l1p
    softplus = jnp.maximum(z, 0.0) + l1p
    sig = jnp.where(z >= 0.0, 1.0, e) / (1.0 + e)
    g = -jnp.exp(bias_ref[1:2, :]) * softplus
    if n_pad:
        vm = (row + (i % tiles_per_batch) * tm >= n_pad).astype(F32)
        sig = sig * vm
        g = g * vm
    val = jnp.where(lane < BETA_LANE, logf,
                    jnp.where(lane < G_LANE, sig, jnp.where(lane < N_SMALL, g, 0.0)))
    in_chunk = row % DN_CHUNK
    is_logf = lane < BETA_LANE
    is_gpre = jnp.logical_and(lane >= G_LANE, lane < GSUF_LANE)
    is_gsuf = jnp.logical_and(lane >= GSUF_LANE, lane < GRAW_LANE)
    scan = val
    sh = 1
    while sh < tm:
        take_up = jnp.logical_and(is_logf, row >= sh)
        if sh < DN_CHUNK:
            take_up = jnp.logical_or(take_up, jnp.logical_and(is_gpre, in_chunk >= sh))
            take_down = jnp.logical_and(is_gsuf, in_chunk + sh < DN_CHUNK)
            below = jnp.where(take_down, pltpu.roll(scan, tm - sh, axis=0), 0.0)
        else:
            below = 0.0
        scan = scan + jnp.where(take_up, pltpu.roll(scan, sh, axis=0), below)
        sh *= 2
    out = scan + jnp.where(is_logf, carry_ref[...], 0.0)
    carry_ref[...] = out[tm - 1:tm, :]
    scol_ref[...] = out
    srow_ref[...] = out.T[:SROW, :]
    nc = jnp.where(is_logf, out * -LOG2E, 0.0)
    hi = nc.astype(BF16).astype(F32)
    mid = (nc - hi).astype(BF16).astype(F32)
    lo = nc - hi - mid
    kbias = hi + pltpu.roll(mid, FOX_HEADS, axis=1) + pltpu.roll(lo, 2 * FOX_HEADS, axis=1)
    kbias_ref[...] = kbias.astype(BF16)

    q_t = _dot_nt(wqv_t_ref[:FOX_WIDTH, :], hn) * (FOX_HEAD_DIM ** -0.5 * LOG2E)
    q_t_ref[...] = q_t.astype(BF16)
    v_t = _dot_nt(wqv_t_ref[FOX_WIDTH:, :], hn).astype(BF16)
    ones = jnp.ones((V_ROWS - FOX_HEAD_DIM, tm), BF16)
    v_t_ref[...] = jnp.concatenate(
        [piece for h in range(FOX_HEADS)
         for piece in (v_t[h * FOX_HEAD_DIM:(h + 1) * FOX_HEAD_DIM], ones)], axis=0)

    for c in range(N_MAIN // COL_CHUNK):
        acc = _dot(hn, wmain_ref[:, c * COL_CHUNK:(c + 1) * COL_CHUNK])
        if c == 0:
            k_ref[...] = acc.astype(BF16)
        elif c < 4:
            dqkv_ref[:, (c - 1) * COL_CHUNK:c * COL_CHUNK] = acc
        elif c == 4:
            dzs_ref[...] = acc * (1.0 / (1.0 + jnp.exp(-acc)))
        else:
            gates_ref[:, (c - 5) * COL_CHUNK:(c - 4) * COL_CHUNK] = 1.0 / (1.0 + jnp.exp(-acc))


def _inproj(x2d, nw, wqv_t, wmain, wsmall, bias, carry0, *, tm, tiles_per_batch, n_pad):
    m = x2d.shape[0]
    row = lambda w: pl.BlockSpec((tm, w), lambda i: (i, 0))
    col = lambda h: pl.BlockSpec((h, tm), lambda i: (0, i))
    return pl.pallas_call(
        functools.partial(_inproj_kernel, tm=tm, tiles_per_batch=tiles_per_batch, n_pad=n_pad),
        grid=(m // tm,),
        in_specs=[row(D_MODEL), _const_spec((1, D_MODEL)), _const_spec((2 * FOX_WIDTH, D_MODEL)),
                  _const_spec((D_MODEL, N_MAIN)),
                  _const_spec((D_MODEL, LANES)), _const_spec((SUBLANES, LANES)),
                  _const_spec((1, LANES))],
        out_specs=[col(FOX_WIDTH), row(FOX_WIDTH), col(FOX_HEADS * V_ROWS), row(LANES),
                   row(3 * DN_WIDTH), row(DN_WIDTH), row(2 * D_MODEL),
                   row(LANES),
                   pl.BlockSpec((None, SROW, tm),
                                lambda i: (i // tiles_per_batch, 0, i % tiles_per_batch))],
        out_shape=[jax.ShapeDtypeStruct((FOX_WIDTH, m), BF16),
                   jax.ShapeDtypeStruct((m, FOX_WIDTH), BF16),
                   jax.ShapeDtypeStruct((FOX_HEADS * V_ROWS, m), BF16),
                   jax.ShapeDtypeStruct((m, LANES), BF16),
                   jax.ShapeDtypeStruct((m, 3 * DN_WIDTH), F32),
                   jax.ShapeDtypeStruct((m, DN_WIDTH), F32),
                   jax.ShapeDtypeStruct((m, 2 * D_MODEL), F32),
                   jax.ShapeDtypeStruct((m, LANES), F32),
                   jax.ShapeDtypeStruct((m // (tm * tiles_per_batch), SROW,
                                         tm * tiles_per_batch), F32)],
        scratch_shapes=[pltpu.VMEM((1, LANES), F32)],
        compiler_params=pltpu.CompilerParams(dimension_semantics=("arbitrary",),
                                             vmem_limit_bytes=VMEM_LIMIT),
        name="inproj",
    )(x2d, nw, wqv_t, wmain, wsmall, bias, carry0)


def _fox_kernel(q_t_ref, k_ref, v_t_ref, kb_ref, kp_ref, v_tp_ref, kbp_ref,
                o_ref, m_ref, l_ref, acc_ref, t_ref, *, tq, tk):
    p = pl.program_id(1)
    i = pl.program_id(2)
    sub = lax.broadcasted_iota(jnp.int32, (LANES, 1), 0)
    q_t = q_t_ref[...]
    zero = jnp.zeros_like(q_t)
    q_th = [jnp.where(sub < FOX_HEAD_DIM, q_t, zero), jnp.where(sub >= FOX_HEAD_DIM, q_t, zero)]
    sel = [jnp.where(jnp.logical_and(sub < 3 * FOX_HEADS, (sub % FOX_HEADS) == 2 * p + h),
                     1.0, 0.0).astype(BF16) * jnp.ones((1, tq), BF16) for h in range(2)]
    rhs = [jnp.concatenate([q_th[h], sel[h]], axis=0) for h in range(2)]

    def scores(kt, kbt):
        lhs = jnp.concatenate([kt, kbt], axis=1)
        return [_dot(lhs, rhs[h]) for h in range(2)]

    def scores_into(slot, off):
        ts = scores(k_ref[pl.ds(off, tk), :], kb_ref[pl.ds(off, tk), :])
        for h in range(2):
            t_ref[slot, h] = ts[h]

    def update(read_t, v_aug, mask, first):
        def masked_t(h):
            t = read_t(h)
            return t if mask is None else jnp.where(mask, t, NEG_INF)

        pvs, alphas = [], []
        for h in range(2):
            t_max = jnp.max(masked_t(h), axis=0, keepdims=True)
            if first:
                m_new = t_max
            else:
                m_prev = m_ref[h]
                m_new = jnp.maximum(m_prev, t_max)
            pm = jnp.exp2(masked_t(h) - m_new).astype(BF16)
            r = _dot(v_aug[h * V_ROWS:(h + 1) * V_ROWS, :], pm)
            pvs.append(r[:FOX_HEAD_DIM])
            psum = r[FOX_HEAD_DIM:FOX_HEAD_DIM + 1]
            if first:
                l_ref[h] = psum
            else:
                alpha = jnp.exp2(m_prev - m_new)
                l_ref[h] = alpha * l_ref[h] + psum
                alphas.append(jnp.broadcast_to(alpha, (FOX_HEAD_DIM, tq)))
            m_ref[h] = m_new
        pv = jnp.concatenate(pvs, axis=0)
        if first:
            acc_ref[...] = pv
        else:
            acc_ref[...] = jnp.concatenate(alphas, axis=0) * acc_ref[...] + pv

    ts_p = scores(kp_ref[...], kbp_ref[...])
    update(lambda h: ts_p[h], v_tp_ref[...], None, True)

    def v_at(off):
        return v_t_ref[:, pl.ds(off, tk)]

    def slot(n):
        return lambda h: t_ref[n, h]

    scores_into(0, 0)

    def body(jj, carry):
        off = pl.multiple_of(jj * tq, tq)
        scores_into(1, off + tk)
        update(slot(0), v_at(off), None, False)
        scores_into(0, off + tq)
        update(slot(1), v_at(off + tk), None, False)
        return carry

    lax.fori_loop(0, i, body, 0)

    off = pl.multiple_of(i * tq, tq)
    rr = lax.broadcasted_iota(jnp.int32, (tk, tq), 0)
    cc = lax.broadcasted_iota(jnp.int32, (tk, tq), 1)
    scores_into(1, off + tk)
    update(slot(0), v_at(off), rr <= cc, False)
    update(slot(1), v_at(off + tk), rr + tk <= cc, False)

    l_all = jnp.concatenate([jnp.broadcast_to(l_ref[h], (FOX_HEAD_DIM, tq)) for h in range(2)], axis=0)
    o_ref[...] = (acc_ref[...] / l_all).astype(BF16)


def _fox(q_t, k, v_t, kbias, k_p, v_tp, kbias_p, *, tq, tk):
    b, s, _ = k.shape
    npairs = FOX_HEADS // 2
    nq = s // tq
    return pl.pallas_call(
        functools.partial(_fox_kernel, tq=tq, tk=tk),
        grid=(b, npairs, nq),
        in_specs=[
            pl.BlockSpec((LANES, tq), lambda bi, p, i: (p, bi * nq + i)),
            pl.BlockSpec((None, s, LANES), lambda bi, p, i: (bi, 0, p)),
            pl.BlockSpec((2 * V_ROWS, s), lambda bi, p, i: (p, bi)),
            pl.BlockSpec((None, s, LANES), lambda bi, p, i: (bi, 0, 0)),
            pl.BlockSpec((N_META, LANES), lambda bi, p, i: (0, p)),
            pl.BlockSpec((2 * V_ROWS, N_META), lambda bi, p, i: (p, 0)),
            pl.BlockSpec((N_META, LANES), lambda bi, p, i: (0, 0)),
        ],
        out_specs=pl.BlockSpec((LANES, tq), lambda bi, p, i: (p, bi * nq + i)),
        out_shape=jax.ShapeDtypeStruct((FOX_WIDTH, b * s), BF16),
        scratch_shapes=[pltpu.VMEM((2, 1, tq), F32), pltpu.VMEM((2, 1, tq), F32),
                        pltpu.VMEM((LANES, tq), F32), pltpu.VMEM((2, 2, tk, tq), F32)],
        compiler_params=pltpu.CompilerParams(
            dimension_semantics=("arbitrary", "arbitrary", "arbitrary"),
            vmem_limit_bytes=VMEM_LIMIT),
        name="fox_attention",
    )(q_t, k, v_t, kbias, k_p, v_tp, kbias_p)


def _dn_kernel(dqkv_ref, halo_ref, halo0_ref, convw_ref, scol_ref, srow_ref, dzs_ref, onw_ref,
               s0_ref, odn_ref, sfin_ref, state_ref, conv_ref, *, tt, n_pad):
    i = pl.program_id(1)
    nchunk = tt // DN_CHUNK

    @pl.when(i == 0)
    def _():
        state_ref[...] = s0_ref[...]

    u = dqkv_ref[...]
    if n_pad:
        row = lax.broadcasted_iota(jnp.int32, (tt, 1), 0) + i * tt
        u = u * (row >= n_pad).astype(F32)
    conv_ref[0:SUBLANES, :] = jnp.where(i == 0, halo0_ref[...], halo_ref[...])
    conv_ref[SUBLANES:, :] = u

    bs = DN_BLOCK
    heads = range(DN_HEADS)
    units = [(sb, h) for sb in range(tt // bs) for h in heads]

    def conv_silu(sb, c0):
        r0 = SUBLANES + sb * bs
        a = convw_ref[CONV_WIDTH - 1:CONV_WIDTH, c0:c0 + LANES] * conv_ref[r0:r0 + bs, c0:c0 + LANES]
        for t in range(CONV_WIDTH - 1):
            start = r0 - (CONV_WIDTH - 1) + t
            a = a + convw_ref[t:t + 1, c0:c0 + LANES] * conv_ref[start:start + bs, c0:c0 + LANES]
        return a * (1.0 / (1.0 + jnp.exp(-a)))

    def small_col(sb, lane):
        return scol_ref[sb * bs:(sb + 1) * bs, lane:lane + 1]

    rr = lax.broadcasted_iota(jnp.int32, (bs, bs), 0)
    cc = lax.broadcasted_iota(jnp.int32, (bs, bs), 1)
    rc_xor = jnp.bitwise_xor(rr, cc)
    same = rc_xor < DN_CHUNK
    tril_m = jnp.logical_and(same, rr >= cc)
    strict_m = jnp.logical_and(same, rr > cc)
    eye = (rr == cc).astype(F32)

    qs = [conv_silu(sb, h * LANES) for sb, h in units]
    ks = [conv_silu(sb, DN_WIDTH + h * LANES) for sb, h in units]
    vs = [conv_silu(sb, 2 * DN_WIDTH + h * LANES) for sb, h in units]
    states = [state_ref[h] for h in heads]
    qs = [q * lax.rsqrt(jnp.sum(q * q, axis=-1, keepdims=True) + EPS) * (DN_HEAD_DIM ** -0.5)
          for q in qs]
    ks = [k * lax.rsqrt(jnp.sum(k * k, axis=-1, keepdims=True) + EPS) for k in ks]
    betas = [small_col(sb, BETA_LANE + h) for sb, h in units]
    gcs = [small_col(sb, G_LANE + h) for sb, h in units]
    gls = [gcs[n] + small_col(sb, GSUF_LANE + h) - small_col(sb, GRAW_LANE + h)
           for n, (sb, h) in enumerate(units)]
    decays = [jnp.exp(jnp.where(
        tril_m, gcs[n] - srow_ref[G_LANE + h:G_LANE + h + 1, sb * bs:(sb + 1) * bs], NEG_INF))
        for n, (sb, h) in enumerate(units)]
    nu = range(len(units))
    egcs = [jnp.exp(g) for g in gcs]
    kbs = [ks[n] * betas[n] for n in nu]
    ks_b = [k.astype(BF16) for k in ks]
    lmats = [jnp.where(strict_m, _dot_nt(kbs[n].astype(BF16), ks_b[n]) * decays[n], 0.0) for n in nu]
    attns = [(_dot_nt(qs[n].astype(BF16), ks_b[n]) * decays[n]).astype(BF16) for n in nu]
    ainvs = [eye - jnp.where(rc_xor == 1, lm, 0.0) for lm in lmats]
    for lvl in range(1, 6):
        blk = 2 ** lvl
        lvl_m = jnp.logical_and(rc_xor >= blk, rc_xor < 2 * blk)
        ainvs_b = [a.astype(BF16) for a in ainvs]
        mids = [_dot(jnp.where(lvl_m, lmats[n], 0.0).astype(BF16), ainvs_b[n]).astype(BF16)
                for n in nu]
        ainvs = [ainvs[n] - _dot(ainvs_b[n], mids[n]) for n in nu]
    sols = [_dot(ainvs[n].astype(BF16),
                 jnp.concatenate([vs[n] * betas[n], kbs[n] * egcs[n]], axis=1).astype(BF16))
            for n in nu]
    uus = [sol[:, :DN_HEAD_DIM] for sol in sols]
    wws = [sol[:, DN_HEAD_DIM:] for sol in sols]
    qds = [qs[n] * egcs[n] for n in nu]
    kds = [(ks[n] * jnp.exp(gls[n] - gcs[n])).astype(BF16) for n in nu]

    vnews = [[] for _ in nu]
    qss = [[] for _ in nu]
    for c in range(nchunk):
        sb, lc = divmod(c, bs // DN_CHUNK)
        lo, hi = lc * DN_CHUNK, (lc + 1) * DN_CHUNK
        un = [sb * DN_HEADS + h for h in heads]
        rs = [_dot(jnp.concatenate([wws[n][lo:hi], qds[n][lo:hi]], axis=0).astype(BF16),
                   states[h].astype(BF16)) for h, n in enumerate(un)]
        vns = [uus[n][lo:hi] - rs[h][:DN_CHUNK] for h, n in enumerate(un)]
        states = [jnp.exp(gls[n][lo:lo + 1, :]) * states[h]
                  + _dot_tn(kds[n][lo:hi], vns[h].astype(BF16)) for h, n in enumerate(un)]
        for h, n in enumerate(un):
            qss[n].append(rs[h][DN_CHUNK:])
            vnews[n].append(vns[h])
    for h in heads:
        state_ref[h] = states[h]
        sfin_ref[h] = states[h]
    for n, (sb, h) in enumerate(units):
        o = (jnp.concatenate(qss[n], axis=0)
             + _dot(attns[n], jnp.concatenate(vnews[n], axis=0).astype(BF16)))
        o = o * lax.rsqrt(jnp.mean(o * o, axis=-1, keepdims=True) + EPS) * onw_ref[...]
        o = o * dzs_ref[sb * bs:(sb + 1) * bs, h * LANES:(h + 1) * LANES]
        odn_ref[sb * bs:(sb + 1) * bs, h * LANES:(h + 1) * LANES] = o.astype(BF16)


def _deltanet(dqkv, halo0, convw, scol, srow, dzs, onw, s0, *, tt, n_pad):
    b, s, _ = dqkv.shape
    nt = s // tt
    hb = tt // SUBLANES
    return pl.pallas_call(
        functools.partial(_dn_kernel, tt=tt, n_pad=n_pad),
        grid=(b, nt),
        in_specs=[
            pl.BlockSpec((None, tt, 3 * DN_WIDTH), lambda bi, i: (bi, i, 0)),
            pl.BlockSpec((None, SUBLANES, 3 * DN_WIDTH),
                         lambda bi, i: (bi, jnp.maximum(i * hb - 1, 0), 0)),
            pl.BlockSpec((SUBLANES, 3 * DN_WIDTH), lambda bi, i: (0, 0)),
            pl.BlockSpec((CONV_WIDTH, 3 * DN_WIDTH), lambda bi, i: (0, 0)),
            pl.BlockSpec((None, tt, LANES), lambda bi, i: (bi, i, 0)),
            pl.BlockSpec((None, SROW, tt), lambda bi, i: (bi, 0, i)),
            pl.BlockSpec((None, tt, DN_WIDTH), lambda bi, i: (bi, i, 0)),
            pl.BlockSpec((1, DN_HEAD_DIM), lambda bi, i: (0, 0)),
            pl.BlockSpec((DN_HEADS, DN_HEAD_DIM, DN_HEAD_DIM), lambda bi, i: (0, 0, 0)),
        ],
        out_specs=[pl.BlockSpec((None, tt, DN_WIDTH), lambda bi, i: (bi, i, 0)),
                   pl.BlockSpec((DN_HEADS, DN_HEAD_DIM, DN_HEAD_DIM), lambda bi, i: (0, 0, 0))],
        out_shape=[jax.ShapeDtypeStruct((b, s, DN_WIDTH), BF16),
                   jax.ShapeDtypeStruct((DN_HEADS, DN_HEAD_DIM, DN_HEAD_DIM), F32)],
        scratch_shapes=[pltpu.VMEM((DN_HEADS, DN_HEAD_DIM, DN_HEAD_DIM), F32),
                        pltpu.VMEM((tt + SUBLANES, 3 * DN_WIDTH), F32)],
        compiler_params=pltpu.CompilerParams(dimension_semantics=("arbitrary", "arbitrary"),
                                             vmem_limit_bytes=VMEM_LIMIT),
        name="deltanet",
    )(dqkv, dqkv, halo0, convw, scol, srow, dzs, onw, s0)


def _tail_kernel(x_ref, ofox_ref, odn_ref, gates_ref, wbf_ref, wbd_ref, wout_ref, fnw_ref,
                 wg_ref, wu_ref, wd_ref, finw_ref, o_ref, *, ff_chunk):
    a = _dot_tn(ofox_ref[...], wbf_ref[...])
    bb = _dot(odn_ref[...], wbd_ref[...])
    y = gates_ref[:, :D_MODEL] * a + gates_ref[:, D_MODEL:] * bb
    h1 = x_ref[...] + _dot(y.astype(BF16), wout_ref[...])
    n = (h1 * lax.rsqrt(jnp.mean(h1 * h1, axis=-1, keepdims=True) + EPS) * fnw_ref[...]).astype(BF16)
    acc = h1
    for c in range(D_FF // ff_chunk):
        lo, hi = c * ff_chunk, (c + 1) * ff_chunk
        gt = _dot(n, wg_ref[:, lo:hi])
        up = _dot(n, wu_ref[:, lo:hi])
        act = (gt * (1.0 / (1.0 + jnp.exp(-gt))) * up).astype(BF16)
        acc = acc + _dot(act, wd_ref[lo:hi, :])
    o_ref[...] = acc * lax.rsqrt(jnp.mean(acc * acc, axis=-1, keepdims=True) + EPS) * finw_ref[...]


def _tail(x2d, ofox, odn, gates, wbf, wbd, wout, fnw, wg, wu, wd, finw, *, tm, ff_chunk):
    m = x2d.shape[0]
    row = lambda w: pl.BlockSpec((tm, w), lambda i: (i, 0))
    return pl.pallas_call(
        functools.partial(_tail_kernel, ff_chunk=ff_chunk),
        grid=(m // tm,),
        in_specs=[row(D_MODEL), pl.BlockSpec((FOX_WIDTH, tm), lambda i: (0, i)), row(DN_WIDTH),
                  row(2 * D_MODEL),
                  _const_spec((FOX_WIDTH, D_MODEL)), _const_spec((DN_WIDTH, D_MODEL)),
                  _const_spec((D_MODEL, D_MODEL)), _const_spec((1, D_MODEL)),
                  _const_spec((D_MODEL, D_FF)), _const_spec((D_MODEL, D_FF)),
                  _const_spec((D_FF, D_MODEL)), _const_spec((1, D_MODEL))],
        out_specs=row(D_MODEL),
        out_shape=jax.ShapeDtypeStruct((m, D_MODEL), F32),
        compiler_params=pltpu.CompilerParams(dimension_semantics=("arbitrary",),
                                             vmem_limit_bytes=VMEM_LIMIT),
        name="merge_ffn",
    )(x2d, ofox, odn, gates, wbf, wbd, wout, fnw, wg, wu, wd, finw)


def _pick_tile(n, pref):
    t = min(pref, n)
    while n % t:
        t //= 2
    return t


def kernel(x, meta_tokens, mix_norm_w, w_in, fox_forget_bias, dn_conv_w, dn_a_log, dn_dt_bias,
           dn_out_norm_w, w_branch_fox, w_branch_dn, w_out, ffn_norm_w, w_ffn_gate, w_ffn_up,
           w_ffn_down, final_norm_w):
    b, s, _ = x.shape
    assert mix_norm_w.shape[0] == 1, "single layer only"
    assert s % PREFIX == 0
    m = b * s

    wi = w_in[0]
    o_small0 = 3 * FOX_WIDTH
    o_dn = o_small0 + FOX_HEADS
    o_small1 = o_dn + 3 * DN_WIDTH
    o_rest = o_small1 + 2 * DN_HEADS
    wqv_t = jnp.concatenate([wi[:, :FOX_WIDTH], wi[:, 2 * FOX_WIDTH:o_small0]], axis=1).T.astype(BF16)
    wmain = jnp.concatenate([wi[:, FOX_WIDTH:2 * FOX_WIDTH], wi[:, o_dn:o_small1], wi[:, o_rest:]],
                            axis=1).astype(BF16)
    w_alogit = wi[:, o_small1 + DN_HEADS:o_rest]
    wsmall = jnp.concatenate([wi[:, o_small0:o_dn], wi[:, o_small1:o_small1 + DN_HEADS],
                              w_alogit, w_alogit, w_alogit], axis=1)
    wsmall = jnp.pad(wsmall, ((0, 0), (0, LANES - N_SMALL))).astype(BF16)
    bias = jnp.zeros((SUBLANES, LANES), F32)
    bias = bias.at[0, LOGF_LANE:LOGF_LANE + FOX_HEADS].set(fox_forget_bias[0].astype(F32))
    bias = bias.at[0, G_LANE:N_SMALL].set(jnp.tile(dn_dt_bias[0].astype(F32), 3))
    bias = bias.at[1, G_LANE:N_SMALL].set(jnp.tile(dn_a_log[0].astype(F32), 3))
    nw = mix_norm_w[0].reshape(1, D_MODEL).astype(F32)

    x_p = jnp.concatenate([jnp.zeros((N_PAD, D_MODEL), F32), meta_tokens.astype(F32)], axis=0)
    _, k_p, v_tp, kbias_p, dqkv_p, dzs_p, _, scol_p, srow_p = _inproj(
        x_p, nw, wqv_t, wmain, wsmall, bias, jnp.zeros((1, LANES), F32),
        tm=PREFIX, tiles_per_batch=1, n_pad=N_PAD)

    tm = _pick_tile(s, 512)
    q_t, k, v_t, kbias, dqkv, dzs, gates, scol, srow3 = _inproj(
        x.reshape(m, D_MODEL), nw, wqv_t, wmain, wsmall, bias, scol_p[PREFIX - 1:PREFIX, :],
        tm=tm, tiles_per_batch=s // tm, n_pad=0)

    scol3 = scol.reshape(b, s, LANES)
    tq = _pick_tile(s, 512)
    ofox_t = _fox(q_t, k.reshape(b, s, FOX_WIDTH), v_t, kbias.reshape(b, s, LANES),
                  k_p[N_PAD:], v_tp[:, N_PAD:], kbias_p[N_PAD:], tq=tq, tk=tq // 2)

    convw = dn_conv_w[0].astype(F32)
    onw = dn_out_norm_w[0].reshape(1, DN_HEAD_DIM).astype(F32)
    _, s_prefix = _deltanet(
        dqkv_p[None], jnp.zeros((SUBLANES, 3 * DN_WIDTH), F32), convw, scol_p[None], srow_p,
        dzs_p[None], onw, jnp.zeros((DN_HEADS, DN_HEAD_DIM, DN_HEAD_DIM), F32),
        tt=PREFIX, n_pad=N_PAD)
    odn, _ = _deltanet(
        dqkv.reshape(b, s, 3 * DN_WIDTH), dqkv_p[PREFIX - SUBLANES:], convw, scol3, srow3,
        dzs.reshape(b, s, DN_WIDTH), onw, s_prefix, tt=_pick_tile(s, 512), n_pad=0)

    out = _tail(
        x.reshape(m, D_MODEL), ofox_t, odn.reshape(m, DN_WIDTH), gates,
        w_branch_fox[0].astype(BF16), w_branch_dn[0].astype(BF16), w_out[0].astype(BF16),
        ffn_norm_w[0].reshape(1, D_MODEL).astype(F32),
        w_ffn_gate[0].astype(BF16), w_ffn_up[0].astype(BF16), w_ffn_down[0].astype(BF16),
        final_norm_w.reshape(1, D_MODEL).astype(F32),
        tm=_pick_tile(m, 512), ff_chunk=D_FF // 2)
    return out.reshape(b, s, D_MODEL)
```

```python
import functools

import jax
import jax.numpy as jnp
from jax import lax
from jax.experimental import pallas as pl
from jax.experimental.pallas import tpu as pltpu

F32 = jnp.float32
BF16 = jnp.bfloat16
HIGHEST = lax.Precision.HIGHEST

D_MODEL = 1024
N_META = 16
PREFIX = 128
N_PAD = PREFIX - N_META
FOX_HEADS = 8
FOX_HEAD_DIM = 64
FOX_WIDTH = FOX_HEADS * FOX_HEAD_DIM
V_ROWS = FOX_HEAD_DIM + 8
DN_HEADS = 4
DN_HEAD_DIM = 128
DN_WIDTH = DN_HEADS * DN_HEAD_DIM
DN_CHUNK = 64
DN_BLOCK = 2 * DN_CHUNK
CONV_WIDTH = 4
D_FF = 2816
EPS = 1e-6
NEG_INF = -1e30
LOG2E = 1.4426950408889634

LANES = 128
SUBLANES = 8
N_MAIN = FOX_WIDTH + 3 * DN_WIDTH + DN_WIDTH + 2 * D_MODEL
COL_CHUNK = 512
LOGF_LANE = 0
BETA_LANE = FOX_HEADS
G_LANE = BETA_LANE + DN_HEADS
GSUF_LANE = G_LANE + DN_HEADS
GRAW_LANE = GSUF_LANE + DN_HEADS
N_SMALL = GRAW_LANE + DN_HEADS
SROW = 24
VMEM_LIMIT = 56 * 1024 * 1024


def _const_spec(shape):
    nd = len(shape)
    return pl.BlockSpec(shape, lambda *_: (0,) * nd, pipeline_mode=pl.Buffered(1))


def _sigmoid(x):
    return 0.5 * jnp.tanh(0.5 * x) + 0.5


def _dot(a, b, **kw):
    return jnp.dot(a, b, preferred_element_type=F32, **kw)


def _dot_nt(a, b, **kw):
    return lax.dot_general(a, b, (((1,), (1,)), ((), ())), preferred_element_type=F32, **kw)


def _dot_tn(a, b, **kw):
    return lax.dot_general(a, b, (((0,), (0,)), ((), ())), preferred_element_type=F32, **kw)


def _inproj_kernel(x_ref, nw_ref, wqv_t_ref, wmain_ref, wsmall_ref, bias_ref, carry0_ref,
                   convw_ref, halo0_ref,
                   q_t_ref, k_ref, v_t_ref, kbias_ref, dqkv_ref, dzs_ref, gates_ref, scol_ref,
                   srow_ref, tail_ref, carry_ref, halo_ref, conv_ref, *, tm, tiles_per_batch, n_pad):
    i = pl.program_id(0)

    @pl.when(i % tiles_per_batch == 0)
    def _():
        carry_ref[...] = carry0_ref[...]
        halo_ref[...] = halo0_ref[...]

    x = x_ref[...]
    ms = jnp.mean(x * x, axis=-1, keepdims=True)
    hn = (x * lax.rsqrt(ms + EPS) * nw_ref[...]).astype(BF16)

    z = _dot(hn, wsmall_ref[...]) + bias_ref[0:1, :]
    lane = lax.broadcasted_iota(jnp.int32, (tm, LANES), 1)
    row = lax.broadcasted_iota(jnp.int32, (tm, LANES), 0)
    e = jnp.exp(-jnp.abs(z))
    l1p = jnp.log1p(e)
    logf = jnp.minimum(z, 0.0) - l1p
    softplus = jnp.maximum(z, 0.0) + l1p
    sig = jnp.where(z >= 0.0, 1.0, e) / (1.0 + e)
    g = -jnp.exp(bias_ref[1:2, :]) * softplus
    if n_pad:
        vm = (row + (i % tiles_per_batch) * tm >= n_pad).astype(F32)
        sig = sig * vm
        g = g * vm
    val = jnp.where(lane < BETA_LANE, logf,
                    jnp.where(lane < G_LANE, sig, jnp.where(lane < N_SMALL, g, 0.0)))
    in_chunk = row % DN_CHUNK
    is_logf = lane < BETA_LANE
    is_gpre = jnp.logical_and(lane >= G_LANE, lane < GSUF_LANE)
    is_gsuf = jnp.logical_and(lane >= GSUF_LANE, lane < GRAW_LANE)
    scan = val
    sh = 1
    while sh < tm:
        take_up = jnp.logical_and(is_logf, row >= sh)
        if sh < DN_CHUNK:
            take_up = jnp.logical_or(take_up, jnp.logical_and(is_gpre, in_chunk >= sh))
            take_down = jnp.logical_and(is_gsuf, in_chunk + sh < DN_CHUNK)
            below = jnp.where(take_down, pltpu.roll(scan, tm - sh, axis=0), 0.0)
        else:
            below = 0.0
        scan = scan + jnp.where(take_up, pltpu.roll(scan, sh, axis=0), below)
        sh *= 2
    out = scan + jnp.where(is_logf, carry_ref[...], 0.0)
    carry_ref[...] = out[tm - 1:tm, :]
    scol_ref[...] = out
    srow_ref[...] = out.T[:SROW, :]
    nc = jnp.where(is_logf, out * -LOG2E, 0.0)
    hi = nc.astype(BF16).astype(F32)
    mid = (nc - hi).astype(BF16).astype(F32)
    lo = nc - hi - mid
    kbias = hi + pltpu.roll(mid, FOX_HEADS, axis=1) + pltpu.roll(lo, 2 * FOX_HEADS, axis=1)
    kbias_ref[...] = kbias.astype(BF16)

    def main_chunk(c):
        return lambda: _dot(hn, wmain_ref[:, c * COL_CHUNK:(c + 1) * COL_CHUNK])

    def store_q_t(acc):
        q_t_ref[...] = (acc * (FOX_HEAD_DIM ** -0.5 * LOG2E)).astype(BF16)

    def store_v_t(acc):
        v_t = acc.astype(BF16)
        ones = jnp.ones((V_ROWS - FOX_HEAD_DIM, tm), BF16)
        v_t_ref[...] = jnp.concatenate(
            [piece for h in range(FOX_HEADS)
             for piece in (v_t[h * FOX_HEAD_DIM:(h + 1) * FOX_HEAD_DIM], ones)], axis=0)

    def store_k(acc):
        k_ref[...] = acc.astype(BF16)

    def store_dn(which):
        def epilogue(acc):
            cols = slice(which * COL_CHUNK, (which + 1) * COL_CHUNK)
            if n_pad:
                acc = acc * vm[:, :1]
            conv_ref[which, 0:SUBLANES, :] = halo_ref[:, cols]
            conv_ref[which, SUBLANES:, :] = acc
            halo_ref[:, cols] = acc[tm - SUBLANES:, :]
            tail_ref[:, cols] = acc[tm - SUBLANES:, :]
            a = convw_ref[CONV_WIDTH - 1:CONV_WIDTH, cols] * acc
            for t in range(CONV_WIDTH - 1):
                start = SUBLANES - (CONV_WIDTH - 1) + t
                a = a + convw_ref[t:t + 1, cols] * conv_ref[which, start:start + tm, :]
            a = a * _sigmoid(a)
            if which == 2:
                dqkv_ref[:, cols] = a
            else:
                scale = DN_HEAD_DIM ** -0.5 if which == 0 else 1.0
                for h in range(DN_HEADS):
                    ah = a[:, h * LANES:(h + 1) * LANES]
                    inv = lax.rsqrt(jnp.sum(ah * ah, axis=-1, keepdims=True) + EPS) * scale
                    lo_col = which * COL_CHUNK + h * LANES
                    dqkv_ref[:, lo_col:lo_col + LANES] = ah * inv
        return epilogue

    def store_dzs(acc):
        dzs_ref[...] = acc * _sigmoid(acc)

    def store_gates(j):
        def epilogue(acc):
            gates_ref[:, j * COL_CHUNK:(j + 1) * COL_CHUNK] = _sigmoid(acc)
        return epilogue

    store_q_t(_dot_nt(wqv_t_ref[:FOX_WIDTH, :], hn))
    store_v_t(_dot_nt(wqv_t_ref[FOX_WIDTH:, :], hn))
    store_k(main_chunk(0)())
    for which in range(3):
        store_dn(which)(main_chunk(1 + which)())
    store_dzs(main_chunk(4)())
    for j in range(2 * D_MODEL // COL_CHUNK):
        store_gates(j)(main_chunk(5 + j)())


def _inproj(x2d, nw, wqv_t, wmain, wsmall, bias, carry0, convw, halo0, *, tm, tiles_per_batch,
            n_pad):
    m = x2d.shape[0]
    row = lambda w: pl.BlockSpec((tm, w), lambda i: (i, 0))
    col = lambda h: pl.BlockSpec((h, tm), lambda i: (0, i))
    return pl.pallas_call(
        functools.partial(_inproj_kernel, tm=tm, tiles_per_batch=tiles_per_batch, n_pad=n_pad),
        grid=(m // tm,),
        in_specs=[row(D_MODEL), _const_spec((1, D_MODEL)), _const_spec((2 * FOX_WIDTH, D_MODEL)),
                  _const_spec((D_MODEL, N_MAIN)),
                  _const_spec((D_MODEL, LANES)), _const_spec((SUBLANES, LANES)),
                  _const_spec((1, LANES)), _const_spec((CONV_WIDTH, 3 * DN_WIDTH)),
                  _const_spec((SUBLANES, 3 * DN_WIDTH))],
        out_specs=[col(FOX_WIDTH), row(FOX_WIDTH), col(FOX_HEADS * V_ROWS), row(LANES),
                   row(3 * DN_WIDTH), row(DN_WIDTH), row(2 * D_MODEL),
                   row(LANES),
                   pl.BlockSpec((None, SROW, tm),
                                lambda i: (i // tiles_per_batch, 0, i % tiles_per_batch)),
                   pl.BlockSpec((SUBLANES, 3 * DN_WIDTH), lambda i: (0, 0))],
        out_shape=[jax.ShapeDtypeStruct((FOX_WIDTH, m), BF16),
                   jax.ShapeDtypeStruct((m, FOX_WIDTH), BF16),
                   jax.ShapeDtypeStruct((FOX_HEADS * V_ROWS, m), BF16),
                   jax.ShapeDtypeStruct((m, LANES), BF16),
                   jax.ShapeDtypeStruct((m, 3 * DN_WIDTH), F32),
                   jax.ShapeDtypeStruct((m, DN_WIDTH), F32),
                   jax.ShapeDtypeStruct((m, 2 * D_MODEL), F32),
                   jax.ShapeDtypeStruct((m, LANES), F32),
                   jax.ShapeDtypeStruct((m // (tm * tiles_per_batch), SROW,
                                         tm * tiles_per_batch), F32),
                   jax.ShapeDtypeStruct((SUBLANES, 3 * DN_WIDTH), F32)],
        scratch_shapes=[pltpu.VMEM((1, LANES), F32), pltpu.VMEM((SUBLANES, 3 * DN_WIDTH), F32),
                        pltpu.VMEM((3, tm + SUBLANES, COL_CHUNK), F32)],
        compiler_params=pltpu.CompilerParams(dimension_semantics=("arbitrary",),
                                             vmem_limit_bytes=VMEM_LIMIT),
        name="inproj",
    )(x2d, nw, wqv_t, wmain, wsmall, bias, carry0, convw, halo0)


def _fox_kernel(q_t_ref, k_ref, v_t_ref, kb_ref, kp_ref, v_tp_ref, kbp_ref,
                o_ref, m_ref, l_ref, acc_ref, t_ref, *, tq, tk):
    p = pl.program_id(1)
    i = pl.program_id(2)
    sub = lax.broadcasted_iota(jnp.int32, (LANES, 1), 0)
    q_t = q_t_ref[...]
    zero = jnp.zeros_like(q_t)
    q_th = [jnp.where(sub < FOX_HEAD_DIM, q_t, zero), jnp.where(sub >= FOX_HEAD_DIM, q_t, zero)]
    sel = [jnp.where(jnp.logical_and(sub < 3 * FOX_HEADS, (sub % FOX_HEADS) == 2 * p + h),
                     1.0, 0.0).astype(BF16) * jnp.ones((1, tq), BF16) for h in range(2)]
    rhs = [jnp.concatenate([q_th[h], sel[h]], axis=0) for h in range(2)]

    def scores(kt, kbt):
        lhs = jnp.concatenate([kt, kbt], axis=1)
        return [_dot(lhs, rhs[h]) for h in range(2)]

    def scores_into(slot, off):
        ts = scores(k_ref[pl.ds(off, tk), :], kb_ref[pl.ds(off, tk), :])
        for h in range(2):
            t_ref[slot, h] = ts[h]

    def update(read_t, v_aug, mask, first):
        def masked_t(h):
            t = read_t(h)
            return t if mask is None else jnp.where(mask, t, NEG_INF)

        pvs, alphas = [], []
        for h in range(2):
            t_max = jnp.max(masked_t(h), axis=0, keepdims=True)
            if first:
                m_new = t_max
            else:
                m_prev = m_ref[h]
                m_new = jnp.maximum(m_prev, t_max)
            pm = jnp.exp2(masked_t(h) - m_new).astype(BF16)
            r = _dot(v_aug[h * V_ROWS:(h + 1) * V_ROWS, :], pm)
            pvs.append(r[:FOX_HEAD_DIM])
            psum = r[FOX_HEAD_DIM:FOX_HEAD_DIM + 1]
            if first:
                l_ref[h] = psum
            else:
                alpha = jnp.exp2(m_prev - m_new)
                l_ref[h] = alpha * l_ref[h] + psum
                alphas.append(jnp.broadcast_to(alpha, (FOX_HEAD_DIM, tq)))
            m_ref[h] = m_new
        pv = jnp.concatenate(pvs, axis=0)
        if first:
            acc_ref[...] = pv
        else:
            acc_ref[...] = jnp.concatenate(alphas, axis=0) * acc_ref[...] + pv

    ts_p = scores(kp_ref[...], kbp_ref[...])
    update(lambda h: ts_p[h], v_tp_ref[...], None, True)

    def v_at(off):
        return v_t_ref[:, pl.ds(off, tk)]

    def slot(n):
        return lambda h: t_ref[n, h]

    pairs_per_q = tq // (2 * tk)
    scores_into(0, 0)

    def body(jj, carry):
        off = pl.multiple_of(jj * 2 * tk, 2 * tk)
        scores_into(1, off + tk)
        update(slot(0), v_at(off), None, False)
        scores_into(0, off + 2 * tk)
        update(slot(1), v_at(off + tk), None, False)
        return carry

    lax.fori_loop(0, i * pairs_per_q, body, 0)

    rr = lax.broadcasted_iota(jnp.int32, (tk, tq), 0)
    cc = lax.broadcasted_iota(jnp.int32, (tk, tq), 1)
    for dp in range(pairs_per_q):
        off = pl.multiple_of(i * tq + dp * 2 * tk, 2 * tk)
        scores_into(1, off + tk)
        update(slot(0), v_at(off), rr + dp * 2 * tk <= cc, False)
        if dp + 1 < pairs_per_q:
            scores_into(0, off + 2 * tk)
        update(slot(1), v_at(off + tk), rr + (dp * 2 + 1) * tk <= cc, False)

    l_all = jnp.concatenate([jnp.broadcast_to(l_ref[h], (FOX_HEAD_DIM, tq)) for h in range(2)], axis=0)
    o_ref[...] = (acc_ref[...] / l_all).astype(BF16)


def _fox(q_t, k, v_t, kbias, k_p, v_tp, kbias_p, *, tq, tk):
    b, s, _ = k.shape
    npairs = FOX_HEADS // 2
    nq = s // tq
    return pl.pallas_call(
        functools.partial(_fox_kernel, tq=tq, tk=tk),
        grid=(b, npairs, nq),
        in_specs=[
            pl.BlockSpec((LANES, tq), lambda bi, p, i: (p, bi * nq + i)),
            pl.BlockSpec((None, s, LANES), lambda bi, p, i: (bi, 0, p)),
            pl.BlockSpec((2 * V_ROWS, s), lambda bi, p, i: (p, bi)),
            pl.BlockSpec((None, s, LANES), lambda bi, p, i: (bi, 0, 0)),
            pl.BlockSpec((N_META, LANES), lambda bi, p, i: (0, p)),
            pl.BlockSpec((2 * V_ROWS, N_META), lambda bi, p, i: (p, 0)),
            pl.BlockSpec((N_META, LANES), lambda bi, p, i: (0, 0)),
        ],
        out_specs=pl.BlockSpec((LANES, tq), lambda bi, p, i: (p, bi * nq + i)),
        out_shape=jax.ShapeDtypeStruct((FOX_WIDTH, b * s), BF16),
        scratch_shapes=[pltpu.VMEM((2, 1, tq), F32), pltpu.VMEM((2, 1, tq), F32),
                        pltpu.VMEM((LANES, tq), F32), pltpu.VMEM((2, 2, tk, tq), F32)],
        compiler_params=pltpu.CompilerParams(
            dimension_semantics=("arbitrary", "arbitrary", "arbitrary"),
            vmem_limit_bytes=VMEM_LIMIT),
        name="fox_attention",
    )(q_t, k, v_t, kbias, k_p, v_tp, kbias_p)


def _dn_kernel(qkv_ref, scol_ref, srow_ref, dzs_ref, onw_ref,
               s0_ref, odn_ref, sfin_ref, state_ref, *, tt):
    i = pl.program_id(1)
    nchunk = tt // DN_CHUNK

    @pl.when(i == 0)
    def _():
        state_ref[...] = s0_ref[...]

    bs = DN_BLOCK
    heads = range(DN_HEADS)
    units = [(sb, h) for sb in range(tt // bs) for h in heads]

    def slab(sb, c0):
        return qkv_ref[sb * bs:(sb + 1) * bs, c0:c0 + LANES]

    def small_col(sb, lane):
        return scol_ref[sb * bs:(sb + 1) * bs, lane:lane + 1]

    rr = lax.broadcasted_iota(jnp.int32, (bs, bs), 0)
    cc = lax.broadcasted_iota(jnp.int32, (bs, bs), 1)
    rc_xor = jnp.bitwise_xor(rr, cc)
    same = rc_xor < DN_CHUNK
    tril_m = jnp.logical_and(same, rr >= cc)
    strict_m = jnp.logical_and(same, rr > cc)
    eye = (rr == cc).astype(F32)

    qs = [slab(sb, h * LANES) for sb, h in units]
    ks = [slab(sb, DN_WIDTH + h * LANES) for sb, h in units]
    vs = [slab(sb, 2 * DN_WIDTH + h * LANES) for sb, h in units]
    states = [state_ref[h] for h in heads]
    betas = [small_col(sb, BETA_LANE + h) for sb, h in units]
    gcs = [small_col(sb, G_LANE + h) for sb, h in units]
    gls = [gcs[n] + small_col(sb, GSUF_LANE + h) - small_col(sb, GRAW_LANE + h)
           for n, (sb, h) in enumerate(units)]
    decays = [jnp.exp(jnp.where(
        tril_m, gcs[n] - srow_ref[G_LANE + h:G_LANE + h + 1, sb * bs:(sb + 1) * bs], NEG_INF))
        for n, (sb, h) in enumerate(units)]
    nu = range(len(units))
    egcs = [jnp.exp(g) for g in gcs]
    kbs = [ks[n] * betas[n] for n in nu]
    ks_b = [k.astype(BF16) for k in ks]
    lmats = [jnp.where(strict_m, _dot_nt(kbs[n].astype(BF16), ks_b[n]) * decays[n], 0.0) for n in nu]
    attns = [(_dot_nt(qs[n].astype(BF16), ks_b[n]) * decays[n]).astype(BF16) for n in nu]
    ainvs = [eye - jnp.where(rc_xor == 1, lm, 0.0) for lm in lmats]
    for lvl in range(1, 6):
        blk = 2 ** lvl
        lvl_m = jnp.logical_and(rc_xor >= blk, rc_xor < 2 * blk)
        ainvs_b = [a.astype(BF16) for a in ainvs]
        mids = [_dot(jnp.where(lvl_m, lmats[n], 0.0).astype(BF16), ainvs_b[n]).astype(BF16)
                for n in nu]
        ainvs = [ainvs[n] - _dot(ainvs_b[n], mids[n]) for n in nu]
    sols = [_dot(ainvs[n].astype(BF16),
                 jnp.concatenate([vs[n] * betas[n], kbs[n] * egcs[n]], axis=1).astype(BF16))
            for n in nu]
    uus = [sol[:, :DN_HEAD_DIM] for sol in sols]
    wws = [sol[:, DN_HEAD_DIM:] for sol in sols]
    qds = [qs[n] * egcs[n] for n in nu]
    kds = [(ks[n] * jnp.exp(gls[n] - gcs[n])).astype(BF16) for n in nu]

    vnews = [[] for _ in nu]
    qss = [[] for _ in nu]
    for c in range(nchunk):
        sb, lc = divmod(c, bs // DN_CHUNK)
        lo, hi = lc * DN_CHUNK, (lc + 1) * DN_CHUNK
        un = [sb * DN_HEADS + h for h in heads]
        rs = [_dot(jnp.concatenate([wws[n][lo:hi], qds[n][lo:hi]], axis=0).astype(BF16),
                   states[h].astype(BF16)) for h, n in enumerate(un)]
        vns = [uus[n][lo:hi] - rs[h][:DN_CHUNK] for h, n in enumerate(un)]
        states = [jnp.exp(gls[n][lo:lo + 1, :]) * states[h]
                  + _dot_tn(kds[n][lo:hi], vns[h].astype(BF16)) for h, n in enumerate(un)]
        for h, n in enumerate(un):
            qss[n].append(rs[h][DN_CHUNK:])
            vnews[n].append(vns[h])
    for h in heads:
        state_ref[h] = states[h]
        sfin_ref[h] = states[h]
    for n, (sb, h) in enumerate(units):
        o = (jnp.concatenate(qss[n], axis=0)
             + _dot(attns[n], jnp.concatenate(vnews[n], axis=0).astype(BF16)))
        o = o * lax.rsqrt(jnp.mean(o * o, axis=-1, keepdims=True) + EPS) * onw_ref[...]
        o = o * dzs_ref[sb * bs:(sb + 1) * bs, h * LANES:(h + 1) * LANES]
        odn_ref[sb * bs:(sb + 1) * bs, h * LANES:(h + 1) * LANES] = o.astype(BF16)


def _deltanet(qkv, scol, srow, dzs, onw, s0, *, tt):
    b, s, _ = qkv.shape
    nt = s // tt
    return pl.pallas_call(
        functools.partial(_dn_kernel, tt=tt),
        grid=(b, nt),
        in_specs=[
            pl.BlockSpec((None, tt, 3 * DN_WIDTH), lambda bi, i: (bi, i, 0)),
            pl.BlockSpec((None, tt, LANES), lambda bi, i: (bi, i, 0)),
            pl.BlockSpec((None, SROW, tt), lambda bi, i: (bi, 0, i)),
            pl.BlockSpec((None, tt, DN_WIDTH), lambda bi, i: (bi, i, 0)),
            pl.BlockSpec((1, DN_HEAD_DIM), lambda bi, i: (0, 0)),
            pl.BlockSpec((DN_HEADS, DN_HEAD_DIM, DN_HEAD_DIM), lambda bi, i: (0, 0, 0)),
        ],
        out_specs=[pl.BlockSpec((None, tt, DN_WIDTH), lambda bi, i: (bi, i, 0)),
                   pl.BlockSpec((DN_HEADS, DN_HEAD_DIM, DN_HEAD_DIM), lambda bi, i: (0, 0, 0))],
        out_shape=[jax.ShapeDtypeStruct((b, s, DN_WIDTH), BF16),
                   jax.ShapeDtypeStruct((DN_HEADS, DN_HEAD_DIM, DN_HEAD_DIM), F32)],
        scratch_shapes=[pltpu.VMEM((DN_HEADS, DN_HEAD_DIM, DN_HEAD_DIM), F32)],
        compiler_params=pltpu.CompilerParams(dimension_semantics=("arbitrary", "arbitrary"),
                                             vmem_limit_bytes=VMEM_LIMIT),
        name="deltanet",
    )(qkv, scol, srow, dzs, onw, s0)


def _tail_kernel(x_ref, ofox_ref, odn_ref, gates_ref, wbf_ref, wbd_ref, wout_ref, fnw_ref,
                 wg_ref, wu_ref, wd_ref, finw_ref, o_ref, *, ff_chunk):
    a = _dot_tn(ofox_ref[...], wbf_ref[...])
    bb = _dot(odn_ref[...], wbd_ref[...])
    y = gates_ref[:, :D_MODEL] * a + gates_ref[:, D_MODEL:] * bb
    h1 = x_ref[...] + _dot(y.astype(BF16), wout_ref[...])
    n = (h1 * lax.rsqrt(jnp.mean(h1 * h1, axis=-1, keepdims=True) + EPS) * fnw_ref[...]).astype(BF16)
    acc = h1
    for c in range(D_FF // ff_chunk):
        lo, hi = c * ff_chunk, (c + 1) * ff_chunk
        gt = _dot(n, wg_ref[:, lo:hi])
        up = _dot(n, wu_ref[:, lo:hi])
        act = (gt * _sigmoid(gt) * up).astype(BF16)
        acc = acc + _dot(act, wd_ref[lo:hi, :])
    o_ref[...] = acc * lax.rsqrt(jnp.mean(acc * acc, axis=-1, keepdims=True) + EPS) * finw_ref[...]


def _tail(x2d, ofox, odn, gates, wbf, wbd, wout, fnw, wg, wu, wd, finw, *, tm, ff_chunk):
    m = x2d.shape[0]
    row = lambda w: pl.BlockSpec((tm, w), lambda i: (i, 0))
    return pl.pallas_call(
        functools.partial(_tail_kernel, ff_chunk=ff_chunk),
        grid=(m // tm,),
        in_specs=[row(D_MODEL), pl.BlockSpec((FOX_WIDTH, tm), lambda i: (0, i)), row(DN_WIDTH),
                  row(2 * D_MODEL),
                  _const_spec((FOX_WIDTH, D_MODEL)), _const_spec((DN_WIDTH, D_MODEL)),
                  _const_spec((D_MODEL, D_MODEL)), _const_spec((1, D_MODEL)),
                  _const_spec((D_MODEL, D_FF)), _const_spec((D_MODEL, D_FF)),
                  _const_spec((D_FF, D_MODEL)), _const_spec((1, D_MODEL))],
        out_specs=row(D_MODEL),
        out_shape=jax.ShapeDtypeStruct((m, D_MODEL), F32),
        compiler_params=pltpu.CompilerParams(dimension_semantics=("arbitrary",),
                                             vmem_limit_bytes=VMEM_LIMIT),
        name="merge_ffn",
    )(x2d, ofox, odn, gates, wbf, wbd, wout, fnw, wg, wu, wd, finw)


def _pick_tile(n, pref):
    t = min(pref, n)
    while n % t:
        t //= 2
    return t


def kernel(x, meta_tokens, mix_norm_w, w_in, fox_forget_bias, dn_conv_w, dn_a_log, dn_dt_bias,
           dn_out_norm_w, w_branch_fox, w_branch_dn, w_out, ffn_norm_w, w_ffn_gate, w_ffn_up,
           w_ffn_down, final_norm_w):
    b, s, _ = x.shape
    assert mix_norm_w.shape[0] == 1, "single layer only"
    assert s % PREFIX == 0
    m = b * s

    wi = w_in[0]
    o_small0 = 3 * FOX_WIDTH
    o_dn = o_small0 + FOX_HEADS
    o_small1 = o_dn + 3 * DN_WIDTH
    o_rest = o_small1 + 2 * DN_HEADS
    wqv_t = jnp.concatenate([wi[:, :FOX_WIDTH], wi[:, 2 * FOX_WIDTH:o_small0]], axis=1).T.astype(BF16)
    wmain = jnp.concatenate([wi[:, FOX_WIDTH:2 * FOX_WIDTH], wi[:, o_dn:o_small1], wi[:, o_rest:]],
                            axis=1).astype(BF16)
    w_alogit = wi[:, o_small1 + DN_HEADS:o_rest]
    wsmall = jnp.concatenate([wi[:, o_small0:o_dn], wi[:, o_small1:o_small1 + DN_HEADS],
                              w_alogit, w_alogit, w_alogit], axis=1)
    wsmall = jnp.pad(wsmall, ((0, 0), (0, LANES - N_SMALL))).astype(BF16)
    bias = jnp.zeros((SUBLANES, LANES), F32)
    bias = bias.at[0, LOGF_LANE:LOGF_LANE + FOX_HEADS].set(fox_forget_bias[0].astype(F32))
    bias = bias.at[0, G_LANE:N_SMALL].set(jnp.tile(dn_dt_bias[0].astype(F32), 3))
    bias = bias.at[1, G_LANE:N_SMALL].set(jnp.tile(dn_a_log[0].astype(F32), 3))
    nw = mix_norm_w[0].reshape(1, D_MODEL).astype(F32)

    x_p = jnp.concatenate([jnp.zeros((N_PAD, D_MODEL), F32), meta_tokens.astype(F32)], axis=0)
    convw = dn_conv_w[0].astype(F32)
    _, k_p, v_tp, kbias_p, qkv_p, dzs_p, _, scol_p, srow_p, conv_tail_p = _inproj(
        x_p, nw, wqv_t, wmain, wsmall, bias, jnp.zeros((1, LANES), F32), convw,
        jnp.zeros((SUBLANES, 3 * DN_WIDTH), F32), tm=PREFIX, tiles_per_batch=1, n_pad=N_PAD)

    tm = _pick_tile(s, 512)
    q_t, k, v_t, kbias, qkv, dzs, gates, scol, srow3, _ = _inproj(
        x.reshape(m, D_MODEL), nw, wqv_t, wmain, wsmall, bias, scol_p[PREFIX - 1:PREFIX, :], convw,
        conv_tail_p, tm=tm, tiles_per_batch=s // tm, n_pad=0)

    scol3 = scol.reshape(b, s, LANES)
    tq = _pick_tile(s, 512)
    ofox_t = _fox(q_t, k.reshape(b, s, FOX_WIDTH), v_t, kbias.reshape(b, s, LANES),
                  k_p[N_PAD:], v_tp[:, N_PAD:], kbias_p[N_PAD:], tq=tq, tk=min(256, tq // 2))

    onw = dn_out_norm_w[0].reshape(1, DN_HEAD_DIM).astype(F32)
    _, s_prefix = _deltanet(
        qkv_p[None], scol_p[None], srow_p, dzs_p[None], onw,
        jnp.zeros((DN_HEADS, DN_HEAD_DIM, DN_HEAD_DIM), F32), tt=PREFIX)
    odn, _ = _deltanet(
        qkv.reshape(b, s, 3 * DN_WIDTH), scol3, srow3, dzs.reshape(b, s, DN_WIDTH), onw, s_prefix,
        tt=_pick_tile(s, 512))

    out = _tail(
        x.reshape(m, D_MODEL), ofox_t, odn.reshape(m, DN_WIDTH), gates,
        w_branch_fox[0].astype(BF16), w_branch_dn[0].astype(BF16), w_out[0].astype(BF16),
        ffn_norm_w[0].reshape(1, D_MODEL).astype(F32),
        w_ffn_gate[0].astype(BF16), w_ffn_up[0].astype(BF16), w_ffn_down[0].astype(BF16),
        final_norm_w.reshape(1, D_MODEL).astype(F32),
        tm=_pick_tile(m, 512), ff_chunk=D_FF // 2)
    return out.reshape(b, s, D_MODEL)
```

```python
import functools

import jax
import jax.numpy as jnp
from jax import lax
from jax.experimental import pallas as pl
from jax.experimental.pallas import tpu as pltpu

F32 = jnp.float32
BF16 = jnp.bfloat16
HIGHEST = lax.Precision.HIGHEST

D_MODEL = 1024
N_META = 16
PREFIX = 128
N_PAD = PREFIX - N_META
FOX_HEADS = 8
FOX_HEAD_DIM = 64
FOX_WIDTH = FOX_HEADS * FOX_HEAD_DIM
V_ROWS = FOX_HEAD_DIM + 8
DN_HEADS = 4
DN_HEAD_DIM = 128
DN_WIDTH = DN_HEADS * DN_HEAD_DIM
DN_CHUNK = 64
DN_BLOCK = 2 * DN_CHUNK
DN_UNITS = 16
CONV_WIDTH = 4
D_FF = 2816
EPS = 1e-6
NEG_INF = -1e30
LOG2E = 1.4426950408889634

LANES = 128
SUBLANES = 8
N_MAIN = FOX_WIDTH + 3 * DN_WIDTH + DN_WIDTH + 2 * D_MODEL
COL_CHUNK = 512
LOGF_LANE = 0
BETA_LANE = FOX_HEADS
G_LANE = BETA_LANE + DN_HEADS
GSUF_LANE = G_LANE + DN_HEADS
GRAW_LANE = GSUF_LANE + DN_HEADS
N_SMALL = GRAW_LANE + DN_HEADS
SROW = 24
VMEM_LIMIT = 56 * 1024 * 1024


def _const_spec(shape):
    nd = len(shape)
    return pl.BlockSpec(shape, lambda *_: (0,) * nd, pipeline_mode=pl.Buffered(1))


def _sigmoid(x):
    return 0.5 * jnp.tanh(0.5 * x) + 0.5


def _dot(a, b, **kw):
    return jnp.dot(a, b, preferred_element_type=F32, **kw)


def _dot_nt(a, b, **kw):
    return lax.dot_general(a, b, (((1,), (1,)), ((), ())), preferred_element_type=F32, **kw)


def _dot_tn(a, b, **kw):
    return lax.dot_general(a, b, (((0,), (0,)), ((), ())), preferred_element_type=F32, **kw)


def _inproj_kernel(x_ref, nw_ref, wqv_t_ref, wmain_ref, wsmall_ref, bias_ref, carry0_ref,
                   convw_ref, halo0_ref,
                   q_t_ref, k_ref, v_t_ref, kbias_ref, dqkv_ref, dzs_ref, gates_ref, scol_ref,
                   srow_ref, tail_ref, carry_ref, halo_ref, conv_ref, *, tm, tiles_per_batch, n_pad):
    i = pl.program_id(0)

    @pl.when(i % tiles_per_batch == 0)
    def _():
        carry_ref[...] = carry0_ref[...]
        halo_ref[...] = halo0_ref[...]

    x = x_ref[...]
    ms = jnp.mean(x * x, axis=-1, keepdims=True)
    hn = (x * lax.rsqrt(ms + EPS) * nw_ref[...]).astype(BF16)

    z = _dot(hn, wsmall_ref[...]) + bias_ref[0:1, :]
    lane = lax.broadcasted_iota(jnp.int32, (tm, LANES), 1)
    row = lax.broadcasted_iota(jnp.int32, (tm, LANES), 0)
    e = jnp.exp(-jnp.abs(z))
    l1p = jnp.log1p(e)
    logf = jnp.minimum(z, 0.0) - l1p
    softplus = jnp.maximum(z, 0.0) + l1p
    sig = jnp.where(z >= 0.0, 1.0, e) / (1.0 + e)
    g = -jnp.exp(bias_ref[1:2, :]) * softplus
    if n_pad:
        vm = (row + (i % tiles_per_batch) * tm >= n_pad).astype(F32)
        sig = sig * vm
        g = g * vm
    val = jnp.where(lane < BETA_LANE, logf,
                    jnp.where(lane < G_LANE, sig, jnp.where(lane < N_SMALL, g, 0.0)))
    in_chunk = row % DN_CHUNK
    is_logf = lane < BETA_LANE
    is_gpre = jnp.logical_and(lane >= G_LANE, lane < GSUF_LANE)
    is_gsuf = jnp.logical_and(lane >= GSUF_LANE, lane < GRAW_LANE)
    scan = val
    sh = 1
    while sh < tm:
        take_up = jnp.logical_and(is_logf, row >= sh)
        if sh < DN_CHUNK:
            take_up = jnp.logical_or(take_up, jnp.logical_and(is_gpre, in_chunk >= sh))
            take_down = jnp.logical_and(is_gsuf, in_chunk + sh < DN_CHUNK)
            below = jnp.where(take_down, pltpu.roll(scan, tm - sh, axis=0), 0.0)
        else:
            below = 0.0
        scan = scan + jnp.where(take_up, pltpu.roll(scan, sh, axis=0), below)
        sh *= 2
    out = scan + jnp.where(is_logf, carry_ref[...], 0.0)
    carry_ref[...] = out[tm - 1:tm, :]
    scol_ref[...] = out
    srow_ref[...] = out.T[:SROW, :]
    nc = jnp.where(is_logf, out * -LOG2E, 0.0)
    hi = nc.astype(BF16).astype(F32)
    mid = (nc - hi).astype(BF16).astype(F32)
    lo = nc - hi - mid
    kbias = hi + pltpu.roll(mid, FOX_HEADS, axis=1) + pltpu.roll(lo, 2 * FOX_HEADS, axis=1)
    kbias_ref[...] = kbias.astype(BF16)

    def main_chunk(c):
        return lambda: _dot(hn, wmain_ref[:, c * COL_CHUNK:(c + 1) * COL_CHUNK])

    def store_q_t(acc):
        q_t_ref[...] = (acc * (FOX_HEAD_DIM ** -0.5 * LOG2E)).astype(BF16)

    def store_v_t(acc):
        v_t = acc.astype(BF16)
        ones = jnp.ones((V_ROWS - FOX_HEAD_DIM, tm), BF16)
        v_t_ref[...] = jnp.concatenate(
            [piece for h in range(FOX_HEADS)
             for piece in (v_t[h * FOX_HEAD_DIM:(h + 1) * FOX_HEAD_DIM], ones)], axis=0)

    def store_k(acc):
        k_ref[...] = acc.astype(BF16)

    def store_dn(which):
        def epilogue(acc):
            cols = slice(which * COL_CHUNK, (which + 1) * COL_CHUNK)
            if n_pad:
                acc = acc * vm[:, :1]
            conv_ref[which, 0:SUBLANES, :] = halo_ref[:, cols]
            conv_ref[which, SUBLANES:, :] = acc
            halo_ref[:, cols] = acc[tm - SUBLANES:, :]
            tail_ref[:, cols] = acc[tm - SUBLANES:, :]
            a = convw_ref[CONV_WIDTH - 1:CONV_WIDTH, cols] * acc
            for t in range(CONV_WIDTH - 1):
                start = SUBLANES - (CONV_WIDTH - 1) + t
                a = a + convw_ref[t:t + 1, cols] * conv_ref[which, start:start + tm, :]
            a = a * _sigmoid(a)
            if which == 2:
                dqkv_ref[:, cols] = a
            else:
                scale = DN_HEAD_DIM ** -0.5 if which == 0 else 1.0
                for h in range(DN_HEADS):
                    ah = a[:, h * LANES:(h + 1) * LANES]
                    inv = lax.rsqrt(jnp.sum(ah * ah, axis=-1, keepdims=True) + EPS) * scale
                    lo_col = which * COL_CHUNK + h * LANES
                    dqkv_ref[:, lo_col:lo_col + LANES] = ah * inv
        return epilogue

    def store_dzs(acc):
        dzs_ref[...] = acc * _sigmoid(acc)

    def store_gates(j):
        def epilogue(acc):
            gates_ref[:, j * COL_CHUNK:(j + 1) * COL_CHUNK] = _sigmoid(acc)
        return epilogue

    store_q_t(_dot_nt(wqv_t_ref[:FOX_WIDTH, :], hn))
    store_v_t(_dot_nt(wqv_t_ref[FOX_WIDTH:, :], hn))
    store_k(main_chunk(0)())
    for which in range(3):
        store_dn(which)(main_chunk(1 + which)())
    store_dzs(main_chunk(4)())
    for j in range(2 * D_MODEL // COL_CHUNK):
        store_gates(j)(main_chunk(5 + j)())


def _inproj(x2d, nw, wqv_t, wmain, wsmall, bias, carry0, convw, halo0, *, tm, tiles_per_batch,
            n_pad):
    m = x2d.shape[0]
    row = lambda w: pl.BlockSpec((tm, w), lambda i: (i, 0))
    col = lambda h: pl.BlockSpec((h, tm), lambda i: (0, i))
    return pl.pallas_call(
        functools.partial(_inproj_kernel, tm=tm, tiles_per_batch=tiles_per_batch, n_pad=n_pad),
        grid=(m // tm,),
        in_specs=[row(D_MODEL), _const_spec((1, D_MODEL)), _const_spec((2 * FOX_WIDTH, D_MODEL)),
                  _const_spec((D_MODEL, N_MAIN)),
                  _const_spec((D_MODEL, LANES)), _const_spec((SUBLANES, LANES)),
                  _const_spec((1, LANES)), _const_spec((CONV_WIDTH, 3 * DN_WIDTH)),
                  _const_spec((SUBLANES, 3 * DN_WIDTH))],
        out_specs=[col(FOX_WIDTH), row(FOX_WIDTH), col(FOX_HEADS * V_ROWS), row(LANES),
                   row(3 * DN_WIDTH), row(DN_WIDTH), row(2 * D_MODEL),
                   row(LANES),
                   pl.BlockSpec((None, SROW, tm),
                                lambda i: (i // tiles_per_batch, 0, i % tiles_per_batch)),
                   pl.BlockSpec((SUBLANES, 3 * DN_WIDTH), lambda i: (0, 0))],
        out_shape=[jax.ShapeDtypeStruct((FOX_WIDTH, m), BF16),
                   jax.ShapeDtypeStruct((m, FOX_WIDTH), BF16),
                   jax.ShapeDtypeStruct((FOX_HEADS * V_ROWS, m), BF16),
                   jax.ShapeDtypeStruct((m, LANES), BF16),
                   jax.ShapeDtypeStruct((m, 3 * DN_WIDTH), F32),
                   jax.ShapeDtypeStruct((m, DN_WIDTH), F32),
                   jax.ShapeDtypeStruct((m, 2 * D_MODEL), F32),
                   jax.ShapeDtypeStruct((m, LANES), F32),
                   jax.ShapeDtypeStruct((m // (tm * tiles_per_batch), SROW,
                                         tm * tiles_per_batch), F32),
                   jax.ShapeDtypeStruct((SUBLANES, 3 * DN_WIDTH), F32)],
        scratch_shapes=[pltpu.VMEM((1, LANES), F32), pltpu.VMEM((SUBLANES, 3 * DN_WIDTH), F32),
                        pltpu.VMEM((3, tm + SUBLANES, COL_CHUNK), F32)],
        compiler_params=pltpu.CompilerParams(dimension_semantics=("arbitrary",),
                                             vmem_limit_bytes=VMEM_LIMIT),
        name="inproj",
    )(x2d, nw, wqv_t, wmain, wsmall, bias, carry0, convw, halo0)


def _fox_kernel(q_t_ref, k_ref, v_t_ref, kb_ref, kp_ref, v_tp_ref, kbp_ref,
                o_ref, m_ref, l_ref, acc_ref, t_ref, *, tq, tk):
    p = pl.program_id(1)
    i = pl.program_id(2)
    sub = lax.broadcasted_iota(jnp.int32, (LANES, 1), 0)
    q_t = q_t_ref[...]
    zero = jnp.zeros_like(q_t)
    q_th = [jnp.where(sub < FOX_HEAD_DIM, q_t, zero), jnp.where(sub >= FOX_HEAD_DIM, q_t, zero)]
    sel = [jnp.where(jnp.logical_and(sub < 3 * FOX_HEADS, (sub % FOX_HEADS) == 2 * p + h),
                     1.0, 0.0).astype(BF16) * jnp.ones((1, tq), BF16) for h in range(2)]
    rhs = [jnp.concatenate([q_th[h], sel[h]], axis=0) for h in range(2)]

    def scores(kt, kbt):
        lhs = jnp.concatenate([kt, kbt], axis=1)
        return [_dot(lhs, rhs[h]) for h in range(2)]

    def scores_into(slot, off):
        ts = scores(k_ref[pl.ds(off, tk), :], kb_ref[pl.ds(off, tk), :])
        for h in range(2):
            t_ref[slot, h] = ts[h]

    def update(read_t, v_aug, mask, first):
        def masked_t(h):
            t = read_t(h)
            return t if mask is None else jnp.where(mask, t, NEG_INF)

        pvs, alphas = [], []
        for h in range(2):
            t_max = jnp.max(masked_t(h), axis=0, keepdims=True)
            if first:
                m_new = t_max
            else:
                m_prev = m_ref[h]
                m_new = jnp.maximum(m_prev, t_max)
            pm = jnp.exp2(masked_t(h) - m_new).astype(BF16)
            r = _dot(v_aug[h * V_ROWS:(h + 1) * V_ROWS, :], pm)
            pvs.append(r[:FOX_HEAD_DIM])
            psum = r[FOX_HEAD_DIM:FOX_HEAD_DIM + 1]
            if first:
                l_ref[h] = psum
            else:
                alpha = jnp.exp2(m_prev - m_new)
                l_ref[h] = alpha * l_ref[h] + psum
                alphas.append(jnp.broadcast_to(alpha, (FOX_HEAD_DIM, tq)))
            m_ref[h] = m_new
        pv = jnp.concatenate(pvs, axis=0)
        if first:
            acc_ref[...] = pv
        else:
            acc_ref[...] = jnp.concatenate(alphas, axis=0) * acc_ref[...] + pv

    ts_p = scores(kp_ref[...], kbp_ref[...])
    update(lambda h: ts_p[h], v_tp_ref[...], None, True)

    def v_at(off):
        return v_t_ref[:, pl.ds(off, tk)]

    def slot(n):
        return lambda h: t_ref[n, h]

    pairs_per_q = tq // (2 * tk)
    scores_into(0, 0)

    def body(jj, carry):
        off = pl.multiple_of(jj * 2 * tk, 2 * tk)
        scores_into(1, off + tk)
        update(slot(0), v_at(off), None, False)
        scores_into(0, off + 2 * tk)
        update(slot(1), v_at(off + tk), None, False)
        return carry

    lax.fori_loop(0, i * pairs_per_q, body, 0)

    rr = lax.broadcasted_iota(jnp.int32, (tk, tq), 0)
    cc = lax.broadcasted_iota(jnp.int32, (tk, tq), 1)
    for dp in range(pairs_per_q):
        off = pl.multiple_of(i * tq + dp * 2 * tk, 2 * tk)
        scores_into(1, off + tk)
        update(slot(0), v_at(off), rr + dp * 2 * tk <= cc, False)
        if dp + 1 < pairs_per_q:
            scores_into(0, off + 2 * tk)
        update(slot(1), v_at(off + tk), rr + (dp * 2 + 1) * tk <= cc, False)

    l_all = jnp.concatenate([jnp.broadcast_to(l_ref[h], (FOX_HEAD_DIM, tq)) for h in range(2)], axis=0)
    o_ref[...] = (acc_ref[...] / l_all).astype(BF16)


def _fox(q_t, k, v_t, kbias, k_p, v_tp, kbias_p, *, tq, tk):
    b, s, _ = k.shape
    npairs = FOX_HEADS // 2
    nq = s // tq
    return pl.pallas_call(
        functools.partial(_fox_kernel, tq=tq, tk=tk),
        grid=(b, npairs, nq),
        in_specs=[
            pl.BlockSpec((LANES, tq), lambda bi, p, i: (p, bi * nq + i)),
            pl.BlockSpec((None, s, LANES), lambda bi, p, i: (bi, 0, p)),
            pl.BlockSpec((2 * V_ROWS, s), lambda bi, p, i: (p, bi)),
            pl.BlockSpec((None, s, LANES), lambda bi, p, i: (bi, 0, 0)),
            pl.BlockSpec((N_META, LANES), lambda bi, p, i: (0, p)),
            pl.BlockSpec((2 * V_ROWS, N_META), lambda bi, p, i: (p, 0)),
            pl.BlockSpec((N_META, LANES), lambda bi, p, i: (0, 0)),
        ],
        out_specs=pl.BlockSpec((LANES, tq), lambda bi, p, i: (p, bi * nq + i)),
        out_shape=jax.ShapeDtypeStruct((FOX_WIDTH, b * s), BF16),
        scratch_shapes=[pltpu.VMEM((2, 1, tq), F32), pltpu.VMEM((2, 1, tq), F32),
                        pltpu.VMEM((LANES, tq), F32), pltpu.VMEM((2, 2, tk, tq), F32)],
        compiler_params=pltpu.CompilerParams(
            dimension_semantics=("arbitrary", "arbitrary", "arbitrary"),
            vmem_limit_bytes=VMEM_LIMIT),
        name="fox_attention",
    )(q_t, k, v_t, kbias, k_p, v_tp, kbias_p)


def _dn_kernel(qkv_ref, scol_ref, srow_ref, dzs_ref, onw_ref,
               s0_ref, odn_ref, sfin_ref, state_ref, *, tt):
    i = pl.program_id(1)
    nchunk = tt // DN_CHUNK

    @pl.when(i == 0)
    def _():
        for g in range(qkv_ref.shape[0]):
            state_ref[g] = s0_ref[...]

    bs = DN_BLOCK
    heads = range(DN_HEADS)
    groups = range(qkv_ref.shape[0])
    units = [(g, sb, h) for g in groups for sb in range(tt // bs) for h in heads]

    def slab(g, sb, c0):
        return qkv_ref[g, sb * bs:(sb + 1) * bs, c0:c0 + LANES]

    def small_col(g, sb, lane):
        return scol_ref[g, sb * bs:(sb + 1) * bs, lane:lane + 1]

    rr = lax.broadcasted_iota(jnp.int32, (bs, bs), 0)
    cc = lax.broadcasted_iota(jnp.int32, (bs, bs), 1)
    rc_xor = jnp.bitwise_xor(rr, cc)
    same = rc_xor < DN_CHUNK
    tril_m = jnp.logical_and(same, rr >= cc)
    strict_m = jnp.logical_and(same, rr > cc)
    eye = (rr == cc).astype(F32)

    qs = [slab(g, sb, h * LANES) for g, sb, h in units]
    ks = [slab(g, sb, DN_WIDTH + h * LANES) for g, sb, h in units]
    vs = [slab(g, sb, 2 * DN_WIDTH + h * LANES) for g, sb, h in units]
    states = {(g, h): state_ref[g, h] for g in groups for h in heads}
    betas = [small_col(g, sb, BETA_LANE + h) for g, sb, h in units]
    gcs = [small_col(g, sb, G_LANE + h) for g, sb, h in units]
    gls = [gcs[n] + small_col(g, sb, GSUF_LANE + h) - small_col(g, sb, GRAW_LANE + h)
           for n, (g, sb, h) in enumerate(units)]
    decays = [jnp.exp(jnp.where(
        tril_m, gcs[n] - srow_ref[g, G_LANE + h:G_LANE + h + 1, sb * bs:(sb + 1) * bs], NEG_INF))
        for n, (g, sb, h) in enumerate(units)]
    nu = range(len(units))
    egcs = [jnp.exp(g) for g in gcs]
    kbs = [ks[n] * betas[n] for n in nu]
    ks_b = [k.astype(BF16) for k in ks]
    lmats = [jnp.where(strict_m, _dot_nt(kbs[n].astype(BF16), ks_b[n]) * decays[n], 0.0) for n in nu]
    attns = [(_dot_nt(qs[n].astype(BF16), ks_b[n]) * decays[n]).astype(BF16) for n in nu]
    ainvs = [eye - jnp.where(rc_xor == 1, lm, 0.0) for lm in lmats]
    for lvl in range(1, 6):
        blk = 2 ** lvl
        lvl_m = jnp.logical_and(rc_xor >= blk, rc_xor < 2 * blk)
        ainvs_b = [a.astype(BF16) for a in ainvs]
        mids = [_dot(jnp.where(lvl_m, lmats[n], 0.0).astype(BF16), ainvs_b[n]).astype(BF16)
                for n in nu]
        ainvs = [ainvs[n] - _dot(ainvs_b[n], mids[n]) for n in nu]
    sols = [_dot(ainvs[n].astype(BF16),
                 jnp.concatenate([vs[n] * betas[n], kbs[n] * egcs[n]], axis=1).astype(BF16))
            for n in nu]
    uus = [sol[:, :DN_HEAD_DIM] for sol in sols]
    wws = [sol[:, DN_HEAD_DIM:] for sol in sols]
    qds = [qs[n] * egcs[n] for n in nu]
    kds = [(ks[n] * jnp.exp(gls[n] - gcs[n])).astype(BF16) for n in nu]

    vnews = [[] for _ in nu]
    qss = [[] for _ in nu]
    for c in range(nchunk):
        csb, lc = divmod(c, bs // DN_CHUNK)
        lo, hi = lc * DN_CHUNK, (lc + 1) * DN_CHUNK
        cur = [(n, (g, h)) for n, (g, sb, h) in enumerate(units) if sb == csb]
        rs = {n: _dot(jnp.concatenate([wws[n][lo:hi], qds[n][lo:hi]], axis=0).astype(BF16),
                      states[key].astype(BF16)) for n, key in cur}
        vns = {n: uus[n][lo:hi] - rs[n][:DN_CHUNK] for n, _ in cur}
        states.update({key: jnp.exp(gls[n][lo:lo + 1, :]) * states[key]
                       + _dot_tn(kds[n][lo:hi], vns[n].astype(BF16)) for n, key in cur})
        for n, _ in cur:
            qss[n].append(rs[n][DN_CHUNK:])
            vnews[n].append(vns[n])
    for (g, h), state in states.items():
        state_ref[g, h] = state
    for h in heads:
        sfin_ref[h] = states[(0, h)]
    for n, (g, sb, h) in enumerate(units):
        o = (jnp.concatenate(qss[n], axis=0)
             + _dot(attns[n], jnp.concatenate(vnews[n], axis=0).astype(BF16)))
        o = o * lax.rsqrt(jnp.mean(o * o, axis=-1, keepdims=True) + EPS) * onw_ref[...]
        o = o * dzs_ref[g, sb * bs:(sb + 1) * bs, h * LANES:(h + 1) * LANES]
        odn_ref[g, sb * bs:(sb + 1) * bs, h * LANES:(h + 1) * LANES] = o.astype(BF16)


def _deltanet(qkv, scol, srow, dzs, onw, s0, *, group, tt):
    b, s, _ = qkv.shape
    nt = s // tt
    return pl.pallas_call(
        functools.partial(_dn_kernel, tt=tt),
        grid=(b // group, nt),
        in_specs=[
            pl.BlockSpec((group, tt, 3 * DN_WIDTH), lambda bi, i: (bi, i, 0)),
            pl.BlockSpec((group, tt, LANES), lambda bi, i: (bi, i, 0)),
            pl.BlockSpec((group, SROW, tt), lambda bi, i: (bi, 0, i)),
            pl.BlockSpec((group, tt, DN_WIDTH), lambda bi, i: (bi, i, 0)),
            pl.BlockSpec((1, DN_HEAD_DIM), lambda bi, i: (0, 0)),
            pl.BlockSpec((DN_HEADS, DN_HEAD_DIM, DN_HEAD_DIM), lambda bi, i: (0, 0, 0)),
        ],
        out_specs=[pl.BlockSpec((group, tt, DN_WIDTH), lambda bi, i: (bi, i, 0)),
                   pl.BlockSpec((DN_HEADS, DN_HEAD_DIM, DN_HEAD_DIM), lambda bi, i: (0, 0, 0))],
        out_shape=[jax.ShapeDtypeStruct((b, s, DN_WIDTH), BF16),
                   jax.ShapeDtypeStruct((DN_HEADS, DN_HEAD_DIM, DN_HEAD_DIM), F32)],
        scratch_shapes=[pltpu.VMEM((group, DN_HEADS, DN_HEAD_DIM, DN_HEAD_DIM), F32)],
        compiler_params=pltpu.CompilerParams(dimension_semantics=("arbitrary", "arbitrary"),
                                             vmem_limit_bytes=VMEM_LIMIT),
        name="deltanet",
    )(qkv, scol, srow, dzs, onw, s0)


def _tail_kernel(x_ref, ofox_ref, odn_ref, gates_ref, wbf_ref, wbd_ref, wout_ref, fnw_ref,
                 wg_ref, wu_ref, wd_ref, finw_ref, o_ref, *, ff_chunk):
    a = _dot_tn(ofox_ref[...], wbf_ref[...])
    bb = _dot(odn_ref[...], wbd_ref[...])
    y = gates_ref[:, :D_MODEL] * a + gates_ref[:, D_MODEL:] * bb
    h1 = x_ref[...] + _dot(y.astype(BF16), wout_ref[...])
    n = (h1 * lax.rsqrt(jnp.mean(h1 * h1, axis=-1, keepdims=True) + EPS) * fnw_ref[...]).astype(BF16)
    acc = h1
    for c in range(D_FF // ff_chunk):
        lo, hi = c * ff_chunk, (c + 1) * ff_chunk
        gt = _dot(n, wg_ref[:, lo:hi])
        up = _dot(n, wu_ref[:, lo:hi])
        act = (gt * _sigmoid(gt) * up).astype(BF16)
        acc = acc + _dot(act, wd_ref[lo:hi, :])
    o_ref[...] = acc * lax.rsqrt(jnp.mean(acc * acc, axis=-1, keepdims=True) + EPS) * finw_ref[...]


def _tail(x2d, ofox, odn, gates, wbf, wbd, wout, fnw, wg, wu, wd, finw, *, tm, ff_chunk):
    m = x2d.shape[0]
    row = lambda w: pl.BlockSpec((tm, w), lambda i: (i, 0))
    return pl.pallas_call(
        functools.partial(_tail_kernel, ff_chunk=ff_chunk),
        grid=(m // tm,),
        in_specs=[row(D_MODEL), pl.BlockSpec((FOX_WIDTH, tm), lambda i: (0, i)), row(DN_WIDTH),
                  row(2 * D_MODEL),
                  _const_spec((FOX_WIDTH, D_MODEL)), _const_spec((DN_WIDTH, D_MODEL)),
                  _const_spec((D_MODEL, D_MODEL)), _const_spec((1, D_MODEL)),
                  _const_spec((D_MODEL, D_FF)), _const_spec((D_MODEL, D_FF)),
                  _const_spec((D_FF, D_MODEL)), _const_spec((1, D_MODEL))],
        out_specs=row(D_MODEL),
        out_shape=jax.ShapeDtypeStruct((m, D_MODEL), F32),
        compiler_params=pltpu.CompilerParams(dimension_semantics=("arbitrary",),
                                             vmem_limit_bytes=VMEM_LIMIT),
        name="merge_ffn",
    )(x2d, ofox, odn, gates, wbf, wbd, wout, fnw, wg, wu, wd, finw)


def _pick_tile(n, pref):
    t = min(pref, n)
    while n % t:
        t //= 2
    return t


def kernel(x, meta_tokens, mix_norm_w, w_in, fox_forget_bias, dn_conv_w, dn_a_log, dn_dt_bias,
           dn_out_norm_w, w_branch_fox, w_branch_dn, w_out, ffn_norm_w, w_ffn_gate, w_ffn_up,
           w_ffn_down, final_norm_w):
    b, s, _ = x.shape
    assert mix_norm_w.shape[0] == 1, "single layer only"
    assert s % PREFIX == 0
    m = b * s

    wi = w_in[0]
    o_small0 = 3 * FOX_WIDTH
    o_dn = o_small0 + FOX_HEADS
    o_small1 = o_dn + 3 * DN_WIDTH
    o_rest = o_small1 + 2 * DN_HEADS
    wqv_t = jnp.concatenate([wi[:, :FOX_WIDTH].astype(BF16).T,
                             wi[:, 2 * FOX_WIDTH:o_small0].astype(BF16).T], axis=0)
    wmain = jnp.concatenate([wi[:, FOX_WIDTH:2 * FOX_WIDTH], wi[:, o_dn:o_small1], wi[:, o_rest:]],
                            axis=1).astype(BF16)
    w_alogit = wi[:, o_small1 + DN_HEADS:o_rest]
    wsmall = jnp.concatenate([wi[:, o_small0:o_dn], wi[:, o_small1:o_small1 + DN_HEADS],
                              w_alogit, w_alogit, w_alogit], axis=1)
    wsmall = jnp.pad(wsmall, ((0, 0), (0, LANES - N_SMALL))).astype(BF16)
    bias = jnp.zeros((SUBLANES, LANES), F32)
    bias = bias.at[0, LOGF_LANE:LOGF_LANE + FOX_HEADS].set(fox_forget_bias[0].astype(F32))
    bias = bias.at[0, G_LANE:N_SMALL].set(jnp.tile(dn_dt_bias[0].astype(F32), 3))
    bias = bias.at[1, G_LANE:N_SMALL].set(jnp.tile(dn_a_log[0].astype(F32), 3))
    nw = mix_norm_w[0].reshape(1, D_MODEL).astype(F32)

    x_p = jnp.concatenate([jnp.zeros((N_PAD, D_MODEL), F32), meta_tokens.astype(F32)], axis=0)
    convw = dn_conv_w[0].astype(F32)
    _, k_p, v_tp, kbias_p, qkv_p, dzs_p, _, scol_p, srow_p, conv_tail_p = _inproj(
        x_p, nw, wqv_t, wmain, wsmall, bias, jnp.zeros((1, LANES), F32), convw,
        jnp.zeros((SUBLANES, 3 * DN_WIDTH), F32), tm=PREFIX, tiles_per_batch=1, n_pad=N_PAD)

    tm = _pick_tile(s, 512)
    q_t, k, v_t, kbias, qkv, dzs, gates, scol, srow3, _ = _inproj(
        x.reshape(m, D_MODEL), nw, wqv_t, wmain, wsmall, bias, scol_p[PREFIX - 1:PREFIX, :], convw,
        conv_tail_p, tm=tm, tiles_per_batch=s // tm, n_pad=0)

    scol3 = scol.reshape(b, s, LANES)
    tq = _pick_tile(s, 512)
    ofox_t = _fox(q_t, k.reshape(b, s, FOX_WIDTH), v_t, kbias.reshape(b, s, LANES),
                  k_p[N_PAD:], v_tp[:, N_PAD:], kbias_p[N_PAD:], tq=tq, tk=min(256, tq // 2))

    onw = dn_out_norm_w[0].reshape(1, DN_HEAD_DIM).astype(F32)
    _, s_prefix = _deltanet(
        qkv_p[None], scol_p[None], srow_p, dzs_p[None], onw,
        jnp.zeros((DN_HEADS, DN_HEAD_DIM, DN_HEAD_DIM), F32), group=1, tt=PREFIX)
    group = _pick_tile(b, DN_UNITS // DN_HEADS)
    odn, _ = _deltanet(
        qkv.reshape(b, s, 3 * DN_WIDTH), scol3, srow3, dzs.reshape(b, s, DN_WIDTH), onw, s_prefix,
        group=group, tt=_pick_tile(s, DN_UNITS // (DN_HEADS * group) * DN_BLOCK))

    out = _tail(
        x.reshape(m, D_MODEL), ofox_t, odn.reshape(m, DN_WIDTH), gates,
        w_branch_fox[0].astype(BF16), w_branch_dn[0].astype(BF16), w_out[0].astype(BF16),
        ffn_norm_w[0].reshape(1, D_MODEL).astype(F32),
        w_ffn_gate[0].astype(BF16), w_ffn_up[0].astype(BF16), w_ffn_down[0].astype(BF16),
        final_norm_w.reshape(1, D_MODEL).astype(F32),
        tm=_pick_tile(m, 512), ff_chunk=D_FF // 2)
    return out.reshape(b, s, D_MODEL)
```

```python
import functools

import jax
import jax.numpy as jnp
from jax import lax
from jax.experimental import pallas as pl
from jax.experimental.pallas import tpu as pltpu

F32 = jnp.float32
BF16 = jnp.bfloat16
HIGHEST = lax.Precision.HIGHEST

D_MODEL = 1024
N_META = 16
PREFIX = 128
N_PAD = PREFIX - N_META
FOX_HEADS = 8
FOX_HEAD_DIM = 64
FOX_WIDTH = FOX_HEADS * FOX_HEAD_DIM
V_ROWS = FOX_HEAD_DIM + 8
FOX_PAIRS_PER_STEP = 4
DN_HEADS = 4
DN_HEAD_DIM = 128
DN_WIDTH = DN_HEADS * DN_HEAD_DIM
DN_CHUNK = 64
DN_BLOCK = 2 * DN_CHUNK
DN_UNITS = 16
CONV_WIDTH = 4
D_FF = 2816
EPS = 1e-6
NEG_INF = -1e30
LOG2E = 1.4426950408889634

LANES = 128
SUBLANES = 8
N_MAIN = FOX_WIDTH + 3 * DN_WIDTH + DN_WIDTH + 2 * D_MODEL
COL_CHUNK = 512
LOGF_LANE = 0
BETA_LANE = FOX_HEADS
G_LANE = BETA_LANE + DN_HEADS
GSUF_LANE = G_LANE + DN_HEADS
GRAW_LANE = GSUF_LANE + DN_HEADS
N_SMALL = GRAW_LANE + DN_HEADS
SROW = 24
VMEM_LIMIT = 56 * 1024 * 1024


def _const_spec(shape):
    nd = len(shape)
    return pl.BlockSpec(shape, lambda *_: (0,) * nd, pipeline_mode=pl.Buffered(1))


def _sigmoid(x):
    return 0.5 * jnp.tanh(0.5 * x) + 0.5


def _dot(a, b, **kw):
    return jnp.dot(a, b, preferred_element_type=F32, **kw)


def _dot_nt(a, b, **kw):
    return lax.dot_general(a, b, (((1,), (1,)), ((), ())), preferred_element_type=F32, **kw)


def _dot_tn(a, b, **kw):
    return lax.dot_general(a, b, (((0,), (0,)), ((), ())), preferred_element_type=F32, **kw)


def _inproj_kernel(x_ref, nw_ref, wqv_t_ref, wmain_ref, wsmall_ref, bias_ref, carry0_ref,
                   convw_ref, halo0_ref,
                   q_t_ref, k_ref, v_t_ref, kbias_ref, dqkv_ref, dzs_ref, gates_ref, scol_ref,
                   srow_ref, tail_ref, carry_ref, halo_ref, conv_ref, *, tm, tiles_per_batch, n_pad):
    i = pl.program_id(0)

    @pl.when(i % tiles_per_batch == 0)
    def _():
        carry_ref[...] = carry0_ref[...]
        halo_ref[...] = halo0_ref[...]

    x = x_ref[...]
    ms = jnp.mean(x * x, axis=-1, keepdims=True)
    hn = (x * lax.rsqrt(ms + EPS) * nw_ref[...]).astype(BF16)

    z = _dot(hn, wsmall_ref[...]) + bias_ref[0:1, :]
    lane = lax.broadcasted_iota(jnp.int32, (tm, LANES), 1)
    row = lax.broadcasted_iota(jnp.int32, (tm, LANES), 0)
    e = jnp.exp(-jnp.abs(z))
    l1p = jnp.log1p(e)
    logf = jnp.minimum(z, 0.0) - l1p
    softplus = jnp.maximum(z, 0.0) + l1p
    sig = jnp.where(z >= 0.0, 1.0, e) / (1.0 + e)
    g = -jnp.exp(bias_ref[1:2, :]) * softplus
    if n_pad:
        vm = (row + (i % tiles_per_batch) * tm >= n_pad).astype(F32)
        sig = sig * vm
        g = g * vm
    val = jnp.where(lane < BETA_LANE, logf,
                    jnp.where(lane < G_LANE, sig, jnp.where(lane < N_SMALL, g, 0.0)))
    in_chunk = row % DN_CHUNK
    is_logf = lane < BETA_LANE
    is_gpre = jnp.logical_and(lane >= G_LANE, lane < GSUF_LANE)
    is_gsuf = jnp.logical_and(lane >= GSUF_LANE, lane < GRAW_LANE)
    scan = val
    sh = 1
    while sh < tm:
        take_up = jnp.logical_and(is_logf, row >= sh)
        if sh < DN_CHUNK:
            take_up = jnp.logical_or(take_up, jnp.logical_and(is_gpre, in_chunk >= sh))
            take_down = jnp.logical_and(is_gsuf, in_chunk + sh < DN_CHUNK)
            below = jnp.where(take_down, pltpu.roll(scan, tm - sh, axis=0), 0.0)
        else:
            below = 0.0
        scan = scan + jnp.where(take_up, pltpu.roll(scan, sh, axis=0), below)
        sh *= 2
    out = scan + jnp.where(is_logf, carry_ref[...], 0.0)
    carry_ref[...] = out[tm - 1:tm, :]
    scol_ref[...] = out
    srow_ref[...] = out.T[:SROW, :]
    nc = jnp.where(is_logf, out * -LOG2E, 0.0)
    hi = nc.astype(BF16).astype(F32)
    mid = (nc - hi).astype(BF16).astype(F32)
    lo = nc - hi - mid
    kbias = hi + pltpu.roll(mid, FOX_HEADS, axis=1) + pltpu.roll(lo, 2 * FOX_HEADS, axis=1)
    kbias_ref[...] = kbias.astype(BF16)

    def main_chunk(c):
        return lambda: _dot(hn, wmain_ref[:, c * COL_CHUNK:(c + 1) * COL_CHUNK])

    def store_q_t(acc):
        q_t_ref[...] = (acc * (FOX_HEAD_DIM ** -0.5 * LOG2E)).astype(BF16)

    def store_v_t(acc):
        v_t = acc.astype(BF16)
        ones = jnp.ones((V_ROWS - FOX_HEAD_DIM, tm), BF16)
        v_t_ref[...] = jnp.concatenate(
            [piece for h in range(FOX_HEADS)
             for piece in (v_t[h * FOX_HEAD_DIM:(h + 1) * FOX_HEAD_DIM], ones)], axis=0)

    def store_k(acc):
        k_ref[...] = acc.astype(BF16)

    def store_dn(which):
        def epilogue(acc):
            cols = slice(which * COL_CHUNK, (which + 1) * COL_CHUNK)
            if n_pad:
                acc = acc * vm[:, :1]
            conv_ref[which, 0:SUBLANES, :] = halo_ref[:, cols]
            conv_ref[which, SUBLANES:, :] = acc
            halo_ref[:, cols] = acc[tm - SUBLANES:, :]
            tail_ref[:, cols] = acc[tm - SUBLANES:, :]
            a = convw_ref[CONV_WIDTH - 1:CONV_WIDTH, cols] * acc
            for t in range(CONV_WIDTH - 1):
                start = SUBLANES - (CONV_WIDTH - 1) + t
                a = a + convw_ref[t:t + 1, cols] * conv_ref[which, start:start + tm, :]
            a = a * _sigmoid(a)
            if which == 2:
                dqkv_ref[:, cols] = a
            else:
                scale = DN_HEAD_DIM ** -0.5 if which == 0 else 1.0
                for h in range(DN_HEADS):
                    ah = a[:, h * LANES:(h + 1) * LANES]
                    inv = lax.rsqrt(jnp.sum(ah * ah, axis=-1, keepdims=True) + EPS) * scale
                    lo_col = which * COL_CHUNK + h * LANES
                    dqkv_ref[:, lo_col:lo_col + LANES] = ah * inv
        return epilogue

    def store_dzs(acc):
        dzs_ref[...] = acc * _sigmoid(acc)

    def store_gates(j):
        def epilogue(acc):
            gates_ref[:, j * COL_CHUNK:(j + 1) * COL_CHUNK] = _sigmoid(acc)
        return epilogue

    store_q_t(_dot_nt(wqv_t_ref[:FOX_WIDTH, :], hn))
    store_v_t(_dot_nt(wqv_t_ref[FOX_WIDTH:, :], hn))
    store_k(main_chunk(0)())
    for which in range(3):
        store_dn(which)(main_chunk(1 + which)())
    store_dzs(main_chunk(4)())
    for j in range(2 * D_MODEL // COL_CHUNK):
        store_gates(j)(main_chunk(5 + j)())


def _inproj(x2d, nw, wqv_t, wmain, wsmall, bias, carry0, convw, halo0, *, tm, tiles_per_batch,
            n_pad):
    m = x2d.shape[0]
    row = lambda w: pl.BlockSpec((tm, w), lambda i: (i, 0))
    col = lambda h: pl.BlockSpec((h, tm), lambda i: (0, i))
    return pl.pallas_call(
        functools.partial(_inproj_kernel, tm=tm, tiles_per_batch=tiles_per_batch, n_pad=n_pad),
        grid=(m // tm,),
        in_specs=[row(D_MODEL), _const_spec((1, D_MODEL)), _const_spec((2 * FOX_WIDTH, D_MODEL)),
                  _const_spec((D_MODEL, N_MAIN)),
                  _const_spec((D_MODEL, LANES)), _const_spec((SUBLANES, LANES)),
                  _const_spec((1, LANES)), _const_spec((CONV_WIDTH, 3 * DN_WIDTH)),
                  _const_spec((SUBLANES, 3 * DN_WIDTH))],
        out_specs=[col(FOX_WIDTH), row(FOX_WIDTH), col(FOX_HEADS * V_ROWS), row(LANES),
                   row(3 * DN_WIDTH), row(DN_WIDTH), row(2 * D_MODEL),
                   row(LANES),
                   pl.BlockSpec((None, SROW, tm),
                                lambda i: (i // tiles_per_batch, 0, i % tiles_per_batch)),
                   pl.BlockSpec((SUBLANES, 3 * DN_WIDTH), lambda i: (0, 0))],
        out_shape=[jax.ShapeDtypeStruct((FOX_WIDTH, m), BF16),
                   jax.ShapeDtypeStruct((m, FOX_WIDTH), BF16),
                   jax.ShapeDtypeStruct((FOX_HEADS * V_ROWS, m), BF16),
                   jax.ShapeDtypeStruct((m, LANES), BF16),
                   jax.ShapeDtypeStruct((m, 3 * DN_WIDTH), F32),
                   jax.ShapeDtypeStruct((m, DN_WIDTH), F32),
                   jax.ShapeDtypeStruct((m, 2 * D_MODEL), F32),
                   jax.ShapeDtypeStruct((m, LANES), F32),
                   jax.ShapeDtypeStruct((m // (tm * tiles_per_batch), SROW,
                                         tm * tiles_per_batch), F32),
                   jax.ShapeDtypeStruct((SUBLANES, 3 * DN_WIDTH), F32)],
        scratch_shapes=[pltpu.VMEM((1, LANES), F32), pltpu.VMEM((SUBLANES, 3 * DN_WIDTH), F32),
                        pltpu.VMEM((3, tm + SUBLANES, COL_CHUNK), F32)],
        compiler_params=pltpu.CompilerParams(dimension_semantics=("arbitrary",),
                                             vmem_limit_bytes=VMEM_LIMIT),
        name="inproj",
    )(x2d, nw, wqv_t, wmain, wsmall, bias, carry0, convw, halo0)


def _fox_kernel(q_t_ref, k_ref, v_t_ref, kb_ref, kp_ref, v_tp_ref, kbp_ref,
                o_ref, m_ref, l_ref, acc_ref, t_ref, *, tq, tk, pairs):
    pg = pl.program_id(1)
    i = pl.program_id(2)
    heads = range(2 * pairs)
    sub = lax.broadcasted_iota(jnp.int32, (LANES, 1), 0)
    rhs = []
    for hh in heads:
        pp, h = divmod(hh, 2)
        q_t = q_t_ref[pp * LANES:(pp + 1) * LANES, :]
        mine = sub < FOX_HEAD_DIM if h == 0 else sub >= FOX_HEAD_DIM
        head = 2 * (pg * pairs + pp) + h
        sel = jnp.where(jnp.logical_and(sub < 3 * FOX_HEADS, (sub % FOX_HEADS) == head),
                        1.0, 0.0).astype(BF16) * jnp.ones((1, tq), BF16)
        rhs.append(jnp.concatenate([jnp.where(mine, q_t, jnp.zeros_like(q_t)), sel], axis=0))

    def scores(kt, kbt):
        lhs = [jnp.concatenate([kt[:, pp * LANES:(pp + 1) * LANES], kbt], axis=1)
               for pp in range(pairs)]
        return [_dot(lhs[hh // 2], rhs[hh]) for hh in heads]

    def scores_into(slot, off):
        ts = scores(k_ref[pl.ds(off, tk), :], kb_ref[pl.ds(off, tk), :])
        for hh in heads:
            t_ref[slot, hh] = ts[hh]

    def update(read_t, v_aug, mask, first):
        def masked_t(hh):
            t = read_t(hh)
            return t if mask is None else jnp.where(mask, t, NEG_INF)

        t_max = [jnp.max(masked_t(hh), axis=0, keepdims=True) for hh in heads]
        if first:
            m_new = t_max
        else:
            m_prev = [m_ref[hh] for hh in heads]
            m_new = [jnp.maximum(m_prev[hh], t_max[hh]) for hh in heads]
        pm = [jnp.exp2(masked_t(hh) - m_new[hh]).astype(BF16) for hh in heads]
        r = [_dot(v_aug[hh * V_ROWS:(hh + 1) * V_ROWS, :], pm[hh]) for hh in heads]
        pv = jnp.concatenate([r[hh][:FOX_HEAD_DIM] for hh in heads], axis=0)
        psum = [r[hh][FOX_HEAD_DIM:FOX_HEAD_DIM + 1] for hh in heads]
        if first:
            for hh in heads:
                l_ref[hh] = psum[hh]
            acc_ref[...] = pv
        else:
            alpha = [jnp.exp2(m_prev[hh] - m_new[hh]) for hh in heads]
            for hh in heads:
                l_ref[hh] = alpha[hh] * l_ref[hh] + psum[hh]
            alpha_rows = jnp.concatenate(
                [jnp.broadcast_to(alpha[hh], (FOX_HEAD_DIM, tq)) for hh in heads], axis=0)
            acc_ref[...] = alpha_rows * acc_ref[...] + pv
        for hh in heads:
            m_ref[hh] = m_new[hh]

    ts_p = scores(kp_ref[...], kbp_ref[...])
    update(lambda hh: ts_p[hh], v_tp_ref[...], None, True)

    def v_at(off):
        return v_t_ref[:, pl.ds(off, tk)]

    def slot(n):
        return lambda hh: t_ref[n, hh]

    pairs_per_q = tq // (2 * tk)
    scores_into(0, 0)

    def body(jj, carry):
        off = pl.multiple_of(jj * 2 * tk, 2 * tk)
        scores_into(1, off + tk)
        update(slot(0), v_at(off), None, False)
        scores_into(0, off + 2 * tk)
        update(slot(1), v_at(off + tk), None, False)
        return carry

    lax.fori_loop(0, i * pairs_per_q, body, 0)

    rr = lax.broadcasted_iota(jnp.int32, (tk, tq), 0)
    cc = lax.broadcasted_iota(jnp.int32, (tk, tq), 1)
    for dp in range(pairs_per_q):
        off = pl.multiple_of(i * tq + dp * 2 * tk, 2 * tk)
        scores_into(1, off + tk)
        update(slot(0), v_at(off), rr + dp * 2 * tk <= cc, False)
        if dp + 1 < pairs_per_q:
            scores_into(0, off + 2 * tk)
        update(slot(1), v_at(off + tk), rr + (dp * 2 + 1) * tk <= cc, False)

    l_rows = jnp.concatenate([jnp.broadcast_to(l_ref[hh], (FOX_HEAD_DIM, tq)) for hh in heads], axis=0)
    o_ref[...] = (acc_ref[...] / l_rows).astype(BF16)


def _fox(q_t, k, v_t, kbias, k_p, v_tp, kbias_p, *, tq, tk, pairs):
    b, s, _ = k.shape
    ngroups = FOX_HEADS // (2 * pairs)
    nq = s // tq
    wl = pairs * LANES
    return pl.pallas_call(
        functools.partial(_fox_kernel, tq=tq, tk=tk, pairs=pairs),
        grid=(b, ngroups, nq),
        in_specs=[
            pl.BlockSpec((wl, tq), lambda bi, pg, i: (pg, bi * nq + i)),
            pl.BlockSpec((None, s, wl), lambda bi, pg, i: (bi, 0, pg)),
            pl.BlockSpec((2 * pairs * V_ROWS, s), lambda bi, pg, i: (pg, bi)),
            pl.BlockSpec((None, s, LANES), lambda bi, pg, i: (bi, 0, 0)),
            pl.BlockSpec((N_META, wl), lambda bi, pg, i: (0, pg)),
            pl.BlockSpec((2 * pairs * V_ROWS, N_META), lambda bi, pg, i: (pg, 0)),
            pl.BlockSpec((N_META, LANES), lambda bi, pg, i: (0, 0)),
        ],
        out_specs=pl.BlockSpec((wl, tq), lambda bi, pg, i: (pg, bi * nq + i)),
        out_shape=jax.ShapeDtypeStruct((FOX_WIDTH, b * s), BF16),
        scratch_shapes=[pltpu.VMEM((2 * pairs, 1, tq), F32), pltpu.VMEM((2 * pairs, 1, tq), F32),
                        pltpu.VMEM((wl, tq), F32), pltpu.VMEM((2, 2 * pairs, tk, tq), F32)],
        compiler_params=pltpu.CompilerParams(
            dimension_semantics=("arbitrary", "arbitrary", "arbitrary"),
            vmem_limit_bytes=VMEM_LIMIT),
        name="fox_attention",
    )(q_t, k, v_t, kbias, k_p, v_tp, kbias_p)


def _dn_kernel(qkv_ref, scol_ref, srow_ref, dzs_ref, onw_ref,
               s0_ref, odn_ref, sfin_ref, state_ref, *, tt):
    i = pl.program_id(1)
    nchunk = tt // DN_CHUNK

    @pl.when(i == 0)
    def _():
        for g in range(qkv_ref.shape[0]):
            state_ref[g] = s0_ref[...]

    bs = DN_BLOCK
    heads = range(DN_HEADS)
    groups = range(qkv_ref.shape[0])
    units = [(g, sb, h) for g in groups for sb in range(tt // bs) for h in heads]

    def slab(g, sb, c0):
        return qkv_ref[g, sb * bs:(sb + 1) * bs, c0:c0 + LANES]

    def small_col(g, sb, lane):
        return scol_ref[g, sb * bs:(sb + 1) * bs, lane:lane + 1]

    rr = lax.broadcasted_iota(jnp.int32, (bs, bs), 0)
    cc = lax.broadcasted_iota(jnp.int32, (bs, bs), 1)
    rc_xor = jnp.bitwise_xor(rr, cc)
    same = rc_xor < DN_CHUNK
    tril_m = jnp.logical_and(same, rr >= cc)
    strict_m = jnp.logical_and(same, rr > cc)
    eye = (rr == cc).astype(F32)

    qs = [slab(g, sb, h * LANES) for g, sb, h in units]
    ks = [slab(g, sb, DN_WIDTH + h * LANES) for g, sb, h in units]
    vs = [slab(g, sb, 2 * DN_WIDTH + h * LANES) for g, sb, h in units]
    states = {(g, h): state_ref[g, h] for g in groups for h in heads}
    betas = [small_col(g, sb, BETA_LANE + h) for g, sb, h in units]
    gcs = [small_col(g, sb, G_LANE + h) for g, sb, h in units]
    gls = [gcs[n] + small_col(g, sb, GSUF_LANE + h) - small_col(g, sb, GRAW_LANE + h)
           for n, (g, sb, h) in enumerate(units)]
    decays = [jnp.exp(jnp.where(
        tril_m, gcs[n] - srow_ref[g, G_LANE + h:G_LANE + h + 1, sb * bs:(sb + 1) * bs], NEG_INF))
        for n, (g, sb, h) in enumerate(units)]
    nu = range(len(units))
    egcs = [jnp.exp(g) for g in gcs]
    kbs = [ks[n] * betas[n] for n in nu]
    ks_b = [k.astype(BF16) for k in ks]
    lmats = [jnp.where(strict_m, _dot_nt(kbs[n].astype(BF16), ks_b[n]) * decays[n], 0.0) for n in nu]
    attns = [(_dot_nt(qs[n].astype(BF16), ks_b[n]) * decays[n]).astype(BF16) for n in nu]
    ainvs = [eye - jnp.where(rc_xor == 1, lm, 0.0) for lm in lmats]
    for lvl in range(1, 6):
        blk = 2 ** lvl
        lvl_m = jnp.logical_and(rc_xor >= blk, rc_xor < 2 * blk)
        ainvs_b = [a.astype(BF16) for a in ainvs]
        mids = [_dot(jnp.where(lvl_m, lmats[n], 0.0).astype(BF16), ainvs_b[n]).astype(BF16)
                for n in nu]
        ainvs = [ainvs[n] - _dot(ainvs_b[n], mids[n]) for n in nu]
    sols = [_dot(ainvs[n].astype(BF16),
                 jnp.concatenate([vs[n] * betas[n], kbs[n] * egcs[n]], axis=1).astype(BF16))
            for n in nu]
    uus = [sol[:, :DN_HEAD_DIM] for sol in sols]
    wws = [sol[:, DN_HEAD_DIM:] for sol in sols]
    qds = [qs[n] * egcs[n] for n in nu]
    kds = [(ks[n] * jnp.exp(gls[n] - gcs[n])).astype(BF16) for n in nu]

    vnews = [[] for _ in nu]
    qss = [[] for _ in nu]
    for c in range(nchunk):
        csb, lc = divmod(c, bs // DN_CHUNK)
        lo, hi = lc * DN_CHUNK, (lc + 1) * DN_CHUNK
        cur = [(n, (g, h)) for n, (g, sb, h) in enumerate(units) if sb == csb]
        rs = {n: _dot(jnp.concatenate([wws[n][lo:hi], qds[n][lo:hi]], axis=0).astype(BF16),
                      states[key].astype(BF16)) for n, key in cur}
        vns = {n: uus[n][lo:hi] - rs[n][:DN_CHUNK] for n, _ in cur}
        states.update({key: jnp.exp(gls[n][lo:lo + 1, :]) * states[key]
                       + _dot_tn(kds[n][lo:hi], vns[n].astype(BF16)) for n, key in cur})
        for n, _ in cur:
            qss[n].append(rs[n][DN_CHUNK:])
            vnews[n].append(vns[n])
    for (g, h), state in states.items():
        state_ref[g, h] = state
    for h in heads:
        sfin_ref[h] = states[(0, h)]
    for n, (g, sb, h) in enumerate(units):
        o = (jnp.concatenate(qss[n], axis=0)
             + _dot(attns[n], jnp.concatenate(vnews[n], axis=0).astype(BF16)))
        o = o * lax.rsqrt(jnp.mean(o * o, axis=-1, keepdims=True) + EPS) * onw_ref[...]
        o = o * dzs_ref[g, sb * bs:(sb + 1) * bs, h * LANES:(h + 1) * LANES]
        odn_ref[g, sb * bs:(sb + 1) * bs, h * LANES:(h + 1) * LANES] = o.astype(BF16)


def _deltanet(qkv, scol, srow, dzs, onw, s0, *, group, tt):
    b, s, _ = qkv.shape
    nt = s // tt
    return pl.pallas_call(
        functools.partial(_dn_kernel, tt=tt),
        grid=(b // group, nt),
        in_specs=[
            pl.BlockSpec((group, tt, 3 * DN_WIDTH), lambda bi, i: (bi, i, 0)),
            pl.BlockSpec((group, tt, LANES), lambda bi, i: (bi, i, 0)),
            pl.BlockSpec((group, SROW, tt), lambda bi, i: (bi, 0, i)),
            pl.BlockSpec((group, tt, DN_WIDTH), lambda bi, i: (bi, i, 0)),
            pl.BlockSpec((1, DN_HEAD_DIM), lambda bi, i: (0, 0)),
            pl.BlockSpec((DN_HEADS, DN_HEAD_DIM, DN_HEAD_DIM), lambda bi, i: (0, 0, 0)),
        ],
        out_specs=[pl.BlockSpec((group, tt, DN_WIDTH), lambda bi, i: (bi, i, 0)),
                   pl.BlockSpec((DN_HEADS, DN_HEAD_DIM, DN_HEAD_DIM), lambda bi, i: (0, 0, 0))],
        out_shape=[jax.ShapeDtypeStruct((b, s, DN_WIDTH), BF16),
                   jax.ShapeDtypeStruct((DN_HEADS, DN_HEAD_DIM, DN_HEAD_DIM), F32)],
        scratch_shapes=[pltpu.VMEM((group, DN_HEADS, DN_HEAD_DIM, DN_HEAD_DIM), F32)],
        compiler_params=pltpu.CompilerParams(dimension_semantics=("arbitrary", "arbitrary"),
                                             vmem_limit_bytes=VMEM_LIMIT),
        name="deltanet",
    )(qkv, scol, srow, dzs, onw, s0)


def _tail_kernel(x_ref, ofox_ref, odn_ref, gates_ref, wbf_ref, wbd_ref, wout_ref, fnw_ref,
                 wg_ref, wu_ref, wd_ref, finw_ref, o_ref, *, ff_chunk):
    a = _dot_tn(ofox_ref[...], wbf_ref[...])
    bb = _dot(odn_ref[...], wbd_ref[...])
    y = gates_ref[:, :D_MODEL] * a + gates_ref[:, D_MODEL:] * bb
    h1 = x_ref[...] + _dot(y.astype(BF16), wout_ref[...])
    n = (h1 * lax.rsqrt(jnp.mean(h1 * h1, axis=-1, keepdims=True) + EPS) * fnw_ref[...]).astype(BF16)
    acc = h1
    for c in range(D_FF // ff_chunk):
        lo, hi = c * ff_chunk, (c + 1) * ff_chunk
        gt = _dot(n, wg_ref[:, lo:hi])
        up = _dot(n, wu_ref[:, lo:hi])
        act = (gt * _sigmoid(gt) * up).astype(BF16)
        acc = acc + _dot(act, wd_ref[lo:hi, :])
    o_ref[...] = acc * lax.rsqrt(jnp.mean(acc * acc, axis=-1, keepdims=True) + EPS) * finw_ref[...]


def _tail(x2d, ofox, odn, gates, wbf, wbd, wout, fnw, wg, wu, wd, finw, *, tm, ff_chunk):
    m = x2d.shape[0]
    row = lambda w: pl.BlockSpec((tm, w), lambda i: (i, 0))
    return pl.pallas_call(
        functools.partial(_tail_kernel, ff_chunk=ff_chunk),
        grid=(m // tm,),
        in_specs=[row(D_MODEL), pl.BlockSpec((FOX_WIDTH, tm), lambda i: (0, i)), row(DN_WIDTH),
                  row(2 * D_MODEL),
                  _const_spec((FOX_WIDTH, D_MODEL)), _const_spec((DN_WIDTH, D_MODEL)),
                  _const_spec((D_MODEL, D_MODEL)), _const_spec((1, D_MODEL)),
                  _const_spec((D_MODEL, D_FF)), _const_spec((D_MODEL, D_FF)),
                  _const_spec((D_FF, D_MODEL)), _const_spec((1, D_MODEL))],
        out_specs=row(D_MODEL),
        out_shape=jax.ShapeDtypeStruct((m, D_MODEL), F32),
        compiler_params=pltpu.CompilerParams(dimension_semantics=("arbitrary",),
                                             vmem_limit_bytes=VMEM_LIMIT),
        name="merge_ffn",
    )(x2d, ofox, odn, gates, wbf, wbd, wout, fnw, wg, wu, wd, finw)


def _pick_tile(n, pref):
    t = min(pref, n)
    while n % t:
        t //= 2
    return t


def kernel(x, meta_tokens, mix_norm_w, w_in, fox_forget_bias, dn_conv_w, dn_a_log, dn_dt_bias,
           dn_out_norm_w, w_branch_fox, w_branch_dn, w_out, ffn_norm_w, w_ffn_gate, w_ffn_up,
           w_ffn_down, final_norm_w):
    b, s, _ = x.shape
    assert mix_norm_w.shape[0] == 1, "single layer only"
    assert s % PREFIX == 0
    m = b * s

    wi = w_in[0]
    o_small0 = 3 * FOX_WIDTH
    o_dn = o_small0 + FOX_HEADS
    o_small1 = o_dn + 3 * DN_WIDTH
    o_rest = o_small1 + 2 * DN_HEADS
    wqv_t = jnp.concatenate([wi[:, :FOX_WIDTH].astype(BF16).T,
                             wi[:, 2 * FOX_WIDTH:o_small0].astype(BF16).T], axis=0)
    wmain = jnp.concatenate([wi[:, FOX_WIDTH:2 * FOX_WIDTH], wi[:, o_dn:o_small1], wi[:, o_rest:]],
                            axis=1).astype(BF16)
    w_alogit = wi[:, o_small1 + DN_HEADS:o_rest]
    wsmall = jnp.concatenate([wi[:, o_small0:o_dn], wi[:, o_small1:o_small1 + DN_HEADS],
                              w_alogit, w_alogit, w_alogit], axis=1)
    wsmall = jnp.pad(wsmall, ((0, 0), (0, LANES - N_SMALL))).astype(BF16)
    bias = jnp.zeros((SUBLANES, LANES), F32)
    bias = bias.at[0, LOGF_LANE:LOGF_LANE + FOX_HEADS].set(fox_forget_bias[0].astype(F32))
    bias = bias.at[0, G_LANE:N_SMALL].set(jnp.tile(dn_dt_bias[0].astype(F32), 3))
    bias = bias.at[1, G_LANE:N_SMALL].set(jnp.tile(dn_a_log[0].astype(F32), 3))
    nw = mix_norm_w[0].reshape(1, D_MODEL).astype(F32)

    x_p = jnp.concatenate([jnp.zeros((N_PAD, D_MODEL), F32), meta_tokens.astype(F32)], axis=0)
    convw = dn_conv_w[0].astype(F32)
    _, k_p, v_tp, kbias_p, qkv_p, dzs_p, _, scol_p, srow_p, conv_tail_p = _inproj(
        x_p, nw, wqv_t, wmain, wsmall, bias, jnp.zeros((1, LANES), F32), convw,
        jnp.zeros((SUBLANES, 3 * DN_WIDTH), F32), tm=PREFIX, tiles_per_batch=1, n_pad=N_PAD)

    tm = _pick_tile(s, 512)
    q_t, k, v_t, kbias, qkv, dzs, gates, scol, srow3, _ = _inproj(
        x.reshape(m, D_MODEL), nw, wqv_t, wmain, wsmall, bias, scol_p[PREFIX - 1:PREFIX, :], convw,
        conv_tail_p, tm=tm, tiles_per_batch=s // tm, n_pad=0)

    scol3 = scol.reshape(b, s, LANES)
    tq = _pick_tile(s, 512)
    ofox_t = _fox(q_t, k.reshape(b, s, FOX_WIDTH), v_t, kbias.reshape(b, s, LANES),
                  k_p[N_PAD:], v_tp[:, N_PAD:], kbias_p[N_PAD:], tq=tq, tk=min(256, tq // 2),
                  pairs=FOX_PAIRS_PER_STEP)

    onw = dn_out_norm_w[0].reshape(1, DN_HEAD_DIM).astype(F32)
    _, s_prefix = _deltanet(
        qkv_p[None], scol_p[None], srow_p, dzs_p[None], onw,
        jnp.zeros((DN_HEADS, DN_HEAD_DIM, DN_HEAD_DIM), F32), group=1, tt=PREFIX)
    group = _pick_tile(b, DN_UNITS // DN_HEADS)
    odn, _ = _deltanet(
        qkv.reshape(b, s, 3 * DN_WIDTH), scol3, srow3, dzs.reshape(b, s, DN_WIDTH), onw, s_prefix,
        group=group, tt=_pick_tile(s, DN_UNITS // (DN_HEADS * group) * DN_BLOCK))

    out = _tail(
        x.reshape(m, D_MODEL), ofox_t, odn.reshape(m, DN_WIDTH), gates,
        w_branch_fox[0].astype(BF16), w_branch_dn[0].astype(BF16), w_out[0].astype(BF16),
        ffn_norm_w[0].reshape(1, D_MODEL).astype(F32),
        w_ffn_gate[0].astype(BF16), w_ffn_up[0].astype(BF16), w_ffn_down[0].astype(BF16),
        final_norm_w.reshape(1, D_MODEL).astype(F32),
        tm=_pick_tile(m, 512), ff_chunk=D_FF // 2)
    return out.reshape(b, s, D_MODEL)
```

```python
import functools

import jax
import jax.numpy as jnp
from jax import lax
from jax.experimental import pallas as pl
from jax.experimental.pallas import tpu as pltpu

F32 = jnp.float32
BF16 = jnp.bfloat16
HIGHEST = lax.Precision.HIGHEST

D_MODEL = 1024
N_META = 16
PREFIX = 128
N_PAD = PREFIX - N_META
FOX_HEADS = 8
FOX_HEAD_DIM = 64
FOX_WIDTH = FOX_HEADS * FOX_HEAD_DIM
V_ROWS = FOX_HEAD_DIM + 8
FOX_PAIRS_PER_STEP = 4
DN_HEADS = 4
DN_HEAD_DIM = 128
DN_WIDTH = DN_HEADS * DN_HEAD_DIM
DN_CHUNK = 64
DN_BLOCK = 2 * DN_CHUNK
DN_UNITS = 16
CONV_WIDTH = 4
D_FF = 2816
EPS = 1e-6
NEG_INF = -1e30
LOG2E = 1.4426950408889634

LANES = 128
SUBLANES = 8
COL_CHUNK = 512
LOGF_LANE = 0
BETA_LANE = FOX_HEADS
G_LANE = BETA_LANE + DN_HEADS
GSUF_LANE = G_LANE + DN_HEADS
GRAW_LANE = GSUF_LANE + DN_HEADS
N_SMALL = GRAW_LANE + DN_HEADS
SROW = 24
VMEM_LIMIT = 56 * 1024 * 1024


def _const_spec(shape):
    nd = len(shape)
    return pl.BlockSpec(shape, lambda *_: (0,) * nd, pipeline_mode=pl.Buffered(1))


def _sigmoid(x):
    return 0.5 * jnp.tanh(0.5 * x) + 0.5


def _dot(a, b, **kw):
    return jnp.dot(a, b, preferred_element_type=F32, **kw)


def _dot_nt(a, b, **kw):
    return lax.dot_general(a, b, (((1,), (1,)), ((), ())), preferred_element_type=F32, **kw)


def _dot_tn(a, b, **kw):
    return lax.dot_general(a, b, (((0,), (0,)), ((), ())), preferred_element_type=F32, **kw)


def _inproj_kernel(x_ref, nw_ref, wq_t_ref, wv_t_ref, wk_ref, wdn_ref, wrest_ref, wsmall_ref, bias_ref,
                   carry0_ref,
                   convw_ref, halo0_ref,
                   q_t_ref, k_ref, v_t_ref, kbias_ref, dqkv_ref, dzs_ref, gates_ref, scol_ref,
                   srow_ref, tail_ref, carry_ref, halo_ref, conv_ref, *, tm, tiles_per_batch, n_pad):
    i = pl.program_id(0)

    @pl.when(i % tiles_per_batch == 0)
    def _():
        carry_ref[...] = carry0_ref[...]
        halo_ref[...] = halo0_ref[...]

    x = x_ref[...]
    ms = jnp.mean(x * x, axis=-1, keepdims=True)
    hn = (x * lax.rsqrt(ms + EPS) * nw_ref[...]).astype(BF16)

    z = _dot(hn, wsmall_ref[...]) + bias_ref[0:1, :]
    lane = lax.broadcasted_iota(jnp.int32, (tm, LANES), 1)
    row = lax.broadcasted_iota(jnp.int32, (tm, LANES), 0)
    e = jnp.exp(-jnp.abs(z))
    l1p = jnp.log1p(e)
    logf = jnp.minimum(z, 0.0) - l1p
    softplus = jnp.maximum(z, 0.0) + l1p
    sig = jnp.where(z >= 0.0, 1.0, e) / (1.0 + e)
    g = -jnp.exp(bias_ref[1:2, :]) * softplus
    if n_pad:
        vm = (row + (i % tiles_per_batch) * tm >= n_pad).astype(F32)
        sig = sig * vm
        g = g * vm
    val = jnp.where(lane < BETA_LANE, logf,
                    jnp.where(lane < G_LANE, sig, jnp.where(lane < N_SMALL, g, 0.0)))
    in_chunk = row % DN_CHUNK
    is_logf = lane < BETA_LANE
    is_gpre = jnp.logical_and(lane >= G_LANE, lane < GSUF_LANE)
    is_gsuf = jnp.logical_and(lane >= GSUF_LANE, lane < GRAW_LANE)
    scan = val
    sh = 1
    while sh < tm:
        take_up = jnp.logical_and(is_logf, row >= sh)
        if sh < DN_CHUNK:
            take_up = jnp.logical_or(take_up, jnp.logical_and(is_gpre, in_chunk >= sh))
            take_down = jnp.logical_and(is_gsuf, in_chunk + sh < DN_CHUNK)
            below = jnp.where(take_down, pltpu.roll(scan, tm - sh, axis=0), 0.0)
        else:
            below = 0.0
        scan = scan + jnp.where(take_up, pltpu.roll(scan, sh, axis=0), below)
        sh *= 2
    out = scan + jnp.where(is_logf, carry_ref[...], 0.0)
    carry_ref[...] = out[tm - 1:tm, :]
    scol_ref[...] = out
    srow_ref[...] = out.T[:SROW, :]
    nc = jnp.where(is_logf, out * -LOG2E, 0.0)
    hi = nc.astype(BF16).astype(F32)
    mid = (nc - hi).astype(BF16).astype(F32)
    lo = nc - hi - mid
    kbias = hi + pltpu.roll(mid, FOX_HEADS, axis=1) + pltpu.roll(lo, 2 * FOX_HEADS, axis=1)
    kbias_ref[...] = kbias.astype(BF16)

    def main_chunk(c):
        if c == 0:
            return lambda: _dot(hn, wk_ref[...])
        ref, first = (wdn_ref, 1) if c < 4 else (wrest_ref, 4)
        return lambda: _dot(hn, ref[:, (c - first) * COL_CHUNK:(c - first + 1) * COL_CHUNK])

    def store_q_t(acc):
        q_t_ref[...] = (acc * (FOX_HEAD_DIM ** -0.5 * LOG2E)).astype(BF16)

    def store_v_t(acc):
        v_t = acc.astype(BF16)
        ones = jnp.ones((V_ROWS - FOX_HEAD_DIM, tm), BF16)
        v_t_ref[...] = jnp.concatenate(
            [piece for h in range(FOX_HEADS)
             for piece in (v_t[h * FOX_HEAD_DIM:(h + 1) * FOX_HEAD_DIM], ones)], axis=0)

    def store_k(acc):
        k_ref[...] = acc.astype(BF16)

    def store_dn(which):
        def epilogue(acc):
            cols = slice(which * COL_CHUNK, (which + 1) * COL_CHUNK)
            if n_pad:
                acc = acc * vm[:, :1]
            conv_ref[which, 0:SUBLANES, :] = halo_ref[:, cols]
            conv_ref[which, SUBLANES:, :] = acc
            halo_ref[:, cols] = acc[tm - SUBLANES:, :]
            tail_ref[:, cols] = acc[tm - SUBLANES:, :]
            a = convw_ref[CONV_WIDTH - 1:CONV_WIDTH, cols] * acc
            for t in range(CONV_WIDTH - 1):
                start = SUBLANES - (CONV_WIDTH - 1) + t
                a = a + convw_ref[t:t + 1, cols] * conv_ref[which, start:start + tm, :]
            a = a * _sigmoid(a)
            if which == 2:
                dqkv_ref[:, cols] = a
            else:
                scale = DN_HEAD_DIM ** -0.5 if which == 0 else 1.0
                for h in range(DN_HEADS):
                    ah = a[:, h * LANES:(h + 1) * LANES]
                    inv = lax.rsqrt(jnp.sum(ah * ah, axis=-1, keepdims=True) + EPS) * scale
                    lo_col = which * COL_CHUNK + h * LANES
                    dqkv_ref[:, lo_col:lo_col + LANES] = ah * inv
        return epilogue

    def store_dzs(acc):
        dzs_ref[...] = acc * _sigmoid(acc)

    def store_gates(j):
        def epilogue(acc):
            gates_ref[:, j * COL_CHUNK:(j + 1) * COL_CHUNK] = _sigmoid(acc)
        return epilogue

    store_q_t(_dot_nt(wq_t_ref[...], hn))
    store_v_t(_dot_nt(wv_t_ref[...], hn))
    store_k(main_chunk(0)())
    for which in range(3):
        store_dn(which)(main_chunk(1 + which)())
    store_dzs(main_chunk(4)())
    for j in range(2 * D_MODEL // COL_CHUNK):
        store_gates(j)(main_chunk(5 + j)())


def _inproj(x2d, nw, weights, wsmall, bias, carry0, convw, halo0, *, tm, tiles_per_batch, n_pad):
    m = x2d.shape[0]
    row = lambda w: pl.BlockSpec((tm, w), lambda i: (i, 0))
    col = lambda h: pl.BlockSpec((h, tm), lambda i: (0, i))
    return pl.pallas_call(
        functools.partial(_inproj_kernel, tm=tm, tiles_per_batch=tiles_per_batch, n_pad=n_pad),
        grid=(m // tm,),
        in_specs=[row(D_MODEL), _const_spec((1, D_MODEL))] + [_const_spec(w.shape) for w in weights] + [
                  _const_spec((D_MODEL, LANES)), _const_spec((SUBLANES, LANES)),
                  _const_spec((1, LANES)), _const_spec((CONV_WIDTH, 3 * DN_WIDTH)),
                  _const_spec((SUBLANES, 3 * DN_WIDTH))],
        out_specs=[col(FOX_WIDTH), row(FOX_WIDTH), col(FOX_HEADS * V_ROWS), row(LANES),
                   row(3 * DN_WIDTH), row(DN_WIDTH), row(2 * D_MODEL),
                   row(LANES),
                   pl.BlockSpec((None, SROW, tm),
                                lambda i: (i // tiles_per_batch, 0, i % tiles_per_batch)),
                   pl.BlockSpec((SUBLANES, 3 * DN_WIDTH), lambda i: (0, 0))],
        out_shape=[jax.ShapeDtypeStruct((FOX_WIDTH, m), BF16),
                   jax.ShapeDtypeStruct((m, FOX_WIDTH), BF16),
                   jax.ShapeDtypeStruct((FOX_HEADS * V_ROWS, m), BF16),
                   jax.ShapeDtypeStruct((m, LANES), BF16),
                   jax.ShapeDtypeStruct((m, 3 * DN_WIDTH), F32),
                   jax.ShapeDtypeStruct((m, DN_WIDTH), F32),
                   jax.ShapeDtypeStruct((m, 2 * D_MODEL), F32),
                   jax.ShapeDtypeStruct((m, LANES), F32),
                   jax.ShapeDtypeStruct((m // (tm * tiles_per_batch), SROW,
                                         tm * tiles_per_batch), F32),
                   jax.ShapeDtypeStruct((SUBLANES, 3 * DN_WIDTH), F32)],
        scratch_shapes=[pltpu.VMEM((1, LANES), F32), pltpu.VMEM((SUBLANES, 3 * DN_WIDTH), F32),
                        pltpu.VMEM((3, tm + SUBLANES, COL_CHUNK), F32)],
        compiler_params=pltpu.CompilerParams(dimension_semantics=("arbitrary",),
                                             vmem_limit_bytes=VMEM_LIMIT),
        name="inproj",
    )(x2d, nw, *weights, wsmall, bias, carry0, convw, halo0)


def _fox_kernel(q_t_ref, k_ref, v_t_ref, kb_ref, kp_ref, v_tp_ref, kbp_ref,
                o_ref, m_ref, l_ref, acc_ref, t_ref, *, tq, tk, pairs):
    pg = pl.program_id(1)
    i = pl.program_id(2)
    heads = range(2 * pairs)
    sub = lax.broadcasted_iota(jnp.int32, (LANES, 1), 0)
    rhs = []
    for hh in heads:
        pp, h = divmod(hh, 2)
        q_t = q_t_ref[pp * LANES:(pp + 1) * LANES, :]
        mine = sub < FOX_HEAD_DIM if h == 0 else sub >= FOX_HEAD_DIM
        head = 2 * (pg * pairs + pp) + h
        sel = jnp.where(jnp.logical_and(sub < 3 * FOX_HEADS, (sub % FOX_HEADS) == head),
                        1.0, 0.0).astype(BF16) * jnp.ones((1, tq), BF16)
        rhs.append(jnp.concatenate([jnp.where(mine, q_t, jnp.zeros_like(q_t)), sel], axis=0))

    def scores(kt, kbt):
        lhs = [jnp.concatenate([kt[:, pp * LANES:(pp + 1) * LANES], kbt], axis=1)
               for pp in range(pairs)]
        return [_dot(lhs[hh // 2], rhs[hh]) for hh in heads]

    def scores_into(slot, off):
        ts = scores(k_ref[pl.ds(off, tk), :], kb_ref[pl.ds(off, tk), :])
        for hh in heads:
            t_ref[slot, hh] = ts[hh]

    def update(read_t, v_aug, mask, first):
        def masked_t(hh):
            t = read_t(hh)
            return t if mask is None else jnp.where(mask, t, NEG_INF)

        t_max = [jnp.max(masked_t(hh), axis=0, keepdims=True) for hh in heads]
        if first:
            m_new = t_max
        else:
            m_prev = [m_ref[hh] for hh in heads]
            m_new = [jnp.maximum(m_prev[hh], t_max[hh]) for hh in heads]
        pm = [jnp.exp2(masked_t(hh) - m_new[hh]).astype(BF16) for hh in heads]
        r = [_dot(v_aug[hh * V_ROWS:(hh + 1) * V_ROWS, :], pm[hh]) for hh in heads]
        pv = jnp.concatenate([r[hh][:FOX_HEAD_DIM] for hh in heads], axis=0)
        psum = [r[hh][FOX_HEAD_DIM:FOX_HEAD_DIM + 1] for hh in heads]
        if first:
            for hh in heads:
                l_ref[hh] = psum[hh]
            acc_ref[...] = pv
        else:
            alpha = [jnp.exp2(m_prev[hh] - m_new[hh]) for hh in heads]
            for hh in heads:
                l_ref[hh] = alpha[hh] * l_ref[hh] + psum[hh]
            alpha_rows = jnp.concatenate(
                [jnp.broadcast_to(alpha[hh], (FOX_HEAD_DIM, tq)) for hh in heads], axis=0)
            acc_ref[...] = alpha_rows * acc_ref[...] + pv
        for hh in heads:
            m_ref[hh] = m_new[hh]

    ts_p = scores(kp_ref[...], kbp_ref[...])
    update(lambda hh: ts_p[hh], v_tp_ref[...], None, True)

    def v_at(off):
        return v_t_ref[:, pl.ds(off, tk)]

    def slot(n):
        return lambda hh: t_ref[n, hh]

    pairs_per_q = tq // (2 * tk)
    scores_into(0, 0)

    def body(jj, carry):
        off = pl.multiple_of(jj * 2 * tk, 2 * tk)
        scores_into(1, off + tk)
        update(slot(0), v_at(off), None, False)
        scores_into(0, off + 2 * tk)
        update(slot(1), v_at(off + tk), None, False)
        return carry

    lax.fori_loop(0, i * pairs_per_q, body, 0)

    rr = lax.broadcasted_iota(jnp.int32, (tk, tq), 0)
    cc = lax.broadcasted_iota(jnp.int32, (tk, tq), 1)
    for dp in range(pairs_per_q):
        off = pl.multiple_of(i * tq + dp * 2 * tk, 2 * tk)
        scores_into(1, off + tk)
        update(slot(0), v_at(off), rr + dp * 2 * tk <= cc, False)
        if dp + 1 < pairs_per_q:
            scores_into(0, off + 2 * tk)
        update(slot(1), v_at(off + tk), rr + (dp * 2 + 1) * tk <= cc, False)

    l_rows = jnp.concatenate([jnp.broadcast_to(l_ref[hh], (FOX_HEAD_DIM, tq)) for hh in heads], axis=0)
    o_ref[...] = (acc_ref[...] / l_rows).astype(BF16)


def _fox(q_t, k, v_t, kbias, k_p, v_tp, kbias_p, *, tq, tk, pairs):
    b, s, _ = k.shape
    ngroups = FOX_HEADS // (2 * pairs)
    nq = s // tq
    wl = pairs * LANES
    return pl.pallas_call(
        functools.partial(_fox_kernel, tq=tq, tk=tk, pairs=pairs),
        grid=(b, ngroups, nq),
        in_specs=[
            pl.BlockSpec((wl, tq), lambda bi, pg, i: (pg, bi * nq + i)),
            pl.BlockSpec((None, s, wl), lambda bi, pg, i: (bi, 0, pg)),
            pl.BlockSpec((2 * pairs * V_ROWS, s), lambda bi, pg, i: (pg, bi)),
            pl.BlockSpec((None, s, LANES), lambda bi, pg, i: (bi, 0, 0)),
            pl.BlockSpec((N_META, wl), lambda bi, pg, i: (0, pg)),
            pl.BlockSpec((2 * pairs * V_ROWS, N_META), lambda bi, pg, i: (pg, 0)),
            pl.BlockSpec((N_META, LANES), lambda bi, pg, i: (0, 0)),
        ],
        out_specs=pl.BlockSpec((wl, tq), lambda bi, pg, i: (pg, bi * nq + i)),
        out_shape=jax.ShapeDtypeStruct((FOX_WIDTH, b * s), BF16),
        scratch_shapes=[pltpu.VMEM((2 * pairs, 1, tq), F32), pltpu.VMEM((2 * pairs, 1, tq), F32),
                        pltpu.VMEM((wl, tq), F32), pltpu.VMEM((2, 2 * pairs, tk, tq), F32)],
        compiler_params=pltpu.CompilerParams(
            dimension_semantics=("arbitrary", "arbitrary", "arbitrary"),
            vmem_limit_bytes=VMEM_LIMIT),
        name="fox_attention",
    )(q_t, k, v_t, kbias, k_p, v_tp, kbias_p)


def _dn_kernel(qkv_ref, scol_ref, srow_ref, dzs_ref, onw_ref,
               s0_ref, odn_ref, sfin_ref, state_ref, *, tt):
    i = pl.program_id(1)
    nchunk = tt // DN_CHUNK

    @pl.when(i == 0)
    def _():
        for g in range(qkv_ref.shape[0]):
            state_ref[g] = s0_ref[...]

    bs = DN_BLOCK
    heads = range(DN_HEADS)
    groups = range(qkv_ref.shape[0])
    units = [(g, sb, h) for g in groups for sb in range(tt // bs) for h in heads]

    def slab(g, sb, c0):
        return qkv_ref[g, sb * bs:(sb + 1) * bs, c0:c0 + LANES]

    def small_col(g, sb, lane):
        return scol_ref[g, sb * bs:(sb + 1) * bs, lane:lane + 1]

    rr = lax.broadcasted_iota(jnp.int32, (bs, bs), 0)
    cc = lax.broadcasted_iota(jnp.int32, (bs, bs), 1)
    rc_xor = jnp.bitwise_xor(rr, cc)
    same = rc_xor < DN_CHUNK
    tril_m = jnp.logical_and(same, rr >= cc)
    strict_m = jnp.logical_and(same, rr > cc)
    eye = (rr == cc).astype(F32)

    qs = [slab(g, sb, h * LANES) for g, sb, h in units]
    ks = [slab(g, sb, DN_WIDTH + h * LANES) for g, sb, h in units]
    vs = [slab(g, sb, 2 * DN_WIDTH + h * LANES) for g, sb, h in units]
    states = {(g, h): state_ref[g, h] for g in groups for h in heads}
    betas = [small_col(g, sb, BETA_LANE + h) for g, sb, h in units]
    gcs = [small_col(g, sb, G_LANE + h) for g, sb, h in units]
    gls = [gcs[n] + small_col(g, sb, GSUF_LANE + h) - small_col(g, sb, GRAW_LANE + h)
           for n, (g, sb, h) in enumerate(units)]
    decays = [jnp.exp(jnp.where(
        tril_m, gcs[n] - srow_ref[g, G_LANE + h:G_LANE + h + 1, sb * bs:(sb + 1) * bs], NEG_INF))
        for n, (g, sb, h) in enumerate(units)]
    nu = range(len(units))
    egcs = [jnp.exp(g) for g in gcs]
    kbs = [ks[n] * betas[n] for n in nu]
    ks_b = [k.astype(BF16) for k in ks]
    lmats = [jnp.where(strict_m, _dot_nt(kbs[n].astype(BF16), ks_b[n]) * decays[n], 0.0) for n in nu]
    attns = [(_dot_nt(qs[n].astype(BF16), ks_b[n]) * decays[n]).astype(BF16) for n in nu]
    ainvs = [eye - jnp.where(rc_xor == 1, lm, 0.0) for lm in lmats]
    for lvl in range(1, 6):
        blk = 2 ** lvl
        lvl_m = jnp.logical_and(rc_xor >= blk, rc_xor < 2 * blk)
        ainvs_b = [a.astype(BF16) for a in ainvs]
        mids = [_dot(jnp.where(lvl_m, lmats[n], 0.0).astype(BF16), ainvs_b[n]).astype(BF16)
                for n in nu]
        ainvs = [ainvs[n] - _dot(ainvs_b[n], mids[n]) for n in nu]
    sols = [_dot(ainvs[n].astype(BF16),
                 jnp.concatenate([vs[n] * betas[n], kbs[n] * egcs[n]], axis=1).astype(BF16))
            for n in nu]
    uus = [sol[:, :DN_HEAD_DIM] for sol in sols]
    wws = [sol[:, DN_HEAD_DIM:] for sol in sols]
    qds = [qs[n] * egcs[n] for n in nu]
    kds = [(ks[n] * jnp.exp(gls[n] - gcs[n])).astype(BF16) for n in nu]

    vnews = [[] for _ in nu]
    qss = [[] for _ in nu]
    for c in range(nchunk):
        csb, lc = divmod(c, bs // DN_CHUNK)
        lo, hi = lc * DN_CHUNK, (lc + 1) * DN_CHUNK
        cur = [(n, (g, h)) for n, (g, sb, h) in enumerate(units) if sb == csb]
        rs = {n: _dot(jnp.concatenate([wws[n][lo:hi], qds[n][lo:hi]], axis=0).astype(BF16),
                      states[key].astype(BF16)) for n, key in cur}
        vns = {n: uus[n][lo:hi] - rs[n][:DN_CHUNK] for n, _ in cur}
        states.update({key: jnp.exp(gls[n][lo:lo + 1, :]) * states[key]
                       + _dot_tn(kds[n][lo:hi], vns[n].astype(BF16)) for n, key in cur})
        for n, _ in cur:
            qss[n].append(rs[n][DN_CHUNK:])
            vnews[n].append(vns[n])
    for (g, h), state in states.items():
        state_ref[g, h] = state
    for h in heads:
        sfin_ref[h] = states[(0, h)]
    for n, (g, sb, h) in enumerate(units):
        o = (jnp.concatenate(qss[n], axis=0)
             + _dot(attns[n], jnp.concatenate(vnews[n], axis=0).astype(BF16)))
        o = o * lax.rsqrt(jnp.mean(o * o, axis=-1, keepdims=True) + EPS) * onw_ref[...]
        o = o * dzs_ref[g, sb * bs:(sb + 1) * bs, h * LANES:(h + 1) * LANES]
        odn_ref[g, sb * bs:(sb + 1) * bs, h * LANES:(h + 1) * LANES] = o.astype(BF16)


def _deltanet(qkv, scol, srow, dzs, onw, s0, *, group, tt):
    b, s, _ = qkv.shape
    nt = s // tt
    return pl.pallas_call(
        functools.partial(_dn_kernel, tt=tt),
        grid=(b // group, nt),
        in_specs=[
            pl.BlockSpec((group, tt, 3 * DN_WIDTH), lambda bi, i: (bi, i, 0)),
            pl.BlockSpec((group, tt, LANES), lambda bi, i: (bi, i, 0)),
            pl.BlockSpec((group, SROW, tt), lambda bi, i: (bi, 0, i)),
            pl.BlockSpec((group, tt, DN_WIDTH), lambda bi, i: (bi, i, 0)),
            pl.BlockSpec((1, DN_HEAD_DIM), lambda bi, i: (0, 0)),
            pl.BlockSpec((DN_HEADS, DN_HEAD_DIM, DN_HEAD_DIM), lambda bi, i: (0, 0, 0)),
        ],
        out_specs=[pl.BlockSpec((group, tt, DN_WIDTH), lambda bi, i: (bi, i, 0)),
                   pl.BlockSpec((DN_HEADS, DN_HEAD_DIM, DN_HEAD_DIM), lambda bi, i: (0, 0, 0))],
        out_shape=[jax.ShapeDtypeStruct((b, s, DN_WIDTH), BF16),
                   jax.ShapeDtypeStruct((DN_HEADS, DN_HEAD_DIM, DN_HEAD_DIM), F32)],
        scratch_shapes=[pltpu.VMEM((group, DN_HEADS, DN_HEAD_DIM, DN_HEAD_DIM), F32)],
        compiler_params=pltpu.CompilerParams(dimension_semantics=("arbitrary", "arbitrary"),
                                             vmem_limit_bytes=VMEM_LIMIT),
        name="deltanet",
    )(qkv, scol, srow, dzs, onw, s0)


def _tail_kernel(x_ref, ofox_ref, odn_ref, gates_ref, wbf_ref, wbd_ref, wout_ref, fnw_ref,
                 wg_ref, wu_ref, wd_ref, finw_ref, o_ref, *, ff_chunk):
    a = _dot_tn(ofox_ref[...], wbf_ref[...])
    bb = _dot(odn_ref[...], wbd_ref[...])
    y = gates_ref[:, :D_MODEL] * a + gates_ref[:, D_MODEL:] * bb
    h1 = x_ref[...] + _dot(y.astype(BF16), wout_ref[...])
    n = (h1 * lax.rsqrt(jnp.mean(h1 * h1, axis=-1, keepdims=True) + EPS) * fnw_ref[...]).astype(BF16)
    acc = h1
    for c in range(D_FF // ff_chunk):
        lo, hi = c * ff_chunk, (c + 1) * ff_chunk
        gt = _dot(n, wg_ref[:, lo:hi])
        up = _dot(n, wu_ref[:, lo:hi])
        act = (gt * _sigmoid(gt) * up).astype(BF16)
        acc = acc + _dot(act, wd_ref[lo:hi, :])
    o_ref[...] = acc * lax.rsqrt(jnp.mean(acc * acc, axis=-1, keepdims=True) + EPS) * finw_ref[...]


def _tail(x2d, ofox, odn, gates, wbf, wbd, wout, fnw, wg, wu, wd, finw, *, tm, ff_chunk):
    m = x2d.shape[0]
    row = lambda w: pl.BlockSpec((tm, w), lambda i: (i, 0))
    return pl.pallas_call(
        functools.partial(_tail_kernel, ff_chunk=ff_chunk),
        grid=(m // tm,),
        in_specs=[row(D_MODEL), pl.BlockSpec((FOX_WIDTH, tm), lambda i: (0, i)), row(DN_WIDTH),
                  row(2 * D_MODEL),
                  _const_spec((FOX_WIDTH, D_MODEL)), _const_spec((DN_WIDTH, D_MODEL)),
                  _const_spec((D_MODEL, D_MODEL)), _const_spec((1, D_MODEL)),
                  _const_spec((D_MODEL, D_FF)), _const_spec((D_MODEL, D_FF)),
                  _const_spec((D_FF, D_MODEL)), _const_spec((1, D_MODEL))],
        out_specs=row(D_MODEL),
        out_shape=jax.ShapeDtypeStruct((m, D_MODEL), F32),
        compiler_params=pltpu.CompilerParams(dimension_semantics=("arbitrary",),
                                             vmem_limit_bytes=VMEM_LIMIT),
        name="merge_ffn",
    )(x2d, ofox, odn, gates, wbf, wbd, wout, fnw, wg, wu, wd, finw)


def _pick_tile(n, pref):
    t = min(pref, n)
    while n % t:
        t //= 2
    return t


def kernel(x, meta_tokens, mix_norm_w, w_in, fox_forget_bias, dn_conv_w, dn_a_log, dn_dt_bias,
           dn_out_norm_w, w_branch_fox, w_branch_dn, w_out, ffn_norm_w, w_ffn_gate, w_ffn_up,
           w_ffn_down, final_norm_w):
    b, s, _ = x.shape
    assert mix_norm_w.shape[0] == 1, "single layer only"
    assert s % PREFIX == 0
    m = b * s

    wi = w_in[0]
    o_small0 = 3 * FOX_WIDTH
    o_dn = o_small0 + FOX_HEADS
    o_small1 = o_dn + 3 * DN_WIDTH
    o_rest = o_small1 + 2 * DN_HEADS
    weights = (wi[:, :FOX_WIDTH].T.astype(BF16), wi[:, 2 * FOX_WIDTH:o_small0].T.astype(BF16),
               wi[:, FOX_WIDTH:2 * FOX_WIDTH].astype(BF16), wi[:, o_dn:o_small1].astype(BF16),
               wi[:, o_rest:].astype(BF16))
    w_alogit = wi[:, o_small1 + DN_HEADS:o_rest]
    wsmall = jnp.concatenate([wi[:, o_small0:o_dn], wi[:, o_small1:o_small1 + DN_HEADS],
                              w_alogit, w_alogit, w_alogit], axis=1)
    wsmall = jnp.pad(wsmall, ((0, 0), (0, LANES - N_SMALL))).astype(BF16)
    bias = jnp.zeros((SUBLANES, LANES), F32)
    bias = bias.at[0, LOGF_LANE:LOGF_LANE + FOX_HEADS].set(fox_forget_bias[0].astype(F32))
    bias = bias.at[0, G_LANE:N_SMALL].set(jnp.tile(dn_dt_bias[0].astype(F32), 3))
    bias = bias.at[1, G_LANE:N_SMALL].set(jnp.tile(dn_a_log[0].astype(F32), 3))
    nw = mix_norm_w[0].reshape(1, D_MODEL).astype(F32)

    x_p = jnp.concatenate([jnp.zeros((N_PAD, D_MODEL), F32), meta_tokens.astype(F32)], axis=0)
    convw = dn_conv_w[0].astype(F32)
    _, k_p, v_tp, kbias_p, qkv_p, dzs_p, _, scol_p, srow_p, conv_tail_p = _inproj(
        x_p, nw, weights, wsmall, bias, jnp.zeros((1, LANES), F32), convw,
        jnp.zeros((SUBLANES, 3 * DN_WIDTH), F32), tm=PREFIX, tiles_per_batch=1, n_pad=N_PAD)

    tm = _pick_tile(s, 512)
    q_t, k, v_t, kbias, qkv, dzs, gates, scol, srow3, _ = _inproj(
        x.reshape(m, D_MODEL), nw, weights, wsmall, bias, scol_p[PREFIX - 1:PREFIX, :], convw,
        conv_tail_p, tm=tm, tiles_per_batch=s // tm, n_pad=0)

    scol3 = scol.reshape(b, s, LANES)
    tq = _pick_tile(s, 512)
    ofox_t = _fox(q_t, k.reshape(b, s, FOX_WIDTH), v_t, kbias.reshape(b, s, LANES),
                  k_p[N_PAD:], v_tp[:, N_PAD:], kbias_p[N_PAD:], tq=tq, tk=min(256, tq // 2),
                  pairs=FOX_PAIRS_PER_STEP)

    onw = dn_out_norm_w[0].reshape(1, DN_HEAD_DIM).astype(F32)
    _, s_prefix = _deltanet(
        qkv_p[None], scol_p[None], srow_p, dzs_p[None], onw,
        jnp.zeros((DN_HEADS, DN_HEAD_DIM, DN_HEAD_DIM), F32), group=1, tt=PREFIX)
    group = _pick_tile(b, DN_UNITS // DN_HEADS)
    odn, _ = _deltanet(
        qkv.reshape(b, s, 3 * DN_WIDTH), scol3, srow3, dzs.reshape(b, s, DN_WIDTH), onw, s_prefix,
        group=group, tt=_pick_tile(s, DN_UNITS // (DN_HEADS * group) * DN_BLOCK))

    out = _tail(
        x.reshape(m, D_MODEL), ofox_t, odn.reshape(m, DN_WIDTH), gates,
        w_branch_fox[0].astype(BF16), w_branch_dn[0].astype(BF16), w_out[0].astype(BF16),
        ffn_norm_w[0].reshape(1, D_MODEL).astype(F32),
        w_ffn_gate[0].astype(BF16), w_ffn_up[0].astype(BF16), w_ffn_down[0].astype(BF16),
        final_norm_w.reshape(1, D_MODEL).astype(F32),
        tm=_pick_tile(m, 512), ff_chunk=D_FF // 2)
    return out.reshape(b, s, D_MODEL)
```

```python
import functools

import jax
import jax.numpy as jnp
from jax import lax
from jax.experimental import pallas as pl
from jax.experimental.pallas import tpu as pltpu

F32 = jnp.float32
BF16 = jnp.bfloat16
HIGHEST = lax.Precision.HIGHEST

D_MODEL = 1024
N_META = 16
PREFIX = 128
N_PAD = PREFIX - N_META
FOX_HEADS = 8
FOX_HEAD_DIM = 64
FOX_WIDTH = FOX_HEADS * FOX_HEAD_DIM
V_ROWS = FOX_HEAD_DIM + 8
FOX_PAIRS_PER_STEP = 4
DN_HEADS = 4
DN_HEAD_DIM = 128
DN_WIDTH = DN_HEADS * DN_HEAD_DIM
DN_CHUNK = 64
DN_BLOCK = 2 * DN_CHUNK
DN_UNITS = 16
CONV_WIDTH = 4
D_FF = 2816
MXU_DIM = 256
FF_BOUNDS = (0, (D_FF // MXU_DIM + 1) // 2 * MXU_DIM, D_FF)
EPS = 1e-6
NEG_INF = -1e30
LOG2E = 1.4426950408889634

LANES = 128
SUBLANES = 8
COL_CHUNK = 512
LOGF_LANE = 0
BETA_LANE = FOX_HEADS
G_LANE = BETA_LANE + DN_HEADS
GSUF_LANE = G_LANE + DN_HEADS
GRAW_LANE = GSUF_LANE + DN_HEADS
N_SMALL = GRAW_LANE + DN_HEADS
SROW = 24
VMEM_LIMIT = 56 * 1024 * 1024


def _const_spec(shape):
    nd = len(shape)
    return pl.BlockSpec(shape, lambda *_: (0,) * nd, pipeline_mode=pl.Buffered(1))


def _sigmoid(x):
    return 0.5 * jnp.tanh(0.5 * x) + 0.5


def _dot(a, b, **kw):
    return jnp.dot(a, b, preferred_element_type=F32, **kw)


def _dot_nt(a, b, **kw):
    return lax.dot_general(a, b, (((1,), (1,)), ((), ())), preferred_element_type=F32, **kw)


def _dot_tn(a, b, **kw):
    return lax.dot_general(a, b, (((0,), (0,)), ((), ())), preferred_element_type=F32, **kw)


def _inproj_kernel(x_ref, nw_ref, wq_t_ref, wv_t_ref, wk_ref, wdn_ref, wrest_ref, wsmall_ref, bias_ref,
                   carry0_ref,
                   convw_ref, halo0_ref,
                   q_t_ref, k_ref, v_t_ref, kbias_ref, dqkv_ref, dzs_ref, gates_ref, scol_ref,
                   srow_ref, tail_ref, carry_ref, halo_ref, conv_ref, *, tm, tiles_per_batch, n_pad):
    i = pl.program_id(0)

    @pl.when(i % tiles_per_batch == 0)
    def _():
        carry_ref[...] = carry0_ref[...]
        halo_ref[...] = halo0_ref[...]

    x = x_ref[...]
    ms = jnp.mean(x * x, axis=-1, keepdims=True)
    hn = (x * lax.rsqrt(ms + EPS) * nw_ref[...]).astype(BF16)

    z = _dot(hn, wsmall_ref[...]) + bias_ref[0:1, :]
    lane = lax.broadcasted_iota(jnp.int32, (tm, LANES), 1)
    row = lax.broadcasted_iota(jnp.int32, (tm, LANES), 0)
    e = jnp.exp(-jnp.abs(z))
    l1p = jnp.log1p(e)
    logf = jnp.minimum(z, 0.0) - l1p
    softplus = jnp.maximum(z, 0.0) + l1p
    sig = jnp.where(z >= 0.0, 1.0, e) / (1.0 + e)
    g = -jnp.exp(bias_ref[1:2, :]) * softplus
    if n_pad:
        vm = (row + (i % tiles_per_batch) * tm >= n_pad).astype(F32)
        sig = sig * vm
        g = g * vm
    val = jnp.where(lane < BETA_LANE, logf,
                    jnp.where(lane < G_LANE, sig, jnp.where(lane < N_SMALL, g, 0.0)))
    in_chunk = row % DN_CHUNK
    is_logf = lane < BETA_LANE
    is_gpre = jnp.logical_and(lane >= G_LANE, lane < GSUF_LANE)
    is_gsuf = jnp.logical_and(lane >= GSUF_LANE, lane < GRAW_LANE)
    scan = val
    sh = 1
    while sh < tm:
        take_up = jnp.logical_and(is_logf, row >= sh)
        if sh < DN_CHUNK:
            take_up = jnp.logical_or(take_up, jnp.logical_and(is_gpre, in_chunk >= sh))
            take_down = jnp.logical_and(is_gsuf, in_chunk + sh < DN_CHUNK)
            below = jnp.where(take_down, pltpu.roll(scan, tm - sh, axis=0), 0.0)
        else:
            below = 0.0
        scan = scan + jnp.where(take_up, pltpu.roll(scan, sh, axis=0), below)
        sh *= 2
    out = scan + jnp.where(is_logf, carry_ref[...], 0.0)
    carry_ref[...] = out[tm - 1:tm, :]
    scol_ref[...] = out
    srow_ref[...] = out.T[:SROW, :]
    nc = jnp.where(is_logf, out * -LOG2E, 0.0)
    hi = nc.astype(BF16).astype(F32)
    mid = (nc - hi).astype(BF16).astype(F32)
    lo = nc - hi - mid
    kbias = hi + pltpu.roll(mid, FOX_HEADS, axis=1) + pltpu.roll(lo, 2 * FOX_HEADS, axis=1)
    kbias_ref[...] = kbias.astype(BF16)

    def main_chunk(c):
        if c == 0:
            return lambda: _dot(hn, wk_ref[...])
        ref, first = (wdn_ref, 1) if c < 4 else (wrest_ref, 4)
        return lambda: _dot(hn, ref[:, (c - first) * COL_CHUNK:(c - first + 1) * COL_CHUNK])

    def store_q_t(acc):
        q_t_ref[...] = (acc * (FOX_HEAD_DIM ** -0.5 * LOG2E)).astype(BF16)

    def store_v_t(acc):
        v_t = acc.astype(BF16)
        ones = jnp.ones((V_ROWS - FOX_HEAD_DIM, tm), BF16)
        v_t_ref[...] = jnp.concatenate(
            [piece for h in range(FOX_HEADS)
             for piece in (v_t[h * FOX_HEAD_DIM:(h + 1) * FOX_HEAD_DIM], ones)], axis=0)

    def store_k(acc):
        k_ref[...] = acc.astype(BF16)

    def store_dn(which):
        def epilogue(acc):
            cols = slice(which * COL_CHUNK, (which + 1) * COL_CHUNK)
            if n_pad:
                acc = acc * vm[:, :1]
            conv_ref[which, 0:SUBLANES, :] = halo_ref[:, cols]
            conv_ref[which, SUBLANES:, :] = acc
            halo_ref[:, cols] = acc[tm - SUBLANES:, :]
            tail_ref[:, cols] = acc[tm - SUBLANES:, :]
            a = convw_ref[CONV_WIDTH - 1:CONV_WIDTH, cols] * acc
            for t in range(CONV_WIDTH - 1):
                start = SUBLANES - (CONV_WIDTH - 1) + t
                a = a + convw_ref[t:t + 1, cols] * conv_ref[which, start:start + tm, :]
            a = a * _sigmoid(a)
            if which == 2:
                dqkv_ref[:, cols] = a
            else:
                scale = DN_HEAD_DIM ** -0.5 if which == 0 else 1.0
                for h in range(DN_HEADS):
                    ah = a[:, h * LANES:(h + 1) * LANES]
                    inv = lax.rsqrt(jnp.sum(ah * ah, axis=-1, keepdims=True) + EPS) * scale
                    lo_col = which * COL_CHUNK + h * LANES
                    dqkv_ref[:, lo_col:lo_col + LANES] = ah * inv
        return epilogue

    def store_dzs(acc):
        dzs_ref[...] = acc * _sigmoid(acc)

    def store_gates(j):
        def epilogue(acc):
            gates_ref[:, j * COL_CHUNK:(j + 1) * COL_CHUNK] = _sigmoid(acc)
        return epilogue

    store_q_t(_dot_nt(wq_t_ref[...], hn))
    store_v_t(_dot_nt(wv_t_ref[...], hn))
    store_k(main_chunk(0)())
    for which in range(3):
        store_dn(which)(main_chunk(1 + which)())
    store_dzs(main_chunk(4)())
    for j in range(2 * D_MODEL // COL_CHUNK):
        store_gates(j)(main_chunk(5 + j)())


def _inproj(x2d, nw, weights, wsmall, bias, carry0, convw, halo0, *, tm, tiles_per_batch, n_pad):
    m = x2d.shape[0]
    row = lambda w: pl.BlockSpec((tm, w), lambda i: (i, 0))
    col = lambda h: pl.BlockSpec((h, tm), lambda i: (0, i))
    return pl.pallas_call(
        functools.partial(_inproj_kernel, tm=tm, tiles_per_batch=tiles_per_batch, n_pad=n_pad),
        grid=(m // tm,),
        in_specs=[row(D_MODEL), _const_spec((1, D_MODEL))] + [_const_spec(w.shape) for w in weights] + [
                  _const_spec((D_MODEL, LANES)), _const_spec((SUBLANES, LANES)),
                  _const_spec((1, LANES)), _const_spec((CONV_WIDTH, 3 * DN_WIDTH)),
                  _const_spec((SUBLANES, 3 * DN_WIDTH))],
        out_specs=[col(FOX_WIDTH), row(FOX_WIDTH), col(FOX_HEADS * V_ROWS), row(LANES),
                   row(3 * DN_WIDTH), row(DN_WIDTH), row(2 * D_MODEL),
                   row(LANES),
                   pl.BlockSpec((None, SROW, tm),
                                lambda i: (i // tiles_per_batch, 0, i % tiles_per_batch)),
                   pl.BlockSpec((SUBLANES, 3 * DN_WIDTH), lambda i: (0, 0))],
        out_shape=[jax.ShapeDtypeStruct((FOX_WIDTH, m), BF16),
                   jax.ShapeDtypeStruct((m, FOX_WIDTH), BF16),
                   jax.ShapeDtypeStruct((FOX_HEADS * V_ROWS, m), BF16),
                   jax.ShapeDtypeStruct((m, LANES), BF16),
                   jax.ShapeDtypeStruct((m, 3 * DN_WIDTH), F32),
                   jax.ShapeDtypeStruct((m, DN_WIDTH), F32),
                   jax.ShapeDtypeStruct((m, 2 * D_MODEL), F32),
                   jax.ShapeDtypeStruct((m, LANES), F32),
                   jax.ShapeDtypeStruct((m // (tm * tiles_per_batch), SROW,
                                         tm * tiles_per_batch), F32),
                   jax.ShapeDtypeStruct((SUBLANES, 3 * DN_WIDTH), F32)],
        scratch_shapes=[pltpu.VMEM((1, LANES), F32), pltpu.VMEM((SUBLANES, 3 * DN_WIDTH), F32),
                        pltpu.VMEM((3, tm + SUBLANES, COL_CHUNK), F32)],
        compiler_params=pltpu.CompilerParams(dimension_semantics=("arbitrary",),
                                             vmem_limit_bytes=VMEM_LIMIT),
        name="inproj",
    )(x2d, nw, *weights, wsmall, bias, carry0, convw, halo0)


def _fox_kernel(q_t_ref, k_ref, v_t_ref, kb_ref, kp_ref, v_tp_ref, kbp_ref,
                o_ref, m_ref, l_ref, acc_ref, t_ref, *, tq, tk, pairs):
    pg = pl.program_id(1)
    i = pl.program_id(2)
    heads = range(2 * pairs)
    sub = lax.broadcasted_iota(jnp.int32, (LANES, 1), 0)
    rhs = []
    for hh in heads:
        pp, h = divmod(hh, 2)
        q_t = q_t_ref[pp * LANES:(pp + 1) * LANES, :]
        mine = sub < FOX_HEAD_DIM if h == 0 else sub >= FOX_HEAD_DIM
        head = 2 * (pg * pairs + pp) + h
        sel = jnp.where(jnp.logical_and(sub < 3 * FOX_HEADS, (sub % FOX_HEADS) == head),
                        1.0, 0.0).astype(BF16) * jnp.ones((1, tq), BF16)
        rhs.append(jnp.concatenate([jnp.where(mine, q_t, jnp.zeros_like(q_t)), sel], axis=0))

    def scores(kt, kbt):
        lhs = [jnp.concatenate([kt[:, pp * LANES:(pp + 1) * LANES], kbt], axis=1)
               for pp in range(pairs)]
        return [_dot(lhs[hh // 2], rhs[hh]) for hh in heads]

    def scores_into(slot, off):
        ts = scores(k_ref[pl.ds(off, tk), :], kb_ref[pl.ds(off, tk), :])
        for hh in heads:
            t_ref[slot, hh] = ts[hh]

    def update(read_t, v_aug, mask, first):
        def masked_t(hh):
            t = read_t(hh)
            return t if mask is None else jnp.where(mask, t, NEG_INF)

        t_max = [jnp.max(masked_t(hh), axis=0, keepdims=True) for hh in heads]
        if first:
            m_new = t_max
        else:
            m_prev = [m_ref[hh] for hh in heads]
            m_new = [jnp.maximum(m_prev[hh], t_max[hh]) for hh in heads]
        pm = [jnp.exp2(masked_t(hh) - m_new[hh]).astype(BF16) for hh in heads]
        r = [_dot(v_aug[hh * V_ROWS:(hh + 1) * V_ROWS, :], pm[hh]) for hh in heads]
        pv = jnp.concatenate([r[hh][:FOX_HEAD_DIM] for hh in heads], axis=0)
        psum = [r[hh][FOX_HEAD_DIM:FOX_HEAD_DIM + 1] for hh in heads]
        if first:
            for hh in heads:
                l_ref[hh] = psum[hh]
            acc_ref[...] = pv
        else:
            alpha = [jnp.exp2(m_prev[hh] - m_new[hh]) for hh in heads]
            for hh in heads:
                l_ref[hh] = alpha[hh] * l_ref[hh] + psum[hh]
            alpha_rows = jnp.concatenate(
                [jnp.broadcast_to(alpha[hh], (FOX_HEAD_DIM, tq)) for hh in heads], axis=0)
            acc_ref[...] = alpha_rows * acc_ref[...] + pv
        for hh in heads:
            m_ref[hh] = m_new[hh]

    ts_p = scores(kp_ref[...], kbp_ref[...])
    update(lambda hh: ts_p[hh], v_tp_ref[...], None, True)

    def v_at(off):
        return v_t_ref[:, pl.ds(off, tk)]

    def slot(n):
        return lambda hh: t_ref[n, hh]

    pairs_per_q = tq // (2 * tk)
    scores_into(0, 0)

    def body(jj, carry):
        off = pl.multiple_of(jj * 2 * tk, 2 * tk)
        scores_into(1, off + tk)
        update(slot(0), v_at(off), None, False)
        scores_into(0, off + 2 * tk)
        update(slot(1), v_at(off + tk), None, False)
        return carry

    lax.fori_loop(0, i * pairs_per_q, body, 0)

    rr = lax.broadcasted_iota(jnp.int32, (tk, tq), 0)
    cc = lax.broadcasted_iota(jnp.int32, (tk, tq), 1)
    for dp in range(pairs_per_q):
        off = pl.multiple_of(i * tq + dp * 2 * tk, 2 * tk)
        scores_into(1, off + tk)
        update(slot(0), v_at(off), rr + dp * 2 * tk <= cc, False)
        if dp + 1 < pairs_per_q:
            scores_into(0, off + 2 * tk)
        update(slot(1), v_at(off + tk), rr + (dp * 2 + 1) * tk <= cc, False)

    l_rows = jnp.concatenate([jnp.broadcast_to(l_ref[hh], (FOX_HEAD_DIM, tq)) for hh in heads], axis=0)
    o_ref[...] = (acc_ref[...] / l_rows).astype(BF16)


def _fox(q_t, k, v_t, kbias, k_p, v_tp, kbias_p, *, tq, tk, pairs):
    b, s, _ = k.shape
    ngroups = FOX_HEADS // (2 * pairs)
    nq = s // tq
    wl = pairs * LANES
    return pl.pallas_call(
        functools.partial(_fox_kernel, tq=tq, tk=tk, pairs=pairs),
        grid=(b, ngroups, nq),
        in_specs=[
            pl.BlockSpec((wl, tq), lambda bi, pg, i: (pg, bi * nq + i)),
            pl.BlockSpec((None, s, wl), lambda bi, pg, i: (bi, 0, pg)),
            pl.BlockSpec((2 * pairs * V_ROWS, s), lambda bi, pg, i: (pg, bi)),
            pl.BlockSpec((None, s, LANES), lambda bi, pg, i: (bi, 0, 0)),
            pl.BlockSpec((N_META, wl), lambda bi, pg, i: (0, pg)),
            pl.BlockSpec((2 * pairs * V_ROWS, N_META), lambda bi, pg, i: (pg, 0)),
            pl.BlockSpec((N_META, LANES), lambda bi, pg, i: (0, 0)),
        ],
        out_specs=pl.BlockSpec((wl, tq), lambda bi, pg, i: (pg, bi * nq + i)),
        out_shape=jax.ShapeDtypeStruct((FOX_WIDTH, b * s), BF16),
        scratch_shapes=[pltpu.VMEM((2 * pairs, 1, tq), F32), pltpu.VMEM((2 * pairs, 1, tq), F32),
                        pltpu.VMEM((wl, tq), F32), pltpu.VMEM((2, 2 * pairs, tk, tq), F32)],
        compiler_params=pltpu.CompilerParams(
            dimension_semantics=("arbitrary", "arbitrary", "arbitrary"),
            vmem_limit_bytes=VMEM_LIMIT),
        name="fox_attention",
    )(q_t, k, v_t, kbias, k_p, v_tp, kbias_p)


def _dn_kernel(qkv_ref, scol_ref, srow_ref, dzs_ref, onw_ref,
               s0_ref, odn_ref, sfin_ref, state_ref, *, tt):
    i = pl.program_id(1)
    nchunk = tt // DN_CHUNK

    @pl.when(i == 0)
    def _():
        for g in range(qkv_ref.shape[0]):
            state_ref[g] = s0_ref[...]

    bs = DN_BLOCK
    heads = range(DN_HEADS)
    groups = range(qkv_ref.shape[0])
    units = [(g, sb, h) for g in groups for sb in range(tt // bs) for h in heads]

    def slab(g, sb, c0):
        return qkv_ref[g, sb * bs:(sb + 1) * bs, c0:c0 + LANES]

    def small_col(g, sb, lane):
        return scol_ref[g, sb * bs:(sb + 1) * bs, lane:lane + 1]

    rr = lax.broadcasted_iota(jnp.int32, (bs, bs), 0)
    cc = lax.broadcasted_iota(jnp.int32, (bs, bs), 1)
    rc_xor = jnp.bitwise_xor(rr, cc)
    same = rc_xor < DN_CHUNK
    tril_m = jnp.logical_and(same, rr >= cc)
    strict_m = jnp.logical_and(same, rr > cc)
    eye = (rr == cc).astype(F32)

    qs = [slab(g, sb, h * LANES) for g, sb, h in units]
    ks = [slab(g, sb, DN_WIDTH + h * LANES) for g, sb, h in units]
    vs = [slab(g, sb, 2 * DN_WIDTH + h * LANES) for g, sb, h in units]
    states = {(g, h): state_ref[g, h] for g in groups for h in heads}
    betas = [small_col(g, sb, BETA_LANE + h) for g, sb, h in units]
    gcs = [small_col(g, sb, G_LANE + h) for g, sb, h in units]
    gls = [gcs[n] + small_col(g, sb, GSUF_LANE + h) - small_col(g, sb, GRAW_LANE + h)
           for n, (g, sb, h) in enumerate(units)]
    decays = [jnp.exp(jnp.where(
        tril_m, gcs[n] - srow_ref[g, G_LANE + h:G_LANE + h + 1, sb * bs:(sb + 1) * bs], NEG_INF))
        for n, (g, sb, h) in enumerate(units)]
    nu = range(len(units))
    egcs = [jnp.exp(g) for g in gcs]
    kbs = [ks[n] * betas[n] for n in nu]
    ks_b = [k.astype(BF16) for k in ks]
    lmats = [jnp.where(strict_m, _dot_nt(kbs[n].astype(BF16), ks_b[n]) * decays[n], 0.0) for n in nu]
    attns = [(_dot_nt(qs[n].astype(BF16), ks_b[n]) * decays[n]).astype(BF16) for n in nu]
    ainvs = [eye - jnp.where(rc_xor == 1, lm, 0.0) for lm in lmats]
    for lvl in range(1, 6):
        blk = 2 ** lvl
        lvl_m = jnp.logical_and(rc_xor >= blk, rc_xor < 2 * blk)
        ainvs_b = [a.astype(BF16) for a in ainvs]
        mids = [_dot(jnp.where(lvl_m, lmats[n], 0.0).astype(BF16), ainvs_b[n]).astype(BF16)
                for n in nu]
        ainvs = [ainvs[n] - _dot(ainvs_b[n], mids[n]) for n in nu]
    sols = [_dot(ainvs[n].astype(BF16),
                 jnp.concatenate([vs[n] * betas[n], kbs[n] * egcs[n]], axis=1).astype(BF16))
            for n in nu]
    uus = [sol[:, :DN_HEAD_DIM] for sol in sols]
    wws = [sol[:, DN_HEAD_DIM:] for sol in sols]
    qds = [qs[n] * egcs[n] for n in nu]
    kds = [(ks[n] * jnp.exp(gls[n] - gcs[n])).astype(BF16) for n in nu]

    vnews = [[] for _ in nu]
    qss = [[] for _ in nu]
    for c in range(nchunk):
        csb, lc = divmod(c, bs // DN_CHUNK)
        lo, hi = lc * DN_CHUNK, (lc + 1) * DN_CHUNK
        cur = [(n, (g, h)) for n, (g, sb, h) in enumerate(units) if sb == csb]
        rs = {n: _dot(jnp.concatenate([wws[n][lo:hi], qds[n][lo:hi]], axis=0).astype(BF16),
                      states[key].astype(BF16)) for n, key in cur}
        vns = {n: uus[n][lo:hi] - rs[n][:DN_CHUNK] for n, _ in cur}
        states.update({key: jnp.exp(gls[n][lo:lo + 1, :]) * states[key]
                       + _dot_tn(kds[n][lo:hi], vns[n].astype(BF16)) for n, key in cur})
        for n, _ in cur:
            qss[n].append(rs[n][DN_CHUNK:])
            vnews[n].append(vns[n])
    for (g, h), state in states.items():
        state_ref[g, h] = state
    for h in heads:
        sfin_ref[h] = states[(0, h)]
    for n, (g, sb, h) in enumerate(units):
        o = (jnp.concatenate(qss[n], axis=0)
             + _dot(attns[n], jnp.concatenate(vnews[n], axis=0).astype(BF16)))
        o = o * lax.rsqrt(jnp.mean(o * o, axis=-1, keepdims=True) + EPS) * onw_ref[...]
        o = o * dzs_ref[g, sb * bs:(sb + 1) * bs, h * LANES:(h + 1) * LANES]
        odn_ref[g, sb * bs:(sb + 1) * bs, h * LANES:(h + 1) * LANES] = o.astype(BF16)


def _deltanet(qkv, scol, srow, dzs, onw, s0, *, group, tt):
    b, s, _ = qkv.shape
    nt = s // tt
    return pl.pallas_call(
        functools.partial(_dn_kernel, tt=tt),
        grid=(b // group, nt),
        in_specs=[
            pl.BlockSpec((group, tt, 3 * DN_WIDTH), lambda bi, i: (bi, i, 0)),
            pl.BlockSpec((group, tt, LANES), lambda bi, i: (bi, i, 0)),
            pl.BlockSpec((group, SROW, tt), lambda bi, i: (bi, 0, i)),
            pl.BlockSpec((group, tt, DN_WIDTH), lambda bi, i: (bi, i, 0)),
            pl.BlockSpec((1, DN_HEAD_DIM), lambda bi, i: (0, 0)),
            pl.BlockSpec((DN_HEADS, DN_HEAD_DIM, DN_HEAD_DIM), lambda bi, i: (0, 0, 0)),
        ],
        out_specs=[pl.BlockSpec((group, tt, DN_WIDTH), lambda bi, i: (bi, i, 0)),
                   pl.BlockSpec((DN_HEADS, DN_HEAD_DIM, DN_HEAD_DIM), lambda bi, i: (0, 0, 0))],
        out_shape=[jax.ShapeDtypeStruct((b, s, DN_WIDTH), BF16),
                   jax.ShapeDtypeStruct((DN_HEADS, DN_HEAD_DIM, DN_HEAD_DIM), F32)],
        scratch_shapes=[pltpu.VMEM((group, DN_HEADS, DN_HEAD_DIM, DN_HEAD_DIM), F32)],
        compiler_params=pltpu.CompilerParams(dimension_semantics=("arbitrary", "arbitrary"),
                                             vmem_limit_bytes=VMEM_LIMIT),
        name="deltanet",
    )(qkv, scol, srow, dzs, onw, s0)


def _tail_kernel(x_ref, ofox_ref, odn_ref, gates_ref, wbf_ref, wbd_ref, wout_ref, fnw_ref,
                 wg_ref, wu_ref, wd_ref, finw_ref, o_ref, *, ff_bounds):
    a = _dot_tn(ofox_ref[...], wbf_ref[...])
    bb = _dot(odn_ref[...], wbd_ref[...])
    y = gates_ref[:, :D_MODEL] * a + gates_ref[:, D_MODEL:] * bb
    h1 = x_ref[...] + _dot(y.astype(BF16), wout_ref[...])
    n = (h1 * lax.rsqrt(jnp.mean(h1 * h1, axis=-1, keepdims=True) + EPS) * fnw_ref[...]).astype(BF16)
    acc = h1
    for lo, hi in zip(ff_bounds[:-1], ff_bounds[1:]):
        gt = _dot(n, wg_ref[:, lo:hi])
        up = _dot(n, wu_ref[:, lo:hi])
        act = (gt * _sigmoid(gt) * up).astype(BF16)
        acc = acc + _dot(act, wd_ref[lo:hi, :])
    o_ref[...] = acc * lax.rsqrt(jnp.mean(acc * acc, axis=-1, keepdims=True) + EPS) * finw_ref[...]


def _tail(x2d, ofox, odn, gates, wbf, wbd, wout, fnw, wg, wu, wd, finw, *, tm, ff_bounds):
    m = x2d.shape[0]
    row = lambda w: pl.BlockSpec((tm, w), lambda i: (i, 0))
    return pl.pallas_call(
        functools.partial(_tail_kernel, ff_bounds=ff_bounds),
        grid=(m // tm,),
        in_specs=[row(D_MODEL), pl.BlockSpec((FOX_WIDTH, tm), lambda i: (0, i)), row(DN_WIDTH),
                  row(2 * D_MODEL),
                  _const_spec((FOX_WIDTH, D_MODEL)), _const_spec((DN_WIDTH, D_MODEL)),
                  _const_spec((D_MODEL, D_MODEL)), _const_spec((1, D_MODEL)),
                  _const_spec((D_MODEL, D_FF)), _const_spec((D_MODEL, D_FF)),
                  _const_spec((D_FF, D_MODEL)), _const_spec((1, D_MODEL))],
        out_specs=row(D_MODEL),
        out_shape=jax.ShapeDtypeStruct((m, D_MODEL), F32),
        compiler_params=pltpu.CompilerParams(dimension_semantics=("arbitrary",),
                                             vmem_limit_bytes=VMEM_LIMIT),
        name="merge_ffn",
    )(x2d, ofox, odn, gates, wbf, wbd, wout, fnw, wg, wu, wd, finw)


def _pick_tile(n, pref):
    t = min(pref, n)
    while n % t:
        t //= 2
    return t


def kernel(x, meta_tokens, mix_norm_w, w_in, fox_forget_bias, dn_conv_w, dn_a_log, dn_dt_bias,
           dn_out_norm_w, w_branch_fox, w_branch_dn, w_out, ffn_norm_w, w_ffn_gate, w_ffn_up,
           w_ffn_down, final_norm_w):
    b, s, _ = x.shape
    assert mix_norm_w.shape[0] == 1, "single layer only"
    assert s % PREFIX == 0
    m = b * s

    wi = w_in[0]
    o_small0 = 3 * FOX_WIDTH
    o_dn = o_small0 + FOX_HEADS
    o_small1 = o_dn + 3 * DN_WIDTH
    o_rest = o_small1 + 2 * DN_HEADS
    weights = (wi[:, :FOX_WIDTH].T.astype(BF16), wi[:, 2 * FOX_WIDTH:o_small0].T.astype(BF16),
               wi[:, FOX_WIDTH:2 * FOX_WIDTH].astype(BF16), wi[:, o_dn:o_small1].astype(BF16),
               wi[:, o_rest:].astype(BF16))
    w_alogit = wi[:, o_small1 + DN_HEADS:o_rest]
    wsmall = jnp.concatenate([wi[:, o_small0:o_dn], wi[:, o_small1:o_small1 + DN_HEADS],
                              w_alogit, w_alogit, w_alogit], axis=1)
    wsmall = jnp.pad(wsmall, ((0, 0), (0, LANES - N_SMALL))).astype(BF16)
    bias = jnp.zeros((SUBLANES, LANES), F32)
    bias = bias.at[0, LOGF_LANE:LOGF_LANE + FOX_HEADS].set(fox_forget_bias[0].astype(F32))
    bias = bias.at[0, G_LANE:N_SMALL].set(jnp.tile(dn_dt_bias[0].astype(F32), 3))
    bias = bias.at[1, G_LANE:N_SMALL].set(jnp.tile(dn_a_log[0].astype(F32), 3))
    nw = mix_norm_w[0].reshape(1, D_MODEL).astype(F32)

    x_p = jnp.concatenate([jnp.zeros((N_PAD, D_MODEL), F32), meta_tokens.astype(F32)], axis=0)
    convw = dn_conv_w[0].astype(F32)
    _, k_p, v_tp, kbias_p, qkv_p, dzs_p, _, scol_p, srow_p, conv_tail_p = _inproj(
        x_p, nw, weights, wsmall, bias, jnp.zeros((1, LANES), F32), convw,
        jnp.zeros((SUBLANES, 3 * DN_WIDTH), F32), tm=PREFIX, tiles_per_batch=1, n_pad=N_PAD)

    tm = _pick_tile(s, 512)
    q_t, k, v_t, kbias, qkv, dzs, gates, scol, srow3, _ = _inproj(
        x.reshape(m, D_MODEL), nw, weights, wsmall, bias, scol_p[PREFIX - 1:PREFIX, :], convw,
        conv_tail_p, tm=tm, tiles_per_batch=s // tm, n_pad=0)

    scol3 = scol.reshape(b, s, LANES)
    tq = _pick_tile(s, 512)
    ofox_t = _fox(q_t, k.reshape(b, s, FOX_WIDTH), v_t, kbias.reshape(b, s, LANES),
                  k_p[N_PAD:], v_tp[:, N_PAD:], kbias_p[N_PAD:], tq=tq, tk=min(256, tq // 2),
                  pairs=FOX_PAIRS_PER_STEP)

    onw = dn_out_norm_w[0].reshape(1, DN_HEAD_DIM).astype(F32)
    _, s_prefix = _deltanet(
        qkv_p[None], scol_p[None], srow_p, dzs_p[None], onw,
        jnp.zeros((DN_HEADS, DN_HEAD_DIM, DN_HEAD_DIM), F32), group=1, tt=PREFIX)
    group = _pick_tile(b, DN_UNITS // DN_HEADS)
    odn, _ = _deltanet(
        qkv.reshape(b, s, 3 * DN_WIDTH), scol3, srow3, dzs.reshape(b, s, DN_WIDTH), onw, s_prefix,
        group=group, tt=_pick_tile(s, DN_UNITS // (DN_HEADS * group) * DN_BLOCK))

    out = _tail(
        x.reshape(m, D_MODEL), ofox_t, odn.reshape(m, DN_WIDTH), gates,
        w_branch_fox[0].astype(BF16), w_branch_dn[0].astype(BF16), w_out[0].astype(BF16),
        ffn_norm_w[0].reshape(1, D_MODEL).astype(F32),
        w_ffn_gate[0].astype(BF16), w_ffn_up[0].astype(BF16), w_ffn_down[0].astype(BF16),
        final_norm_w.reshape(1, D_MODEL).astype(F32),
        tm=_pick_tile(m, 512), ff_bounds=FF_BOUNDS)
    return out.reshape(b, s, D_MODEL)
```

```python
import functools

import jax
import jax.numpy as jnp
from jax import lax
from jax.experimental import pallas as pl
from jax.experimental.pallas import tpu as pltpu

F32 = jnp.float32
BF16 = jnp.bfloat16
HIGHEST = lax.Precision.HIGHEST

D_MODEL = 1024
N_META = 16
PREFIX = 128
N_PAD = PREFIX - N_META
FOX_HEADS = 8
FOX_HEAD_DIM = 64
FOX_WIDTH = FOX_HEADS * FOX_HEAD_DIM
V_ROWS = FOX_HEAD_DIM + 8
FOX_PAIRS_PER_STEP = 4
DN_HEADS = 4
DN_HEAD_DIM = 128
DN_WIDTH = DN_HEADS * DN_HEAD_DIM
DN_CHUNK = 64
DN_BLOCK = 2 * DN_CHUNK
DN_UNITS = 16
CONV_WIDTH = 4
D_FF = 2816
MXU_DIM = 256
FF_BOUNDS = (0, (D_FF // MXU_DIM + 1) // 2 * MXU_DIM, D_FF)
EPS = 1e-6
NEG_INF = -1e30
LOG2E = 1.4426950408889634

LANES = 128
SUBLANES = 8
COL_CHUNK = 512
LOGF_LANE = 0
BETA_LANE = FOX_HEADS
G_LANE = BETA_LANE + DN_HEADS
GSUF_LANE = G_LANE + DN_HEADS
GRAW_LANE = GSUF_LANE + DN_HEADS
N_SMALL = GRAW_LANE + DN_HEADS
SROW = 24
VMEM_LIMIT = 56 * 1024 * 1024


def _const_spec(shape):
    nd = len(shape)
    return pl.BlockSpec(shape, lambda *_: (0,) * nd, pipeline_mode=pl.Buffered(1))


def _sigmoid(x):
    return 0.5 * jnp.tanh(0.5 * x) + 0.5


def _dot(a, b, **kw):
    return jnp.dot(a, b, preferred_element_type=F32, **kw)


def _dot_nt(a, b, **kw):
    return lax.dot_general(a, b, (((1,), (1,)), ((), ())), preferred_element_type=F32, **kw)


def _dot_tn(a, b, **kw):
    return lax.dot_general(a, b, (((0,), (0,)), ((), ())), preferred_element_type=F32, **kw)


def _inproj_kernel(x_ref, nw_ref, wq_t_ref, wv_t_ref, wk_ref, wdn_ref, wrest_ref, wsmall_ref, bias_ref,
                   carry0_ref,
                   convw_ref, halo0_ref,
                   q_t_ref, k_ref, v_t_ref, kbias_ref, dqkv_ref, dzs_ref, gates_ref, scol_ref,
                   srow_ref, tail_ref, carry_ref, halo_ref, conv_ref, *, tm, tiles_per_batch, n_pad):
    i = pl.program_id(0)

    @pl.when(i % tiles_per_batch == 0)
    def _():
        carry_ref[...] = carry0_ref[...]
        halo_ref[...] = halo0_ref[...]

    x = x_ref[...]
    ms = jnp.mean(x * x, axis=-1, keepdims=True)
    hn = (x * lax.rsqrt(ms + EPS) * nw_ref[...]).astype(BF16)

    z = _dot(hn, wsmall_ref[...]) + bias_ref[0:1, :]
    lane = lax.broadcasted_iota(jnp.int32, (tm, LANES), 1)
    row = lax.broadcasted_iota(jnp.int32, (tm, LANES), 0)
    e = jnp.exp(-jnp.abs(z))
    l1p = jnp.log1p(e)
    logf = jnp.minimum(z, 0.0) - l1p
    softplus = jnp.maximum(z, 0.0) + l1p
    sig = jnp.where(z >= 0.0, 1.0, e) / (1.0 + e)
    g = -jnp.exp(bias_ref[1:2, :]) * softplus
    if n_pad:
        vm = (row + (i % tiles_per_batch) * tm >= n_pad).astype(F32)
        sig = sig * vm
        g = g * vm
    val = jnp.where(lane < BETA_LANE, logf,
                    jnp.where(lane < G_LANE, sig, jnp.where(lane < N_SMALL, g, 0.0)))
    in_chunk = row % DN_CHUNK
    is_logf = lane < BETA_LANE
    is_gpre = jnp.logical_and(lane >= G_LANE, lane < GSUF_LANE)
    is_gsuf = jnp.logical_and(lane >= GSUF_LANE, lane < GRAW_LANE)
    scan = val
    sh = 1
    while sh < tm:
        take_up = jnp.logical_and(is_logf, row >= sh)
        if sh < DN_CHUNK:
            take_up = jnp.logical_or(take_up, jnp.logical_and(is_gpre, in_chunk >= sh))
            take_down = jnp.logical_and(is_gsuf, in_chunk + sh < DN_CHUNK)
            below = jnp.where(take_down, pltpu.roll(scan, tm - sh, axis=0), 0.0)
        else:
            below = 0.0
        scan = scan + jnp.where(take_up, pltpu.roll(scan, sh, axis=0), below)
        sh *= 2
    out = scan + jnp.where(is_logf, carry_ref[...], 0.0)
    carry_ref[...] = out[tm - 1:tm, :]
    scol_ref[...] = out
    srow_ref[...] = out.T[:SROW, :]
    nc = jnp.where(is_logf, out * -LOG2E, 0.0)
    hi = nc.astype(BF16).astype(F32)
    mid = (nc - hi).astype(BF16).astype(F32)
    lo = nc - hi - mid
    kbias = hi + pltpu.roll(mid, FOX_HEADS, axis=1) + pltpu.roll(lo, 2 * FOX_HEADS, axis=1)
    kbias_ref[...] = kbias.astype(BF16)

    def main_chunk(c):
        if c == 0:
            return lambda: _dot(hn, wk_ref[...])
        ref, first = (wdn_ref, 1) if c < 4 else (wrest_ref, 4)
        return lambda: _dot(hn, ref[:, (c - first) * COL_CHUNK:(c - first + 1) * COL_CHUNK])

    def store_q_t(acc):
        q_t_ref[...] = (acc * (FOX_HEAD_DIM ** -0.5 * LOG2E)).astype(BF16)

    def store_v_t(acc):
        v_t = acc.astype(BF16)
        ones = jnp.ones((V_ROWS - FOX_HEAD_DIM, tm), BF16)
        v_t_ref[...] = jnp.concatenate(
            [piece for h in range(FOX_HEADS)
             for piece in (v_t[h * FOX_HEAD_DIM:(h + 1) * FOX_HEAD_DIM], ones)], axis=0)

    def store_k(acc):
        k_ref[...] = acc.astype(BF16)

    def store_dn(which):
        def epilogue(acc):
            cols = slice(which * COL_CHUNK, (which + 1) * COL_CHUNK)
            if n_pad:
                acc = acc * vm[:, :1]
            conv_ref[which, 0:SUBLANES, :] = halo_ref[:, cols]
            conv_ref[which, SUBLANES:, :] = acc
            halo_ref[:, cols] = acc[tm - SUBLANES:, :]
            tail_ref[:, cols] = acc[tm - SUBLANES:, :]
            a = convw_ref[CONV_WIDTH - 1:CONV_WIDTH, cols] * acc
            for t in range(CONV_WIDTH - 1):
                start = SUBLANES - (CONV_WIDTH - 1) + t
                a = a + convw_ref[t:t + 1, cols] * conv_ref[which, start:start + tm, :]
            a = a * _sigmoid(a)
            if which == 2:
                dqkv_ref[:, cols] = a
            else:
                scale = DN_HEAD_DIM ** -0.5 if which == 0 else 1.0
                for h in range(DN_HEADS):
                    ah = a[:, h * LANES:(h + 1) * LANES]
                    inv = lax.rsqrt(jnp.sum(ah * ah, axis=-1, keepdims=True) + EPS) * scale
                    lo_col = which * COL_CHUNK + h * LANES
                    dqkv_ref[:, lo_col:lo_col + LANES] = ah * inv
        return epilogue

    def store_dzs(acc):
        dzs_ref[...] = acc * _sigmoid(acc)

    def store_gates(j):
        def epilogue(acc):
            gates_ref[:, j * COL_CHUNK:(j + 1) * COL_CHUNK] = _sigmoid(acc)
        return epilogue

    store_q_t(_dot_nt(wq_t_ref[...], hn))
    store_v_t(_dot_nt(wv_t_ref[...], hn))
    store_k(main_chunk(0)())
    for which in range(3):
        store_dn(which)(main_chunk(1 + which)())
    store_dzs(main_chunk(4)())
    for j in range(2 * D_MODEL // COL_CHUNK):
        store_gates(j)(main_chunk(5 + j)())


def _inproj(x2d, nw, weights, wsmall, bias, carry0, convw, halo0, *, tm, tiles_per_batch, n_pad):
    m = x2d.shape[0]
    row = lambda w: pl.BlockSpec((tm, w), lambda i: (i, 0))
    col = lambda h: pl.BlockSpec((h, tm), lambda i: (0, i))
    return pl.pallas_call(
        functools.partial(_inproj_kernel, tm=tm, tiles_per_batch=tiles_per_batch, n_pad=n_pad),
        grid=(m // tm,),
        in_specs=[row(D_MODEL), _const_spec((1, D_MODEL))] + [_const_spec(w.shape) for w in weights] + [
                  _const_spec((D_MODEL, LANES)), _const_spec((SUBLANES, LANES)),
                  _const_spec((1, LANES)), _const_spec((CONV_WIDTH, 3 * DN_WIDTH)),
                  _const_spec((SUBLANES, 3 * DN_WIDTH))],
        out_specs=[col(FOX_WIDTH), row(FOX_WIDTH), col(FOX_HEADS * V_ROWS), row(LANES),
                   row(3 * DN_WIDTH), row(DN_WIDTH), row(2 * D_MODEL),
                   row(LANES),
                   pl.BlockSpec((None, SROW, tm),
                                lambda i: (i // tiles_per_batch, 0, i % tiles_per_batch)),
                   pl.BlockSpec((SUBLANES, 3 * DN_WIDTH), lambda i: (0, 0))],
        out_shape=[jax.ShapeDtypeStruct((FOX_WIDTH, m), BF16),
                   jax.ShapeDtypeStruct((m, FOX_WIDTH), BF16),
                   jax.ShapeDtypeStruct((FOX_HEADS * V_ROWS, m), BF16),
                   jax.ShapeDtypeStruct((m, LANES), BF16),
                   jax.ShapeDtypeStruct((m, 3 * DN_WIDTH), F32),
                   jax.ShapeDtypeStruct((m, DN_WIDTH), F32),
                   jax.ShapeDtypeStruct((m, 2 * D_MODEL), F32),
                   jax.ShapeDtypeStruct((m, LANES), F32),
                   jax.ShapeDtypeStruct((m // (tm * tiles_per_batch), SROW,
                                         tm * tiles_per_batch), F32),
                   jax.ShapeDtypeStruct((SUBLANES, 3 * DN_WIDTH), F32)],
        scratch_shapes=[pltpu.VMEM((1, LANES), F32), pltpu.VMEM((SUBLANES, 3 * DN_WIDTH), F32),
                        pltpu.VMEM((3, tm + SUBLANES, COL_CHUNK), F32)],
        compiler_params=pltpu.CompilerParams(dimension_semantics=("arbitrary",),
                                             vmem_limit_bytes=VMEM_LIMIT),
        name="inproj",
    )(x2d, nw, *weights, wsmall, bias, carry0, convw, halo0)


def _fox_kernel(q_t_ref, k_ref, v_t_ref, kb_ref, kp_ref, v_tp_ref, kbp_ref,
                o_ref, m_ref, l_ref, acc_ref, t_ref, *, tq, tk, pairs):
    pg = pl.program_id(1)
    i = pl.program_id(2)
    heads = range(2 * pairs)
    sub = lax.broadcasted_iota(jnp.int32, (LANES, 1), 0)
    rhs = []
    for hh in heads:
        pp, h = divmod(hh, 2)
        q_t = q_t_ref[pp * LANES:(pp + 1) * LANES, :]
        mine = sub < FOX_HEAD_DIM if h == 0 else sub >= FOX_HEAD_DIM
        head = 2 * (pg * pairs + pp) + h
        sel = jnp.where(jnp.logical_and(sub < 3 * FOX_HEADS, (sub % FOX_HEADS) == head),
                        1.0, 0.0).astype(BF16) * jnp.ones((1, tq), BF16)
        rhs.append(jnp.concatenate([jnp.where(mine, q_t, jnp.zeros_like(q_t)), sel], axis=0))

    def scores(kt, kbt):
        lhs = [jnp.concatenate([kt[:, pp * LANES:(pp + 1) * LANES], kbt], axis=1)
               for pp in range(pairs)]
        return [_dot(lhs[hh // 2], rhs[hh]) for hh in heads]

    def scores_into(slot, off):
        ts = scores(k_ref[pl.ds(off, tk), :], kb_ref[pl.ds(off, tk), :])
        for hh in heads:
            t_ref[slot, hh] = ts[hh]

    def update(read_t, v_aug, mask, first):
        def masked_t(hh):
            t = read_t(hh)
            return t if mask is None else jnp.where(mask, t, NEG_INF)

        t_max = [jnp.max(masked_t(hh), axis=0, keepdims=True) for hh in heads]
        if first:
            m_new = t_max
        else:
            m_prev = [m_ref[hh] for hh in heads]
            m_new = [jnp.maximum(m_prev[hh], t_max[hh]) for hh in heads]
        pm = [jnp.exp2(masked_t(hh) - m_new[hh]).astype(BF16) for hh in heads]
        r = [_dot(v_aug[hh * V_ROWS:(hh + 1) * V_ROWS, :], pm[hh]) for hh in heads]
        pv = jnp.concatenate([r[hh][:FOX_HEAD_DIM] for hh in heads], axis=0)
        psum = [r[hh][FOX_HEAD_DIM:FOX_HEAD_DIM + 1] for hh in heads]
        if first:
            for hh in heads:
                l_ref[hh] = psum[hh]
            acc_ref[...] = pv
        else:
            alpha = [jnp.exp2(m_prev[hh] - m_new[hh]) for hh in heads]
            for hh in heads:
                l_ref[hh] = alpha[hh] * l_ref[hh] + psum[hh]
            alpha_rows = jnp.concatenate(
                [jnp.broadcast_to(alpha[hh], (FOX_HEAD_DIM, tq)) for hh in heads], axis=0)
            acc_ref[...] = alpha_rows * acc_ref[...] + pv
        for hh in heads:
            m_ref[hh] = m_new[hh]

    ts_p = scores(kp_ref[...], kbp_ref[...])
    update(lambda hh: ts_p[hh], v_tp_ref[...], None, True)

    def v_at(off):
        return v_t_ref[:, pl.ds(off, tk)]

    def slot(n):
        return lambda hh: t_ref[n, hh]

    pairs_per_q = tq // (2 * tk)
    scores_into(0, 0)

    def body(jj, carry):
        off = pl.multiple_of(jj * 2 * tk, 2 * tk)
        scores_into(1, off + tk)
        update(slot(0), v_at(off), None, False)
        scores_into(0, off + 2 * tk)
        update(slot(1), v_at(off + tk), None, False)
        return carry

    lax.fori_loop(0, i * pairs_per_q, body, 0)

    rr = lax.broadcasted_iota(jnp.int32, (tk, tq), 0)
    cc = lax.broadcasted_iota(jnp.int32, (tk, tq), 1)
    for dp in range(pairs_per_q):
        off = pl.multiple_of(i * tq + dp * 2 * tk, 2 * tk)
        scores_into(1, off + tk)
        update(slot(0), v_at(off), rr + dp * 2 * tk <= cc, False)
        if dp + 1 < pairs_per_q:
            scores_into(0, off + 2 * tk)
        update(slot(1), v_at(off + tk), rr + (dp * 2 + 1) * tk <= cc, False)

    l_rows = jnp.concatenate([jnp.broadcast_to(l_ref[hh], (FOX_HEAD_DIM, tq)) for hh in heads], axis=0)
    o_ref[...] = (acc_ref[...] / l_rows).astype(BF16)


def _fox(q_t, k, v_t, kbias, k_p, v_tp, kbias_p, *, tq, tk, pairs):
    b, s, _ = k.shape
    ngroups = FOX_HEADS // (2 * pairs)
    nq = s // tq
    wl = pairs * LANES
    return pl.pallas_call(
        functools.partial(_fox_kernel, tq=tq, tk=tk, pairs=pairs),
        grid=(b, ngroups, nq),
        in_specs=[
            pl.BlockSpec((wl, tq), lambda bi, pg, i: (pg, bi * nq + i)),
            pl.BlockSpec((None, s, wl), lambda bi, pg, i: (bi, 0, pg)),
            pl.BlockSpec((2 * pairs * V_ROWS, s), lambda bi, pg, i: (pg, bi)),
            pl.BlockSpec((None, s, LANES), lambda bi, pg, i: (bi, 0, 0)),
            pl.BlockSpec((N_META, wl), lambda bi, pg, i: (0, pg)),
            pl.BlockSpec((2 * pairs * V_ROWS, N_META), lambda bi, pg, i: (pg, 0)),
            pl.BlockSpec((N_META, LANES), lambda bi, pg, i: (0, 0)),
        ],
        out_specs=pl.BlockSpec((wl, tq), lambda bi, pg, i: (pg, bi * nq + i)),
        out_shape=jax.ShapeDtypeStruct((FOX_WIDTH, b * s), BF16),
        scratch_shapes=[pltpu.VMEM((2 * pairs, 1, tq), F32), pltpu.VMEM((2 * pairs, 1, tq), F32),
                        pltpu.VMEM((wl, tq), F32), pltpu.VMEM((2, 2 * pairs, tk, tq), F32)],
        compiler_params=pltpu.CompilerParams(
            dimension_semantics=("arbitrary", "arbitrary", "arbitrary"),
            vmem_limit_bytes=VMEM_LIMIT),
        name="fox_attention",
    )(q_t, k, v_t, kbias, k_p, v_tp, kbias_p)


def _dn_kernel(qkv_ref, scol_ref, srow_ref, dzs_ref, onw_ref, s0_ref, *refs, tt, n_cast):
    cast_in, (odn_ref, sfin_ref), cast_out = refs[:n_cast], refs[n_cast:n_cast + 2], refs[n_cast + 2:-1]
    state_ref = refs[-1]
    for src, dst in zip(cast_in, cast_out):
        dst[...] = src[...].astype(BF16)
    i = pl.program_id(1)
    nchunk = tt // DN_CHUNK

    @pl.when(i == 0)
    def _():
        for g in range(qkv_ref.shape[0]):
            state_ref[g] = s0_ref[...]

    bs = DN_BLOCK
    heads = range(DN_HEADS)
    groups = range(qkv_ref.shape[0])
    units = [(g, sb, h) for g in groups for sb in range(tt // bs) for h in heads]

    def slab(g, sb, c0):
        return qkv_ref[g, sb * bs:(sb + 1) * bs, c0:c0 + LANES]

    def small_col(g, sb, lane):
        return scol_ref[g, sb * bs:(sb + 1) * bs, lane:lane + 1]

    rr = lax.broadcasted_iota(jnp.int32, (bs, bs), 0)
    cc = lax.broadcasted_iota(jnp.int32, (bs, bs), 1)
    rc_xor = jnp.bitwise_xor(rr, cc)
    same = rc_xor < DN_CHUNK
    tril_m = jnp.logical_and(same, rr >= cc)
    strict_m = jnp.logical_and(same, rr > cc)
    eye = (rr == cc).astype(F32)

    qs = [slab(g, sb, h * LANES) for g, sb, h in units]
    ks = [slab(g, sb, DN_WIDTH + h * LANES) for g, sb, h in units]
    vs = [slab(g, sb, 2 * DN_WIDTH + h * LANES) for g, sb, h in units]
    states = {(g, h): state_ref[g, h] for g in groups for h in heads}
    betas = [small_col(g, sb, BETA_LANE + h) for g, sb, h in units]
    gcs = [small_col(g, sb, G_LANE + h) for g, sb, h in units]
    gls = [gcs[n] + small_col(g, sb, GSUF_LANE + h) - small_col(g, sb, GRAW_LANE + h)
           for n, (g, sb, h) in enumerate(units)]
    decays = [jnp.exp(jnp.where(
        tril_m, gcs[n] - srow_ref[g, G_LANE + h:G_LANE + h + 1, sb * bs:(sb + 1) * bs], NEG_INF))
        for n, (g, sb, h) in enumerate(units)]
    nu = range(len(units))
    egcs = [jnp.exp(g) for g in gcs]
    kbs = [ks[n] * betas[n] for n in nu]
    ks_b = [k.astype(BF16) for k in ks]
    lmats = [jnp.where(strict_m, _dot_nt(kbs[n].astype(BF16), ks_b[n]) * decays[n], 0.0) for n in nu]
    attns = [(_dot_nt(qs[n].astype(BF16), ks_b[n]) * decays[n]).astype(BF16) for n in nu]
    ainvs = [eye - jnp.where(rc_xor == 1, lm, 0.0) for lm in lmats]
    for lvl in range(1, 6):
        blk = 2 ** lvl
        lvl_m = jnp.logical_and(rc_xor >= blk, rc_xor < 2 * blk)
        ainvs_b = [a.astype(BF16) for a in ainvs]
        mids = [_dot(jnp.where(lvl_m, lmats[n], 0.0).astype(BF16), ainvs_b[n]).astype(BF16)
                for n in nu]
        ainvs = [ainvs[n] - _dot(ainvs_b[n], mids[n]) for n in nu]
    sols = [_dot(ainvs[n].astype(BF16),
                 jnp.concatenate([vs[n] * betas[n], kbs[n] * egcs[n]], axis=1).astype(BF16))
            for n in nu]
    uus = [sol[:, :DN_HEAD_DIM] for sol in sols]
    wws = [sol[:, DN_HEAD_DIM:] for sol in sols]
    qds = [qs[n] * egcs[n] for n in nu]
    kds = [(ks[n] * jnp.exp(gls[n] - gcs[n])).astype(BF16) for n in nu]

    vnews = [[] for _ in nu]
    qss = [[] for _ in nu]
    for c in range(nchunk):
        csb, lc = divmod(c, bs // DN_CHUNK)
        lo, hi = lc * DN_CHUNK, (lc + 1) * DN_CHUNK
        cur = [(n, (g, h)) for n, (g, sb, h) in enumerate(units) if sb == csb]
        rs = {n: _dot(jnp.concatenate([wws[n][lo:hi], qds[n][lo:hi]], axis=0).astype(BF16),
                      states[key].astype(BF16)) for n, key in cur}
        vns = {n: uus[n][lo:hi] - rs[n][:DN_CHUNK] for n, _ in cur}
        states.update({key: jnp.exp(gls[n][lo:lo + 1, :]) * states[key]
                       + _dot_tn(kds[n][lo:hi], vns[n].astype(BF16)) for n, key in cur})
        for n, _ in cur:
            qss[n].append(rs[n][DN_CHUNK:])
            vnews[n].append(vns[n])
    for (g, h), state in states.items():
        state_ref[g, h] = state
    for h in heads:
        sfin_ref[h] = states[(0, h)]
    for n, (g, sb, h) in enumerate(units):
        o = (jnp.concatenate(qss[n], axis=0)
             + _dot(attns[n], jnp.concatenate(vnews[n], axis=0).astype(BF16)))
        o = o * lax.rsqrt(jnp.mean(o * o, axis=-1, keepdims=True) + EPS) * onw_ref[...]
        o = o * dzs_ref[g, sb * bs:(sb + 1) * bs, h * LANES:(h + 1) * LANES]
        odn_ref[g, sb * bs:(sb + 1) * bs, h * LANES:(h + 1) * LANES] = o.astype(BF16)


def _cast_row_block(rows, nsteps):
    tile = 2 * SUBLANES
    for blk in range(tile, rows, tile):
        if rows % blk == 0 and rows // blk <= nsteps:
            return blk
    return rows


def _deltanet(qkv, scol, srow, dzs, onw, s0, *, group, tt, cast_weights=()):
    b, s, _ = qkv.shape
    nt = s // tt
    nsteps = (b // group) * nt
    cast_specs = []
    for w in cast_weights:
        blk = _cast_row_block(w.shape[0], nsteps)
        last = w.shape[0] // blk - 1
        cast_specs.append(pl.BlockSpec(
            (blk, w.shape[1]), lambda bi, i, last=last: (jnp.minimum(bi * nt + i, last), 0)))
    return pl.pallas_call(
        functools.partial(_dn_kernel, tt=tt, n_cast=len(cast_weights)),
        grid=(b // group, nt),
        in_specs=[
            pl.BlockSpec((group, tt, 3 * DN_WIDTH), lambda bi, i: (bi, i, 0)),
            pl.BlockSpec((group, tt, LANES), lambda bi, i: (bi, i, 0)),
            pl.BlockSpec((group, SROW, tt), lambda bi, i: (bi, 0, i)),
            pl.BlockSpec((group, tt, DN_WIDTH), lambda bi, i: (bi, i, 0)),
            pl.BlockSpec((1, DN_HEAD_DIM), lambda bi, i: (0, 0)),
            pl.BlockSpec((DN_HEADS, DN_HEAD_DIM, DN_HEAD_DIM), lambda bi, i: (0, 0, 0)),
        ] + cast_specs,
        out_specs=[pl.BlockSpec((group, tt, DN_WIDTH), lambda bi, i: (bi, i, 0)),
                   pl.BlockSpec((DN_HEADS, DN_HEAD_DIM, DN_HEAD_DIM), lambda bi, i: (0, 0, 0))]
        + cast_specs,
        out_shape=[jax.ShapeDtypeStruct((b, s, DN_WIDTH), BF16),
                   jax.ShapeDtypeStruct((DN_HEADS, DN_HEAD_DIM, DN_HEAD_DIM), F32)]
        + [jax.ShapeDtypeStruct(w.shape, BF16) for w in cast_weights],
        scratch_shapes=[pltpu.VMEM((group, DN_HEADS, DN_HEAD_DIM, DN_HEAD_DIM), F32)],
        compiler_params=pltpu.CompilerParams(dimension_semantics=("arbitrary", "arbitrary"),
                                             vmem_limit_bytes=VMEM_LIMIT),
        name="deltanet",
    )(qkv, scol, srow, dzs, onw, s0, *cast_weights)


def _tail_kernel(x_ref, ofox_ref, odn_ref, gates_ref, wbf_ref, wbd_ref, wout_ref, fnw_ref,
                 wg_ref, wu_ref, wd_ref, finw_ref, o_ref, *, ff_bounds):
    a = _dot_tn(ofox_ref[...], wbf_ref[...])
    bb = _dot(odn_ref[...], wbd_ref[...])
    y = gates_ref[:, :D_MODEL] * a + gates_ref[:, D_MODEL:] * bb
    h1 = x_ref[...] + _dot(y.astype(BF16), wout_ref[...])
    n = (h1 * lax.rsqrt(jnp.mean(h1 * h1, axis=-1, keepdims=True) + EPS) * fnw_ref[...]).astype(BF16)
    acc = h1
    for lo, hi in zip(ff_bounds[:-1], ff_bounds[1:]):
        gt = _dot(n, wg_ref[:, lo:hi])
        up = _dot(n, wu_ref[:, lo:hi])
        act = (gt * _sigmoid(gt) * up).astype(BF16)
        acc = acc + _dot(act, wd_ref[lo:hi, :])
    o_ref[...] = acc * lax.rsqrt(jnp.mean(acc * acc, axis=-1, keepdims=True) + EPS) * finw_ref[...]


def _tail(x2d, ofox, odn, gates, wbf, wbd, wout, fnw, wg, wu, wd, finw, *, tm, ff_bounds):
    m = x2d.shape[0]
    row = lambda w: pl.BlockSpec((tm, w), lambda i: (i, 0))
    return pl.pallas_call(
        functools.partial(_tail_kernel, ff_bounds=ff_bounds),
        grid=(m // tm,),
        in_specs=[row(D_MODEL), pl.BlockSpec((FOX_WIDTH, tm), lambda i: (0, i)), row(DN_WIDTH),
                  row(2 * D_MODEL),
                  _const_spec((FOX_WIDTH, D_MODEL)), _const_spec((DN_WIDTH, D_MODEL)),
                  _const_spec((D_MODEL, D_MODEL)), _const_spec((1, D_MODEL)),
                  _const_spec((D_MODEL, D_FF)), _const_spec((D_MODEL, D_FF)),
                  _const_spec((D_FF, D_MODEL)), _const_spec((1, D_MODEL))],
        out_specs=row(D_MODEL),
        out_shape=jax.ShapeDtypeStruct((m, D_MODEL), F32),
        compiler_params=pltpu.CompilerParams(dimension_semantics=("arbitrary",),
                                             vmem_limit_bytes=VMEM_LIMIT),
        name="merge_ffn",
    )(x2d, ofox, odn, gates, wbf, wbd, wout, fnw, wg, wu, wd, finw)


def _pick_tile(n, pref):
    t = min(pref, n)
    while n % t:
        t //= 2
    return t


def kernel(x, meta_tokens, mix_norm_w, w_in, fox_forget_bias, dn_conv_w, dn_a_log, dn_dt_bias,
           dn_out_norm_w, w_branch_fox, w_branch_dn, w_out, ffn_norm_w, w_ffn_gate, w_ffn_up,
           w_ffn_down, final_norm_w):
    b, s, _ = x.shape
    assert mix_norm_w.shape[0] == 1, "single layer only"
    assert s % PREFIX == 0
    m = b * s

    wi = w_in[0]
    o_small0 = 3 * FOX_WIDTH
    o_dn = o_small0 + FOX_HEADS
    o_small1 = o_dn + 3 * DN_WIDTH
    o_rest = o_small1 + 2 * DN_HEADS
    weights = (wi[:, :FOX_WIDTH].T.astype(BF16), wi[:, 2 * FOX_WIDTH:o_small0].T.astype(BF16),
               wi[:, FOX_WIDTH:2 * FOX_WIDTH].astype(BF16), wi[:, o_dn:o_small1].astype(BF16),
               wi[:, o_rest:].astype(BF16))
    w_alogit = wi[:, o_small1 + DN_HEADS:o_rest]
    wsmall = jnp.concatenate([wi[:, o_small0:o_dn], wi[:, o_small1:o_small1 + DN_HEADS],
                              w_alogit, w_alogit, w_alogit], axis=1)
    wsmall = jnp.pad(wsmall, ((0, 0), (0, LANES - N_SMALL))).astype(BF16)
    bias = jnp.zeros((SUBLANES, LANES), F32)
    bias = bias.at[0, LOGF_LANE:LOGF_LANE + FOX_HEADS].set(fox_forget_bias[0].astype(F32))
    bias = bias.at[0, G_LANE:N_SMALL].set(jnp.tile(dn_dt_bias[0].astype(F32), 3))
    bias = bias.at[1, G_LANE:N_SMALL].set(jnp.tile(dn_a_log[0].astype(F32), 3))
    nw = mix_norm_w[0].reshape(1, D_MODEL).astype(F32)

    x_p = jnp.concatenate([jnp.zeros((N_PAD, D_MODEL), F32), meta_tokens.astype(F32)], axis=0)
    convw = dn_conv_w[0].astype(F32)
    _, k_p, v_tp, kbias_p, qkv_p, dzs_p, _, scol_p, srow_p, conv_tail_p = _inproj(
        x_p, nw, weights, wsmall, bias, jnp.zeros((1, LANES), F32), convw,
        jnp.zeros((SUBLANES, 3 * DN_WIDTH), F32), tm=PREFIX, tiles_per_batch=1, n_pad=N_PAD)

    tm = _pick_tile(s, 512)
    q_t, k, v_t, kbias, qkv, dzs, gates, scol, srow3, _ = _inproj(
        x.reshape(m, D_MODEL), nw, weights, wsmall, bias, scol_p[PREFIX - 1:PREFIX, :], convw,
        conv_tail_p, tm=tm, tiles_per_batch=s // tm, n_pad=0)

    scol3 = scol.reshape(b, s, LANES)
    tq = _pick_tile(s, 512)
    ofox_t = _fox(q_t, k.reshape(b, s, FOX_WIDTH), v_t, kbias.reshape(b, s, LANES),
                  k_p[N_PAD:], v_tp[:, N_PAD:], kbias_p[N_PAD:], tq=tq, tk=min(256, tq // 2),
                  pairs=FOX_PAIRS_PER_STEP)

    onw = dn_out_norm_w[0].reshape(1, DN_HEAD_DIM).astype(F32)
    _, s_prefix = _deltanet(
        qkv_p[None], scol_p[None], srow_p, dzs_p[None], onw,
        jnp.zeros((DN_HEADS, DN_HEAD_DIM, DN_HEAD_DIM), F32), group=1, tt=PREFIX)
    group = _pick_tile(b, DN_UNITS // DN_HEADS)
    odn, _, wbf, wbd, wout, wg, wu, wd = _deltanet(
        qkv.reshape(b, s, 3 * DN_WIDTH), scol3, srow3, dzs.reshape(b, s, DN_WIDTH), onw, s_prefix,
        group=group, tt=_pick_tile(s, DN_UNITS // (DN_HEADS * group) * DN_BLOCK),
        cast_weights=(w_branch_fox[0], w_branch_dn[0], w_out[0], w_ffn_gate[0], w_ffn_up[0],
                      w_ffn_down[0]))

    out = _tail(
        x.reshape(m, D_MODEL), ofox_t, odn.reshape(m, DN_WIDTH), gates, wbf, wbd, wout,
        ffn_norm_w[0].reshape(1, D_MODEL).astype(F32), wg, wu, wd,
        final_norm_w.reshape(1, D_MODEL).astype(F32),
        tm=_pick_tile(m, 512), ff_bounds=FF_BOUNDS)
    return out.reshape(b, s, D_MODEL)
```

```python
import functools

import jax
import jax.numpy as jnp
from jax import lax
from jax.experimental import pallas as pl
from jax.experimental.pallas import tpu as pltpu

F32 = jnp.float32
BF16 = jnp.bfloat16
HIGHEST = lax.Precision.HIGHEST

D_MODEL = 1024
N_META = 16
PREFIX = 128
N_PAD = PREFIX - N_META
FOX_HEADS = 8
FOX_HEAD_DIM = 64
FOX_WIDTH = FOX_HEADS * FOX_HEAD_DIM
V_ROWS = FOX_HEAD_DIM + 8
FOX_PAIRS_PER_STEP = 4
DN_HEADS = 4
DN_HEAD_DIM = 128
DN_WIDTH = DN_HEADS * DN_HEAD_DIM
DN_CHUNK = 64
DN_BLOCK = 2 * DN_CHUNK
DN_UNITS = 16
CONV_WIDTH = 4
D_FF = 2816
MXU_DIM = 256
FF_BOUNDS = (0, (D_FF // MXU_DIM + 1) // 2 * MXU_DIM, D_FF)
EPS = 1e-6
NEG_INF = -1e30
LOG2E = 1.4426950408889634

LANES = 128
SUBLANES = 8
O_FQ = 0
O_FK = O_FQ + FOX_WIDTH
O_FV = O_FK + FOX_WIDTH
O_FLOGIT = O_FV + FOX_WIDTH
O_DN = O_FLOGIT + FOX_HEADS
O_BETA = O_DN + 3 * DN_WIDTH
O_ALOGIT = O_BETA + DN_HEADS
O_DZ = O_ALOGIT + DN_HEADS
COL_CHUNK = 512
LOGF_LANE = 0
BETA_LANE = FOX_HEADS
G_LANE = BETA_LANE + DN_HEADS
GSUF_LANE = G_LANE + DN_HEADS
GRAW_LANE = GSUF_LANE + DN_HEADS
N_SMALL = GRAW_LANE + DN_HEADS
SROW = 24
VMEM_LIMIT = 56 * 1024 * 1024


def _const_spec(shape):
    nd = len(shape)
    return pl.BlockSpec(shape, lambda *_: (0,) * nd, pipeline_mode=pl.Buffered(1))


def _sigmoid(x):
    return 0.5 * jnp.tanh(0.5 * x) + 0.5


def _dot(a, b, **kw):
    return jnp.dot(a, b, preferred_element_type=F32, **kw)


def _dot_nt(a, b, **kw):
    return lax.dot_general(a, b, (((1,), (1,)), ((), ())), preferred_element_type=F32, **kw)


def _dot_tn(a, b, **kw):
    return lax.dot_general(a, b, (((0,), (0,)), ((), ())), preferred_element_type=F32, **kw)


def _wprep_kernel(w_ref, wq_t_ref, wv_t_ref, wk_ref, wdn_ref, wrest_ref, wsmall_ref):
    rows = w_ref.shape[0]
    wq_t_ref[...] = w_ref[:, O_FQ:O_FQ + FOX_WIDTH].T.astype(BF16)
    wv_t_ref[...] = w_ref[:, O_FV:O_FV + FOX_WIDTH].T.astype(BF16)
    wk_ref[...] = w_ref[:, O_FK:O_FK + FOX_WIDTH].astype(BF16)
    wdn_ref[...] = w_ref[:, O_DN:O_DN + 3 * DN_WIDTH].astype(BF16)
    wrest_ref[...] = w_ref[:, O_DZ:].astype(BF16)
    a_logit = w_ref[:, O_ALOGIT:O_ALOGIT + DN_HEADS]
    small = jnp.concatenate(
        [w_ref[:, O_FLOGIT:O_FLOGIT + FOX_HEADS], w_ref[:, O_BETA:O_BETA + DN_HEADS],
         a_logit, a_logit, a_logit, jnp.zeros((rows, LANES - N_SMALL), F32)], axis=1)
    wsmall_ref[...] = small.astype(BF16)


def _wprep(w_in2d):
    d, n = w_in2d.shape
    rb = _pick_tile(d, LANES)
    n_rest = n - O_DZ
    rowblk = lambda w: pl.BlockSpec((rb, w), lambda r: (r, 0))
    colblk = pl.BlockSpec((FOX_WIDTH, rb), lambda r: (0, r))
    return pl.pallas_call(
        _wprep_kernel,
        grid=(d // rb,),
        in_specs=[rowblk(n)],
        out_specs=[colblk, colblk, rowblk(FOX_WIDTH), rowblk(3 * DN_WIDTH), rowblk(n_rest),
                   rowblk(LANES)],
        out_shape=[jax.ShapeDtypeStruct((FOX_WIDTH, d), BF16), jax.ShapeDtypeStruct((FOX_WIDTH, d), BF16),
                   jax.ShapeDtypeStruct((d, FOX_WIDTH), BF16),
                   jax.ShapeDtypeStruct((d, 3 * DN_WIDTH), BF16),
                   jax.ShapeDtypeStruct((d, n_rest), BF16), jax.ShapeDtypeStruct((d, LANES), BF16)],
        compiler_params=pltpu.CompilerParams(dimension_semantics=("arbitrary",),
                                             vmem_limit_bytes=VMEM_LIMIT),
        name="weight_prep",
    )(w_in2d)


def _inproj_kernel(x_ref, nw_ref, wq_t_ref, wv_t_ref, wk_ref, wdn_ref, wrest_ref, wsmall_ref, bias_ref,
                   carry0_ref,
                   convw_ref, halo0_ref,
                   q_t_ref, k_ref, v_t_ref, kbias_ref, dqkv_ref, dzs_ref, gates_ref, scol_ref,
                   srow_ref, tail_ref, carry_ref, halo_ref, conv_ref, *, tm, tiles_per_batch, n_pad):
    i = pl.program_id(0)

    @pl.when(i % tiles_per_batch == 0)
    def _():
        carry_ref[...] = carry0_ref[...]
        halo_ref[...] = halo0_ref[...]

    x = x_ref[...]
    ms = jnp.mean(x * x, axis=-1, keepdims=True)
    hn = (x * lax.rsqrt(ms + EPS) * nw_ref[...]).astype(BF16)

    z = _dot(hn, wsmall_ref[...]) + bias_ref[0:1, :]
    lane = lax.broadcasted_iota(jnp.int32, (tm, LANES), 1)
    row = lax.broadcasted_iota(jnp.int32, (tm, LANES), 0)
    e = jnp.exp(-jnp.abs(z))
    l1p = jnp.log1p(e)
    logf = jnp.minimum(z, 0.0) - l1p
    softplus = jnp.maximum(z, 0.0) + l1p
    sig = jnp.where(z >= 0.0, 1.0, e) / (1.0 + e)
    g = -jnp.exp(bias_ref[1:2, :]) * softplus
    if n_pad:
        vm = (row + (i % tiles_per_batch) * tm >= n_pad).astype(F32)
        sig = sig * vm
        g = g * vm
    val = jnp.where(lane < BETA_LANE, logf,
                    jnp.where(lane < G_LANE, sig, jnp.where(lane < N_SMALL, g, 0.0)))
    in_chunk = row % DN_CHUNK
    is_logf = lane < BETA_LANE
    is_gpre = jnp.logical_and(lane >= G_LANE, lane < GSUF_LANE)
    is_gsuf = jnp.logical_and(lane >= GSUF_LANE, lane < GRAW_LANE)
    scan = val
    sh = 1
    while sh < tm:
        take_up = jnp.logical_and(is_logf, row >= sh)
        if sh < DN_CHUNK:
            take_up = jnp.logical_or(take_up, jnp.logical_and(is_gpre, in_chunk >= sh))
            take_down = jnp.logical_and(is_gsuf, in_chunk + sh < DN_CHUNK)
            below = jnp.where(take_down, pltpu.roll(scan, tm - sh, axis=0), 0.0)
        else:
            below = 0.0
        scan = scan + jnp.where(take_up, pltpu.roll(scan, sh, axis=0), below)
        sh *= 2
    out = scan + jnp.where(is_logf, carry_ref[...], 0.0)
    carry_ref[...] = out[tm - 1:tm, :]
    scol_ref[...] = out
    srow_ref[...] = out.T[:SROW, :]
    nc = jnp.where(is_logf, out * -LOG2E, 0.0)
    hi = nc.astype(BF16).astype(F32)
    mid = (nc - hi).astype(BF16).astype(F32)
    lo = nc - hi - mid
    kbias = hi + pltpu.roll(mid, FOX_HEADS, axis=1) + pltpu.roll(lo, 2 * FOX_HEADS, axis=1)
    kbias_ref[...] = kbias.astype(BF16)

    def main_chunk(c):
        if c == 0:
            return lambda: _dot(hn, wk_ref[...])
        ref, first = (wdn_ref, 1) if c < 4 else (wrest_ref, 4)
        return lambda: _dot(hn, ref[:, (c - first) * COL_CHUNK:(c - first + 1) * COL_CHUNK])

    def store_q_t(acc):
        q_t_ref[...] = (acc * (FOX_HEAD_DIM ** -0.5 * LOG2E)).astype(BF16)

    def store_v_t(acc):
        v_t = acc.astype(BF16)
        ones = jnp.ones((V_ROWS - FOX_HEAD_DIM, tm), BF16)
        v_t_ref[...] = jnp.concatenate(
            [piece for h in range(FOX_HEADS)
             for piece in (v_t[h * FOX_HEAD_DIM:(h + 1) * FOX_HEAD_DIM], ones)], axis=0)

    def store_k(acc):
        k_ref[...] = acc.astype(BF16)

    def store_dn(which):
        def epilogue(acc):
            cols = slice(which * COL_CHUNK, (which + 1) * COL_CHUNK)
            if n_pad:
                acc = acc * vm[:, :1]
            conv_ref[which, 0:SUBLANES, :] = halo_ref[:, cols]
            conv_ref[which, SUBLANES:, :] = acc
            halo_ref[:, cols] = acc[tm - SUBLANES:, :]
            tail_ref[:, cols] = acc[tm - SUBLANES:, :]
            a = convw_ref[CONV_WIDTH - 1:CONV_WIDTH, cols] * acc
            for t in range(CONV_WIDTH - 1):
                start = SUBLANES - (CONV_WIDTH - 1) + t
                a = a + convw_ref[t:t + 1, cols] * conv_ref[which, start:start + tm, :]
            a = a * _sigmoid(a)
            if which == 2:
                dqkv_ref[:, cols] = a
            else:
                scale = DN_HEAD_DIM ** -0.5 if which == 0 else 1.0
                for h in range(DN_HEADS):
                    ah = a[:, h * LANES:(h + 1) * LANES]
                    inv = lax.rsqrt(jnp.sum(ah * ah, axis=-1, keepdims=True) + EPS) * scale
                    lo_col = which * COL_CHUNK + h * LANES
                    dqkv_ref[:, lo_col:lo_col + LANES] = ah * inv
        return epilogue

    def store_dzs(acc):
        dzs_ref[...] = acc * _sigmoid(acc)

    def store_gates(j):
        def epilogue(acc):
            gates_ref[:, j * COL_CHUNK:(j + 1) * COL_CHUNK] = _sigmoid(acc)
        return epilogue

    store_q_t(_dot_nt(wq_t_ref[...], hn))
    store_v_t(_dot_nt(wv_t_ref[...], hn))
    store_k(main_chunk(0)())
    for which in range(3):
        store_dn(which)(main_chunk(1 + which)())
    store_dzs(main_chunk(4)())
    for j in range(2 * D_MODEL // COL_CHUNK):
        store_gates(j)(main_chunk(5 + j)())


def _inproj(x2d, nw, weights, wsmall, bias, carry0, convw, halo0, *, tm, tiles_per_batch, n_pad):
    m = x2d.shape[0]
    row = lambda w: pl.BlockSpec((tm, w), lambda i: (i, 0))
    col = lambda h: pl.BlockSpec((h, tm), lambda i: (0, i))
    return pl.pallas_call(
        functools.partial(_inproj_kernel, tm=tm, tiles_per_batch=tiles_per_batch, n_pad=n_pad),
        grid=(m // tm,),
        in_specs=[row(D_MODEL), _const_spec((1, D_MODEL))] + [_const_spec(w.shape) for w in weights] + [
                  _const_spec((D_MODEL, LANES)), _const_spec((SUBLANES, LANES)),
                  _const_spec((1, LANES)), _const_spec((CONV_WIDTH, 3 * DN_WIDTH)),
                  _const_spec((SUBLANES, 3 * DN_WIDTH))],
        out_specs=[col(FOX_WIDTH), row(FOX_WIDTH), col(FOX_HEADS * V_ROWS), row(LANES),
                   row(3 * DN_WIDTH), row(DN_WIDTH), row(2 * D_MODEL),
                   row(LANES),
                   pl.BlockSpec((None, SROW, tm),
                                lambda i: (i // tiles_per_batch, 0, i % tiles_per_batch)),
                   pl.BlockSpec((SUBLANES, 3 * DN_WIDTH), lambda i: (0, 0))],
        out_shape=[jax.ShapeDtypeStruct((FOX_WIDTH, m), BF16),
                   jax.ShapeDtypeStruct((m, FOX_WIDTH), BF16),
                   jax.ShapeDtypeStruct((FOX_HEADS * V_ROWS, m), BF16),
                   jax.ShapeDtypeStruct((m, LANES), BF16),
                   jax.ShapeDtypeStruct((m, 3 * DN_WIDTH), F32),
                   jax.ShapeDtypeStruct((m, DN_WIDTH), F32),
                   jax.ShapeDtypeStruct((m, 2 * D_MODEL), F32),
                   jax.ShapeDtypeStruct((m, LANES), F32),
                   jax.ShapeDtypeStruct((m // (tm * tiles_per_batch), SROW,
                                         tm * tiles_per_batch), F32),
                   jax.ShapeDtypeStruct((SUBLANES, 3 * DN_WIDTH), F32)],
        scratch_shapes=[pltpu.VMEM((1, LANES), F32), pltpu.VMEM((SUBLANES, 3 * DN_WIDTH), F32),
                        pltpu.VMEM((3, tm + SUBLANES, COL_CHUNK), F32)],
        compiler_params=pltpu.CompilerParams(dimension_semantics=("arbitrary",),
                                             vmem_limit_bytes=VMEM_LIMIT),
        name="inproj",
    )(x2d, nw, *weights, wsmall, bias, carry0, convw, halo0)


def _fox_kernel(q_t_ref, k_ref, v_t_ref, kb_ref, kp_ref, v_tp_ref, kbp_ref,
                o_ref, m_ref, l_ref, acc_ref, t_ref, *, tq, tk, pairs):
    pg = pl.program_id(1)
    i = pl.program_id(2)
    heads = range(2 * pairs)
    sub = lax.broadcasted_iota(jnp.int32, (LANES, 1), 0)
    rhs = []
    for hh in heads:
        pp, h = divmod(hh, 2)
        q_t = q_t_ref[pp * LANES:(pp + 1) * LANES, :]
        mine = sub < FOX_HEAD_DIM if h == 0 else sub >= FOX_HEAD_DIM
        head = 2 * (pg * pairs + pp) + h
        sel = jnp.where(jnp.logical_and(sub < 3 * FOX_HEADS, (sub % FOX_HEADS) == head),
                        1.0, 0.0).astype(BF16) * jnp.ones((1, tq), BF16)
        rhs.append(jnp.concatenate([jnp.where(mine, q_t, jnp.zeros_like(q_t)), sel], axis=0))

    def scores(kt, kbt):
        lhs = [jnp.concatenate([kt[:, pp * LANES:(pp + 1) * LANES], kbt], axis=1)
               for pp in range(pairs)]
        return [_dot(lhs[hh // 2], rhs[hh]) for hh in heads]

    def scores_into(slot, off):
        ts = scores(k_ref[pl.ds(off, tk), :], kb_ref[pl.ds(off, tk), :])
        for hh in heads:
            t_ref[slot, hh] = ts[hh]

    def update(read_t, v_aug, mask, first):
        def masked_t(hh):
            t = read_t(hh)
            return t if mask is None else jnp.where(mask, t, NEG_INF)

        t_max = [jnp.max(masked_t(hh), axis=0, keepdims=True) for hh in heads]
        if first:
            m_new = t_max
        else:
            m_prev = [m_ref[hh] for hh in heads]
            m_new = [jnp.maximum(m_prev[hh], t_max[hh]) for hh in heads]
        pm = [jnp.exp2(masked_t(hh) - m_new[hh]).astype(BF16) for hh in heads]
        r = [_dot(v_aug[hh * V_ROWS:(hh + 1) * V_ROWS, :], pm[hh]) for hh in heads]
        pv = jnp.concatenate([r[hh][:FOX_HEAD_DIM] for hh in heads], axis=0)
        psum = [r[hh][FOX_HEAD_DIM:FOX_HEAD_DIM + 1] for hh in heads]
        if first:
            for hh in heads:
                l_ref[hh] = psum[hh]
            acc_ref[...] = pv
        else:
            alpha = [jnp.exp2(m_prev[hh] - m_new[hh]) for hh in heads]
            for hh in heads:
                l_ref[hh] = alpha[hh] * l_ref[hh] + psum[hh]
            alpha_rows = jnp.concatenate(
                [jnp.broadcast_to(alpha[hh], (FOX_HEAD_DIM, tq)) for hh in heads], axis=0)
            acc_ref[...] = alpha_rows * acc_ref[...] + pv
        for hh in heads:
            m_ref[hh] = m_new[hh]

    ts_p = scores(kp_ref[...], kbp_ref[...])
    update(lambda hh: ts_p[hh], v_tp_ref[...], None, True)

    def v_at(off):
        return v_t_ref[:, pl.ds(off, tk)]

    def slot(n):
        return lambda hh: t_ref[n, hh]

    pairs_per_q = tq // (2 * tk)
    scores_into(0, 0)

    def body(jj, carry):
        off = pl.multiple_of(jj * 2 * tk, 2 * tk)
        scores_into(1, off + tk)
        update(slot(0), v_at(off), None, False)
        scores_into(0, off + 2 * tk)
        update(slot(1), v_at(off + tk), None, False)
        return carry

    lax.fori_loop(0, i * pairs_per_q, body, 0)

    rr = lax.broadcasted_iota(jnp.int32, (tk, tq), 0)
    cc = lax.broadcasted_iota(jnp.int32, (tk, tq), 1)
    for dp in range(pairs_per_q):
        off = pl.multiple_of(i * tq + dp * 2 * tk, 2 * tk)
        scores_into(1, off + tk)
        update(slot(0), v_at(off), rr + dp * 2 * tk <= cc, False)
        if dp + 1 < pairs_per_q:
            scores_into(0, off + 2 * tk)
        update(slot(1), v_at(off + tk), rr + (dp * 2 + 1) * tk <= cc, False)

    l_rows = jnp.concatenate([jnp.broadcast_to(l_ref[hh], (FOX_HEAD_DIM, tq)) for hh in heads], axis=0)
    o_ref[...] = (acc_ref[...] / l_rows).astype(BF16)


def _fox(q_t, k, v_t, kbias, k_p, v_tp, kbias_p, *, tq, tk, pairs):
    b, s, _ = k.shape
    ngroups = FOX_HEADS // (2 * pairs)
    nq = s // tq
    wl = pairs * LANES
    return pl.pallas_call(
        functools.partial(_fox_kernel, tq=tq, tk=tk, pairs=pairs),
        grid=(b, ngroups, nq),
        in_specs=[
            pl.BlockSpec((wl, tq), lambda bi, pg, i: (pg, bi * nq + i)),
            pl.BlockSpec((None, s, wl), lambda bi, pg, i: (bi, 0, pg)),
            pl.BlockSpec((2 * pairs * V_ROWS, s), lambda bi, pg, i: (pg, bi)),
            pl.BlockSpec((None, s, LANES), lambda bi, pg, i: (bi, 0, 0)),
            pl.BlockSpec((N_META, wl), lambda bi, pg, i: (0, pg)),
            pl.BlockSpec((2 * pairs * V_ROWS, N_META), lambda bi, pg, i: (pg, 0)),
            pl.BlockSpec((N_META, LANES), lambda bi, pg, i: (0, 0)),
        ],
        out_specs=pl.BlockSpec((wl, tq), lambda bi, pg, i: (pg, bi * nq + i)),
        out_shape=jax.ShapeDtypeStruct((FOX_WIDTH, b * s), BF16),
        scratch_shapes=[pltpu.VMEM((2 * pairs, 1, tq), F32), pltpu.VMEM((2 * pairs, 1, tq), F32),
                        pltpu.VMEM((wl, tq), F32), pltpu.VMEM((2, 2 * pairs, tk, tq), F32)],
        compiler_params=pltpu.CompilerParams(
            dimension_semantics=("arbitrary", "arbitrary", "arbitrary"),
            vmem_limit_bytes=VMEM_LIMIT),
        name="fox_attention",
    )(q_t, k, v_t, kbias, k_p, v_tp, kbias_p)


def _dn_kernel(qkv_ref, scol_ref, srow_ref, dzs_ref, onw_ref, s0_ref, *refs, tt, n_cast):
    cast_in, (odn_ref, sfin_ref), cast_out = refs[:n_cast], refs[n_cast:n_cast + 2], refs[n_cast + 2:-1]
    state_ref = refs[-1]
    for src, dst in zip(cast_in, cast_out):
        dst[...] = src[...].astype(BF16)
    i = pl.program_id(1)
    nchunk = tt // DN_CHUNK

    @pl.when(i == 0)
    def _():
        for g in range(qkv_ref.shape[0]):
            state_ref[g] = s0_ref[...]

    bs = DN_BLOCK
    heads = range(DN_HEADS)
    groups = range(qkv_ref.shape[0])
    units = [(g, sb, h) for g in groups for sb in range(tt // bs) for h in heads]

    def slab(g, sb, c0):
        return qkv_ref[g, sb * bs:(sb + 1) * bs, c0:c0 + LANES]

    def small_col(g, sb, lane):
        return scol_ref[g, sb * bs:(sb + 1) * bs, lane:lane + 1]

    rr = lax.broadcasted_iota(jnp.int32, (bs, bs), 0)
    cc = lax.broadcasted_iota(jnp.int32, (bs, bs), 1)
    rc_xor = jnp.bitwise_xor(rr, cc)
    same = rc_xor < DN_CHUNK
    tril_m = jnp.logical_and(same, rr >= cc)
    strict_m = jnp.logical_and(same, rr > cc)
    eye = (rr == cc).astype(F32)

    qs = [slab(g, sb, h * LANES) for g, sb, h in units]
    ks = [slab(g, sb, DN_WIDTH + h * LANES) for g, sb, h in units]
    vs = [slab(g, sb, 2 * DN_WIDTH + h * LANES) for g, sb, h in units]
    states = {(g, h): state_ref[g, h] for g in groups for h in heads}
    betas = [small_col(g, sb, BETA_LANE + h) for g, sb, h in units]
    gcs = [small_col(g, sb, G_LANE + h) for g, sb, h in units]
    gls = [gcs[n] + small_col(g, sb, GSUF_LANE + h) - small_col(g, sb, GRAW_LANE + h)
           for n, (g, sb, h) in enumerate(units)]
    decays = [jnp.exp(jnp.where(
        tril_m, gcs[n] - srow_ref[g, G_LANE + h:G_LANE + h + 1, sb * bs:(sb + 1) * bs], NEG_INF))
        for n, (g, sb, h) in enumerate(units)]
    nu = range(len(units))
    egcs = [jnp.exp(g) for g in gcs]
    kbs = [ks[n] * betas[n] for n in nu]
    ks_b = [k.astype(BF16) for k in ks]
    lmats = [jnp.where(strict_m, _dot_nt(kbs[n].astype(BF16), ks_b[n]) * decays[n], 0.0) for n in nu]
    attns = [(_dot_nt(qs[n].astype(BF16), ks_b[n]) * decays[n]).astype(BF16) for n in nu]
    ainvs = [eye - jnp.where(rc_xor == 1, lm, 0.0) for lm in lmats]
    for lvl in range(1, 6):
        blk = 2 ** lvl
        lvl_m = jnp.logical_and(rc_xor >= blk, rc_xor < 2 * blk)
        ainvs_b = [a.astype(BF16) for a in ainvs]
        mids = [_dot(jnp.where(lvl_m, lmats[n], 0.0).astype(BF16), ainvs_b[n]).astype(BF16)
                for n in nu]
        ainvs = [ainvs[n] - _dot(ainvs_b[n], mids[n]) for n in nu]
    sols = [_dot(ainvs[n].astype(BF16),
                 jnp.concatenate([vs[n] * betas[n], kbs[n] * egcs[n]], axis=1).astype(BF16))
            for n in nu]
    uus = [sol[:, :DN_HEAD_DIM] for sol in sols]
    wws = [sol[:, DN_HEAD_DIM:] for sol in sols]
    qds = [qs[n] * egcs[n] for n in nu]
    kds = [(ks[n] * jnp.exp(gls[n] - gcs[n])).astype(BF16) for n in nu]

    vnews = [[] for _ in nu]
    qss = [[] for _ in nu]
    for c in range(nchunk):
        csb, lc = divmod(c, bs // DN_CHUNK)
        lo, hi = lc * DN_CHUNK, (lc + 1) * DN_CHUNK
        cur = [(n, (g, h)) for n, (g, sb, h) in enumerate(units) if sb == csb]
        rs = {n: _dot(jnp.concatenate([wws[n][lo:hi], qds[n][lo:hi]], axis=0).astype(BF16),
                      states[key].astype(BF16)) for n, key in cur}
        vns = {n: uus[n][lo:hi] - rs[n][:DN_CHUNK] for n, _ in cur}
        states.update({key: jnp.exp(gls[n][lo:lo + 1, :]) * states[key]
                       + _dot_tn(kds[n][lo:hi], vns[n].astype(BF16)) for n, key in cur})
        for n, _ in cur:
            qss[n].append(rs[n][DN_CHUNK:])
            vnews[n].append(vns[n])
    for (g, h), state in states.items():
        state_ref[g, h] = state
    for h in heads:
        sfin_ref[h] = states[(0, h)]
    for n, (g, sb, h) in enumerate(units):
        o = (jnp.concatenate(qss[n], axis=0)
             + _dot(attns[n], jnp.concatenate(vnews[n], axis=0).astype(BF16)))
        o = o * lax.rsqrt(jnp.mean(o * o, axis=-1, keepdims=True) + EPS) * onw_ref[...]
        o = o * dzs_ref[g, sb * bs:(sb + 1) * bs, h * LANES:(h + 1) * LANES]
        odn_ref[g, sb * bs:(sb + 1) * bs, h * LANES:(h + 1) * LANES] = o.astype(BF16)


def _cast_row_block(rows, nsteps):
    tile = 2 * SUBLANES
    for blk in range(tile, rows, tile):
        if rows % blk == 0 and rows // blk <= nsteps:
            return blk
    return rows


def _deltanet(qkv, scol, srow, dzs, onw, s0, *, group, tt, cast_weights=()):
    b, s, _ = qkv.shape
    nt = s // tt
    nsteps = (b // group) * nt
    cast_specs = []
    for w in cast_weights:
        blk = _cast_row_block(w.shape[0], nsteps)
        last = w.shape[0] // blk - 1
        cast_specs.append(pl.BlockSpec(
            (blk, w.shape[1]), lambda bi, i, last=last: (jnp.minimum(bi * nt + i, last), 0)))
    return pl.pallas_call(
        functools.partial(_dn_kernel, tt=tt, n_cast=len(cast_weights)),
        grid=(b // group, nt),
        in_specs=[
            pl.BlockSpec((group, tt, 3 * DN_WIDTH), lambda bi, i: (bi, i, 0)),
            pl.BlockSpec((group, tt, LANES), lambda bi, i: (bi, i, 0)),
            pl.BlockSpec((group, SROW, tt), lambda bi, i: (bi, 0, i)),
            pl.BlockSpec((group, tt, DN_WIDTH), lambda bi, i: (bi, i, 0)),
            pl.BlockSpec((1, DN_HEAD_DIM), lambda bi, i: (0, 0)),
            pl.BlockSpec((DN_HEADS, DN_HEAD_DIM, DN_HEAD_DIM), lambda bi, i: (0, 0, 0)),
        ] + cast_specs,
        out_specs=[pl.BlockSpec((group, tt, DN_WIDTH), lambda bi, i: (bi, i, 0)),
                   pl.BlockSpec((DN_HEADS, DN_HEAD_DIM, DN_HEAD_DIM), lambda bi, i: (0, 0, 0))]
        + cast_specs,
        out_shape=[jax.ShapeDtypeStruct((b, s, DN_WIDTH), BF16),
                   jax.ShapeDtypeStruct((DN_HEADS, DN_HEAD_DIM, DN_HEAD_DIM), F32)]
        + [jax.ShapeDtypeStruct(w.shape, BF16) for w in cast_weights],
        scratch_shapes=[pltpu.VMEM((group, DN_HEADS, DN_HEAD_DIM, DN_HEAD_DIM), F32)],
        compiler_params=pltpu.CompilerParams(dimension_semantics=("arbitrary", "arbitrary"),
                                             vmem_limit_bytes=VMEM_LIMIT),
        name="deltanet",
    )(qkv, scol, srow, dzs, onw, s0, *cast_weights)


def _tail_kernel(x_ref, ofox_ref, odn_ref, gates_ref, wbf_ref, wbd_ref, wout_ref, fnw_ref,
                 wg_ref, wu_ref, wd_ref, finw_ref, o_ref, *, ff_bounds):
    a = _dot_tn(ofox_ref[...], wbf_ref[...])
    bb = _dot(odn_ref[...], wbd_ref[...])
    y = gates_ref[:, :D_MODEL] * a + gates_ref[:, D_MODEL:] * bb
    h1 = x_ref[...] + _dot(y.astype(BF16), wout_ref[...])
    n = (h1 * lax.rsqrt(jnp.mean(h1 * h1, axis=-1, keepdims=True) + EPS) * fnw_ref[...]).astype(BF16)
    acc = h1
    for lo, hi in zip(ff_bounds[:-1], ff_bounds[1:]):
        gt = _dot(n, wg_ref[:, lo:hi])
        up = _dot(n, wu_ref[:, lo:hi])
        act = (gt * _sigmoid(gt) * up).astype(BF16)
        acc = acc + _dot(act, wd_ref[lo:hi, :])
    o_ref[...] = acc * lax.rsqrt(jnp.mean(acc * acc, axis=-1, keepdims=True) + EPS) * finw_ref[...]


def _tail(x2d, ofox, odn, gates, wbf, wbd, wout, fnw, wg, wu, wd, finw, *, tm, ff_bounds):
    m = x2d.shape[0]
    row = lambda w: pl.BlockSpec((tm, w), lambda i: (i, 0))
    return pl.pallas_call(
        functools.partial(_tail_kernel, ff_bounds=ff_bounds),
        grid=(m // tm,),
        in_specs=[row(D_MODEL), pl.BlockSpec((FOX_WIDTH, tm), lambda i: (0, i)), row(DN_WIDTH),
                  row(2 * D_MODEL),
                  _const_spec((FOX_WIDTH, D_MODEL)), _const_spec((DN_WIDTH, D_MODEL)),
                  _const_spec((D_MODEL, D_MODEL)), _const_spec((1, D_MODEL)),
                  _const_spec((D_MODEL, D_FF)), _const_spec((D_MODEL, D_FF)),
                  _const_spec((D_FF, D_MODEL)), _const_spec((1, D_MODEL))],
        out_specs=row(D_MODEL),
        out_shape=jax.ShapeDtypeStruct((m, D_MODEL), F32),
        compiler_params=pltpu.CompilerParams(dimension_semantics=("arbitrary",),
                                             vmem_limit_bytes=VMEM_LIMIT),
        name="merge_ffn",
    )(x2d, ofox, odn, gates, wbf, wbd, wout, fnw, wg, wu, wd, finw)


def _pick_tile(n, pref):
    t = min(pref, n)
    while n % t:
        t //= 2
    return t


def kernel(x, meta_tokens, mix_norm_w, w_in, fox_forget_bias, dn_conv_w, dn_a_log, dn_dt_bias,
           dn_out_norm_w, w_branch_fox, w_branch_dn, w_out, ffn_norm_w, w_ffn_gate, w_ffn_up,
           w_ffn_down, final_norm_w):
    b, s, _ = x.shape
    assert mix_norm_w.shape[0] == 1, "single layer only"
    assert s % PREFIX == 0
    m = b * s

    *weights, wsmall = _wprep(w_in[0])
    bias = jnp.zeros((SUBLANES, LANES), F32)
    bias = bias.at[0, LOGF_LANE:LOGF_LANE + FOX_HEADS].set(fox_forget_bias[0].astype(F32))
    bias = bias.at[0, G_LANE:N_SMALL].set(jnp.tile(dn_dt_bias[0].astype(F32), 3))
    bias = bias.at[1, G_LANE:N_SMALL].set(jnp.tile(dn_a_log[0].astype(F32), 3))
    nw = mix_norm_w[0].reshape(1, D_MODEL).astype(F32)

    x_p = jnp.concatenate([jnp.zeros((N_PAD, D_MODEL), F32), meta_tokens.astype(F32)], axis=0)
    convw = dn_conv_w[0].astype(F32)
    _, k_p, v_tp, kbias_p, qkv_p, dzs_p, _, scol_p, srow_p, conv_tail_p = _inproj(
        x_p, nw, weights, wsmall, bias, jnp.zeros((1, LANES), F32), convw,
        jnp.zeros((SUBLANES, 3 * DN_WIDTH), F32), tm=PREFIX, tiles_per_batch=1, n_pad=N_PAD)

    tm = _pick_tile(s, 512)
    q_t, k, v_t, kbias, qkv, dzs, gates, scol, srow3, _ = _inproj(
        x.reshape(m, D_MODEL), nw, weights, wsmall, bias, scol_p[PREFIX - 1:PREFIX, :], convw,
        conv_tail_p, tm=tm, tiles_per_batch=s // tm, n_pad=0)

    scol3 = scol.reshape(b, s, LANES)
    tq = _pick_tile(s, 512)
    ofox_t = _fox(q_t, k.reshape(b, s, FOX_WIDTH), v_t, kbias.reshape(b, s, LANES),
                  k_p[N_PAD:], v_tp[:, N_PAD:], kbias_p[N_PAD:], tq=tq, tk=min(256, tq // 2),
                  pairs=FOX_PAIRS_PER_STEP)

    onw = dn_out_norm_w[0].reshape(1, DN_HEAD_DIM).astype(F32)
    _, s_prefix = _deltanet(
        qkv_p[None], scol_p[None], srow_p, dzs_p[None], onw,
        jnp.zeros((DN_HEADS, DN_HEAD_DIM, DN_HEAD_DIM), F32), group=1, tt=PREFIX)
    group = _pick_tile(b, DN_UNITS // DN_HEADS)
    odn, _, wbf, wbd, wout, wg, wu, wd = _deltanet(
        qkv.reshape(b, s, 3 * DN_WIDTH), scol3, srow3, dzs.reshape(b, s, DN_WIDTH), onw, s_prefix,
        group=group, tt=_pick_tile(s, DN_UNITS // (DN_HEADS * group) * DN_BLOCK),
        cast_weights=(w_branch_fox[0], w_branch_dn[0], w_out[0], w_ffn_gate[0], w_ffn_up[0],
                      w_ffn_down[0]))

    out = _tail(
        x.reshape(m, D_MODEL), ofox_t, odn.reshape(m, DN_WIDTH), gates, wbf, wbd, wout,
        ffn_norm_w[0].reshape(1, D_MODEL).astype(F32), wg, wu, wd,
        final_norm_w.reshape(1, D_MODEL).astype(F32),
        tm=_pick_tile(m, 512), ff_bounds=FF_BOUNDS)
    return out.reshape(b, s, D_MODEL)
```

```python
import functools

import jax
import jax.numpy as jnp
from jax import lax
from jax.experimental import pallas as pl
from jax.experimental.pallas import tpu as pltpu

F32 = jnp.float32
BF16 = jnp.bfloat16
HIGHEST = lax.Precision.HIGHEST

D_MODEL = 1024
N_META = 16
PREFIX = 128
N_PAD = PREFIX - N_META
FOX_HEADS = 8
FOX_HEAD_DIM = 64
FOX_WIDTH = FOX_HEADS * FOX_HEAD_DIM
V_ROWS = FOX_HEAD_DIM + 8
FOX_PAIRS_PER_STEP = 4
DN_HEADS = 4
DN_HEAD_DIM = 128
DN_WIDTH = DN_HEADS * DN_HEAD_DIM
DN_CHUNK = 64
DN_BLOCK = 2 * DN_CHUNK
DN_UNITS = 16
CONV_WIDTH = 4
D_FF = 2816
MXU_DIM = 256
FF_BOUNDS = (0, (D_FF // MXU_DIM + 1) // 2 * MXU_DIM, D_FF)
EPS = 1e-6
NEG_INF = -1e30
LOG2E = 1.4426950408889634

LANES = 128
SUBLANES = 8
O_FQ = 0
O_FK = O_FQ + FOX_WIDTH
O_FV = O_FK + FOX_WIDTH
O_FLOGIT = O_FV + FOX_WIDTH
O_DN = O_FLOGIT + FOX_HEADS
O_BETA = O_DN + 3 * DN_WIDTH
O_ALOGIT = O_BETA + DN_HEADS
O_DZ = O_ALOGIT + DN_HEADS
COL_CHUNK = 512
LOGF_LANE = 0
BETA_LANE = FOX_HEADS
G_LANE = BETA_LANE + DN_HEADS
GSUF_LANE = G_LANE + DN_HEADS
GRAW_LANE = GSUF_LANE + DN_HEADS
N_SMALL = GRAW_LANE + DN_HEADS
SROW = 24
VMEM_LIMIT = 56 * 1024 * 1024


def _const_spec(shape):
    nd = len(shape)
    return pl.BlockSpec(shape, lambda *_: (0,) * nd, pipeline_mode=pl.Buffered(1))


def _sigmoid(x):
    return 0.5 * jnp.tanh(0.5 * x) + 0.5


def _dot(a, b, **kw):
    return jnp.dot(a, b, preferred_element_type=F32, **kw)


def _dot_nt(a, b, **kw):
    return lax.dot_general(a, b, (((1,), (1,)), ((), ())), preferred_element_type=F32, **kw)


def _dot_tn(a, b, **kw):
    return lax.dot_general(a, b, (((0,), (0,)), ((), ())), preferred_element_type=F32, **kw)


def _wprep_kernel(w_ref, wq_t_ref, wv_t_ref, wk_ref, wdn_ref, wrest_ref, wsmall_ref):
    rows = w_ref.shape[0]
    wq_t_ref[...] = w_ref[:, O_FQ:O_FQ + FOX_WIDTH].T.astype(BF16)
    wv_t_ref[...] = w_ref[:, O_FV:O_FV + FOX_WIDTH].T.astype(BF16)
    wk_ref[...] = w_ref[:, O_FK:O_FK + FOX_WIDTH].astype(BF16)
    wdn_ref[...] = w_ref[:, O_DN:O_DN + 3 * DN_WIDTH].astype(BF16)
    wrest_ref[...] = w_ref[:, O_DZ:].astype(BF16)
    a_logit = w_ref[:, O_ALOGIT:O_ALOGIT + DN_HEADS]
    small = jnp.concatenate(
        [w_ref[:, O_FLOGIT:O_FLOGIT + FOX_HEADS], w_ref[:, O_BETA:O_BETA + DN_HEADS],
         a_logit, a_logit, a_logit, jnp.zeros((rows, LANES - N_SMALL), F32)], axis=1)
    wsmall_ref[...] = small.astype(BF16)


def _wprep(w_in3d):
    _, d, n = w_in3d.shape
    rb = _pick_tile(d, LANES)
    n_rest = n - O_DZ
    rowblk = lambda w: pl.BlockSpec((rb, w), lambda r: (r, 0))
    colblk = pl.BlockSpec((FOX_WIDTH, rb), lambda r: (0, r))
    return pl.pallas_call(
        _wprep_kernel,
        grid=(d // rb,),
        in_specs=[pl.BlockSpec((None, rb, n), lambda r: (0, r, 0))],
        out_specs=[colblk, colblk, rowblk(FOX_WIDTH), rowblk(3 * DN_WIDTH), rowblk(n_rest),
                   rowblk(LANES)],
        out_shape=[jax.ShapeDtypeStruct((FOX_WIDTH, d), BF16), jax.ShapeDtypeStruct((FOX_WIDTH, d), BF16),
                   jax.ShapeDtypeStruct((d, FOX_WIDTH), BF16),
                   jax.ShapeDtypeStruct((d, 3 * DN_WIDTH), BF16),
                   jax.ShapeDtypeStruct((d, n_rest), BF16), jax.ShapeDtypeStruct((d, LANES), BF16)],
        compiler_params=pltpu.CompilerParams(dimension_semantics=("arbitrary",),
                                             vmem_limit_bytes=VMEM_LIMIT),
        name="weight_prep",
    )(w_in3d)


def _inproj_kernel(x_ref, nw_ref, wq_t_ref, wv_t_ref, wk_ref, wdn_ref, wrest_ref, wsmall_ref, bias_ref,
                   carry0_ref,
                   convw_ref, halo0_ref,
                   q_t_ref, k_ref, v_t_ref, kbias_ref, dqkv_ref, dzs_ref, gates_ref, scol_ref,
                   srow_ref, tail_ref, carry_ref, halo_ref, conv_ref, *, tm, tiles_per_batch, n_pad):
    i = pl.program_id(0)

    @pl.when(i % tiles_per_batch == 0)
    def _():
        carry_ref[...] = carry0_ref[...]
        halo_ref[...] = halo0_ref[...]

    x = x_ref[...]
    ms = jnp.mean(x * x, axis=-1, keepdims=True)
    hn = (x * lax.rsqrt(ms + EPS) * nw_ref[...]).astype(BF16)

    z = _dot(hn, wsmall_ref[...]) + bias_ref[0:1, :]
    lane = lax.broadcasted_iota(jnp.int32, (tm, LANES), 1)
    row = lax.broadcasted_iota(jnp.int32, (tm, LANES), 0)
    e = jnp.exp(-jnp.abs(z))
    l1p = jnp.log1p(e)
    logf = jnp.minimum(z, 0.0) - l1p
    softplus = jnp.maximum(z, 0.0) + l1p
    sig = jnp.where(z >= 0.0, 1.0, e) / (1.0 + e)
    g = -jnp.exp(bias_ref[1:2, :]) * softplus
    if n_pad:
        vm = (row + (i % tiles_per_batch) * tm >= n_pad).astype(F32)
        sig = sig * vm
        g = g * vm
    val = jnp.where(lane < BETA_LANE, logf,
                    jnp.where(lane < G_LANE, sig, jnp.where(lane < N_SMALL, g, 0.0)))
    in_chunk = row % DN_CHUNK
    is_logf = lane < BETA_LANE
    is_gpre = jnp.logical_and(lane >= G_LANE, lane < GSUF_LANE)
    is_gsuf = jnp.logical_and(lane >= GSUF_LANE, lane < GRAW_LANE)
    scan = val
    sh = 1
    while sh < tm:
        take_up = jnp.logical_and(is_logf, row >= sh)
        if sh < DN_CHUNK:
            take_up = jnp.logical_or(take_up, jnp.logical_and(is_gpre, in_chunk >= sh))
            take_down = jnp.logical_and(is_gsuf, in_chunk + sh < DN_CHUNK)
            below = jnp.where(take_down, pltpu.roll(scan, tm - sh, axis=0), 0.0)
        else:
            below = 0.0
        scan = scan + jnp.where(take_up, pltpu.roll(scan, sh, axis=0), below)
        sh *= 2
    out = scan + jnp.where(is_logf, carry_ref[...], 0.0)
    carry_ref[...] = out[tm - 1:tm, :]
    scol_ref[...] = out
    srow_ref[...] = out.T[:SROW, :]
    nc = jnp.where(is_logf, out * -LOG2E, 0.0)
    hi = nc.astype(BF16).astype(F32)
    mid = (nc - hi).astype(BF16).astype(F32)
    lo = nc - hi - mid
    kbias = hi + pltpu.roll(mid, FOX_HEADS, axis=1) + pltpu.roll(lo, 2 * FOX_HEADS, axis=1)
    kbias_ref[...] = kbias.astype(BF16)

    def main_chunk(c):
        if c == 0:
            return lambda: _dot(hn, wk_ref[...])
        ref, first = (wdn_ref, 1) if c < 4 else (wrest_ref, 4)
        return lambda: _dot(hn, ref[:, (c - first) * COL_CHUNK:(c - first + 1) * COL_CHUNK])

    def store_q_t(acc):
        q_t_ref[...] = (acc * (FOX_HEAD_DIM ** -0.5 * LOG2E)).astype(BF16)

    def store_v_t(acc):
        v_t = acc.astype(BF16)
        ones = jnp.ones((V_ROWS - FOX_HEAD_DIM, tm), BF16)
        v_t_ref[...] = jnp.concatenate(
            [piece for h in range(FOX_HEADS)
             for piece in (v_t[h * FOX_HEAD_DIM:(h + 1) * FOX_HEAD_DIM], ones)], axis=0)

    def store_k(acc):
        k_ref[...] = acc.astype(BF16)

    def store_dn(which):
        def epilogue(acc):
            cols = slice(which * COL_CHUNK, (which + 1) * COL_CHUNK)
            if n_pad:
                acc = acc * vm[:, :1]
            conv_ref[which, 0:SUBLANES, :] = halo_ref[:, cols]
            conv_ref[which, SUBLANES:, :] = acc
            halo_ref[:, cols] = acc[tm - SUBLANES:, :]
            tail_ref[:, cols] = acc[tm - SUBLANES:, :]
            a = convw_ref[CONV_WIDTH - 1:CONV_WIDTH, cols] * acc
            for t in range(CONV_WIDTH - 1):
                start = SUBLANES - (CONV_WIDTH - 1) + t
                a = a + convw_ref[t:t + 1, cols] * conv_ref[which, start:start + tm, :]
            a = a * _sigmoid(a)
            if which == 2:
                dqkv_ref[:, cols] = a
            else:
                scale = DN_HEAD_DIM ** -0.5 if which == 0 else 1.0
                for h in range(DN_HEADS):
                    ah = a[:, h * LANES:(h + 1) * LANES]
                    inv = lax.rsqrt(jnp.sum(ah * ah, axis=-1, keepdims=True) + EPS) * scale
                    lo_col = which * COL_CHUNK + h * LANES
                    dqkv_ref[:, lo_col:lo_col + LANES] = ah * inv
        return epilogue

    def store_dzs(acc):
        dzs_ref[...] = acc * _sigmoid(acc)

    def store_gates(j):
        def epilogue(acc):
            gates_ref[:, j * COL_CHUNK:(j + 1) * COL_CHUNK] = _sigmoid(acc)
        return epilogue

    store_q_t(_dot_nt(wq_t_ref[...], hn))
    store_v_t(_dot_nt(wv_t_ref[...], hn))
    store_k(main_chunk(0)())
    for which in range(3):
        store_dn(which)(main_chunk(1 + which)())
    store_dzs(main_chunk(4)())
    for j in range(2 * D_MODEL // COL_CHUNK):
        store_gates(j)(main_chunk(5 + j)())


def _inproj(x2d, nw, weights, wsmall, bias, carry0, convw, halo0, *, tm, tiles_per_batch, n_pad):
    m = x2d.shape[0]
    row = lambda w: pl.BlockSpec((tm, w), lambda i: (i, 0))
    col = lambda h: pl.BlockSpec((h, tm), lambda i: (0, i))
    return pl.pallas_call(
        functools.partial(_inproj_kernel, tm=tm, tiles_per_batch=tiles_per_batch, n_pad=n_pad),
        grid=(m // tm,),
        in_specs=[row(D_MODEL), _const_spec((1, D_MODEL))] + [_const_spec(w.shape) for w in weights] + [
                  _const_spec((D_MODEL, LANES)), _const_spec((SUBLANES, LANES)),
                  _const_spec((1, LANES)), _const_spec((CONV_WIDTH, 3 * DN_WIDTH)),
                  _const_spec((SUBLANES, 3 * DN_WIDTH))],
        out_specs=[col(FOX_WIDTH), row(FOX_WIDTH), col(FOX_HEADS * V_ROWS), row(LANES),
                   row(3 * DN_WIDTH), row(DN_WIDTH), row(2 * D_MODEL),
                   row(LANES),
                   pl.BlockSpec((None, SROW, tm),
                                lambda i: (i // tiles_per_batch, 0, i % tiles_per_batch)),
                   pl.BlockSpec((SUBLANES, 3 * DN_WIDTH), lambda i: (0, 0))],
        out_shape=[jax.ShapeDtypeStruct((FOX_WIDTH, m), BF16),
                   jax.ShapeDtypeStruct((m, FOX_WIDTH), BF16),
                   jax.ShapeDtypeStruct((FOX_HEADS * V_ROWS, m), BF16),
                   jax.ShapeDtypeStruct((m, LANES), BF16),
                   jax.ShapeDtypeStruct((m, 3 * DN_WIDTH), F32),
                   jax.ShapeDtypeStruct((m, DN_WIDTH), F32),
                   jax.ShapeDtypeStruct((m, 2 * D_MODEL), F32),
                   jax.ShapeDtypeStruct((m, LANES), F32),
                   jax.ShapeDtypeStruct((m // (tm * tiles_per_batch), SROW,
                                         tm * tiles_per_batch), F32),
                   jax.ShapeDtypeStruct((SUBLANES, 3 * DN_WIDTH), F32)],
        scratch_shapes=[pltpu.VMEM((1, LANES), F32), pltpu.VMEM((SUBLANES, 3 * DN_WIDTH), F32),
                        pltpu.VMEM((3, tm + SUBLANES, COL_CHUNK), F32)],
        compiler_params=pltpu.CompilerParams(dimension_semantics=("arbitrary",),
                                             vmem_limit_bytes=VMEM_LIMIT),
        name="inproj",
    )(x2d, nw, *weights, wsmall, bias, carry0, convw, halo0)


def _fox_kernel(q_t_ref, k_ref, v_t_ref, kb_ref, kp_ref, v_tp_ref, kbp_ref,
                o_ref, m_ref, l_ref, acc_ref, t_ref, *, tq, tk, pairs):
    pg = pl.program_id(1)
    i = pl.program_id(2)
    heads = range(2 * pairs)
    sub = lax.broadcasted_iota(jnp.int32, (LANES, 1), 0)
    rhs = []
    for hh in heads:
        pp, h = divmod(hh, 2)
        q_t = q_t_ref[pp * LANES:(pp + 1) * LANES, :]
        mine = sub < FOX_HEAD_DIM if h == 0 else sub >= FOX_HEAD_DIM
        head = 2 * (pg * pairs + pp) + h
        sel = jnp.where(jnp.logical_and(sub < 3 * FOX_HEADS, (sub % FOX_HEADS) == head),
                        1.0, 0.0).astype(BF16) * jnp.ones((1, tq), BF16)
        rhs.append(jnp.concatenate([jnp.where(mine, q_t, jnp.zeros_like(q_t)), sel], axis=0))

    def scores(kt, kbt):
        lhs = [jnp.concatenate([kt[:, pp * LANES:(pp + 1) * LANES], kbt], axis=1)
               for pp in range(pairs)]
        return [_dot(lhs[hh // 2], rhs[hh]) for hh in heads]

    def scores_into(slot, off):
        ts = scores(k_ref[pl.ds(off, tk), :], kb_ref[pl.ds(off, tk), :])
        for hh in heads:
            t_ref[slot, hh] = ts[hh]

    def update(read_t, v_aug, mask, first):
        def masked_t(hh):
            t = read_t(hh)
            return t if mask is None else jnp.where(mask, t, NEG_INF)

        t_max = [jnp.max(masked_t(hh), axis=0, keepdims=True) for hh in heads]
        if first:
            m_new = t_max
        else:
            m_prev = [m_ref[hh] for hh in heads]
            m_new = [jnp.maximum(m_prev[hh], t_max[hh]) for hh in heads]
        pm = [jnp.exp2(masked_t(hh) - m_new[hh]).astype(BF16) for hh in heads]
        r = [_dot(v_aug[hh * V_ROWS:(hh + 1) * V_ROWS, :], pm[hh]) for hh in heads]
        pv = jnp.concatenate([r[hh][:FOX_HEAD_DIM] for hh in heads], axis=0)
        psum = [r[hh][FOX_HEAD_DIM:FOX_HEAD_DIM + 1] for hh in heads]
        if first:
            for hh in heads:
                l_ref[hh] = psum[hh]
            acc_ref[...] = pv
        else:
            alpha = [jnp.exp2(m_prev[hh] - m_new[hh]) for hh in heads]
            for hh in heads:
                l_ref[hh] = alpha[hh] * l_ref[hh] + psum[hh]
            alpha_rows = jnp.concatenate(
                [jnp.broadcast_to(alpha[hh], (FOX_HEAD_DIM, tq)) for hh in heads], axis=0)
            acc_ref[...] = alpha_rows * acc_ref[...] + pv
        for hh in heads:
            m_ref[hh] = m_new[hh]

    ts_p = scores(kp_ref[...], kbp_ref[...])
    update(lambda hh: ts_p[hh], v_tp_ref[...], None, True)

    def v_at(off):
        return v_t_ref[:, pl.ds(off, tk)]

    def slot(n):
        return lambda hh: t_ref[n, hh]

    pairs_per_q = tq // (2 * tk)
    scores_into(0, 0)

    def body(jj, carry):
        off = pl.multiple_of(jj * 2 * tk, 2 * tk)
        scores_into(1, off + tk)
        update(slot(0), v_at(off), None, False)
        scores_into(0, off + 2 * tk)
        update(slot(1), v_at(off + tk), None, False)
        return carry

    lax.fori_loop(0, i * pairs_per_q, body, 0)

    rr = lax.broadcasted_iota(jnp.int32, (tk, tq), 0)
    cc = lax.broadcasted_iota(jnp.int32, (tk, tq), 1)
    for dp in range(pairs_per_q):
        off = pl.multiple_of(i * tq + dp * 2 * tk, 2 * tk)
        scores_into(1, off + tk)
        update(slot(0), v_at(off), rr + dp * 2 * tk <= cc, False)
        if dp + 1 < pairs_per_q:
            scores_into(0, off + 2 * tk)
        update(slot(1), v_at(off + tk), rr + (dp * 2 + 1) * tk <= cc, False)

    l_rows = jnp.concatenate([jnp.broadcast_to(l_ref[hh], (FOX_HEAD_DIM, tq)) for hh in heads], axis=0)
    o_ref[...] = (acc_ref[...] / l_rows).astype(BF16)


def _fox(q_t, k, v_t, kbias, k_p, v_tp, kbias_p, *, tq, tk, pairs):
    b, s, _ = k.shape
    ngroups = FOX_HEADS // (2 * pairs)
    nq = s // tq
    wl = pairs * LANES
    return pl.pallas_call(
        functools.partial(_fox_kernel, tq=tq, tk=tk, pairs=pairs),
        grid=(b, ngroups, nq),
        in_specs=[
            pl.BlockSpec((wl, tq), lambda bi, pg, i: (pg, bi * nq + i)),
            pl.BlockSpec((None, s, wl), lambda bi, pg, i: (bi, 0, pg)),
            pl.BlockSpec((2 * pairs * V_ROWS, s), lambda bi, pg, i: (pg, bi)),
            pl.BlockSpec((None, s, LANES), lambda bi, pg, i: (bi, 0, 0)),
            pl.BlockSpec((N_META, wl), lambda bi, pg, i: (0, pg)),
            pl.BlockSpec((2 * pairs * V_ROWS, N_META), lambda bi, pg, i: (pg, 0)),
            pl.BlockSpec((N_META, LANES), lambda bi, pg, i: (0, 0)),
        ],
        out_specs=pl.BlockSpec((wl, tq), lambda bi, pg, i: (pg, bi * nq + i)),
        out_shape=jax.ShapeDtypeStruct((FOX_WIDTH, b * s), BF16),
        scratch_shapes=[pltpu.VMEM((2 * pairs, 1, tq), F32), pltpu.VMEM((2 * pairs, 1, tq), F32),
                        pltpu.VMEM((wl, tq), F32), pltpu.VMEM((2, 2 * pairs, tk, tq), F32)],
        compiler_params=pltpu.CompilerParams(
            dimension_semantics=("arbitrary", "arbitrary", "arbitrary"),
            vmem_limit_bytes=VMEM_LIMIT),
        name="fox_attention",
    )(q_t, k, v_t, kbias, k_p, v_tp, kbias_p)


def _dn_kernel(qkv_ref, scol_ref, srow_ref, dzs_ref, onw_ref, s0_ref, *refs, tt, n_cast):
    cast_in, (odn_ref, sfin_ref), cast_out = refs[:n_cast], refs[n_cast:n_cast + 2], refs[n_cast + 2:-1]
    state_ref = refs[-1]
    for src, dst in zip(cast_in, cast_out):
        dst[...] = src[...].astype(BF16)
    i = pl.program_id(1)
    nchunk = tt // DN_CHUNK

    @pl.when(i == 0)
    def _():
        for g in range(qkv_ref.shape[0]):
            state_ref[g] = s0_ref[...]

    bs = DN_BLOCK
    heads = range(DN_HEADS)
    groups = range(qkv_ref.shape[0])
    units = [(g, sb, h) for g in groups for sb in range(tt // bs) for h in heads]

    def slab(g, sb, c0):
        return qkv_ref[g, sb * bs:(sb + 1) * bs, c0:c0 + LANES]

    def small_col(g, sb, lane):
        return scol_ref[g, sb * bs:(sb + 1) * bs, lane:lane + 1]

    rr = lax.broadcasted_iota(jnp.int32, (bs, bs), 0)
    cc = lax.broadcasted_iota(jnp.int32, (bs, bs), 1)
    rc_xor = jnp.bitwise_xor(rr, cc)
    same = rc_xor < DN_CHUNK
    tril_m = jnp.logical_and(same, rr >= cc)
    strict_m = jnp.logical_and(same, rr > cc)
    eye = (rr == cc).astype(F32)

    qs = [slab(g, sb, h * LANES) for g, sb, h in units]
    ks = [slab(g, sb, DN_WIDTH + h * LANES) for g, sb, h in units]
    vs = [slab(g, sb, 2 * DN_WIDTH + h * LANES) for g, sb, h in units]
    states = {(g, h): state_ref[g, h] for g in groups for h in heads}
    betas = [small_col(g, sb, BETA_LANE + h) for g, sb, h in units]
    gcs = [small_col(g, sb, G_LANE + h) for g, sb, h in units]
    gls = [gcs[n] + small_col(g, sb, GSUF_LANE + h) - small_col(g, sb, GRAW_LANE + h)
           for n, (g, sb, h) in enumerate(units)]
    decays = [jnp.exp(jnp.where(
        tril_m, gcs[n] - srow_ref[g, G_LANE + h:G_LANE + h + 1, sb * bs:(sb + 1) * bs], NEG_INF))
        for n, (g, sb, h) in enumerate(units)]
    nu = range(len(units))
    egcs = [jnp.exp(g) for g in gcs]
    kbs = [ks[n] * betas[n] for n in nu]
    ks_b = [k.astype(BF16) for k in ks]
    lmats = [jnp.where(strict_m, _dot_nt(kbs[n].astype(BF16), ks_b[n]) * decays[n], 0.0) for n in nu]
    attns = [(_dot_nt(qs[n].astype(BF16), ks_b[n]) * decays[n]).astype(BF16) for n in nu]
    ainvs = [eye - jnp.where(rc_xor == 1, lm, 0.0) for lm in lmats]
    for lvl in range(1, 6):
        blk = 2 ** lvl
        lvl_m = jnp.logical_and(rc_xor >= blk, rc_xor < 2 * blk)
        ainvs_b = [a.astype(BF16) for a in ainvs]
        mids = [_dot(jnp.where(lvl_m, lmats[n], 0.0).astype(BF16), ainvs_b[n]).astype(BF16)
                for n in nu]
        ainvs = [ainvs[n] - _dot(ainvs_b[n], mids[n]) for n in nu]
    sols = [_dot(ainvs[n].astype(BF16),
                 jnp.concatenate([vs[n] * betas[n], kbs[n] * egcs[n]], axis=1).astype(BF16))
            for n in nu]
    uus = [sol[:, :DN_HEAD_DIM] for sol in sols]
    wws = [sol[:, DN_HEAD_DIM:] for sol in sols]
    qds = [qs[n] * egcs[n] for n in nu]
    kds = [(ks[n] * jnp.exp(gls[n] - gcs[n])).astype(BF16) for n in nu]

    vnews = [[] for _ in nu]
    qss = [[] for _ in nu]
    for c in range(nchunk):
        csb, lc = divmod(c, bs // DN_CHUNK)
        lo, hi = lc * DN_CHUNK, (lc + 1) * DN_CHUNK
        cur = [(n, (g, h)) for n, (g, sb, h) in enumerate(units) if sb == csb]
        rs = {n: _dot(jnp.concatenate([wws[n][lo:hi], qds[n][lo:hi]], axis=0).astype(BF16),
                      states[key].astype(BF16)) for n, key in cur}
        vns = {n: uus[n][lo:hi] - rs[n][:DN_CHUNK] for n, _ in cur}
        states.update({key: jnp.exp(gls[n][lo:lo + 1, :]) * states[key]
                       + _dot_tn(kds[n][lo:hi], vns[n].astype(BF16)) for n, key in cur})
        for n, _ in cur:
            qss[n].append(rs[n][DN_CHUNK:])
            vnews[n].append(vns[n])
    for (g, h), state in states.items():
        state_ref[g, h] = state
    for h in heads:
        sfin_ref[h] = states[(0, h)]
    for n, (g, sb, h) in enumerate(units):
        o = (jnp.concatenate(qss[n], axis=0)
             + _dot(attns[n], jnp.concatenate(vnews[n], axis=0).astype(BF16)))
        o = o * lax.rsqrt(jnp.mean(o * o, axis=-1, keepdims=True) + EPS) * onw_ref[...]
        o = o * dzs_ref[g, sb * bs:(sb + 1) * bs, h * LANES:(h + 1) * LANES]
        odn_ref[g, sb * bs:(sb + 1) * bs, h * LANES:(h + 1) * LANES] = o.astype(BF16)


def _cast_row_block(rows, nsteps):
    tile = 2 * SUBLANES
    for blk in range(tile, rows, tile):
        if rows % blk == 0 and rows // blk <= nsteps:
            return blk
    return rows


def _deltanet(qkv, scol, srow, dzs, onw, s0, *, group, tt, cast_weights=()):
    b, s, _ = qkv.shape
    nt = s // tt
    nsteps = (b // group) * nt
    cast_specs = []
    for w in cast_weights:
        blk = _cast_row_block(w.shape[0], nsteps)
        last = w.shape[0] // blk - 1
        cast_specs.append(pl.BlockSpec(
            (blk, w.shape[1]), lambda bi, i, last=last: (jnp.minimum(bi * nt + i, last), 0)))
    return pl.pallas_call(
        functools.partial(_dn_kernel, tt=tt, n_cast=len(cast_weights)),
        grid=(b // group, nt),
        in_specs=[
            pl.BlockSpec((group, tt, 3 * DN_WIDTH), lambda bi, i: (bi, i, 0)),
            pl.BlockSpec((group, tt, LANES), lambda bi, i: (bi, i, 0)),
            pl.BlockSpec((group, SROW, tt), lambda bi, i: (bi, 0, i)),
            pl.BlockSpec((group, tt, DN_WIDTH), lambda bi, i: (bi, i, 0)),
            pl.BlockSpec((1, DN_HEAD_DIM), lambda bi, i: (0, 0)),
            pl.BlockSpec((DN_HEADS, DN_HEAD_DIM, DN_HEAD_DIM), lambda bi, i: (0, 0, 0)),
        ] + cast_specs,
        out_specs=[pl.BlockSpec((group, tt, DN_WIDTH), lambda bi, i: (bi, i, 0)),
                   pl.BlockSpec((DN_HEADS, DN_HEAD_DIM, DN_HEAD_DIM), lambda bi, i: (0, 0, 0))]
        + cast_specs,
        out_shape=[jax.ShapeDtypeStruct((b, s, DN_WIDTH), BF16),
                   jax.ShapeDtypeStruct((DN_HEADS, DN_HEAD_DIM, DN_HEAD_DIM), F32)]
        + [jax.ShapeDtypeStruct(w.shape, BF16) for w in cast_weights],
        scratch_shapes=[pltpu.VMEM((group, DN_HEADS, DN_HEAD_DIM, DN_HEAD_DIM), F32)],
        compiler_params=pltpu.CompilerParams(dimension_semantics=("arbitrary", "arbitrary"),
                                             vmem_limit_bytes=VMEM_LIMIT),
        name="deltanet",
    )(qkv, scol, srow, dzs, onw, s0, *cast_weights)


def _tail_kernel(x_ref, ofox_ref, odn_ref, gates_ref, wbf_ref, wbd_ref, wout_ref, fnw_ref,
                 wg_ref, wu_ref, wd_ref, finw_ref, o_ref, *, ff_bounds):
    a = _dot_tn(ofox_ref[...], wbf_ref[...])
    bb = _dot(odn_ref[...], wbd_ref[...])
    y = gates_ref[:, :D_MODEL] * a + gates_ref[:, D_MODEL:] * bb
    h1 = x_ref[...] + _dot(y.astype(BF16), wout_ref[...])
    n = (h1 * lax.rsqrt(jnp.mean(h1 * h1, axis=-1, keepdims=True) + EPS) * fnw_ref[...]).astype(BF16)
    acc = h1
    for lo, hi in zip(ff_bounds[:-1], ff_bounds[1:]):
        gt = _dot(n, wg_ref[:, lo:hi])
        up = _dot(n, wu_ref[:, lo:hi])
        act = (gt * _sigmoid(gt) * up).astype(BF16)
        acc = acc + _dot(act, wd_ref[lo:hi, :])
    o_ref[...] = acc * lax.rsqrt(jnp.mean(acc * acc, axis=-1, keepdims=True) + EPS) * finw_ref[...]


def _tail(x2d, ofox, odn, gates, wbf, wbd, wout, fnw, wg, wu, wd, finw, *, tm, ff_bounds):
    m = x2d.shape[0]
    row = lambda w: pl.BlockSpec((tm, w), lambda i: (i, 0))
    return pl.pallas_call(
        functools.partial(_tail_kernel, ff_bounds=ff_bounds),
        grid=(m // tm,),
        in_specs=[row(D_MODEL), pl.BlockSpec((FOX_WIDTH, tm), lambda i: (0, i)), row(DN_WIDTH),
                  row(2 * D_MODEL),
                  _const_spec((FOX_WIDTH, D_MODEL)), _const_spec((DN_WIDTH, D_MODEL)),
                  _const_spec((D_MODEL, D_MODEL)), _const_spec((1, D_MODEL)),
                  _const_spec((D_MODEL, D_FF)), _const_spec((D_MODEL, D_FF)),
                  _const_spec((D_FF, D_MODEL)), _const_spec((1, D_MODEL))],
        out_specs=row(D_MODEL),
        out_shape=jax.ShapeDtypeStruct((m, D_MODEL), F32),
        compiler_params=pltpu.CompilerParams(dimension_semantics=("arbitrary",),
                                             vmem_limit_bytes=VMEM_LIMIT),
        name="merge_ffn",
    )(x2d, ofox, odn, gates, wbf, wbd, wout, fnw, wg, wu, wd, finw)


def _pick_tile(n, pref):
    t = min(pref, n)
    while n % t:
        t //= 2
    return t


def kernel(x, meta_tokens, mix_norm_w, w_in, fox_forget_bias, dn_conv_w, dn_a_log, dn_dt_bias,
           dn_out_norm_w, w_branch_fox, w_branch_dn, w_out, ffn_norm_w, w_ffn_gate, w_ffn_up,
           w_ffn_down, final_norm_w):
    b, s, _ = x.shape
    assert mix_norm_w.shape[0] == 1, "single layer only"
    assert s % PREFIX == 0
    m = b * s

    *weights, wsmall = _wprep(w_in)
    bias = jnp.zeros((SUBLANES, LANES), F32)
    bias = bias.at[0, LOGF_LANE:LOGF_LANE + FOX_HEADS].set(fox_forget_bias[0].astype(F32))
    bias = bias.at[0, G_LANE:N_SMALL].set(jnp.tile(dn_dt_bias[0].astype(F32), 3))
    bias = bias.at[1, G_LANE:N_SMALL].set(jnp.tile(dn_a_log[0].astype(F32), 3))
    nw = mix_norm_w[0].reshape(1, D_MODEL).astype(F32)

    x_p = jnp.concatenate([jnp.zeros((N_PAD, D_MODEL), F32), meta_tokens.astype(F32)], axis=0)
    convw = dn_conv_w[0].astype(F32)
    _, k_p, v_tp, kbias_p, qkv_p, dzs_p, _, scol_p, srow_p, conv_tail_p = _inproj(
        x_p, nw, weights, wsmall, bias, jnp.zeros((1, LANES), F32), convw,
        jnp.zeros((SUBLANES, 3 * DN_WIDTH), F32), tm=PREFIX, tiles_per_batch=1, n_pad=N_PAD)

    tm = _pick_tile(s, 512)
    q_t, k, v_t, kbias, qkv, dzs, gates, scol, srow3, _ = _inproj(
        x.reshape(m, D_MODEL), nw, weights, wsmall, bias, scol_p[PREFIX - 1:PREFIX, :], convw,
        conv_tail_p, tm=tm, tiles_per_batch=s // tm, n_pad=0)

    scol3 = scol.reshape(b, s, LANES)
    tq = _pick_tile(s, 512)
    ofox_t = _fox(q_t, k.reshape(b, s, FOX_WIDTH), v_t, kbias.reshape(b, s, LANES),
                  k_p[N_PAD:], v_tp[:, N_PAD:], kbias_p[N_PAD:], tq=tq, tk=min(256, tq // 2),
                  pairs=FOX_PAIRS_PER_STEP)

    onw = dn_out_norm_w[0].reshape(1, DN_HEAD_DIM).astype(F32)
    _, s_prefix = _deltanet(
        qkv_p[None], scol_p[None], srow_p, dzs_p[None], onw,
        jnp.zeros((DN_HEADS, DN_HEAD_DIM, DN_HEAD_DIM), F32), group=1, tt=PREFIX)
    group = _pick_tile(b, DN_UNITS // DN_HEADS)
    odn, _, wbf, wbd, wout, wg, wu, wd = _deltanet(
        qkv.reshape(b, s, 3 * DN_WIDTH), scol3, srow3, dzs.reshape(b, s, DN_WIDTH), onw, s_prefix,
        group=group, tt=_pick_tile(s, DN_UNITS // (DN_HEADS * group) * DN_BLOCK),
        cast_weights=(w_branch_fox[0], w_branch_dn[0], w_out[0], w_ffn_gate[0], w_ffn_up[0],
                      w_ffn_down[0]))

    out = _tail(
        x.reshape(m, D_MODEL), ofox_t, odn.reshape(m, DN_WIDTH), gates, wbf, wbd, wout,
        ffn_norm_w[0].reshape(1, D_MODEL).astype(F32), wg, wu, wd,
        final_norm_w.reshape(1, D_MODEL).astype(F32),
        tm=_pick_tile(m, 512), ff_bounds=FF_BOUNDS)
    return out.reshape(b, s, D_MODEL)
```

```python
import functools

import jax
import jax.numpy as jnp
from jax import lax
from jax.experimental import pallas as pl
from jax.experimental.pallas import tpu as pltpu

F32 = jnp.float32
BF16 = jnp.bfloat16

D_MODEL = 1024
N_META = 16
PREFIX = 128
N_PAD = PREFIX - N_META
FOX_HEADS = 8
FOX_HEAD_DIM = 64
FOX_WIDTH = FOX_HEADS * FOX_HEAD_DIM
V_ROWS = FOX_HEAD_DIM + 8
FOX_PAIRS_PER_STEP = 4
DN_HEADS = 4
DN_HEAD_DIM = 128
DN_WIDTH = DN_HEADS * DN_HEAD_DIM
DN_CHUNK = 64
DN_BLOCK = 2 * DN_CHUNK
DN_UNITS = 16
CONV_WIDTH = 4
D_FF = 2816
MXU_DIM = 256
FF_BOUNDS = (0, (D_FF // MXU_DIM + 1) // 2 * MXU_DIM, D_FF)
EPS = 1e-6
NEG_INF = -1e30
LOG2E = 1.4426950408889634

LANES = 128
SUBLANES = 8
COL_CHUNK = 512
LOGF_LANE = 0
BETA_LANE = FOX_HEADS
G_LANE = BETA_LANE + DN_HEADS
GSUF_LANE = G_LANE + DN_HEADS
GRAW_LANE = GSUF_LANE + DN_HEADS
N_SMALL = GRAW_LANE + DN_HEADS
SROW = 24
VMEM_LIMIT = 56 * 1024 * 1024


def _const_spec(shape):
    nd = len(shape)
    return pl.BlockSpec(shape, lambda *_: (0,) * nd, pipeline_mode=pl.Buffered(1))


def _sigmoid(x):
    return 0.5 * jnp.tanh(0.5 * x) + 0.5


def _dot(a, b, **kw):
    return jnp.dot(a, b, preferred_element_type=F32, **kw)


def _dot_nt(a, b, **kw):
    return lax.dot_general(a, b, (((1,), (1,)), ((), ())), preferred_element_type=F32, **kw)


def _dot_tn(a, b, **kw):
    return lax.dot_general(a, b, (((0,), (0,)), ((), ())), preferred_element_type=F32, **kw)


def _inproj_kernel(x_ref, nw_ref, wq_t_ref, wv_t_ref, wk_ref, wdn_ref, wrest_ref, wsmall_ref, bias_ref,
                   carry0_ref,
                   convw_ref, halo0_ref,
                   q_t_ref, k_ref, v_t_ref, kbias_ref, dqkv_ref, dzs_ref, gates_ref, scol_ref,
                   srow_ref, tail_ref, carry_ref, halo_ref, conv_ref, *, tm, tiles_per_batch, n_pad):
    i = pl.program_id(0)

    @pl.when(i % tiles_per_batch == 0)
    def _():
        carry_ref[...] = carry0_ref[...]
        halo_ref[...] = halo0_ref[...]

    x = x_ref[...]
    ms = jnp.mean(x * x, axis=-1, keepdims=True)
    hn = (x * lax.rsqrt(ms + EPS) * nw_ref[...]).astype(BF16)

    z = _dot(hn, wsmall_ref[...]) + bias_ref[0:1, :]
    lane = lax.broadcasted_iota(jnp.int32, (tm, LANES), 1)
    row = lax.broadcasted_iota(jnp.int32, (tm, LANES), 0)
    e = jnp.exp(-jnp.abs(z))
    l1p = jnp.log1p(e)
    logf = jnp.minimum(z, 0.0) - l1p
    softplus = jnp.maximum(z, 0.0) + l1p
    sig = jnp.where(z >= 0.0, 1.0, e) / (1.0 + e)
    g = -jnp.exp(bias_ref[1:2, :]) * softplus
    if n_pad:
        vm = (row + (i % tiles_per_batch) * tm >= n_pad).astype(F32)
        sig = sig * vm
        g = g * vm
    val = jnp.where(lane < BETA_LANE, logf,
                    jnp.where(lane < G_LANE, sig, jnp.where(lane < N_SMALL, g, 0.0)))
    in_chunk = row % DN_CHUNK
    is_logf = lane < BETA_LANE
    is_gpre = jnp.logical_and(lane >= G_LANE, lane < GSUF_LANE)
    is_gsuf = jnp.logical_and(lane >= GSUF_LANE, lane < GRAW_LANE)
    scan = val
    sh = 1
    while sh < tm:
        take_up = jnp.logical_and(is_logf, row >= sh)
        if sh < DN_CHUNK:
            take_up = jnp.logical_or(take_up, jnp.logical_and(is_gpre, in_chunk >= sh))
            take_down = jnp.logical_and(is_gsuf, in_chunk + sh < DN_CHUNK)
            below = jnp.where(take_down, pltpu.roll(scan, tm - sh, axis=0), 0.0)
        else:
            below = 0.0
        scan = scan + jnp.where(take_up, pltpu.roll(scan, sh, axis=0), below)
        sh *= 2
    out = scan + jnp.where(is_logf, carry_ref[...], 0.0)
    carry_ref[...] = out[tm - 1:tm, :]
    scol_ref[...] = out
    srow_ref[...] = out.T[:SROW, :]
    nc = jnp.where(is_logf, out * -LOG2E, 0.0)
    hi = nc.astype(BF16).astype(F32)
    mid = (nc - hi).astype(BF16).astype(F32)
    lo = nc - hi - mid
    kbias = hi + pltpu.roll(mid, FOX_HEADS, axis=1) + pltpu.roll(lo, 2 * FOX_HEADS, axis=1)
    kbias_ref[...] = kbias.astype(BF16)

    def main_chunk(c):
        if c == 0:
            return lambda: _dot(hn, wk_ref[...])
        ref, first = (wdn_ref, 1) if c < 4 else (wrest_ref, 4)
        return lambda: _dot(hn, ref[:, (c - first) * COL_CHUNK:(c - first + 1) * COL_CHUNK])

    def store_q_t(acc):
        q_t_ref[...] = (acc * (FOX_HEAD_DIM ** -0.5 * LOG2E)).astype(BF16)

    def store_v_t(acc):
        v_t = acc.astype(BF16)
        ones = jnp.ones((V_ROWS - FOX_HEAD_DIM, tm), BF16)
        v_t_ref[...] = jnp.concatenate(
            [piece for h in range(FOX_HEADS)
             for piece in (v_t[h * FOX_HEAD_DIM:(h + 1) * FOX_HEAD_DIM], ones)], axis=0)

    def store_k(acc):
        k_ref[...] = acc.astype(BF16)

    def store_dn(which):
        def epilogue(acc):
            cols = slice(which * COL_CHUNK, (which + 1) * COL_CHUNK)
            if n_pad:
                acc = acc * vm[:, :1]
            conv_ref[which, 0:SUBLANES, :] = halo_ref[:, cols]
            conv_ref[which, SUBLANES:, :] = acc
            halo_ref[:, cols] = acc[tm - SUBLANES:, :]
            tail_ref[:, cols] = acc[tm - SUBLANES:, :]
            a = convw_ref[CONV_WIDTH - 1:CONV_WIDTH, cols] * acc
            for t in range(CONV_WIDTH - 1):
                start = SUBLANES - (CONV_WIDTH - 1) + t
                a = a + convw_ref[t:t + 1, cols] * conv_ref[which, start:start + tm, :]
            a = a * _sigmoid(a)
            if which == 2:
                dqkv_ref[:, cols] = a
            else:
                scale = DN_HEAD_DIM ** -0.5 if which == 0 else 1.0
                for h in range(DN_HEADS):
                    ah = a[:, h * LANES:(h + 1) * LANES]
                    inv = lax.rsqrt(jnp.sum(ah * ah, axis=-1, keepdims=True) + EPS) * scale
                    lo_col = which * COL_CHUNK + h * LANES
                    dqkv_ref[:, lo_col:lo_col + LANES] = ah * inv
        return epilogue

    def store_dzs(acc):
        dzs_ref[...] = acc * _sigmoid(acc)

    def store_gates(j):
        def epilogue(acc):
            gates_ref[:, j * COL_CHUNK:(j + 1) * COL_CHUNK] = _sigmoid(acc)
        return epilogue

    for which in range(3):
        store_dn(which)(main_chunk(1 + which)())
    store_dzs(main_chunk(4)())
    for j in range(2 * D_MODEL // COL_CHUNK):
        store_gates(j)(main_chunk(5 + j)())
    store_v_t(_dot_nt(wv_t_ref[...], hn))
    store_q_t(_dot_nt(wq_t_ref[...], hn))
    store_k(main_chunk(0)())


def _inproj(x2d, nw, weights, wsmall, bias, carry0, convw, halo0, *, tm, tiles_per_batch, n_pad):
    m = x2d.shape[0]
    row = lambda w: pl.BlockSpec((tm, w), lambda i: (i, 0))
    col = lambda h: pl.BlockSpec((h, tm), lambda i: (0, i))
    return pl.pallas_call(
        functools.partial(_inproj_kernel, tm=tm, tiles_per_batch=tiles_per_batch, n_pad=n_pad),
        grid=(m // tm,),
        in_specs=[row(D_MODEL), _const_spec((1, D_MODEL))] + [_const_spec(w.shape) for w in weights] + [
                  _const_spec((D_MODEL, LANES)), _const_spec((SUBLANES, LANES)),
                  _const_spec((1, LANES)), _const_spec((CONV_WIDTH, 3 * DN_WIDTH)),
                  _const_spec((SUBLANES, 3 * DN_WIDTH))],
        out_specs=[col(FOX_WIDTH), row(FOX_WIDTH), col(FOX_HEADS * V_ROWS), row(LANES),
                   row(3 * DN_WIDTH), row(DN_WIDTH), row(2 * D_MODEL),
                   row(LANES),
                   pl.BlockSpec((None, SROW, tm),
                                lambda i: (i // tiles_per_batch, 0, i % tiles_per_batch)),
                   pl.BlockSpec((SUBLANES, 3 * DN_WIDTH), lambda i: (0, 0))],
        out_shape=[jax.ShapeDtypeStruct((FOX_WIDTH, m), BF16),
                   jax.ShapeDtypeStruct((m, FOX_WIDTH), BF16),
                   jax.ShapeDtypeStruct((FOX_HEADS * V_ROWS, m), BF16),
                   jax.ShapeDtypeStruct((m, LANES), BF16),
                   jax.ShapeDtypeStruct((m, 3 * DN_WIDTH), F32),
                   jax.ShapeDtypeStruct((m, DN_WIDTH), F32),
                   jax.ShapeDtypeStruct((m, 2 * D_MODEL), F32),
                   jax.ShapeDtypeStruct((m, LANES), F32),
                   jax.ShapeDtypeStruct((m // (tm * tiles_per_batch), SROW,
                                         tm * tiles_per_batch), F32),
                   jax.ShapeDtypeStruct((SUBLANES, 3 * DN_WIDTH), F32)],
        scratch_shapes=[pltpu.VMEM((1, LANES), F32), pltpu.VMEM((SUBLANES, 3 * DN_WIDTH), F32),
                        pltpu.VMEM((3, tm + SUBLANES, COL_CHUNK), F32)],
        compiler_params=pltpu.CompilerParams(dimension_semantics=("arbitrary",),
                                             vmem_limit_bytes=VMEM_LIMIT),
        name="inproj",
    )(x2d, nw, *weights, wsmall, bias, carry0, convw, halo0)


def _fox_kernel(q_t_ref, k_ref, v_t_ref, kb_ref, kp_ref, v_tp_ref, kbp_ref,
                o_ref, m_ref, l_ref, acc_ref, t_ref, *, tq, tk, pairs):
    pg = pl.program_id(1)
    i = pl.program_id(2)
    heads = range(2 * pairs)
    sub = lax.broadcasted_iota(jnp.int32, (LANES, 1), 0)
    rhs = []
    for hh in heads:
        pp, h = divmod(hh, 2)
        q_t = q_t_ref[pp * LANES:(pp + 1) * LANES, :]
        mine = sub < FOX_HEAD_DIM if h == 0 else sub >= FOX_HEAD_DIM
        head = 2 * (pg * pairs + pp) + h
        sel = jnp.where(jnp.logical_and(sub < 3 * FOX_HEADS, (sub % FOX_HEADS) == head),
                        1.0, 0.0).astype(BF16) * jnp.ones((1, tq), BF16)
        rhs.append(jnp.concatenate([jnp.where(mine, q_t, jnp.zeros_like(q_t)), sel], axis=0))

    def scores(kt, kbt, q_lo=0):
        lhs = [jnp.concatenate([kt[:, pp * LANES:(pp + 1) * LANES], kbt], axis=1)
               for pp in range(pairs)]
        return [_dot(lhs[hh // 2], rhs[hh][:, q_lo:]) for hh in heads]

    def scores_into(slot, off, q_lo=0):
        ts = scores(k_ref[pl.ds(off, tk), :], kb_ref[pl.ds(off, tk), :], q_lo)
        for hh in heads:
            t_ref[slot, hh, :, :tq - q_lo] = ts[hh]

    def update(read_t, v_aug, mask, first, q_lo=0):
        width = tq - q_lo

        def masked_t(hh):
            t = read_t(hh)
            return t if mask is None else jnp.where(mask, t, NEG_INF)

        t_max = [jnp.max(masked_t(hh), axis=0, keepdims=True) for hh in heads]
        if first:
            m_new = t_max
        else:
            m_prev = [m_ref[hh, :, q_lo:] for hh in heads]
            m_new = [jnp.maximum(m_prev[hh], t_max[hh]) for hh in heads]
        pm = [jnp.exp2(masked_t(hh) - m_new[hh]).astype(BF16) for hh in heads]
        r = [_dot(v_aug[hh * V_ROWS:(hh + 1) * V_ROWS, :], pm[hh]) for hh in heads]
        pv = jnp.concatenate([r[hh][:FOX_HEAD_DIM] for hh in heads], axis=0)
        psum = [r[hh][FOX_HEAD_DIM:FOX_HEAD_DIM + 1] for hh in heads]
        if first:
            for hh in heads:
                l_ref[hh, :, q_lo:] = psum[hh]
            acc_ref[:, q_lo:] = pv
        else:
            alpha = [jnp.exp2(m_prev[hh] - m_new[hh]) for hh in heads]
            for hh in heads:
                l_ref[hh, :, q_lo:] = alpha[hh] * l_ref[hh, :, q_lo:] + psum[hh]
            alpha_rows = jnp.concatenate(
                [jnp.broadcast_to(alpha[hh], (FOX_HEAD_DIM, width)) for hh in heads], axis=0)
            acc_ref[:, q_lo:] = alpha_rows * acc_ref[:, q_lo:] + pv
        for hh in heads:
            m_ref[hh, :, q_lo:] = m_new[hh]

    ts_p = scores(kp_ref[...], kbp_ref[...])
    update(lambda hh: ts_p[hh], v_tp_ref[...], None, True)

    def v_at(off):
        return v_t_ref[:, pl.ds(off, tk)]

    def slot(n, q_lo=0):
        return lambda hh: t_ref[n, hh, :, :tq - q_lo]

    pairs_per_q = tq // (2 * tk)
    scores_into(0, 0)

    def body(jj, carry):
        off = pl.multiple_of(jj * 2 * tk, 2 * tk)
        scores_into(1, off + tk)
        update(slot(0), v_at(off), None, False)
        scores_into(0, off + 2 * tk)
        update(slot(1), v_at(off + tk), None, False)
        return carry

    lax.fori_loop(0, i * pairs_per_q, body, 0)

    def causal(q_lo):
        rr = lax.broadcasted_iota(jnp.int32, (tk, tq - q_lo), 0)
        cc = lax.broadcasted_iota(jnp.int32, (tk, tq - q_lo), 1)
        return rr <= cc

    for dp in range(pairs_per_q):
        off = pl.multiple_of(i * tq + dp * 2 * tk, 2 * tk)
        q_even, q_odd = dp * 2 * tk, (dp * 2 + 1) * tk
        scores_into(1, off + tk, q_odd)
        update(slot(0, q_even), v_at(off), causal(q_even), False, q_even)
        if dp + 1 < pairs_per_q:
            scores_into(0, off + 2 * tk, q_even + 2 * tk)
        update(slot(1, q_odd), v_at(off + tk), causal(q_odd), False, q_odd)

    l_rows = jnp.concatenate([jnp.broadcast_to(l_ref[hh], (FOX_HEAD_DIM, tq)) for hh in heads], axis=0)
    o_ref[...] = (acc_ref[...] / l_rows).astype(BF16)


def _fox(q_t, k, v_t, kbias, k_p, v_tp, kbias_p, *, tq, tk, pairs):
    b, s, _ = k.shape
    ngroups = FOX_HEADS // (2 * pairs)
    nq = s // tq
    wl = pairs * LANES
    return pl.pallas_call(
        functools.partial(_fox_kernel, tq=tq, tk=tk, pairs=pairs),
        grid=(b, ngroups, nq),
        in_specs=[
            pl.BlockSpec((wl, tq), lambda bi, pg, i: (pg, bi * nq + i)),
            pl.BlockSpec((None, s, wl), lambda bi, pg, i: (bi, 0, pg)),
            pl.BlockSpec((2 * pairs * V_ROWS, s), lambda bi, pg, i: (pg, bi)),
            pl.BlockSpec((None, s, LANES), lambda bi, pg, i: (bi, 0, 0)),
            pl.BlockSpec((N_META, wl), lambda bi, pg, i: (0, pg)),
            pl.BlockSpec((2 * pairs * V_ROWS, N_META), lambda bi, pg, i: (pg, 0)),
            pl.BlockSpec((N_META, LANES), lambda bi, pg, i: (0, 0)),
        ],
        out_specs=pl.BlockSpec((wl, tq), lambda bi, pg, i: (pg, bi * nq + i)),
        out_shape=jax.ShapeDtypeStruct((FOX_WIDTH, b * s), BF16),
        scratch_shapes=[pltpu.VMEM((2 * pairs, 1, tq), F32), pltpu.VMEM((2 * pairs, 1, tq), F32),
                        pltpu.VMEM((wl, tq), F32), pltpu.VMEM((2, 2 * pairs, tk, tq), F32)],
        compiler_params=pltpu.CompilerParams(
            dimension_semantics=("arbitrary", "arbitrary", "arbitrary"),
            vmem_limit_bytes=VMEM_LIMIT),
        name="fox_attention",
    )(q_t, k, v_t, kbias, k_p, v_tp, kbias_p)


def _dn_kernel(qkv_ref, scol_ref, srow_ref, dzs_ref, onw_ref, s0_ref, *refs, tt, n_cast):
    cast_in, (odn_ref, sfin_ref), cast_out = refs[:n_cast], refs[n_cast:n_cast + 2], refs[n_cast + 2:-1]
    state_ref = refs[-1]
    for src, dst in zip(cast_in, cast_out):
        dst[...] = src[...].astype(BF16)
    i = pl.program_id(1)
    nchunk = tt // DN_CHUNK

    @pl.when(i == 0)
    def _():
        for g in range(qkv_ref.shape[0]):
            state_ref[g] = s0_ref[...]

    bs = DN_BLOCK
    heads = range(DN_HEADS)
    groups = range(qkv_ref.shape[0])
    units = [(g, sb, h) for g in groups for sb in range(tt // bs) for h in heads]

    def slab(g, sb, c0):
        return qkv_ref[g, sb * bs:(sb + 1) * bs, c0:c0 + LANES]

    def small_col(g, sb, lane):
        return scol_ref[g, sb * bs:(sb + 1) * bs, lane:lane + 1]

    rr = lax.broadcasted_iota(jnp.int32, (bs, bs), 0)
    cc = lax.broadcasted_iota(jnp.int32, (bs, bs), 1)
    rc_xor = jnp.bitwise_xor(rr, cc)
    same = rc_xor < DN_CHUNK
    tril_m = jnp.logical_and(same, rr >= cc)
    strict_m = jnp.logical_and(same, rr > cc)
    eye = (rr == cc).astype(F32)

    qs = [slab(g, sb, h * LANES) for g, sb, h in units]
    ks = [slab(g, sb, DN_WIDTH + h * LANES) for g, sb, h in units]
    vs = [slab(g, sb, 2 * DN_WIDTH + h * LANES) for g, sb, h in units]
    states = {(g, h): state_ref[g, h] for g in groups for h in heads}
    betas = [small_col(g, sb, BETA_LANE + h) for g, sb, h in units]
    gcs = [small_col(g, sb, G_LANE + h) for g, sb, h in units]
    gls = [gcs[n] + small_col(g, sb, GSUF_LANE + h) - small_col(g, sb, GRAW_LANE + h)
           for n, (g, sb, h) in enumerate(units)]
    decays = [jnp.exp(jnp.where(
        tril_m, gcs[n] - srow_ref[g, G_LANE + h:G_LANE + h + 1, sb * bs:(sb + 1) * bs], NEG_INF))
        for n, (g, sb, h) in enumerate(units)]
    nu = range(len(units))
    egcs = [jnp.exp(g) for g in gcs]
    kbs = [ks[n] * betas[n] for n in nu]
    ks_b = [k.astype(BF16) for k in ks]
    lmats = [jnp.where(strict_m, _dot_nt(kbs[n].astype(BF16), ks_b[n]) * decays[n], 0.0) for n in nu]
    attns = [(_dot_nt(qs[n].astype(BF16), ks_b[n]) * decays[n]).astype(BF16) for n in nu]
    ainvs = [eye - jnp.where(rc_xor == 1, lm, 0.0) for lm in lmats]
    for lvl in range(1, 6):
        blk = 2 ** lvl
        lvl_m = jnp.logical_and(rc_xor >= blk, rc_xor < 2 * blk)
        ainvs_b = [a.astype(BF16) for a in ainvs]
        mids = [_dot(jnp.where(lvl_m, lmats[n], 0.0).astype(BF16), ainvs_b[n]).astype(BF16)
                for n in nu]
        ainvs = [ainvs[n] - _dot(ainvs_b[n], mids[n]) for n in nu]
    sols = [_dot(ainvs[n].astype(BF16),
                 jnp.concatenate([vs[n] * betas[n], kbs[n] * egcs[n]], axis=1).astype(BF16))
            for n in nu]
    uus = [sol[:, :DN_HEAD_DIM] for sol in sols]
    wws = [sol[:, DN_HEAD_DIM:] for sol in sols]
    qds = [qs[n] * egcs[n] for n in nu]
    kds = [(ks[n] * jnp.exp(gls[n] - gcs[n])).astype(BF16) for n in nu]

    vnews = [[] for _ in nu]
    qss = [[] for _ in nu]
    for c in range(nchunk):
        csb, lc = divmod(c, bs // DN_CHUNK)
        lo, hi = lc * DN_CHUNK, (lc + 1) * DN_CHUNK
        cur = [(n, (g, h)) for n, (g, sb, h) in enumerate(units) if sb == csb]
        rs = {n: _dot(jnp.concatenate([wws[n][lo:hi], qds[n][lo:hi]], axis=0).astype(BF16),
                      states[key].astype(BF16)) for n, key in cur}
        vns = {n: uus[n][lo:hi] - rs[n][:DN_CHUNK] for n, _ in cur}
        states.update({key: jnp.exp(gls[n][lo:lo + 1, :]) * states[key]
                       + _dot_tn(kds[n][lo:hi], vns[n].astype(BF16)) for n, key in cur})
        for n, _ in cur:
            qss[n].append(rs[n][DN_CHUNK:])
            vnews[n].append(vns[n])
    for (g, h), state in states.items():
        state_ref[g, h] = state
    for h in heads:
        sfin_ref[h] = states[(0, h)]
    for n, (g, sb, h) in enumerate(units):
        o = (jnp.concatenate(qss[n], axis=0)
             + _dot(attns[n], jnp.concatenate(vnews[n], axis=0).astype(BF16)))
        o = o * lax.rsqrt(jnp.mean(o * o, axis=-1, keepdims=True) + EPS) * onw_ref[...]
        o = o * dzs_ref[g, sb * bs:(sb + 1) * bs, h * LANES:(h + 1) * LANES]
        odn_ref[g, sb * bs:(sb + 1) * bs, h * LANES:(h + 1) * LANES] = o.astype(BF16)


def _cast_row_block(rows, nsteps):
    tile = 2 * SUBLANES
    for blk in range(tile, rows, tile):
        if rows % blk == 0 and rows // blk <= nsteps:
            return blk
    return rows


def _deltanet(qkv, scol, srow, dzs, onw, s0, *, group, tt, cast_weights=()):
    b, s, _ = qkv.shape
    nt = s // tt
    nsteps = (b // group) * nt
    cast_specs = []
    for w in cast_weights:
        blk = _cast_row_block(w.shape[0], nsteps)
        last = w.shape[0] // blk - 1
        cast_specs.append(pl.BlockSpec(
            (blk, w.shape[1]), lambda bi, i, last=last: (jnp.minimum(bi * nt + i, last), 0)))
    return pl.pallas_call(
        functools.partial(_dn_kernel, tt=tt, n_cast=len(cast_weights)),
        grid=(b // group, nt),
        in_specs=[
            pl.BlockSpec((group, tt, 3 * DN_WIDTH), lambda bi, i: (bi, i, 0)),
            pl.BlockSpec((group, tt, LANES), lambda bi, i: (bi, i, 0)),
            pl.BlockSpec((group, SROW, tt), lambda bi, i: (bi, 0, i)),
            pl.BlockSpec((group, tt, DN_WIDTH), lambda bi, i: (bi, i, 0)),
            pl.BlockSpec((1, DN_HEAD_DIM), lambda bi, i: (0, 0)),
            pl.BlockSpec((DN_HEADS, DN_HEAD_DIM, DN_HEAD_DIM), lambda bi, i: (0, 0, 0)),
        ] + cast_specs,
        out_specs=[pl.BlockSpec((group, tt, DN_WIDTH), lambda bi, i: (bi, i, 0)),
                   pl.BlockSpec((DN_HEADS, DN_HEAD_DIM, DN_HEAD_DIM), lambda bi, i: (0, 0, 0))]
        + cast_specs,
        out_shape=[jax.ShapeDtypeStruct((b, s, DN_WIDTH), BF16),
                   jax.ShapeDtypeStruct((DN_HEADS, DN_HEAD_DIM, DN_HEAD_DIM), F32)]
        + [jax.ShapeDtypeStruct(w.shape, BF16) for w in cast_weights],
        scratch_shapes=[pltpu.VMEM((group, DN_HEADS, DN_HEAD_DIM, DN_HEAD_DIM), F32)],
        compiler_params=pltpu.CompilerParams(dimension_semantics=("arbitrary", "arbitrary"),
                                             vmem_limit_bytes=VMEM_LIMIT),
        name="deltanet",
    )(qkv, scol, srow, dzs, onw, s0, *cast_weights)


def _tail_kernel(x_ref, ofox_ref, odn_ref, gates_ref, wbf_ref, wbd_ref, wout_ref, fnw_ref,
                 wg_ref, wu_ref, wd_ref, finw_ref, o_ref, *, ff_bounds):
    a = _dot_tn(ofox_ref[...], wbf_ref[...])
    bb = _dot(odn_ref[...], wbd_ref[...])
    y = gates_ref[:, :D_MODEL] * a + gates_ref[:, D_MODEL:] * bb
    h1 = x_ref[...] + _dot(y.astype(BF16), wout_ref[...])
    n = (h1 * lax.rsqrt(jnp.mean(h1 * h1, axis=-1, keepdims=True) + EPS) * fnw_ref[...]).astype(BF16)
    acc = h1
    for lo, hi in zip(ff_bounds[:-1], ff_bounds[1:]):
        gt = _dot(n, wg_ref[:, lo:hi])
        up = _dot(n, wu_ref[:, lo:hi])
        act = (gt * _sigmoid(gt) * up).astype(BF16)
        acc = acc + _dot(act, wd_ref[lo:hi, :])
    o_ref[...] = acc * lax.rsqrt(jnp.mean(acc * acc, axis=-1, keepdims=True) + EPS) * finw_ref[...]


def _tail(x2d, ofox, odn, gates, wbf, wbd, wout, fnw, wg, wu, wd, finw, *, tm, ff_bounds):
    m = x2d.shape[0]
    row = lambda w: pl.BlockSpec((tm, w), lambda i: (i, 0))
    return pl.pallas_call(
        functools.partial(_tail_kernel, ff_bounds=ff_bounds),
        grid=(m // tm,),
        in_specs=[row(D_MODEL), pl.BlockSpec((FOX_WIDTH, tm), lambda i: (0, i)), row(DN_WIDTH),
                  row(2 * D_MODEL),
                  _const_spec((FOX_WIDTH, D_MODEL)), _const_spec((DN_WIDTH, D_MODEL)),
                  _const_spec((D_MODEL, D_MODEL)), _const_spec((1, D_MODEL)),
                  _const_spec((D_MODEL, D_FF)), _const_spec((D_MODEL, D_FF)),
                  _const_spec((D_FF, D_MODEL)), _const_spec((1, D_MODEL))],
        out_specs=row(D_MODEL),
        out_shape=jax.ShapeDtypeStruct((m, D_MODEL), F32),
        compiler_params=pltpu.CompilerParams(dimension_semantics=("arbitrary",),
                                             vmem_limit_bytes=VMEM_LIMIT),
        name="merge_ffn",
    )(x2d, ofox, odn, gates, wbf, wbd, wout, fnw, wg, wu, wd, finw)


def _pick_tile(n, pref):
    t = min(pref, n)
    while n % t:
        t //= 2
    return t


def kernel(x, meta_tokens, mix_norm_w, w_in, fox_forget_bias, dn_conv_w, dn_a_log, dn_dt_bias,
           dn_out_norm_w, w_branch_fox, w_branch_dn, w_out, ffn_norm_w, w_ffn_gate, w_ffn_up,
           w_ffn_down, final_norm_w):
    b, s, _ = x.shape
    assert mix_norm_w.shape[0] == 1, "single layer only"
    assert s % PREFIX == 0
    m = b * s

    wi = w_in[0]
    o_small0 = 3 * FOX_WIDTH
    o_dn = o_small0 + FOX_HEADS
    o_small1 = o_dn + 3 * DN_WIDTH
    o_rest = o_small1 + 2 * DN_HEADS
    weights = (wi[:, :FOX_WIDTH].T.astype(BF16), wi[:, 2 * FOX_WIDTH:o_small0].T.astype(BF16),
               wi[:, FOX_WIDTH:2 * FOX_WIDTH].astype(BF16), wi[:, o_dn:o_small1].astype(BF16),
               wi[:, o_rest:].astype(BF16))
    w_alogit = wi[:, o_small1 + DN_HEADS:o_rest]
    wsmall = jnp.concatenate([wi[:, o_small0:o_dn], wi[:, o_small1:o_small1 + DN_HEADS],
                              w_alogit, w_alogit, w_alogit], axis=1)
    wsmall = jnp.pad(wsmall, ((0, 0), (0, LANES - N_SMALL))).astype(BF16)
    bias = jnp.zeros((SUBLANES, LANES), F32)
    bias = bias.at[0, LOGF_LANE:LOGF_LANE + FOX_HEADS].set(fox_forget_bias[0].astype(F32))
    bias = bias.at[0, G_LANE:N_SMALL].set(jnp.tile(dn_dt_bias[0].astype(F32), 3))
    bias = bias.at[1, G_LANE:N_SMALL].set(jnp.tile(dn_a_log[0].astype(F32), 3))
    nw = mix_norm_w[0].reshape(1, D_MODEL).astype(F32)

    x_p = jnp.concatenate([jnp.zeros((N_PAD, D_MODEL), F32), meta_tokens.astype(F32)], axis=0)
    convw = dn_conv_w[0].astype(F32)
    _, k_p, v_tp, kbias_p, qkv_p, dzs_p, _, scol_p, srow_p, conv_tail_p = _inproj(
        x_p, nw, weights, wsmall, bias, jnp.zeros((1, LANES), F32), convw,
        jnp.zeros((SUBLANES, 3 * DN_WIDTH), F32), tm=PREFIX, tiles_per_batch=1, n_pad=N_PAD)

    tm = _pick_tile(s, 512)
    q_t, k, v_t, kbias, qkv, dzs, gates, scol, srow3, _ = _inproj(
        x.reshape(m, D_MODEL), nw, weights, wsmall, bias, scol_p[PREFIX - 1:PREFIX, :], convw,
        conv_tail_p, tm=tm, tiles_per_batch=s // tm, n_pad=0)

    scol3 = scol.reshape(b, s, LANES)
    tq = _pick_tile(s, 512)
    ofox_t = _fox(q_t, k.reshape(b, s, FOX_WIDTH), v_t, kbias.reshape(b, s, LANES),
                  k_p[N_PAD:], v_tp[:, N_PAD:], kbias_p[N_PAD:], tq=tq, tk=min(256, tq // 2),
                  pairs=FOX_PAIRS_PER_STEP)

    onw = dn_out_norm_w[0].reshape(1, DN_HEAD_DIM).astype(F32)
    _, s_prefix = _deltanet(
        qkv_p[None], scol_p[None], srow_p, dzs_p[None], onw,
        jnp.zeros((DN_HEADS, DN_HEAD_DIM, DN_HEAD_DIM), F32), group=1, tt=PREFIX)
    group = _pick_tile(b, DN_UNITS // DN_HEADS)
    odn, _, wbf, wbd, wout, wg, wu, wd = _deltanet(
        qkv.reshape(b, s, 3 * DN_WIDTH), scol3, srow3, dzs.reshape(b, s, DN_WIDTH), onw, s_prefix,
        group=group, tt=_pick_tile(s, DN_UNITS // (DN_HEADS * group) * DN_BLOCK),
        cast_weights=(w_branch_fox[0], w_branch_dn[0], w_out[0], w_ffn_gate[0], w_ffn_up[0],
                      w_ffn_down[0]))

    out = _tail(
        x.reshape(m, D_MODEL), ofox_t, odn.reshape(m, DN_WIDTH), gates, wbf, wbd, wout,
        ffn_norm_w[0].reshape(1, D_MODEL).astype(F32), wg, wu, wd,
        final_norm_w.reshape(1, D_MODEL).astype(F32),
        tm=_pick_tile(m, 512), ff_bounds=FF_BOUNDS)
    return out.reshape(b, s, D_MODEL)
```

```python
import functools

import jax
import jax.numpy as jnp
from jax import lax
from jax.experimental import pallas as pl
from jax.experimental.pallas import tpu as pltpu

F32 = jnp.float32
BF16 = jnp.bfloat16

LANES = 128
SUBLANES = 8
MXU_DIM = 256
ROW_TILE = 512
VMEM_LIMIT = 56 * 1024 * 1024

D_MODEL = 1024
N_META = 16
PREFIX = 128
N_PAD = PREFIX - N_META
FOX_HEADS = 8
FOX_HEAD_DIM = 64
FOX_WIDTH = FOX_HEADS * FOX_HEAD_DIM
V_ROWS = FOX_HEAD_DIM + SUBLANES
FOX_PAIRS_PER_STEP = 4
DN_HEADS = 4
DN_HEAD_DIM = 128
DN_WIDTH = DN_HEADS * DN_HEAD_DIM
DN_CHUNK = 64
DN_BLOCK = 2 * DN_CHUNK
DN_UNITS = 16
CONV_WIDTH = 4
D_FF = 2816
FF_BOUNDS = (0, (D_FF // MXU_DIM + 1) // 2 * MXU_DIM, D_FF)
EPS = 1e-6
NEG_INF = -1e30
LOG2E = 1.4426950408889634

COL_CHUNK = 512
LOGF_LANE = 0
BETA_LANE = FOX_HEADS
G_LANE = BETA_LANE + DN_HEADS
GSUF_LANE = G_LANE + DN_HEADS
GRAW_LANE = GSUF_LANE + DN_HEADS
N_SMALL = GRAW_LANE + DN_HEADS
SROW = -(-N_SMALL // SUBLANES) * SUBLANES


def _const_spec(shape):
    nd = len(shape)
    return pl.BlockSpec(shape, lambda *_: (0,) * nd, pipeline_mode=pl.Buffered(1))


def _sigmoid(x):
    return 0.5 * jnp.tanh(0.5 * x) + 0.5


def _dot(a, b, **kw):
    return jnp.dot(a, b, preferred_element_type=F32, **kw)


def _dot_nt(a, b, **kw):
    return lax.dot_general(a, b, (((1,), (1,)), ((), ())), preferred_element_type=F32, **kw)


def _dot_tn(a, b, **kw):
    return lax.dot_general(a, b, (((0,), (0,)), ((), ())), preferred_element_type=F32, **kw)


def _inproj_kernel(x_ref, nw_ref, wq_t_ref, wv_t_ref, wk_ref, wdn_ref, wrest_ref, wsmall_ref, bias_ref,
                   carry0_ref,
                   convw_ref, halo0_ref,
                   q_t_ref, k_ref, v_t_ref, kbias_ref, dqkv_ref, dzs_ref, gates_ref, scol_ref,
                   srow_ref, tail_ref, carry_ref, halo_ref, conv_ref, *, tm, tiles_per_batch, n_pad):
    i = pl.program_id(0)

    @pl.when(i % tiles_per_batch == 0)
    def _():
        carry_ref[...] = carry0_ref[...]
        halo_ref[...] = halo0_ref[...]

    x = x_ref[...]
    ms = jnp.mean(x * x, axis=-1, keepdims=True)
    hn = (x * lax.rsqrt(ms + EPS) * nw_ref[...]).astype(BF16)

    z = _dot(hn, wsmall_ref[...]) + bias_ref[0:1, :]
    lane = lax.broadcasted_iota(jnp.int32, (tm, LANES), 1)
    row = lax.broadcasted_iota(jnp.int32, (tm, LANES), 0)
    e = jnp.exp(-jnp.abs(z))
    l1p = jnp.log1p(e)
    logf = jnp.minimum(z, 0.0) - l1p
    softplus = jnp.maximum(z, 0.0) + l1p
    sig = jnp.where(z >= 0.0, 1.0, e) / (1.0 + e)
    g = -jnp.exp(bias_ref[1:2, :]) * softplus
    if n_pad:
        vm = (row + (i % tiles_per_batch) * tm >= n_pad).astype(F32)
        sig = sig * vm
        g = g * vm
    val = jnp.where(lane < BETA_LANE, logf,
                    jnp.where(lane < G_LANE, sig, jnp.where(lane < N_SMALL, g, 0.0)))
    in_chunk = row % DN_CHUNK
    is_logf = lane < BETA_LANE
    is_gpre = jnp.logical_and(lane >= G_LANE, lane < GSUF_LANE)
    is_gsuf = jnp.logical_and(lane >= GSUF_LANE, lane < GRAW_LANE)
    scan = val
    sh = 1
    while sh < tm:
        take_up = jnp.logical_and(is_logf, row >= sh)
        if sh < DN_CHUNK:
            take_up = jnp.logical_or(take_up, jnp.logical_and(is_gpre, in_chunk >= sh))
            take_down = jnp.logical_and(is_gsuf, in_chunk + sh < DN_CHUNK)
            below = jnp.where(take_down, pltpu.roll(scan, tm - sh, axis=0), 0.0)
        else:
            below = 0.0
        scan = scan + jnp.where(take_up, pltpu.roll(scan, sh, axis=0), below)
        sh *= 2
    out = scan + jnp.where(is_logf, carry_ref[...], 0.0)
    carry_ref[...] = out[tm - 1:tm, :]
    scol_ref[...] = out
    srow_ref[...] = out.T[:SROW, :]
    nc = jnp.where(is_logf, out * -LOG2E, 0.0)
    hi = nc.astype(BF16).astype(F32)
    mid = (nc - hi).astype(BF16).astype(F32)
    lo = nc - hi - mid
    kbias = hi + pltpu.roll(mid, FOX_HEADS, axis=1) + pltpu.roll(lo, 2 * FOX_HEADS, axis=1)
    kbias_ref[...] = kbias.astype(BF16)

    def main_chunk(c):
        if c == 0:
            return lambda: _dot(hn, wk_ref[...])
        ref, first = (wdn_ref, 1) if c < 4 else (wrest_ref, 4)
        return lambda: _dot(hn, ref[:, (c - first) * COL_CHUNK:(c - first + 1) * COL_CHUNK])

    def store_q_t(acc):
        q_t_ref[...] = (acc * (FOX_HEAD_DIM ** -0.5 * LOG2E)).astype(BF16)

    def store_v_t(acc):
        v_t = acc.astype(BF16)
        ones = jnp.ones((V_ROWS - FOX_HEAD_DIM, tm), BF16)
        v_t_ref[...] = jnp.concatenate(
            [piece for h in range(FOX_HEADS)
             for piece in (v_t[h * FOX_HEAD_DIM:(h + 1) * FOX_HEAD_DIM], ones)], axis=0)

    def store_k(acc):
        k_ref[...] = acc.astype(BF16)

    def store_dn(which):
        def epilogue(acc):
            cols = slice(which * COL_CHUNK, (which + 1) * COL_CHUNK)
            if n_pad:
                acc = acc * vm[:, :1]
            conv_ref[which, 0:SUBLANES, :] = halo_ref[:, cols]
            conv_ref[which, SUBLANES:, :] = acc
            halo_ref[:, cols] = acc[tm - SUBLANES:, :]
            tail_ref[:, cols] = acc[tm - SUBLANES:, :]
            a = convw_ref[CONV_WIDTH - 1:CONV_WIDTH, cols] * acc
            for t in range(CONV_WIDTH - 1):
                start = SUBLANES - (CONV_WIDTH - 1) + t
                a = a + convw_ref[t:t + 1, cols] * conv_ref[which, start:start + tm, :]
            a = a * _sigmoid(a)
            if which == 2:
                dqkv_ref[:, cols] = a
            else:
                scale = DN_HEAD_DIM ** -0.5 if which == 0 else 1.0
                for h in range(DN_HEADS):
                    ah = a[:, h * LANES:(h + 1) * LANES]
                    inv = lax.rsqrt(jnp.sum(ah * ah, axis=-1, keepdims=True) + EPS) * scale
                    lo_col = which * COL_CHUNK + h * LANES
                    dqkv_ref[:, lo_col:lo_col + LANES] = ah * inv
        return epilogue

    def store_dzs(acc):
        dzs_ref[...] = acc * _sigmoid(acc)

    def store_gates(j):
        def epilogue(acc):
            gates_ref[:, j * COL_CHUNK:(j + 1) * COL_CHUNK] = _sigmoid(acc)
        return epilogue

    store_q_t(_dot_nt(wq_t_ref[...], hn))
    store_v_t(_dot_nt(wv_t_ref[...], hn))
    store_k(main_chunk(0)())
    for which in range(3):
        store_dn(which)(main_chunk(1 + which)())
    store_dzs(main_chunk(4)())
    for j in range(2 * D_MODEL // COL_CHUNK):
        store_gates(j)(main_chunk(5 + j)())


def _inproj(x2d, nw, weights, wsmall, bias, carry0, convw, halo0, *, tm, tiles_per_batch, n_pad):
    m = x2d.shape[0]
    row = lambda w: pl.BlockSpec((tm, w), lambda i: (i, 0))
    col = lambda h: pl.BlockSpec((h, tm), lambda i: (0, i))
    return pl.pallas_call(
        functools.partial(_inproj_kernel, tm=tm, tiles_per_batch=tiles_per_batch, n_pad=n_pad),
        grid=(m // tm,),
        in_specs=[row(D_MODEL), _const_spec((1, D_MODEL))] + [_const_spec(w.shape) for w in weights] + [
                  _const_spec((D_MODEL, LANES)), _const_spec((SUBLANES, LANES)),
                  _const_spec((1, LANES)), _const_spec((CONV_WIDTH, 3 * DN_WIDTH)),
                  _const_spec((SUBLANES, 3 * DN_WIDTH))],
        out_specs=[col(FOX_WIDTH), row(FOX_WIDTH), col(FOX_HEADS * V_ROWS), row(LANES),
                   row(3 * DN_WIDTH), row(DN_WIDTH), row(2 * D_MODEL),
                   row(LANES),
                   pl.BlockSpec((None, SROW, tm),
                                lambda i: (i // tiles_per_batch, 0, i % tiles_per_batch)),
                   pl.BlockSpec((SUBLANES, 3 * DN_WIDTH), lambda i: (0, 0))],
        out_shape=[jax.ShapeDtypeStruct((FOX_WIDTH, m), BF16),
                   jax.ShapeDtypeStruct((m, FOX_WIDTH), BF16),
                   jax.ShapeDtypeStruct((FOX_HEADS * V_ROWS, m), BF16),
                   jax.ShapeDtypeStruct((m, LANES), BF16),
                   jax.ShapeDtypeStruct((m, 3 * DN_WIDTH), F32),
                   jax.ShapeDtypeStruct((m, DN_WIDTH), F32),
                   jax.ShapeDtypeStruct((m, 2 * D_MODEL), F32),
                   jax.ShapeDtypeStruct((m, LANES), F32),
                   jax.ShapeDtypeStruct((m // (tm * tiles_per_batch), SROW,
                                         tm * tiles_per_batch), F32),
                   jax.ShapeDtypeStruct((SUBLANES, 3 * DN_WIDTH), F32)],
        scratch_shapes=[pltpu.VMEM((1, LANES), F32), pltpu.VMEM((SUBLANES, 3 * DN_WIDTH), F32),
                        pltpu.VMEM((3, tm + SUBLANES, COL_CHUNK), F32)],
        compiler_params=pltpu.CompilerParams(dimension_semantics=("arbitrary",),
                                             vmem_limit_bytes=VMEM_LIMIT),
        name="inproj",
    )(x2d, nw, *weights, wsmall, bias, carry0, convw, halo0)


def _fox_kernel(q_t_ref, k_ref, v_t_ref, kb_ref, kp_ref, v_tp_ref, kbp_ref,
                o_ref, m_ref, l_ref, acc_ref, t_ref, *, tq, tk, pairs):
    pg = pl.program_id(1)
    i = pl.program_id(2)
    heads = range(2 * pairs)
    sub = lax.broadcasted_iota(jnp.int32, (LANES, 1), 0)
    rhs = []
    for hh in heads:
        pp, h = divmod(hh, 2)
        q_t = q_t_ref[pp * LANES:(pp + 1) * LANES, :]
        mine = sub < FOX_HEAD_DIM if h == 0 else sub >= FOX_HEAD_DIM
        head = 2 * (pg * pairs + pp) + h
        sel = jnp.where(jnp.logical_and(sub < 3 * FOX_HEADS, (sub % FOX_HEADS) == head),
                        1.0, 0.0).astype(BF16) * jnp.ones((1, tq), BF16)
        rhs.append(jnp.concatenate([jnp.where(mine, q_t, jnp.zeros_like(q_t)), sel], axis=0))

    def scores(kt, kbt):
        lhs = [jnp.concatenate([kt[:, pp * LANES:(pp + 1) * LANES], kbt], axis=1)
               for pp in range(pairs)]
        return [_dot(lhs[hh // 2], rhs[hh]) for hh in heads]

    def scores_into(slot, off):
        ts = scores(k_ref[pl.ds(off, tk), :], kb_ref[pl.ds(off, tk), :])
        for hh in heads:
            t_ref[slot, hh] = ts[hh]

    def update(read_t, v_aug, mask, first):
        def masked_t(hh):
            t = read_t(hh)
            return t if mask is None else jnp.where(mask, t, NEG_INF)

        t_max = [jnp.max(masked_t(hh), axis=0, keepdims=True) for hh in heads]
        if first:
            m_new = t_max
        else:
            m_prev = [m_ref[hh] for hh in heads]
            m_new = [jnp.maximum(m_prev[hh], t_max[hh]) for hh in heads]
        pm = [jnp.exp2(masked_t(hh) - m_new[hh]).astype(BF16) for hh in heads]
        r = [_dot(v_aug[hh * V_ROWS:(hh + 1) * V_ROWS, :], pm[hh]) for hh in heads]
        pv = jnp.concatenate([r[hh][:FOX_HEAD_DIM] for hh in heads], axis=0)
        psum = [r[hh][FOX_HEAD_DIM:FOX_HEAD_DIM + 1] for hh in heads]
        if first:
            for hh in heads:
                l_ref[hh] = psum[hh]
            acc_ref[...] = pv
        else:
            alpha = [jnp.exp2(m_prev[hh] - m_new[hh]) for hh in heads]
            for hh in heads:
                l_ref[hh] = alpha[hh] * l_ref[hh] + psum[hh]
            alpha_rows = jnp.concatenate(
                [jnp.broadcast_to(alpha[hh], (FOX_HEAD_DIM, tq)) for hh in heads], axis=0)
            acc_ref[...] = alpha_rows * acc_ref[...] + pv
        for hh in heads:
            m_ref[hh] = m_new[hh]

    ts_p = scores(kp_ref[...], kbp_ref[...])
    update(lambda hh: ts_p[hh], v_tp_ref[...], None, True)

    def v_at(off):
        return v_t_ref[:, pl.ds(off, tk)]

    def slot(n):
        return lambda hh: t_ref[n, hh]

    pairs_per_q = tq // (2 * tk)
    scores_into(0, 0)

    def body(jj, carry):
        off = pl.multiple_of(jj * 2 * tk, 2 * tk)
        scores_into(1, off + tk)
        update(slot(0), v_at(off), None, False)
        scores_into(0, off + 2 * tk)
        update(slot(1), v_at(off + tk), None, False)
        return carry

    lax.fori_loop(0, i * pairs_per_q, body, 0)

    rr = lax.broadcasted_iota(jnp.int32, (tk, tq), 0)
    cc = lax.broadcasted_iota(jnp.int32, (tk, tq), 1)
    for dp in range(pairs_per_q):
        off = pl.multiple_of(i * tq + dp * 2 * tk, 2 * tk)
        scores_into(1, off + tk)
        update(slot(0), v_at(off), rr + dp * 2 * tk <= cc, False)
        if dp + 1 < pairs_per_q:
            scores_into(0, off + 2 * tk)
        update(slot(1), v_at(off + tk), rr + (dp * 2 + 1) * tk <= cc, False)

    l_rows = jnp.concatenate([jnp.broadcast_to(l_ref[hh], (FOX_HEAD_DIM, tq)) for hh in heads], axis=0)
    o_ref[...] = (acc_ref[...] / l_rows).astype(BF16)


def _fox(q_t, k, v_t, kbias, k_p, v_tp, kbias_p, *, tq, tk, pairs):
    b, s, _ = k.shape
    ngroups = FOX_HEADS // (2 * pairs)
    nq = s // tq
    wl = pairs * LANES
    return pl.pallas_call(
        functools.partial(_fox_kernel, tq=tq, tk=tk, pairs=pairs),
        grid=(b, ngroups, nq),
        in_specs=[
            pl.BlockSpec((wl, tq), lambda bi, pg, i: (pg, bi * nq + i)),
            pl.BlockSpec((None, s, wl), lambda bi, pg, i: (bi, 0, pg)),
            pl.BlockSpec((2 * pairs * V_ROWS, s), lambda bi, pg, i: (pg, bi)),
            pl.BlockSpec((None, s, LANES), lambda bi, pg, i: (bi, 0, 0)),
            pl.BlockSpec((N_META, wl), lambda bi, pg, i: (0, pg)),
            pl.BlockSpec((2 * pairs * V_ROWS, N_META), lambda bi, pg, i: (pg, 0)),
            pl.BlockSpec((N_META, LANES), lambda bi, pg, i: (0, 0)),
        ],
        out_specs=pl.BlockSpec((wl, tq), lambda bi, pg, i: (pg, bi * nq + i)),
        out_shape=jax.ShapeDtypeStruct((FOX_WIDTH, b * s), BF16),
        scratch_shapes=[pltpu.VMEM((2 * pairs, 1, tq), F32), pltpu.VMEM((2 * pairs, 1, tq), F32),
                        pltpu.VMEM((wl, tq), F32), pltpu.VMEM((2, 2 * pairs, tk, tq), F32)],
        compiler_params=pltpu.CompilerParams(
            dimension_semantics=("arbitrary", "arbitrary", "arbitrary"),
            vmem_limit_bytes=VMEM_LIMIT),
        name="fox_attention",
    )(q_t, k, v_t, kbias, k_p, v_tp, kbias_p)


def _dn_kernel(qkv_ref, scol_ref, srow_ref, dzs_ref, onw_ref, s0_ref, *refs, tt, n_cast):
    cast_in, (odn_ref, sfin_ref), cast_out = refs[:n_cast], refs[n_cast:n_cast + 2], refs[n_cast + 2:-1]
    state_ref = refs[-1]
    for src, dst in zip(cast_in, cast_out):
        dst[...] = src[...].astype(BF16)
    i = pl.program_id(1)
    nchunk = tt // DN_CHUNK

    @pl.when(i == 0)
    def _():
        for g in range(qkv_ref.shape[0]):
            state_ref[g] = s0_ref[...]

    bs = DN_BLOCK
    heads = range(DN_HEADS)
    groups = range(qkv_ref.shape[0])
    units = [(g, sb, h) for g in groups for sb in range(tt // bs) for h in heads]

    def slab(g, sb, c0):
        return qkv_ref[g, sb * bs:(sb + 1) * bs, c0:c0 + LANES]

    def small_col(g, sb, lane):
        return scol_ref[g, sb * bs:(sb + 1) * bs, lane:lane + 1]

    rr = lax.broadcasted_iota(jnp.int32, (bs, bs), 0)
    cc = lax.broadcasted_iota(jnp.int32, (bs, bs), 1)
    rc_xor = jnp.bitwise_xor(rr, cc)
    same = rc_xor < DN_CHUNK
    tril_m = jnp.logical_and(same, rr >= cc)
    strict_m = jnp.logical_and(same, rr > cc)
    eye = (rr == cc).astype(F32)

    qs = [slab(g, sb, h * LANES) for g, sb, h in units]
    ks = [slab(g, sb, DN_WIDTH + h * LANES) for g, sb, h in units]
    vs = [slab(g, sb, 2 * DN_WIDTH + h * LANES) for g, sb, h in units]
    states = {(g, h): state_ref[g, h] for g in groups for h in heads}
    betas = [small_col(g, sb, BETA_LANE + h) for g, sb, h in units]
    gcs = [small_col(g, sb, G_LANE + h) for g, sb, h in units]
    gls = [gcs[n] + small_col(g, sb, GSUF_LANE + h) - small_col(g, sb, GRAW_LANE + h)
           for n, (g, sb, h) in enumerate(units)]
    decays = [jnp.exp(jnp.where(
        tril_m, gcs[n] - srow_ref[g, G_LANE + h:G_LANE + h + 1, sb * bs:(sb + 1) * bs], NEG_INF))
        for n, (g, sb, h) in enumerate(units)]
    nu = range(len(units))
    egcs = [jnp.exp(g) for g in gcs]
    kbs = [ks[n] * betas[n] for n in nu]
    ks_b = [k.astype(BF16) for k in ks]
    lmats = [jnp.where(strict_m, _dot_nt(kbs[n].astype(BF16), ks_b[n]) * decays[n], 0.0) for n in nu]
    attns = [(_dot_nt(qs[n].astype(BF16), ks_b[n]) * decays[n]).astype(BF16) for n in nu]
    ainvs = [eye - jnp.where(rc_xor == 1, lm, 0.0) for lm in lmats]
    for lvl in range(1, 6):
        blk = 2 ** lvl
        lvl_m = jnp.logical_and(rc_xor >= blk, rc_xor < 2 * blk)
        ainvs_b = [a.astype(BF16) for a in ainvs]
        mids = [_dot(jnp.where(lvl_m, lmats[n], 0.0).astype(BF16), ainvs_b[n]).astype(BF16)
                for n in nu]
        ainvs = [ainvs[n] - _dot(ainvs_b[n], mids[n]) for n in nu]
    sols = [_dot(ainvs[n].astype(BF16),
                 jnp.concatenate([vs[n] * betas[n], kbs[n] * egcs[n]], axis=1).astype(BF16))
            for n in nu]
    uus = [sol[:, :DN_HEAD_DIM] for sol in sols]
    wws = [sol[:, DN_HEAD_DIM:] for sol in sols]
    qds = [qs[n] * egcs[n] for n in nu]
    kds = [(ks[n] * jnp.exp(gls[n] - gcs[n])).astype(BF16) for n in nu]

    vnews = [[] for _ in nu]
    qss = [[] for _ in nu]
    for c in range(nchunk):
        csb, lc = divmod(c, bs // DN_CHUNK)
        lo, hi = lc * DN_CHUNK, (lc + 1) * DN_CHUNK
        cur = [(n, (g, h)) for n, (g, sb, h) in enumerate(units) if sb == csb]
        rs = {n: _dot(jnp.concatenate([wws[n][lo:hi], qds[n][lo:hi]], axis=0).astype(BF16),
                      states[key].astype(BF16)) for n, key in cur}
        vns = {n: uus[n][lo:hi] - rs[n][:DN_CHUNK] for n, _ in cur}
        states.update({key: jnp.exp(gls[n][lo:lo + 1, :]) * states[key]
                       + _dot_tn(kds[n][lo:hi], vns[n].astype(BF16)) for n, key in cur})
        for n, _ in cur:
            qss[n].append(rs[n][DN_CHUNK:])
            vnews[n].append(vns[n])
    for (g, h), state in states.items():
        state_ref[g, h] = state
    for h in heads:
        sfin_ref[h] = states[(0, h)]
    for n, (g, sb, h) in enumerate(units):
        o = (jnp.concatenate(qss[n], axis=0)
             + _dot(attns[n], jnp.concatenate(vnews[n], axis=0).astype(BF16)))
        o = o * lax.rsqrt(jnp.mean(o * o, axis=-1, keepdims=True) + EPS) * onw_ref[...]
        o = o * dzs_ref[g, sb * bs:(sb + 1) * bs, h * LANES:(h + 1) * LANES]
        odn_ref[g, sb * bs:(sb + 1) * bs, h * LANES:(h + 1) * LANES] = o.astype(BF16)


def _cast_row_block(rows, nsteps):
    tile = 2 * SUBLANES
    for blk in range(tile, rows, tile):
        if rows % blk == 0 and rows // blk <= nsteps:
            return blk
    return rows


def _deltanet(qkv, scol, srow, dzs, onw, s0, *, group, tt, cast_weights=()):
    b, s, _ = qkv.shape
    nt = s // tt
    nsteps = (b // group) * nt
    cast_specs = []
    for w in cast_weights:
        blk = _cast_row_block(w.shape[0], nsteps)
        last = w.shape[0] // blk - 1
        cast_specs.append(pl.BlockSpec(
            (blk, w.shape[1]), lambda bi, i, last=last: (jnp.minimum(bi * nt + i, last), 0)))
    return pl.pallas_call(
        functools.partial(_dn_kernel, tt=tt, n_cast=len(cast_weights)),
        grid=(b // group, nt),
        in_specs=[
            pl.BlockSpec((group, tt, 3 * DN_WIDTH), lambda bi, i: (bi, i, 0)),
            pl.BlockSpec((group, tt, LANES), lambda bi, i: (bi, i, 0)),
            pl.BlockSpec((group, SROW, tt), lambda bi, i: (bi, 0, i)),
            pl.BlockSpec((group, tt, DN_WIDTH), lambda bi, i: (bi, i, 0)),
            pl.BlockSpec((1, DN_HEAD_DIM), lambda bi, i: (0, 0)),
            pl.BlockSpec((DN_HEADS, DN_HEAD_DIM, DN_HEAD_DIM), lambda bi, i: (0, 0, 0)),
        ] + cast_specs,
        out_specs=[pl.BlockSpec((group, tt, DN_WIDTH), lambda bi, i: (bi, i, 0)),
                   pl.BlockSpec((DN_HEADS, DN_HEAD_DIM, DN_HEAD_DIM), lambda bi, i: (0, 0, 0))]
        + cast_specs,
        out_shape=[jax.ShapeDtypeStruct((b, s, DN_WIDTH), BF16),
                   jax.ShapeDtypeStruct((DN_HEADS, DN_HEAD_DIM, DN_HEAD_DIM), F32)]
        + [jax.ShapeDtypeStruct(w.shape, BF16) for w in cast_weights],
        scratch_shapes=[pltpu.VMEM((group, DN_HEADS, DN_HEAD_DIM, DN_HEAD_DIM), F32)],
        compiler_params=pltpu.CompilerParams(dimension_semantics=("arbitrary", "arbitrary"),
                                             vmem_limit_bytes=VMEM_LIMIT),
        name="deltanet",
    )(qkv, scol, srow, dzs, onw, s0, *cast_weights)


def _tail_kernel(x_ref, ofox_ref, odn_ref, gates_ref, wbf_ref, wbd_ref, wout_ref, fnw_ref,
                 wg_ref, wu_ref, wd_ref, finw_ref, o_ref, *, ff_bounds):
    a = _dot_tn(ofox_ref[...], wbf_ref[...])
    bb = _dot(odn_ref[...], wbd_ref[...])
    y = gates_ref[:, :D_MODEL] * a + gates_ref[:, D_MODEL:] * bb
    h1 = x_ref[...] + _dot(y.astype(BF16), wout_ref[...])
    n = (h1 * lax.rsqrt(jnp.mean(h1 * h1, axis=-1, keepdims=True) + EPS) * fnw_ref[...]).astype(BF16)
    acc = h1
    for lo, hi in zip(ff_bounds[:-1], ff_bounds[1:]):
        gt = _dot(n, wg_ref[:, lo:hi])
        up = _dot(n, wu_ref[:, lo:hi])
        act = (gt * _sigmoid(gt) * up).astype(BF16)
        acc = acc + _dot(act, wd_ref[lo:hi, :])
    o_ref[...] = acc * lax.rsqrt(jnp.mean(acc * acc, axis=-1, keepdims=True) + EPS) * finw_ref[...]


def _tail(x2d, ofox, odn, gates, wbf, wbd, wout, fnw, wg, wu, wd, finw, *, tm, ff_bounds):
    m = x2d.shape[0]
    row = lambda w: pl.BlockSpec((tm, w), lambda i: (i, 0))
    return pl.pallas_call(
        functools.partial(_tail_kernel, ff_bounds=ff_bounds),
        grid=(m // tm,),
        in_specs=[row(D_MODEL), pl.BlockSpec((FOX_WIDTH, tm), lambda i: (0, i)), row(DN_WIDTH),
                  row(2 * D_MODEL),
                  _const_spec((FOX_WIDTH, D_MODEL)), _const_spec((DN_WIDTH, D_MODEL)),
                  _const_spec((D_MODEL, D_MODEL)), _const_spec((1, D_MODEL)),
                  _const_spec((D_MODEL, D_FF)), _const_spec((D_MODEL, D_FF)),
                  _const_spec((D_FF, D_MODEL)), _const_spec((1, D_MODEL))],
        out_specs=row(D_MODEL),
        out_shape=jax.ShapeDtypeStruct((m, D_MODEL), F32),
        compiler_params=pltpu.CompilerParams(dimension_semantics=("arbitrary",),
                                             vmem_limit_bytes=VMEM_LIMIT),
        name="merge_ffn",
    )(x2d, ofox, odn, gates, wbf, wbd, wout, fnw, wg, wu, wd, finw)


def _pick_tile(n, pref):
    t = min(pref, n)
    while n % t:
        t //= 2
    return t


def kernel(x, meta_tokens, mix_norm_w, w_in, fox_forget_bias, dn_conv_w, dn_a_log, dn_dt_bias,
           dn_out_norm_w, w_branch_fox, w_branch_dn, w_out, ffn_norm_w, w_ffn_gate, w_ffn_up,
           w_ffn_down, final_norm_w):
    b, s, _ = x.shape
    assert mix_norm_w.shape[0] == 1, "single layer only"
    assert s % PREFIX == 0
    m = b * s

    wi = w_in[0]
    o_small0 = 3 * FOX_WIDTH
    o_dn = o_small0 + FOX_HEADS
    o_small1 = o_dn + 3 * DN_WIDTH
    o_rest = o_small1 + 2 * DN_HEADS
    weights = (wi[:, :FOX_WIDTH].T.astype(BF16), wi[:, 2 * FOX_WIDTH:o_small0].T.astype(BF16),
               wi[:, FOX_WIDTH:2 * FOX_WIDTH].astype(BF16), wi[:, o_dn:o_small1].astype(BF16),
               wi[:, o_rest:].astype(BF16))
    w_alogit = wi[:, o_small1 + DN_HEADS:o_rest]
    wsmall = jnp.concatenate([wi[:, o_small0:o_dn], wi[:, o_small1:o_small1 + DN_HEADS],
                              w_alogit, w_alogit, w_alogit], axis=1)
    wsmall = jnp.pad(wsmall, ((0, 0), (0, LANES - N_SMALL))).astype(BF16)
    bias = jnp.zeros((SUBLANES, LANES), F32)
    bias = bias.at[0, LOGF_LANE:LOGF_LANE + FOX_HEADS].set(fox_forget_bias[0].astype(F32))
    bias = bias.at[0, G_LANE:N_SMALL].set(jnp.tile(dn_dt_bias[0].astype(F32), 3))
    bias = bias.at[1, G_LANE:N_SMALL].set(jnp.tile(dn_a_log[0].astype(F32), 3))
    nw = mix_norm_w[0].reshape(1, D_MODEL).astype(F32)

    x_p = jnp.concatenate([jnp.zeros((N_PAD, D_MODEL), F32), meta_tokens.astype(F32)], axis=0)
    convw = dn_conv_w[0].astype(F32)
    _, k_p, v_tp, kbias_p, qkv_p, dzs_p, _, scol_p, srow_p, conv_tail_p = _inproj(
        x_p, nw, weights, wsmall, bias, jnp.zeros((1, LANES), F32), convw,
        jnp.zeros((SUBLANES, 3 * DN_WIDTH), F32), tm=PREFIX, tiles_per_batch=1, n_pad=N_PAD)

    tm = _pick_tile(s, ROW_TILE)
    q_t, k, v_t, kbias, qkv, dzs, gates, scol, srow3, _ = _inproj(
        x.reshape(m, D_MODEL), nw, weights, wsmall, bias, scol_p[PREFIX - 1:PREFIX, :], convw,
        conv_tail_p, tm=tm, tiles_per_batch=s // tm, n_pad=0)

    scol3 = scol.reshape(b, s, LANES)
    tq = _pick_tile(s, ROW_TILE)
    ofox_t = _fox(q_t, k.reshape(b, s, FOX_WIDTH), v_t, kbias.reshape(b, s, LANES),
                  k_p[N_PAD:], v_tp[:, N_PAD:], kbias_p[N_PAD:], tq=tq, tk=min(MXU_DIM, tq // 2),
                  pairs=FOX_PAIRS_PER_STEP)

    onw = dn_out_norm_w[0].reshape(1, DN_HEAD_DIM).astype(F32)
    _, s_prefix = _deltanet(
        qkv_p[None], scol_p[None], srow_p, dzs_p[None], onw,
        jnp.zeros((DN_HEADS, DN_HEAD_DIM, DN_HEAD_DIM), F32), group=1, tt=PREFIX)
    group = _pick_tile(b, DN_UNITS // DN_HEADS)
    odn, _, wbf, wbd, wout, wg, wu, wd = _deltanet(
        qkv.reshape(b, s, 3 * DN_WIDTH), scol3, srow3, dzs.reshape(b, s, DN_WIDTH), onw, s_prefix,
        group=group, tt=_pick_tile(s, DN_UNITS // (DN_HEADS * group) * DN_BLOCK),
        cast_weights=(w_branch_fox[0], w_branch_dn[0], w_out[0], w_ffn_gate[0], w_ffn_up[0],
                      w_ffn_down[0]))

    out = _tail(
        x.reshape(m, D_MODEL), ofox_t, odn.reshape(m, DN_WIDTH), gates, wbf, wbd, wout,
        ffn_norm_w[0].reshape(1, D_MODEL).astype(F32), wg, wu, wd,
        final_norm_w.reshape(1, D_MODEL).astype(F32),
        tm=_pick_tile(m, ROW_TILE), ff_bounds=FF_BOUNDS)
    return out.reshape(b, s, D_MODEL)
```

```python
import functools

import jax
import jax.numpy as jnp
from jax import lax
from jax.experimental import pallas as pl
from jax.experimental.pallas import tpu as pltpu

F32 = jnp.float32
BF16 = jnp.bfloat16

LANES = 128
SUBLANES = 8
MXU_DIM = 256
ROW_TILE = 512
VMEM_LIMIT = 56 * 1024 * 1024

D_MODEL = 1024
N_META = 16
PREFIX = 128
N_PAD = PREFIX - N_META
FOX_HEADS = 8
FOX_HEAD_DIM = 64
FOX_WIDTH = FOX_HEADS * FOX_HEAD_DIM
V_ROWS = FOX_HEAD_DIM + SUBLANES
FOX_PAIRS_PER_STEP = 4
DN_HEADS = 4
DN_HEAD_DIM = 128
DN_WIDTH = DN_HEADS * DN_HEAD_DIM
DN_CHUNK = 64
DN_BLOCK = 2 * DN_CHUNK
DN_UNITS = 16
CONV_WIDTH = 4
D_FF = 2816
FF_BOUNDS = (0, (D_FF // MXU_DIM + 1) // 2 * MXU_DIM, D_FF)
EPS = 1e-6
NEG_INF = -1e30
LOG2E = 1.4426950408889634

COL_CHUNK = 512
LOGF_LANE = 0
BETA_LANE = FOX_HEADS
G_LANE = BETA_LANE + DN_HEADS
GSUF_LANE = G_LANE + DN_HEADS
GRAW_LANE = GSUF_LANE + DN_HEADS
N_SMALL = GRAW_LANE + DN_HEADS
SROW = -(-N_SMALL // SUBLANES) * SUBLANES


def _const_spec(shape):
    nd = len(shape)
    return pl.BlockSpec(shape, lambda *_: (0,) * nd, pipeline_mode=pl.Buffered(1))


def _sigmoid(x):
    return 0.5 * jnp.tanh(0.5 * x) + 0.5


def _dot(a, b, **kw):
    return jnp.dot(a, b, preferred_element_type=F32, **kw)


def _dot_nt(a, b, **kw):
    return lax.dot_general(a, b, (((1,), (1,)), ((), ())), preferred_element_type=F32, **kw)


def _dot_tn(a, b, **kw):
    return lax.dot_general(a, b, (((0,), (0,)), ((), ())), preferred_element_type=F32, **kw)


def _inproj_kernel(x_ref, nw_ref, wq_t_ref, wv_t_ref, wk_ref, wdn_ref, wrest_ref, wsmall_ref, bias_ref,
                   carry0_ref,
                   convw_ref, halo0_ref,
                   q_t_ref, k_ref, v_t_ref, kbias_ref, dqkv_ref, dzs_ref, gates_ref, scol_ref,
                   srow_ref, tail_ref, carry_ref, halo_ref, conv_ref, *, tm, tiles_per_batch, n_pad):
    i = pl.program_id(0)

    @pl.when(i % tiles_per_batch == 0)
    def _():
        carry_ref[...] = carry0_ref[...]
        halo_ref[...] = halo0_ref[...]

    x = x_ref[...]
    ms = jnp.mean(x * x, axis=-1, keepdims=True)
    hn = (x * lax.rsqrt(ms + EPS) * nw_ref[...]).astype(BF16)

    z = _dot(hn, wsmall_ref[...]) + bias_ref[0:1, :]
    lane = lax.broadcasted_iota(jnp.int32, (tm, LANES), 1)
    row = lax.broadcasted_iota(jnp.int32, (tm, LANES), 0)
    e = jnp.exp(-jnp.abs(z))
    l1p = jnp.log1p(e)
    logf = jnp.minimum(z, 0.0) - l1p
    softplus = jnp.maximum(z, 0.0) + l1p
    sig = jnp.where(z >= 0.0, 1.0, e) / (1.0 + e)
    g = -jnp.exp(bias_ref[1:2, :]) * softplus
    if n_pad:
        vm = (row + (i % tiles_per_batch) * tm >= n_pad).astype(F32)
        sig = sig * vm
        g = g * vm
    val = jnp.where(lane < BETA_LANE, logf,
                    jnp.where(lane < G_LANE, sig, jnp.where(lane < N_SMALL, g, 0.0)))
    in_chunk = row % DN_CHUNK
    is_logf = lane < BETA_LANE
    is_gpre = jnp.logical_and(lane >= G_LANE, lane < GSUF_LANE)
    is_gsuf = jnp.logical_and(lane >= GSUF_LANE, lane < GRAW_LANE)
    scan = val
    sh = 1
    while sh < tm:
        take_up = jnp.logical_and(is_logf, row >= sh)
        if sh < DN_CHUNK:
            take_up = jnp.logical_or(take_up, jnp.logical_and(is_gpre, in_chunk >= sh))
            take_down = jnp.logical_and(is_gsuf, in_chunk + sh < DN_CHUNK)
            below = jnp.where(take_down, pltpu.roll(scan, tm - sh, axis=0), 0.0)
        else:
            below = 0.0
        scan = scan + jnp.where(take_up, pltpu.roll(scan, sh, axis=0), below)
        sh *= 2
    out = scan + jnp.where(is_logf, carry_ref[...], 0.0)
    carry_ref[...] = out[tm - 1:tm, :]
    scol_ref[...] = out
    srow_ref[...] = out.T[:SROW, :]
    nc = jnp.where(is_logf, out * -LOG2E, 0.0)
    hi = nc.astype(BF16).astype(F32)
    mid = (nc - hi).astype(BF16).astype(F32)
    lo = nc - hi - mid
    kbias = hi + pltpu.roll(mid, FOX_HEADS, axis=1) + pltpu.roll(lo, 2 * FOX_HEADS, axis=1)
    kbias_ref[...] = kbias.astype(BF16)

    def main_chunk(c):
        if c == 0:
            return lambda: _dot(hn, wk_ref[...])
        ref, first = (wdn_ref, 1) if c < 4 else (wrest_ref, 4)
        return lambda: _dot(hn, ref[:, (c - first) * COL_CHUNK:(c - first + 1) * COL_CHUNK])

    def store_q_t(acc):
        q_t_ref[...] = (acc * (FOX_HEAD_DIM ** -0.5 * LOG2E)).astype(BF16)

    def store_v_t(acc):
        v_t = acc.astype(BF16)
        ones = jnp.ones((V_ROWS - FOX_HEAD_DIM, tm), BF16)
        v_t_ref[...] = jnp.concatenate(
            [piece for h in range(FOX_HEADS)
             for piece in (v_t[h * FOX_HEAD_DIM:(h + 1) * FOX_HEAD_DIM], ones)], axis=0)

    def store_k(acc):
        k_ref[...] = acc.astype(BF16)

    def store_dn(which):
        def epilogue(acc):
            cols = slice(which * COL_CHUNK, (which + 1) * COL_CHUNK)
            if n_pad:
                acc = acc * vm[:, :1]
            conv_ref[which, 0:SUBLANES, :] = halo_ref[:, cols]
            conv_ref[which, SUBLANES:, :] = acc
            halo_ref[:, cols] = acc[tm - SUBLANES:, :]
            tail_ref[:, cols] = acc[tm - SUBLANES:, :]
            a = convw_ref[CONV_WIDTH - 1:CONV_WIDTH, cols] * acc
            for t in range(CONV_WIDTH - 1):
                start = SUBLANES - (CONV_WIDTH - 1) + t
                a = a + convw_ref[t:t + 1, cols] * conv_ref[which, start:start + tm, :]
            a = a * _sigmoid(a)
            if which == 2:
                dqkv_ref[:, cols] = a
            else:
                scale = DN_HEAD_DIM ** -0.5 if which == 0 else 1.0
                for h in range(DN_HEADS):
                    ah = a[:, h * LANES:(h + 1) * LANES]
                    inv = lax.rsqrt(jnp.sum(ah * ah, axis=-1, keepdims=True) + EPS) * scale
                    lo_col = which * COL_CHUNK + h * LANES
                    dqkv_ref[:, lo_col:lo_col + LANES] = ah * inv
        return epilogue

    def store_dzs(acc):
        dzs_ref[...] = acc * _sigmoid(acc)

    def store_gates(j):
        def epilogue(acc):
            gates_ref[:, j * COL_CHUNK:(j + 1) * COL_CHUNK] = _sigmoid(acc)
        return epilogue

    store_q_t(_dot_nt(wq_t_ref[...], hn))
    store_v_t(_dot_nt(wv_t_ref[...], hn))
    store_k(main_chunk(0)())
    for which in range(3):
        store_dn(which)(main_chunk(1 + which)())
    store_dzs(main_chunk(4)())
    for j in range(2 * D_MODEL // COL_CHUNK):
        store_gates(j)(main_chunk(5 + j)())


def _inproj(x2d, nw, weights, wsmall, bias, carry0, convw, halo0, *, tm, tiles_per_batch, n_pad):
    m = x2d.shape[0]
    row = lambda w: pl.BlockSpec((tm, w), lambda i: (i, 0))
    col = lambda h: pl.BlockSpec((h, tm), lambda i: (0, i))
    return pl.pallas_call(
        functools.partial(_inproj_kernel, tm=tm, tiles_per_batch=tiles_per_batch, n_pad=n_pad),
        grid=(m // tm,),
        in_specs=[row(D_MODEL), _const_spec((1, D_MODEL))] + [_const_spec(w.shape) for w in weights] + [
                  _const_spec((D_MODEL, LANES)), _const_spec((SUBLANES, LANES)),
                  _const_spec((1, LANES)), _const_spec((CONV_WIDTH, 3 * DN_WIDTH)),
                  _const_spec((SUBLANES, 3 * DN_WIDTH))],
        out_specs=[col(FOX_WIDTH), row(FOX_WIDTH), col(FOX_HEADS * V_ROWS), row(LANES),
                   row(3 * DN_WIDTH), row(DN_WIDTH), row(2 * D_MODEL),
                   row(LANES),
                   pl.BlockSpec((None, SROW, tm),
                                lambda i: (i // tiles_per_batch, 0, i % tiles_per_batch)),
                   pl.BlockSpec((SUBLANES, 3 * DN_WIDTH), lambda i: (0, 0))],
        out_shape=[jax.ShapeDtypeStruct((FOX_WIDTH, m), BF16),
                   jax.ShapeDtypeStruct((m, FOX_WIDTH), BF16),
                   jax.ShapeDtypeStruct((FOX_HEADS * V_ROWS, m), BF16),
                   jax.ShapeDtypeStruct((m, LANES), BF16),
                   jax.ShapeDtypeStruct((m, 3 * DN_WIDTH), F32),
                   jax.ShapeDtypeStruct((m, DN_WIDTH), F32),
                   jax.ShapeDtypeStruct((m, 2 * D_MODEL), F32),
                   jax.ShapeDtypeStruct((m, LANES), F32),
                   jax.ShapeDtypeStruct((m // (tm * tiles_per_batch), SROW,
                                         tm * tiles_per_batch), F32),
                   jax.ShapeDtypeStruct((SUBLANES, 3 * DN_WIDTH), F32)],
        scratch_shapes=[pltpu.VMEM((1, LANES), F32), pltpu.VMEM((SUBLANES, 3 * DN_WIDTH), F32),
                        pltpu.VMEM((3, tm + SUBLANES, COL_CHUNK), F32)],
        compiler_params=pltpu.CompilerParams(dimension_semantics=("arbitrary",),
                                             vmem_limit_bytes=VMEM_LIMIT),
        name="inproj",
    )(x2d, nw, *weights, wsmall, bias, carry0, convw, halo0)


def _fox_kernel(q_t_ref, k_ref, v_t_ref, kb_ref, kp_ref, v_tp_ref, kbp_ref,
                o_ref, m_ref, l_ref, acc_ref, t_ref, tmax_ref, *, tq, tk, pairs):
    pg = pl.program_id(1)
    i = pl.program_id(2)
    heads = range(2 * pairs)
    sub = lax.broadcasted_iota(jnp.int32, (LANES, 1), 0)
    rhs = []
    for hh in heads:
        pp, h = divmod(hh, 2)
        q_t = q_t_ref[pp * LANES:(pp + 1) * LANES, :]
        mine = sub < FOX_HEAD_DIM if h == 0 else sub >= FOX_HEAD_DIM
        head = 2 * (pg * pairs + pp) + h
        sel = jnp.where(jnp.logical_and(sub < 3 * FOX_HEADS, (sub % FOX_HEADS) == head),
                        1.0, 0.0).astype(BF16) * jnp.ones((1, tq), BF16)
        rhs.append(jnp.concatenate([jnp.where(mine, q_t, jnp.zeros_like(q_t)), sel], axis=0))

    def scores(kt, kbt):
        lhs = [jnp.concatenate([kt[:, pp * LANES:(pp + 1) * LANES], kbt], axis=1)
               for pp in range(pairs)]
        return [_dot(lhs[hh // 2], rhs[hh]) for hh in heads]

    def column_max(ts):
        return [jnp.max(ts[hh], axis=0, keepdims=True) for hh in heads]

    def scores_into(slot, off):
        ts = scores(k_ref[pl.ds(off, tk), :], kb_ref[pl.ds(off, tk), :])
        t_max = column_max(ts)
        for hh in heads:
            t_ref[slot, hh] = ts[hh]
            tmax_ref[slot, hh] = t_max[hh]

    def update(read_t, t_max, v_aug, mask, first):
        def masked_t(hh):
            t = read_t(hh)
            return t if mask is None else jnp.where(mask, t, NEG_INF)

        if t_max is None:
            t_max = [jnp.max(masked_t(hh), axis=0, keepdims=True) for hh in heads]
        if first:
            m_new = t_max
        else:
            m_prev = [m_ref[hh] for hh in heads]
            m_new = [jnp.maximum(m_prev[hh], t_max[hh]) for hh in heads]
        pm = [jnp.exp2(masked_t(hh) - m_new[hh]).astype(BF16) for hh in heads]
        r = [_dot(v_aug[hh * V_ROWS:(hh + 1) * V_ROWS, :], pm[hh]) for hh in heads]
        pv = jnp.concatenate([r[hh][:FOX_HEAD_DIM] for hh in heads], axis=0)
        psum = [r[hh][FOX_HEAD_DIM:FOX_HEAD_DIM + 1] for hh in heads]
        if first:
            for hh in heads:
                l_ref[hh] = psum[hh]
            acc_ref[...] = pv
        else:
            alpha = [jnp.exp2(m_prev[hh] - m_new[hh]) for hh in heads]
            for hh in heads:
                l_ref[hh] = alpha[hh] * l_ref[hh] + psum[hh]
            alpha_rows = jnp.concatenate(
                [jnp.broadcast_to(alpha[hh], (FOX_HEAD_DIM, tq)) for hh in heads], axis=0)
            acc_ref[...] = alpha_rows * acc_ref[...] + pv
        for hh in heads:
            m_ref[hh] = m_new[hh]

    ts_p = scores(kp_ref[...], kbp_ref[...])
    update(lambda hh: ts_p[hh], column_max(ts_p), v_tp_ref[...], None, True)

    def v_at(off):
        return v_t_ref[:, pl.ds(off, tk)]

    def slot(n):
        return lambda hh: t_ref[n, hh]

    def slot_max(n):
        return [tmax_ref[n, hh] for hh in heads]

    pairs_per_q = tq // (2 * tk)
    scores_into(0, 0)

    def body(jj, carry):
        off = pl.multiple_of(jj * 2 * tk, 2 * tk)
        scores_into(1, off + tk)
        update(slot(0), slot_max(0), v_at(off), None, False)
        scores_into(0, off + 2 * tk)
        update(slot(1), slot_max(1), v_at(off + tk), None, False)
        return carry

    lax.fori_loop(0, i * pairs_per_q, body, 0)

    rr = lax.broadcasted_iota(jnp.int32, (tk, tq), 0)
    cc = lax.broadcasted_iota(jnp.int32, (tk, tq), 1)
    for dp in range(pairs_per_q):
        off = pl.multiple_of(i * tq + dp * 2 * tk, 2 * tk)
        scores_into(1, off + tk)
        update(slot(0), None, v_at(off), rr + dp * 2 * tk <= cc, False)
        if dp + 1 < pairs_per_q:
            scores_into(0, off + 2 * tk)
        update(slot(1), None, v_at(off + tk), rr + (dp * 2 + 1) * tk <= cc, False)

    l_rows = jnp.concatenate([jnp.broadcast_to(l_ref[hh], (FOX_HEAD_DIM, tq)) for hh in heads], axis=0)
    o_ref[...] = (acc_ref[...] / l_rows).astype(BF16)


def _fox(q_t, k, v_t, kbias, k_p, v_tp, kbias_p, *, tq, tk, pairs):
    b, s, _ = k.shape
    ngroups = FOX_HEADS // (2 * pairs)
    nq = s // tq
    wl = pairs * LANES
    return pl.pallas_call(
        functools.partial(_fox_kernel, tq=tq, tk=tk, pairs=pairs),
        grid=(b, ngroups, nq),
        in_specs=[
            pl.BlockSpec((wl, tq), lambda bi, pg, i: (pg, bi * nq + i)),
            pl.BlockSpec((None, s, wl), lambda bi, pg, i: (bi, 0, pg)),
            pl.BlockSpec((2 * pairs * V_ROWS, s), lambda bi, pg, i: (pg, bi)),
            pl.BlockSpec((None, s, LANES), lambda bi, pg, i: (bi, 0, 0)),
            pl.BlockSpec((N_META, wl), lambda bi, pg, i: (0, pg)),
            pl.BlockSpec((2 * pairs * V_ROWS, N_META), lambda bi, pg, i: (pg, 0)),
            pl.BlockSpec((N_META, LANES), lambda bi, pg, i: (0, 0)),
        ],
        out_specs=pl.BlockSpec((wl, tq), lambda bi, pg, i: (pg, bi * nq + i)),
        out_shape=jax.ShapeDtypeStruct((FOX_WIDTH, b * s), BF16),
        scratch_shapes=[pltpu.VMEM((2 * pairs, 1, tq), F32), pltpu.VMEM((2 * pairs, 1, tq), F32),
                        pltpu.VMEM((wl, tq), F32), pltpu.VMEM((2, 2 * pairs, tk, tq), F32),
                        pltpu.VMEM((2, 2 * pairs, 1, tq), F32)],
        compiler_params=pltpu.CompilerParams(
            dimension_semantics=("arbitrary", "arbitrary", "arbitrary"),
            vmem_limit_bytes=VMEM_LIMIT),
        name="fox_attention",
    )(q_t, k, v_t, kbias, k_p, v_tp, kbias_p)


def _dn_kernel(qkv_ref, scol_ref, srow_ref, dzs_ref, onw_ref, s0_ref, *refs, tt, n_cast):
    cast_in, (odn_ref, sfin_ref), cast_out = refs[:n_cast], refs[n_cast:n_cast + 2], refs[n_cast + 2:-1]
    state_ref = refs[-1]
    for src, dst in zip(cast_in, cast_out):
        dst[...] = src[...].astype(BF16)
    i = pl.program_id(1)
    nchunk = tt // DN_CHUNK

    @pl.when(i == 0)
    def _():
        for g in range(qkv_ref.shape[0]):
            state_ref[g] = s0_ref[...]

    bs = DN_BLOCK
    heads = range(DN_HEADS)
    groups = range(qkv_ref.shape[0])
    units = [(g, sb, h) for g in groups for sb in range(tt // bs) for h in heads]

    def slab(g, sb, c0):
        return qkv_ref[g, sb * bs:(sb + 1) * bs, c0:c0 + LANES]

    def small_col(g, sb, lane):
        return scol_ref[g, sb * bs:(sb + 1) * bs, lane:lane + 1]

    rr = lax.broadcasted_iota(jnp.int32, (bs, bs), 0)
    cc = lax.broadcasted_iota(jnp.int32, (bs, bs), 1)
    rc_xor = jnp.bitwise_xor(rr, cc)
    same = rc_xor < DN_CHUNK
    tril_m = jnp.logical_and(same, rr >= cc)
    strict_m = jnp.logical_and(same, rr > cc)
    eye = (rr == cc).astype(F32)

    qs = [slab(g, sb, h * LANES) for g, sb, h in units]
    ks = [slab(g, sb, DN_WIDTH + h * LANES) for g, sb, h in units]
    vs = [slab(g, sb, 2 * DN_WIDTH + h * LANES) for g, sb, h in units]
    states = {(g, h): state_ref[g, h] for g in groups for h in heads}
    betas = [small_col(g, sb, BETA_LANE + h) for g, sb, h in units]
    gcs = [small_col(g, sb, G_LANE + h) for g, sb, h in units]
    gls = [gcs[n] + small_col(g, sb, GSUF_LANE + h) - small_col(g, sb, GRAW_LANE + h)
           for n, (g, sb, h) in enumerate(units)]
    decays = [jnp.exp(jnp.where(
        tril_m, gcs[n] - srow_ref[g, G_LANE + h:G_LANE + h + 1, sb * bs:(sb + 1) * bs], NEG_INF))
        for n, (g, sb, h) in enumerate(units)]
    nu = range(len(units))
    egcs = [jnp.exp(g) for g in gcs]
    kbs = [ks[n] * betas[n] for n in nu]
    ks_b = [k.astype(BF16) for k in ks]
    lmats = [jnp.where(strict_m, _dot_nt(kbs[n].astype(BF16), ks_b[n]) * decays[n], 0.0) for n in nu]
    attns = [(_dot_nt(qs[n].astype(BF16), ks_b[n]) * decays[n]).astype(BF16) for n in nu]
    ainvs = [eye - jnp.where(rc_xor == 1, lm, 0.0) for lm in lmats]
    for lvl in range(1, 6):
        blk = 2 ** lvl
        lvl_m = jnp.logical_and(rc_xor >= blk, rc_xor < 2 * blk)
        ainvs_b = [a.astype(BF16) for a in ainvs]
        mids = [_dot(jnp.where(lvl_m, lmats[n], 0.0).astype(BF16), ainvs_b[n]).astype(BF16)
                for n in nu]
        ainvs = [ainvs[n] - _dot(ainvs_b[n], mids[n]) for n in nu]
    sols = [_dot(ainvs[n].astype(BF16),
                 jnp.concatenate([vs[n] * betas[n], kbs[n] * egcs[n]], axis=1).astype(BF16))
            for n in nu]
    uus = [sol[:, :DN_HEAD_DIM] for sol in sols]
    wws = [sol[:, DN_HEAD_DIM:] for sol in sols]
    qds = [qs[n] * egcs[n] for n in nu]
    kds = [(ks[n] * jnp.exp(gls[n] - gcs[n])).astype(BF16) for n in nu]

    vnews = [[] for _ in nu]
    qss = [[] for _ in nu]
    for c in range(nchunk):
        csb, lc = divmod(c, bs // DN_CHUNK)
        lo, hi = lc * DN_CHUNK, (lc + 1) * DN_CHUNK
        cur = [(n, (g, h)) for n, (g, sb, h) in enumerate(units) if sb == csb]
        rs = {n: _dot(jnp.concatenate([wws[n][lo:hi], qds[n][lo:hi]], axis=0).astype(BF16),
                      states[key].astype(BF16)) for n, key in cur}
        vns = {n: uus[n][lo:hi] - rs[n][:DN_CHUNK] for n, _ in cur}
        states.update({key: jnp.exp(gls[n][lo:lo + 1, :]) * states[key]
                       + _dot_tn(kds[n][lo:hi], vns[n].astype(BF16)) for n, key in cur})
        for n, _ in cur:
            qss[n].append(rs[n][DN_CHUNK:])
            vnews[n].append(vns[n])
    for (g, h), state in states.items():
        state_ref[g, h] = state
    for h in heads:
        sfin_ref[h] = states[(0, h)]
    for n, (g, sb, h) in enumerate(units):
        o = (jnp.concatenate(qss[n], axis=0)
             + _dot(attns[n], jnp.concatenate(vnews[n], axis=0).astype(BF16)))
        o = o * lax.rsqrt(jnp.mean(o * o, axis=-1, keepdims=True) + EPS) * onw_ref[...]
        o = o * dzs_ref[g, sb * bs:(sb + 1) * bs, h * LANES:(h + 1) * LANES]
        odn_ref[g, sb * bs:(sb + 1) * bs, h * LANES:(h + 1) * LANES] = o.astype(BF16)


def _cast_row_block(rows, nsteps):
    tile = 2 * SUBLANES
    for blk in range(tile, rows, tile):
        if rows % blk == 0 and rows // blk <= nsteps:
            return blk
    return rows


def _deltanet(qkv, scol, srow, dzs, onw, s0, *, group, tt, cast_weights=()):
    b, s, _ = qkv.shape
    nt = s // tt
    nsteps = (b // group) * nt
    cast_specs = []
    for w in cast_weights:
        blk = _cast_row_block(w.shape[0], nsteps)
        last = w.shape[0] // blk - 1
        cast_specs.append(pl.BlockSpec(
            (blk, w.shape[1]), lambda bi, i, last=last: (jnp.minimum(bi * nt + i, last), 0)))
    return pl.pallas_call(
        functools.partial(_dn_kernel, tt=tt, n_cast=len(cast_weights)),
        grid=(b // group, nt),
        in_specs=[
            pl.BlockSpec((group, tt, 3 * DN_WIDTH), lambda bi, i: (bi, i, 0)),
            pl.BlockSpec((group, tt, LANES), lambda bi, i: (bi, i, 0)),
            pl.BlockSpec((group, SROW, tt), lambda bi, i: (bi, 0, i)),
            pl.BlockSpec((group, tt, DN_WIDTH), lambda bi, i: (bi, i, 0)),
            pl.BlockSpec((1, DN_HEAD_DIM), lambda bi, i: (0, 0)),
            pl.BlockSpec((DN_HEADS, DN_HEAD_DIM, DN_HEAD_DIM), lambda bi, i: (0, 0, 0)),
        ] + cast_specs,
        out_specs=[pl.BlockSpec((group, tt, DN_WIDTH), lambda bi, i: (bi, i, 0)),
                   pl.BlockSpec((DN_HEADS, DN_HEAD_DIM, DN_HEAD_DIM), lambda bi, i: (0, 0, 0))]
        + cast_specs,
        out_shape=[jax.ShapeDtypeStruct((b, s, DN_WIDTH), BF16),
                   jax.ShapeDtypeStruct((DN_HEADS, DN_HEAD_DIM, DN_HEAD_DIM), F32)]
        + [jax.ShapeDtypeStruct(w.shape, BF16) for w in cast_weights],
        scratch_shapes=[pltpu.VMEM((group, DN_HEADS, DN_HEAD_DIM, DN_HEAD_DIM), F32)],
        compiler_params=pltpu.CompilerParams(dimension_semantics=("arbitrary", "arbitrary"),
                                             vmem_limit_bytes=VMEM_LIMIT),
        name="deltanet",
    )(qkv, scol, srow, dzs, onw, s0, *cast_weights)


def _tail_kernel(x_ref, ofox_ref, odn_ref, gates_ref, wbf_ref, wbd_ref, wout_ref, fnw_ref,
                 wg_ref, wu_ref, wd_ref, finw_ref, o_ref, *, ff_bounds):
    a = _dot_tn(ofox_ref[...], wbf_ref[...])
    bb = _dot(odn_ref[...], wbd_ref[...])
    y = gates_ref[:, :D_MODEL] * a + gates_ref[:, D_MODEL:] * bb
    h1 = x_ref[...] + _dot(y.astype(BF16), wout_ref[...])
    n = (h1 * lax.rsqrt(jnp.mean(h1 * h1, axis=-1, keepdims=True) + EPS) * fnw_ref[...]).astype(BF16)
    acc = h1
    for lo, hi in zip(ff_bounds[:-1], ff_bounds[1:]):
        gt = _dot(n, wg_ref[:, lo:hi])
        up = _dot(n, wu_ref[:, lo:hi])
        act = (gt * _sigmoid(gt) * up).astype(BF16)
        acc = acc + _dot(act, wd_ref[lo:hi, :])
    o_ref[...] = acc * lax.rsqrt(jnp.mean(acc * acc, axis=-1, keepdims=True) + EPS) * finw_ref[...]


def _tail(x2d, ofox, odn, gates, wbf, wbd, wout, fnw, wg, wu, wd, finw, *, tm, ff_bounds):
    m = x2d.shape[0]
    row = lambda w: pl.BlockSpec((tm, w), lambda i: (i, 0))
    return pl.pallas_call(
        functools.partial(_tail_kernel, ff_bounds=ff_bounds),
        grid=(m // tm,),
        in_specs=[row(D_MODEL), pl.BlockSpec((FOX_WIDTH, tm), lambda i: (0, i)), row(DN_WIDTH),
                  row(2 * D_MODEL),
                  _const_spec((FOX_WIDTH, D_MODEL)), _const_spec((DN_WIDTH, D_MODEL)),
                  _const_spec((D_MODEL, D_MODEL)), _const_spec((1, D_MODEL)),
                  _const_spec((D_MODEL, D_FF)), _const_spec((D_MODEL, D_FF)),
                  _const_spec((D_FF, D_MODEL)), _const_spec((1, D_MODEL))],
        out_specs=row(D_MODEL),
        out_shape=jax.ShapeDtypeStruct((m, D_MODEL), F32),
        compiler_params=pltpu.CompilerParams(dimension_semantics=("arbitrary",),
                                             vmem_limit_bytes=VMEM_LIMIT),
        name="merge_ffn",
    )(x2d, ofox, odn, gates, wbf, wbd, wout, fnw, wg, wu, wd, finw)


def _pick_tile(n, pref):
    t = min(pref, n)
    while n % t:
        t //= 2
    return t


def kernel(x, meta_tokens, mix_norm_w, w_in, fox_forget_bias, dn_conv_w, dn_a_log, dn_dt_bias,
           dn_out_norm_w, w_branch_fox, w_branch_dn, w_out, ffn_norm_w, w_ffn_gate, w_ffn_up,
           w_ffn_down, final_norm_w):
    b, s, _ = x.shape
    assert mix_norm_w.shape[0] == 1, "single layer only"
    assert s % PREFIX == 0
    m = b * s

    wi = w_in[0]
    o_small0 = 3 * FOX_WIDTH
    o_dn = o_small0 + FOX_HEADS
    o_small1 = o_dn + 3 * DN_WIDTH
    o_rest = o_small1 + 2 * DN_HEADS
    weights = (wi[:, :FOX_WIDTH].T.astype(BF16), wi[:, 2 * FOX_WIDTH:o_small0].T.astype(BF16),
               wi[:, FOX_WIDTH:2 * FOX_WIDTH].astype(BF16), wi[:, o_dn:o_small1].astype(BF16),
               wi[:, o_rest:].astype(BF16))
    w_alogit = wi[:, o_small1 + DN_HEADS:o_rest]
    wsmall = jnp.concatenate([wi[:, o_small0:o_dn], wi[:, o_small1:o_small1 + DN_HEADS],
                              w_alogit, w_alogit, w_alogit], axis=1)
    wsmall = jnp.pad(wsmall, ((0, 0), (0, LANES - N_SMALL))).astype(BF16)
    bias = jnp.zeros((SUBLANES, LANES), F32)
    bias = bias.at[0, LOGF_LANE:LOGF_LANE + FOX_HEADS].set(fox_forget_bias[0].astype(F32))
    bias = bias.at[0, G_LANE:N_SMALL].set(jnp.tile(dn_dt_bias[0].astype(F32), 3))
    bias = bias.at[1, G_LANE:N_SMALL].set(jnp.tile(dn_a_log[0].astype(F32), 3))
    nw = mix_norm_w[0].reshape(1, D_MODEL).astype(F32)

    x_p = jnp.concatenate([jnp.zeros((N_PAD, D_MODEL), F32), meta_tokens.astype(F32)], axis=0)
    convw = dn_conv_w[0].astype(F32)
    _, k_p, v_tp, kbias_p, qkv_p, dzs_p, _, scol_p, srow_p, conv_tail_p = _inproj(
        x_p, nw, weights, wsmall, bias, jnp.zeros((1, LANES), F32), convw,
        jnp.zeros((SUBLANES, 3 * DN_WIDTH), F32), tm=PREFIX, tiles_per_batch=1, n_pad=N_PAD)

    tm = _pick_tile(s, ROW_TILE)
    q_t, k, v_t, kbias, qkv, dzs, gates, scol, srow3, _ = _inproj(
        x.reshape(m, D_MODEL), nw, weights, wsmall, bias, scol_p[PREFIX - 1:PREFIX, :], convw,
        conv_tail_p, tm=tm, tiles_per_batch=s // tm, n_pad=0)

    scol3 = scol.reshape(b, s, LANES)
    tq = _pick_tile(s, ROW_TILE)
    ofox_t = _fox(q_t, k.reshape(b, s, FOX_WIDTH), v_t, kbias.reshape(b, s, LANES),
                  k_p[N_PAD:], v_tp[:, N_PAD:], kbias_p[N_PAD:], tq=tq, tk=min(MXU_DIM, tq // 2),
                  pairs=FOX_PAIRS_PER_STEP)

    onw = dn_out_norm_w[0].reshape(1, DN_HEAD_DIM).astype(F32)
    _, s_prefix = _deltanet(
        qkv_p[None], scol_p[None], srow_p, dzs_p[None], onw,
        jnp.zeros((DN_HEADS, DN_HEAD_DIM, DN_HEAD_DIM), F32), group=1, tt=PREFIX)
    group = _pick_tile(b, DN_UNITS // DN_HEADS)
    odn, _, wbf, wbd, wout, wg, wu, wd = _deltanet(
        qkv.reshape(b, s, 3 * DN_WIDTH), scol3, srow3, dzs.reshape(b, s, DN_WIDTH), onw, s_prefix,
        group=group, tt=_pick_tile(s, DN_UNITS // (DN_HEADS * group) * DN_BLOCK),
        cast_weights=(w_branch_fox[0], w_branch_dn[0], w_out[0], w_ffn_gate[0], w_ffn_up[0],
                      w_ffn_down[0]))

    out = _tail(
        x.reshape(m, D_MODEL), ofox_t, odn.reshape(m, DN_WIDTH), gates, wbf, wbd, wout,
        ffn_norm_w[0].reshape(1, D_MODEL).astype(F32), wg, wu, wd,
        final_norm_w.reshape(1, D_MODEL).astype(F32),
        tm=_pick_tile(m, ROW_TILE), ff_bounds=FF_BOUNDS)
    return out.reshape(b, s, D_MODEL)
```

```python
import functools

import jax
import jax.numpy as jnp
from jax import lax
from jax.experimental import pallas as pl
from jax.experimental.pallas import tpu as pltpu

F32 = jnp.float32
BF16 = jnp.bfloat16

LANES = 128
SUBLANES = 8
MXU_DIM = 256
ROW_TILE = 512
VMEM_LIMIT = 56 * 1024 * 1024

D_MODEL = 1024
N_META = 16
PREFIX = 128
N_PAD = PREFIX - N_META
FOX_HEADS = 8
FOX_HEAD_DIM = 64
FOX_WIDTH = FOX_HEADS * FOX_HEAD_DIM
V_ROWS = FOX_HEAD_DIM + SUBLANES
FOX_PAIRS_PER_STEP = 4
DN_HEADS = 4
DN_HEAD_DIM = 128
DN_WIDTH = DN_HEADS * DN_HEAD_DIM
DN_CHUNK = 64
DN_BLOCK = 2 * DN_CHUNK
DN_UNITS = 16
CONV_WIDTH = 4
D_FF = 2816
FF_BOUNDS = (0, (D_FF // MXU_DIM + 1) // 2 * MXU_DIM, D_FF)
EPS = 1e-6
NEG_INF = -1e30
LOG2E = 1.4426950408889634

COL_CHUNK = 512
LOGF_LANE = 0
BETA_LANE = FOX_HEADS
G_LANE = BETA_LANE + DN_HEADS
GSUF_LANE = G_LANE + DN_HEADS
GRAW_LANE = GSUF_LANE + DN_HEADS
N_SMALL = GRAW_LANE + DN_HEADS
SROW = -(-N_SMALL // SUBLANES) * SUBLANES


def _const_spec(shape):
    nd = len(shape)
    return pl.BlockSpec(shape, lambda *_: (0,) * nd, pipeline_mode=pl.Buffered(1))


def _sigmoid(x):
    return 0.5 * jnp.tanh(0.5 * x) + 0.5


def _dot(a, b, **kw):
    return jnp.dot(a, b, preferred_element_type=F32, **kw)


def _dot_nt(a, b, **kw):
    return lax.dot_general(a, b, (((1,), (1,)), ((), ())), preferred_element_type=F32, **kw)


def _dot_tn(a, b, **kw):
    return lax.dot_general(a, b, (((0,), (0,)), ((), ())), preferred_element_type=F32, **kw)


def _inproj_kernel(x_ref, nw_ref, wq_t_ref, wv_t_ref, wk_ref, wdn_ref, wrest_ref, wsmall_ref, bias_ref,
                   carry0_ref,
                   convw_ref, halo0_ref,
                   q_t_ref, k_ref, v_t_ref, kbias_ref, dqkv_ref, dzs_ref, gates_ref, scol_ref,
                   srow_ref, tail_ref, carry_ref, halo_ref, conv_ref, *, tm, tiles_per_batch, n_pad):
    i = pl.program_id(0)

    @pl.when(i % tiles_per_batch == 0)
    def _():
        carry_ref[...] = carry0_ref[...]
        halo_ref[...] = halo0_ref[...]

    x = x_ref[...]
    ms = jnp.mean(x * x, axis=-1, keepdims=True)
    hn = (x * lax.rsqrt(ms + EPS) * nw_ref[...]).astype(BF16)

    z = _dot(hn, wsmall_ref[...]) + bias_ref[0:1, :]
    lane = lax.broadcasted_iota(jnp.int32, (tm, LANES), 1)
    row = lax.broadcasted_iota(jnp.int32, (tm, LANES), 0)
    e = jnp.exp(-jnp.abs(z))
    l1p = jnp.log1p(e)
    logf = jnp.minimum(z, 0.0) - l1p
    softplus = jnp.maximum(z, 0.0) + l1p
    sig = jnp.where(z >= 0.0, 1.0, e) / (1.0 + e)
    g = -jnp.exp(bias_ref[1:2, :]) * softplus
    if n_pad:
        vm = (row + (i % tiles_per_batch) * tm >= n_pad).astype(F32)
        sig = sig * vm
        g = g * vm
    val = jnp.where(lane < BETA_LANE, logf,
                    jnp.where(lane < G_LANE, sig, jnp.where(lane < N_SMALL, g, 0.0)))
    in_chunk = row % DN_CHUNK
    is_logf = lane < BETA_LANE
    is_gpre = jnp.logical_and(lane >= G_LANE, lane < GSUF_LANE)
    is_gsuf = jnp.logical_and(lane >= GSUF_LANE, lane < GRAW_LANE)
    scan = val
    sh = 1
    while sh < tm:
        take_up = jnp.logical_and(is_logf, row >= sh)
        if sh < DN_CHUNK:
            take_up = jnp.logical_or(take_up, jnp.logical_and(is_gpre, in_chunk >= sh))
            take_down = jnp.logical_and(is_gsuf, in_chunk + sh < DN_CHUNK)
            below = jnp.where(take_down, pltpu.roll(scan, tm - sh, axis=0), 0.0)
        else:
            below = 0.0
        scan = scan + jnp.where(take_up, pltpu.roll(scan, sh, axis=0), below)
        sh *= 2
    out = scan + jnp.where(is_logf, carry_ref[...], 0.0)
    carry_ref[...] = out[tm - 1:tm, :]
    scol_ref[...] = out
    srow_ref[...] = out.T[:SROW, :]
    nc = jnp.where(is_logf, out * -LOG2E, 0.0)
    hi = nc.astype(BF16).astype(F32)
    mid = (nc - hi).astype(BF16).astype(F32)
    lo = nc - hi - mid
    kbias = hi + pltpu.roll(mid, FOX_HEADS, axis=1) + pltpu.roll(lo, 2 * FOX_HEADS, axis=1)
    kbias_ref[...] = kbias.astype(BF16)

    def main_chunk(c):
        if c == 0:
            return lambda: _dot(hn, wk_ref[...])
        ref, first = (wdn_ref, 1) if c < 4 else (wrest_ref, 4)
        return lambda: _dot(hn, ref[:, (c - first) * COL_CHUNK:(c - first + 1) * COL_CHUNK])

    def store_q_t(acc):
        q_t_ref[...] = (acc * (FOX_HEAD_DIM ** -0.5 * LOG2E)).astype(BF16)

    def store_v_t(acc):
        v_t = acc.astype(BF16)
        ones = jnp.ones((V_ROWS - FOX_HEAD_DIM, tm), BF16)
        v_t_ref[...] = jnp.concatenate(
            [piece for h in range(FOX_HEADS)
             for piece in (v_t[h * FOX_HEAD_DIM:(h + 1) * FOX_HEAD_DIM], ones)], axis=0)

    def store_k(acc):
        k_ref[...] = acc.astype(BF16)

    def store_dn(which):
        def epilogue(acc):
            cols = slice(which * COL_CHUNK, (which + 1) * COL_CHUNK)
            if n_pad:
                acc = acc * vm[:, :1]
            conv_ref[which, 0:SUBLANES, :] = halo_ref[:, cols]
            conv_ref[which, SUBLANES:, :] = acc
            halo_ref[:, cols] = acc[tm - SUBLANES:, :]
            tail_ref[:, cols] = acc[tm - SUBLANES:, :]
            a = convw_ref[CONV_WIDTH - 1:CONV_WIDTH, cols] * acc
            for t in range(CONV_WIDTH - 1):
                start = SUBLANES - (CONV_WIDTH - 1) + t
                a = a + convw_ref[t:t + 1, cols] * conv_ref[which, start:start + tm, :]
            a = a * _sigmoid(a)
            if which == 2:
                dqkv_ref[:, cols] = a
            else:
                scale = DN_HEAD_DIM ** -0.5 if which == 0 else 1.0
                for h in range(DN_HEADS):
                    ah = a[:, h * LANES:(h + 1) * LANES]
                    inv = lax.rsqrt(jnp.sum(ah * ah, axis=-1, keepdims=True) + EPS) * scale
                    lo_col = which * COL_CHUNK + h * LANES
                    dqkv_ref[:, lo_col:lo_col + LANES] = ah * inv
        return epilogue

    def store_dzs(acc):
        dzs_ref[...] = acc * _sigmoid(acc)

    def store_gates(j):
        def epilogue(acc):
            gates_ref[:, j * COL_CHUNK:(j + 1) * COL_CHUNK] = _sigmoid(acc)
        return epilogue

    store_q_t(_dot_nt(wq_t_ref[...], hn))
    store_v_t(_dot_nt(wv_t_ref[...], hn))
    store_k(main_chunk(0)())
    for which in range(3):
        store_dn(which)(main_chunk(1 + which)())
    store_dzs(main_chunk(4)())
    for j in range(2 * D_MODEL // COL_CHUNK):
        store_gates(j)(main_chunk(5 + j)())


def _inproj(x2d, nw, weights, wsmall, bias, carry0, convw, halo0, *, tm, tiles_per_batch, n_pad):
    m = x2d.shape[0]
    row = lambda w: pl.BlockSpec((tm, w), lambda i: (i, 0))
    col = lambda h: pl.BlockSpec((h, tm), lambda i: (0, i))
    return pl.pallas_call(
        functools.partial(_inproj_kernel, tm=tm, tiles_per_batch=tiles_per_batch, n_pad=n_pad),
        grid=(m // tm,),
        in_specs=[row(D_MODEL), _const_spec((1, D_MODEL))] + [_const_spec(w.shape) for w in weights] + [
                  _const_spec((D_MODEL, LANES)), _const_spec((SUBLANES, LANES)),
                  _const_spec((1, LANES)), _const_spec((CONV_WIDTH, 3 * DN_WIDTH)),
                  _const_spec((SUBLANES, 3 * DN_WIDTH))],
        out_specs=[col(FOX_WIDTH), row(FOX_WIDTH), col(FOX_HEADS * V_ROWS), row(LANES),
                   row(3 * DN_WIDTH), row(DN_WIDTH), row(2 * D_MODEL),
                   row(LANES),
                   pl.BlockSpec((None, SROW, tm),
                                lambda i: (i // tiles_per_batch, 0, i % tiles_per_batch)),
                   pl.BlockSpec((SUBLANES, 3 * DN_WIDTH), lambda i: (0, 0))],
        out_shape=[jax.ShapeDtypeStruct((FOX_WIDTH, m), BF16),
                   jax.ShapeDtypeStruct((m, FOX_WIDTH), BF16),
                   jax.ShapeDtypeStruct((FOX_HEADS * V_ROWS, m), BF16),
                   jax.ShapeDtypeStruct((m, LANES), BF16),
                   jax.ShapeDtypeStruct((m, 3 * DN_WIDTH), F32),
                   jax.ShapeDtypeStruct((m, DN_WIDTH), F32),
                   jax.ShapeDtypeStruct((m, 2 * D_MODEL), F32),
                   jax.ShapeDtypeStruct((m, LANES), F32),
                   jax.ShapeDtypeStruct((m // (tm * tiles_per_batch), SROW,
                                         tm * tiles_per_batch), F32),
                   jax.ShapeDtypeStruct((SUBLANES, 3 * DN_WIDTH), F32)],
        scratch_shapes=[pltpu.VMEM((1, LANES), F32), pltpu.VMEM((SUBLANES, 3 * DN_WIDTH), F32),
                        pltpu.VMEM((3, tm + SUBLANES, COL_CHUNK), F32)],
        compiler_params=pltpu.CompilerParams(dimension_semantics=("arbitrary",),
                                             vmem_limit_bytes=VMEM_LIMIT),
        name="inproj",
    )(x2d, nw, *weights, wsmall, bias, carry0, convw, halo0)


def _fox_kernel(q_t_ref, k_ref, v_t_ref, kb_ref, kp_ref, v_tp_ref, kbp_ref,
                o_ref, m_ref, l_ref, acc_ref, t_ref, tmax_ref, *, tq, tk, pairs):
    pg = pl.program_id(1)
    i = pl.program_id(2)
    heads = range(2 * pairs)
    sub = lax.broadcasted_iota(jnp.int32, (LANES, 1), 0)
    rhs = []
    for hh in heads:
        pp, h = divmod(hh, 2)
        q_t = q_t_ref[pp * LANES:(pp + 1) * LANES, :]
        mine = sub < FOX_HEAD_DIM if h == 0 else sub >= FOX_HEAD_DIM
        head = 2 * (pg * pairs + pp) + h
        sel = jnp.where(jnp.logical_and(sub < 3 * FOX_HEADS, (sub % FOX_HEADS) == head),
                        1.0, 0.0).astype(BF16) * jnp.ones((1, tq), BF16)
        rhs.append(jnp.concatenate([jnp.where(mine, q_t, jnp.zeros_like(q_t)), sel], axis=0))

    def scores(kt, kbt, q_lo=0):
        lhs = [jnp.concatenate([kt[:, pp * LANES:(pp + 1) * LANES], kbt], axis=1)
               for pp in range(pairs)]
        return [_dot(lhs[hh // 2], rhs[hh][:, q_lo:]) for hh in heads]

    def column_max(ts):
        return [jnp.max(ts[hh], axis=0, keepdims=True) for hh in heads]

    def scores_into(slot, off, q_lo=0):
        ts = scores(k_ref[pl.ds(off, tk), :], kb_ref[pl.ds(off, tk), :], q_lo)
        t_max = column_max(ts)
        for hh in heads:
            t_ref[slot, hh, :, :tq - q_lo] = ts[hh]
            tmax_ref[slot, hh, :, :tq - q_lo] = t_max[hh]

    def update(read_t, t_max, v_aug, mask, first, q_lo=0):
        width = tq - q_lo

        def masked_t(hh):
            t = read_t(hh)
            return t if mask is None else jnp.where(mask, t, NEG_INF)

        if t_max is None:
            t_max = [jnp.max(masked_t(hh), axis=0, keepdims=True) for hh in heads]
        if first:
            m_new = t_max
        else:
            m_prev = [m_ref[hh, :, q_lo:] for hh in heads]
            m_new = [jnp.maximum(m_prev[hh], t_max[hh]) for hh in heads]
        pm = [jnp.exp2(masked_t(hh) - m_new[hh]).astype(BF16) for hh in heads]
        r = [_dot(v_aug[hh * V_ROWS:(hh + 1) * V_ROWS, :], pm[hh]) for hh in heads]
        pv = jnp.concatenate([r[hh][:FOX_HEAD_DIM] for hh in heads], axis=0)
        psum = [r[hh][FOX_HEAD_DIM:FOX_HEAD_DIM + 1] for hh in heads]
        if first:
            for hh in heads:
                l_ref[hh, :, q_lo:] = psum[hh]
            acc_ref[:, q_lo:] = pv
        else:
            alpha = [jnp.exp2(m_prev[hh] - m_new[hh]) for hh in heads]
            for hh in heads:
                l_ref[hh, :, q_lo:] = alpha[hh] * l_ref[hh, :, q_lo:] + psum[hh]
            alpha_rows = jnp.concatenate(
                [jnp.broadcast_to(alpha[hh], (FOX_HEAD_DIM, width)) for hh in heads], axis=0)
            acc_ref[:, q_lo:] = alpha_rows * acc_ref[:, q_lo:] + pv
        for hh in heads:
            m_ref[hh, :, q_lo:] = m_new[hh]

    ts_p = scores(kp_ref[...], kbp_ref[...])
    update(lambda hh: ts_p[hh], column_max(ts_p), v_tp_ref[...], None, True)

    def v_at(off):
        return v_t_ref[:, pl.ds(off, tk)]

    def slot(n, q_lo=0):
        return lambda hh: t_ref[n, hh, :, :tq - q_lo]

    def slot_max(n):
        return [tmax_ref[n, hh] for hh in heads]

    pairs_per_q = tq // (2 * tk)
    scores_into(0, 0)

    def body(jj, carry):
        off = pl.multiple_of(jj * 2 * tk, 2 * tk)
        scores_into(1, off + tk)
        update(slot(0), slot_max(0), v_at(off), None, False)
        scores_into(0, off + 2 * tk)
        update(slot(1), slot_max(1), v_at(off + tk), None, False)
        return carry

    lax.fori_loop(0, i * pairs_per_q, body, 0)

    def causal(q_lo):
        rr = lax.broadcasted_iota(jnp.int32, (tk, tq - q_lo), 0)
        cc = lax.broadcasted_iota(jnp.int32, (tk, tq - q_lo), 1)
        return rr <= cc

    for dp in range(pairs_per_q):
        off = pl.multiple_of(i * tq + dp * 2 * tk, 2 * tk)
        q_even, q_odd = dp * 2 * tk, (dp * 2 + 1) * tk
        scores_into(1, off + tk, q_odd)
        update(slot(0, q_even), None, v_at(off), causal(q_even), False, q_even)
        if dp + 1 < pairs_per_q:
            scores_into(0, off + 2 * tk, q_even + 2 * tk)
        update(slot(1, q_odd), None, v_at(off + tk), causal(q_odd), False, q_odd)

    l_rows = jnp.concatenate([jnp.broadcast_to(l_ref[hh], (FOX_HEAD_DIM, tq)) for hh in heads], axis=0)
    o_ref[...] = (acc_ref[...] / l_rows).astype(BF16)


def _fox(q_t, k, v_t, kbias, k_p, v_tp, kbias_p, *, tq, tk, pairs):
    b, s, _ = k.shape
    ngroups = FOX_HEADS // (2 * pairs)
    nq = s // tq
    wl = pairs * LANES
    return pl.pallas_call(
        functools.partial(_fox_kernel, tq=tq, tk=tk, pairs=pairs),
        grid=(b, ngroups, nq),
        in_specs=[
            pl.BlockSpec((wl, tq), lambda bi, pg, i: (pg, bi * nq + i)),
            pl.BlockSpec((None, s, wl), lambda bi, pg, i: (bi, 0, pg)),
            pl.BlockSpec((2 * pairs * V_ROWS, s), lambda bi, pg, i: (pg, bi)),
            pl.BlockSpec((None, s, LANES), lambda bi, pg, i: (bi, 0, 0)),
            pl.BlockSpec((N_META, wl), lambda bi, pg, i: (0, pg)),
            pl.BlockSpec((2 * pairs * V_ROWS, N_META), lambda bi, pg, i: (pg, 0)),
            pl.BlockSpec((N_META, LANES), lambda bi, pg, i: (0, 0)),
        ],
        out_specs=pl.BlockSpec((wl, tq), lambda bi, pg, i: (pg, bi * nq + i)),
        out_shape=jax.ShapeDtypeStruct((FOX_WIDTH, b * s), BF16),
        scratch_shapes=[pltpu.VMEM((2 * pairs, 1, tq), F32), pltpu.VMEM((2 * pairs, 1, tq), F32),
                        pltpu.VMEM((wl, tq), F32), pltpu.VMEM((2, 2 * pairs, tk, tq), F32),
                        pltpu.VMEM((2, 2 * pairs, 1, tq), F32)],
        compiler_params=pltpu.CompilerParams(
            dimension_semantics=("arbitrary", "arbitrary", "arbitrary"),
            vmem_limit_bytes=VMEM_LIMIT),
        name="fox_attention",
    )(q_t, k, v_t, kbias, k_p, v_tp, kbias_p)


def _dn_kernel(qkv_ref, scol_ref, srow_ref, dzs_ref, onw_ref, s0_ref, *refs, tt, n_cast):
    cast_in, (odn_ref, sfin_ref), cast_out = refs[:n_cast], refs[n_cast:n_cast + 2], refs[n_cast + 2:-1]
    state_ref = refs[-1]
    for src, dst in zip(cast_in, cast_out):
        dst[...] = src[...].astype(BF16)
    i = pl.program_id(1)
    nchunk = tt // DN_CHUNK

    @pl.when(i == 0)
    def _():
        for g in range(qkv_ref.shape[0]):
            state_ref[g] = s0_ref[...]

    bs = DN_BLOCK
    heads = range(DN_HEADS)
    groups = range(qkv_ref.shape[0])
    units = [(g, sb, h) for g in groups for sb in range(tt // bs) for h in heads]

    def slab(g, sb, c0):
        return qkv_ref[g, sb * bs:(sb + 1) * bs, c0:c0 + LANES]

    def small_col(g, sb, lane):
        return scol_ref[g, sb * bs:(sb + 1) * bs, lane:lane + 1]

    rr = lax.broadcasted_iota(jnp.int32, (bs, bs), 0)
    cc = lax.broadcasted_iota(jnp.int32, (bs, bs), 1)
    rc_xor = jnp.bitwise_xor(rr, cc)
    same = rc_xor < DN_CHUNK
    tril_m = jnp.logical_and(same, rr >= cc)
    strict_m = jnp.logical_and(same, rr > cc)
    eye = (rr == cc).astype(F32)

    qs = [slab(g, sb, h * LANES) for g, sb, h in units]
    ks = [slab(g, sb, DN_WIDTH + h * LANES) for g, sb, h in units]
    vs = [slab(g, sb, 2 * DN_WIDTH + h * LANES) for g, sb, h in units]
    states = {(g, h): state_ref[g, h] for g in groups for h in heads}
    betas = [small_col(g, sb, BETA_LANE + h) for g, sb, h in units]
    gcs = [small_col(g, sb, G_LANE + h) for g, sb, h in units]
    gls = [gcs[n] + small_col(g, sb, GSUF_LANE + h) - small_col(g, sb, GRAW_LANE + h)
           for n, (g, sb, h) in enumerate(units)]
    decays = [jnp.exp(jnp.where(
        tril_m, gcs[n] - srow_ref[g, G_LANE + h:G_LANE + h + 1, sb * bs:(sb + 1) * bs], NEG_INF))
        for n, (g, sb, h) in enumerate(units)]
    nu = range(len(units))
    egcs = [jnp.exp(g) for g in gcs]
    kbs = [ks[n] * betas[n] for n in nu]
    ks_b = [k.astype(BF16) for k in ks]
    lmats = [jnp.where(strict_m, _dot_nt(kbs[n].astype(BF16), ks_b[n]) * decays[n], 0.0) for n in nu]
    attns = [(_dot_nt(qs[n].astype(BF16), ks_b[n]) * decays[n]).astype(BF16) for n in nu]
    ainvs = [eye - jnp.where(rc_xor == 1, lm, 0.0) for lm in lmats]
    for lvl in range(1, 6):
        blk = 2 ** lvl
        lvl_m = jnp.logical_and(rc_xor >= blk, rc_xor < 2 * blk)
        ainvs_b = [a.astype(BF16) for a in ainvs]
        mids = [_dot(jnp.where(lvl_m, lmats[n], 0.0).astype(BF16), ainvs_b[n]).astype(BF16)
                for n in nu]
        ainvs = [ainvs[n] - _dot(ainvs_b[n], mids[n]) for n in nu]
    sols = [_dot(ainvs[n].astype(BF16),
                 jnp.concatenate([vs[n] * betas[n], kbs[n] * egcs[n]], axis=1).astype(BF16))
            for n in nu]
    uus = [sol[:, :DN_HEAD_DIM] for sol in sols]
    wws = [sol[:, DN_HEAD_DIM:] for sol in sols]
    qds = [qs[n] * egcs[n] for n in nu]
    kds = [(ks[n] * jnp.exp(gls[n] - gcs[n])).astype(BF16) for n in nu]

    vnews = [[] for _ in nu]
    qss = [[] for _ in nu]
    for c in range(nchunk):
        csb, lc = divmod(c, bs // DN_CHUNK)
        lo, hi = lc * DN_CHUNK, (lc + 1) * DN_CHUNK
        cur = [(n, (g, h)) for n, (g, sb, h) in enumerate(units) if sb == csb]
        rs = {n: _dot(jnp.concatenate([wws[n][lo:hi], qds[n][lo:hi]], axis=0).astype(BF16),
                      states[key].astype(BF16)) for n, key in cur}
        vns = {n: uus[n][lo:hi] - rs[n][:DN_CHUNK] for n, _ in cur}
        states.update({key: jnp.exp(gls[n][lo:lo + 1, :]) * states[key]
                       + _dot_tn(kds[n][lo:hi], vns[n].astype(BF16)) for n, key in cur})
        for n, _ in cur:
            qss[n].append(rs[n][DN_CHUNK:])
            vnews[n].append(vns[n])
    for (g, h), state in states.items():
        state_ref[g, h] = state
    for h in heads:
        sfin_ref[h] = states[(0, h)]
    for n, (g, sb, h) in enumerate(units):
        o = (jnp.concatenate(qss[n], axis=0)
             + _dot(attns[n], jnp.concatenate(vnews[n], axis=0).astype(BF16)))
        o = o * lax.rsqrt(jnp.mean(o * o, axis=-1, keepdims=True) + EPS) * onw_ref[...]
        o = o * dzs_ref[g, sb * bs:(sb + 1) * bs, h * LANES:(h + 1) * LANES]
        odn_ref[g, sb * bs:(sb + 1) * bs, h * LANES:(h + 1) * LANES] = o.astype(BF16)


def _cast_row_block(rows, nsteps):
    tile = 2 * SUBLANES
    for blk in range(tile, rows, tile):
        if rows % blk == 0 and rows // blk <= nsteps:
            return blk
    return rows


def _deltanet(qkv, scol, srow, dzs, onw, s0, *, group, tt, cast_weights=()):
    b, s, _ = qkv.shape
    nt = s // tt
    nsteps = (b // group) * nt
    cast_specs = []
    for w in cast_weights:
        blk = _cast_row_block(w.shape[0], nsteps)
        last = w.shape[0] // blk - 1
        cast_specs.append(pl.BlockSpec(
            (blk, w.shape[1]), lambda bi, i, last=last: (jnp.minimum(bi * nt + i, last), 0)))
    return pl.pallas_call(
        functools.partial(_dn_kernel, tt=tt, n_cast=len(cast_weights)),
        grid=(b // group, nt),
        in_specs=[
            pl.BlockSpec((group, tt, 3 * DN_WIDTH), lambda bi, i: (bi, i, 0)),
            pl.BlockSpec((group, tt, LANES), lambda bi, i: (bi, i, 0)),
            pl.BlockSpec((group, SROW, tt), lambda bi, i: (bi, 0, i)),
            pl.BlockSpec((group, tt, DN_WIDTH), lambda bi, i: (bi, i, 0)),
            pl.BlockSpec((1, DN_HEAD_DIM), lambda bi, i: (0, 0)),
            pl.BlockSpec((DN_HEADS, DN_HEAD_DIM, DN_HEAD_DIM), lambda bi, i: (0, 0, 0)),
        ] + cast_specs,
        out_specs=[pl.BlockSpec((group, tt, DN_WIDTH), lambda bi, i: (bi, i, 0)),
                   pl.BlockSpec((DN_HEADS, DN_HEAD_DIM, DN_HEAD_DIM), lambda bi, i: (0, 0, 0))]
        + cast_specs,
        out_shape=[jax.ShapeDtypeStruct((b, s, DN_WIDTH), BF16),
                   jax.ShapeDtypeStruct((DN_HEADS, DN_HEAD_DIM, DN_HEAD_DIM), F32)]
        + [jax.ShapeDtypeStruct(w.shape, BF16) for w in cast_weights],
        scratch_shapes=[pltpu.VMEM((group, DN_HEADS, DN_HEAD_DIM, DN_HEAD_DIM), F32)],
        compiler_params=pltpu.CompilerParams(dimension_semantics=("arbitrary", "arbitrary"),
                                             vmem_limit_bytes=VMEM_LIMIT),
        name="deltanet",
    )(qkv, scol, srow, dzs, onw, s0, *cast_weights)


def _tail_kernel(x_ref, ofox_ref, odn_ref, gates_ref, wbf_ref, wbd_ref, wout_ref, fnw_ref,
                 wg_ref, wu_ref, wd_ref, finw_ref, o_ref, *, ff_bounds):
    a = _dot_tn(ofox_ref[...], wbf_ref[...])
    bb = _dot(odn_ref[...], wbd_ref[...])
    y = gates_ref[:, :D_MODEL] * a + gates_ref[:, D_MODEL:] * bb
    h1 = x_ref[...] + _dot(y.astype(BF16), wout_ref[...])
    n = (h1 * lax.rsqrt(jnp.mean(h1 * h1, axis=-1, keepdims=True) + EPS) * fnw_ref[...]).astype(BF16)
    acc = h1
    for lo, hi in zip(ff_bounds[:-1], ff_bounds[1:]):
        gt = _dot(n, wg_ref[:, lo:hi])
        up = _dot(n, wu_ref[:, lo:hi])
        act = (gt * _sigmoid(gt) * up).astype(BF16)
        acc = acc + _dot(act, wd_ref[lo:hi, :])
    o_ref[...] = acc * lax.rsqrt(jnp.mean(acc * acc, axis=-1, keepdims=True) + EPS) * finw_ref[...]


def _tail(x2d, ofox, odn, gates, wbf, wbd, wout, fnw, wg, wu, wd, finw, *, tm, ff_bounds):
    m = x2d.shape[0]
    row = lambda w: pl.BlockSpec((tm, w), lambda i: (i, 0))
    return pl.pallas_call(
        functools.partial(_tail_kernel, ff_bounds=ff_bounds),
        grid=(m // tm,),
        in_specs=[row(D_MODEL), pl.BlockSpec((FOX_WIDTH, tm), lambda i: (0, i)), row(DN_WIDTH),
                  row(2 * D_MODEL),
                  _const_spec((FOX_WIDTH, D_MODEL)), _const_spec((DN_WIDTH, D_MODEL)),
                  _const_spec((D_MODEL, D_MODEL)), _const_spec((1, D_MODEL)),
                  _const_spec((D_MODEL, D_FF)), _const_spec((D_MODEL, D_FF)),
                  _const_spec((D_FF, D_MODEL)), _const_spec((1, D_MODEL))],
        out_specs=row(D_MODEL),
        out_shape=jax.ShapeDtypeStruct((m, D_MODEL), F32),
        compiler_params=pltpu.CompilerParams(dimension_semantics=("arbitrary",),
                                             vmem_limit_bytes=VMEM_LIMIT),
        name="merge_ffn",
    )(x2d, ofox, odn, gates, wbf, wbd, wout, fnw, wg, wu, wd, finw)


def _pick_tile(n, pref):
    t = min(pref, n)
    while n % t:
        t //= 2
    return t


def kernel(x, meta_tokens, mix_norm_w, w_in, fox_forget_bias, dn_conv_w, dn_a_log, dn_dt_bias,
           dn_out_norm_w, w_branch_fox, w_branch_dn, w_out, ffn_norm_w, w_ffn_gate, w_ffn_up,
           w_ffn_down, final_norm_w):
    b, s, _ = x.shape
    assert mix_norm_w.shape[0] == 1, "single layer only"
    assert s % PREFIX == 0
    m = b * s

    wi = w_in[0]
    o_small0 = 3 * FOX_WIDTH
    o_dn = o_small0 + FOX_HEADS
    o_small1 = o_dn + 3 * DN_WIDTH
    o_rest = o_small1 + 2 * DN_HEADS
    weights = (wi[:, :FOX_WIDTH].T.astype(BF16), wi[:, 2 * FOX_WIDTH:o_small0].T.astype(BF16),
               wi[:, FOX_WIDTH:2 * FOX_WIDTH].astype(BF16), wi[:, o_dn:o_small1].astype(BF16),
               wi[:, o_rest:].astype(BF16))
    w_alogit = wi[:, o_small1 + DN_HEADS:o_rest]
    wsmall = jnp.concatenate([wi[:, o_small0:o_dn], wi[:, o_small1:o_small1 + DN_HEADS],
                              w_alogit, w_alogit, w_alogit], axis=1)
    wsmall = jnp.pad(wsmall, ((0, 0), (0, LANES - N_SMALL))).astype(BF16)
    bias = jnp.zeros((SUBLANES, LANES), F32)
    bias = bias.at[0, LOGF_LANE:LOGF_LANE + FOX_HEADS].set(fox_forget_bias[0].astype(F32))
    bias = bias.at[0, G_LANE:N_SMALL].set(jnp.tile(dn_dt_bias[0].astype(F32), 3))
    bias = bias.at[1, G_LANE:N_SMALL].set(jnp.tile(dn_a_log[0].astype(F32), 3))
    nw = mix_norm_w[0].reshape(1, D_MODEL).astype(F32)

    x_p = jnp.concatenate([jnp.zeros((N_PAD, D_MODEL), F32), meta_tokens.astype(F32)], axis=0)
    convw = dn_conv_w[0].astype(F32)
    _, k_p, v_tp, kbias_p, qkv_p, dzs_p, _, scol_p, srow_p, conv_tail_p = _inproj(
        x_p, nw, weights, wsmall, bias, jnp.zeros((1, LANES), F32), convw,
        jnp.zeros((SUBLANES, 3 * DN_WIDTH), F32), tm=PREFIX, tiles_per_batch=1, n_pad=N_PAD)

    tm = _pick_tile(s, ROW_TILE)
    q_t, k, v_t, kbias, qkv, dzs, gates, scol, srow3, _ = _inproj(
        x.reshape(m, D_MODEL), nw, weights, wsmall, bias, scol_p[PREFIX - 1:PREFIX, :], convw,
        conv_tail_p, tm=tm, tiles_per_batch=s // tm, n_pad=0)

    scol3 = scol.reshape(b, s, LANES)
    tq = _pick_tile(s, ROW_TILE)
    ofox_t = _fox(q_t, k.reshape(b, s, FOX_WIDTH), v_t, kbias.reshape(b, s, LANES),
                  k_p[N_PAD:], v_tp[:, N_PAD:], kbias_p[N_PAD:], tq=tq, tk=min(MXU_DIM, tq // 2),
                  pairs=FOX_PAIRS_PER_STEP)

    onw = dn_out_norm_w[0].reshape(1, DN_HEAD_DIM).astype(F32)
    _, s_prefix = _deltanet(
        qkv_p[None], scol_p[None], srow_p, dzs_p[None], onw,
        jnp.zeros((DN_HEADS, DN_HEAD_DIM, DN_HEAD_DIM), F32), group=1, tt=PREFIX)
    group = _pick_tile(b, DN_UNITS // DN_HEADS)
    odn, _, wbf, wbd, wout, wg, wu, wd = _deltanet(
        qkv.reshape(b, s, 3 * DN_WIDTH), scol3, srow3, dzs.reshape(b, s, DN_WIDTH), onw, s_prefix,
        group=group, tt=_pick_tile(s, DN_UNITS // (DN_HEADS * group) * DN_BLOCK),
        cast_weights=(w_branch_fox[0], w_branch_dn[0], w_out[0], w_ffn_gate[0], w_ffn_up[0],
                      w_ffn_down[0]))

    out = _tail(
        x.reshape(m, D_MODEL), ofox_t, odn.reshape(m, DN_WIDTH), gates, wbf, wbd, wout,
        ffn_norm_w[0].reshape(1, D_MODEL).astype(F32), wg, wu, wd,
        final_norm_w.reshape(1, D_MODEL).astype(F32),
        tm=_pick_tile(m, ROW_TILE), ff_bounds=FF_BOUNDS)
    return out.reshape(b, s, D_MODEL)
```

```python
import functools

import jax
import jax.numpy as jnp
from jax import lax
from jax.experimental import pallas as pl
from jax.experimental.pallas import tpu as pltpu

F32 = jnp.float32
BF16 = jnp.bfloat16

LANES = 128
SUBLANES = 8
MXU_DIM = 256
ROW_TILE = 512
VMEM_LIMIT = 56 * 1024 * 1024

D_MODEL = 1024
N_META = 16
PREFIX = 128
N_PAD = PREFIX - N_META
FOX_HEADS = 8
FOX_HEAD_DIM = 64
FOX_WIDTH = FOX_HEADS * FOX_HEAD_DIM
V_ROWS = FOX_HEAD_DIM + SUBLANES
FOX_PAIRS_PER_STEP = 4
DN_HEADS = 4
DN_HEAD_DIM = 128
DN_WIDTH = DN_HEADS * DN_HEAD_DIM
DN_CHUNK = 64
DN_BLOCK = 2 * DN_CHUNK
DN_UNITS = 16
CONV_WIDTH = 4
D_FF = 2816
FF_BOUNDS = (0, (D_FF // MXU_DIM + 1) // 2 * MXU_DIM, D_FF)
EPS = 1e-6
NEG_INF = -1e30
LOG2E = 1.4426950408889634

COL_CHUNK = 512
LOGF_LANE = 0
BETA_LANE = FOX_HEADS
G_LANE = BETA_LANE + DN_HEADS
GSUF_LANE = G_LANE + DN_HEADS
GRAW_LANE = GSUF_LANE + DN_HEADS
N_SMALL = GRAW_LANE + DN_HEADS
SROW = -(-N_SMALL // SUBLANES) * SUBLANES


def _const_spec(shape):
    nd = len(shape)
    return pl.BlockSpec(shape, lambda *_: (0,) * nd, pipeline_mode=pl.Buffered(1))


def _sigmoid(x):
    return 0.5 * jnp.tanh(0.5 * x) + 0.5


def _dot(a, b, **kw):
    return jnp.dot(a, b, preferred_element_type=F32, **kw)


def _dot_nt(a, b, **kw):
    return lax.dot_general(a, b, (((1,), (1,)), ((), ())), preferred_element_type=F32, **kw)


def _dot_tn(a, b, **kw):
    return lax.dot_general(a, b, (((0,), (0,)), ((), ())), preferred_element_type=F32, **kw)


def _inproj_kernel(x_ref, nw_ref, wq_t_ref, wv_t_ref, wk_ref, wdn_ref, wrest_ref, wsmall_ref, bias_ref,
                   carry0_ref,
                   convw_ref, halo0_ref,
                   q_t_ref, k_ref, v_t_ref, kbias_ref, dqkv_ref, dzs_ref, gates_ref, scol_ref,
                   srow_ref, tail_ref, carry_ref, halo_ref, conv_ref, *, tm, tiles_per_batch, n_pad):
    i = pl.program_id(0)

    @pl.when(i % tiles_per_batch == 0)
    def _():
        carry_ref[...] = carry0_ref[...]
        halo_ref[...] = halo0_ref[...]

    x = x_ref[...]
    ms = jnp.mean(x * x, axis=-1, keepdims=True)
    hn = (x * lax.rsqrt(ms + EPS) * nw_ref[...]).astype(BF16)

    z = _dot(hn, wsmall_ref[...]) + bias_ref[0:1, :]
    lane = lax.broadcasted_iota(jnp.int32, (tm, LANES), 1)
    row = lax.broadcasted_iota(jnp.int32, (tm, LANES), 0)
    e = jnp.exp(-jnp.abs(z))
    l1p = jnp.log1p(e)
    logf = jnp.minimum(z, 0.0) - l1p
    softplus = jnp.maximum(z, 0.0) + l1p
    sig = jnp.where(z >= 0.0, 1.0, e) / (1.0 + e)
    g = -jnp.exp(bias_ref[1:2, :]) * softplus
    if n_pad:
        vm = (row + (i % tiles_per_batch) * tm >= n_pad).astype(F32)
        sig = sig * vm
        g = g * vm
    val = jnp.where(lane < BETA_LANE, logf,
                    jnp.where(lane < G_LANE, sig, jnp.where(lane < N_SMALL, g, 0.0)))
    in_chunk = row % DN_CHUNK
    is_logf = lane < BETA_LANE
    is_gpre = jnp.logical_and(lane >= G_LANE, lane < GSUF_LANE)
    is_gsuf = jnp.logical_and(lane >= GSUF_LANE, lane < GRAW_LANE)
    scan = val
    sh = 1
    while sh < tm:
        take_up = jnp.logical_and(is_logf, row >= sh)
        if sh < DN_CHUNK:
            take_up = jnp.logical_or(take_up, jnp.logical_and(is_gpre, in_chunk >= sh))
            take_down = jnp.logical_and(is_gsuf, in_chunk + sh < DN_CHUNK)
            below = jnp.where(take_down, pltpu.roll(scan, tm - sh, axis=0), 0.0)
        else:
            below = 0.0
        scan = scan + jnp.where(take_up, pltpu.roll(scan, sh, axis=0), below)
        sh *= 2
    out = scan + jnp.where(is_logf, carry_ref[...], 0.0)
    carry_ref[...] = out[tm - 1:tm, :]
    scol_ref[...] = out
    srow_ref[...] = out.T[:SROW, :]
    nc = jnp.where(is_logf, out * -LOG2E, 0.0)
    hi = nc.astype(BF16).astype(F32)
    mid = (nc - hi).astype(BF16).astype(F32)
    lo = nc - hi - mid
    kbias = hi + pltpu.roll(mid, FOX_HEADS, axis=1) + pltpu.roll(lo, 2 * FOX_HEADS, axis=1)
    kbias_ref[...] = kbias.astype(BF16)

    def main_chunk(c):
        if c == 0:
            return lambda: _dot(hn, wk_ref[...])
        ref, first = (wdn_ref, 1) if c < 4 else (wrest_ref, 4)
        return lambda: _dot(hn, ref[:, (c - first) * COL_CHUNK:(c - first + 1) * COL_CHUNK])

    def store_q_t(acc):
        q_t_ref[...] = (acc * (FOX_HEAD_DIM ** -0.5 * LOG2E)).astype(BF16)

    def store_v_t(acc):
        v_t = acc.astype(BF16)
        ones = jnp.ones((V_ROWS - FOX_HEAD_DIM, tm), BF16)
        v_t_ref[...] = jnp.concatenate(
            [piece for h in range(FOX_HEADS)
             for piece in (v_t[h * FOX_HEAD_DIM:(h + 1) * FOX_HEAD_DIM], ones)], axis=0)

    def store_k(acc):
        k_ref[...] = acc.astype(BF16)

    def store_dn(which):
        def epilogue(acc):
            cols = slice(which * COL_CHUNK, (which + 1) * COL_CHUNK)
            if n_pad:
                acc = acc * vm[:, :1]
            conv_ref[which, 0:SUBLANES, :] = halo_ref[:, cols]
            conv_ref[which, SUBLANES:, :] = acc
            halo_ref[:, cols] = acc[tm - SUBLANES:, :]
            tail_ref[:, cols] = acc[tm - SUBLANES:, :]
            a = convw_ref[CONV_WIDTH - 1:CONV_WIDTH, cols] * acc
            for t in range(CONV_WIDTH - 1):
                start = SUBLANES - (CONV_WIDTH - 1) + t
                a = a + convw_ref[t:t + 1, cols] * conv_ref[which, start:start + tm, :]
            a = a * _sigmoid(a)
            if which == 2:
                dqkv_ref[:, cols] = a
            else:
                scale = DN_HEAD_DIM ** -0.5 if which == 0 else 1.0
                for h in range(DN_HEADS):
                    ah = a[:, h * LANES:(h + 1) * LANES]
                    inv = lax.rsqrt(jnp.sum(ah * ah, axis=-1, keepdims=True) + EPS) * scale
                    lo_col = which * COL_CHUNK + h * LANES
                    dqkv_ref[:, lo_col:lo_col + LANES] = ah * inv
        return epilogue

    def store_dzs(acc):
        dzs_ref[...] = acc * _sigmoid(acc)

    def store_gates(j):
        def epilogue(acc):
            gates_ref[:, j * COL_CHUNK:(j + 1) * COL_CHUNK] = _sigmoid(acc)
        return epilogue

    for which in range(3):
        store_dn(which)(main_chunk(1 + which)())
    store_dzs(main_chunk(4)())
    for j in range(2 * D_MODEL // COL_CHUNK):
        store_gates(j)(main_chunk(5 + j)())
    store_v_t(_dot_nt(wv_t_ref[...], hn))
    store_q_t(_dot_nt(wq_t_ref[...], hn))
    store_k(main_chunk(0)())


def _inproj(x2d, nw, weights, wsmall, bias, carry0, convw, halo0, *, tm, tiles_per_batch, n_pad):
    m = x2d.shape[0]
    row = lambda w: pl.BlockSpec((tm, w), lambda i: (i, 0))
    col = lambda h: pl.BlockSpec((h, tm), lambda i: (0, i))
    return pl.pallas_call(
        functools.partial(_inproj_kernel, tm=tm, tiles_per_batch=tiles_per_batch, n_pad=n_pad),
        grid=(m // tm,),
        in_specs=[row(D_MODEL), _const_spec((1, D_MODEL))] + [_const_spec(w.shape) for w in weights] + [
                  _const_spec((D_MODEL, LANES)), _const_spec((SUBLANES, LANES)),
                  _const_spec((1, LANES)), _const_spec((CONV_WIDTH, 3 * DN_WIDTH)),
                  _const_spec((SUBLANES, 3 * DN_WIDTH))],
        out_specs=[col(FOX_WIDTH), row(FOX_WIDTH), col(FOX_HEADS * V_ROWS), row(LANES),
                   row(3 * DN_WIDTH), row(DN_WIDTH), row(2 * D_MODEL),
                   row(LANES),
                   pl.BlockSpec((None, SROW, tm),
                                lambda i: (i // tiles_per_batch, 0, i % tiles_per_batch)),
                   pl.BlockSpec((SUBLANES, 3 * DN_WIDTH), lambda i: (0, 0))],
        out_shape=[jax.ShapeDtypeStruct((FOX_WIDTH, m), BF16),
                   jax.ShapeDtypeStruct((m, FOX_WIDTH), BF16),
                   jax.ShapeDtypeStruct((FOX_HEADS * V_ROWS, m), BF16),
                   jax.ShapeDtypeStruct((m, LANES), BF16),
                   jax.ShapeDtypeStruct((m, 3 * DN_WIDTH), F32),
                   jax.ShapeDtypeStruct((m, DN_WIDTH), F32),
                   jax.ShapeDtypeStruct((m, 2 * D_MODEL), F32),
                   jax.ShapeDtypeStruct((m, LANES), F32),
                   jax.ShapeDtypeStruct((m // (tm * tiles_per_batch), SROW,
                                         tm * tiles_per_batch), F32),
                   jax.ShapeDtypeStruct((SUBLANES, 3 * DN_WIDTH), F32)],
        scratch_shapes=[pltpu.VMEM((1, LANES), F32), pltpu.VMEM((SUBLANES, 3 * DN_WIDTH), F32),
                        pltpu.VMEM((3, tm + SUBLANES, COL_CHUNK), F32)],
        compiler_params=pltpu.CompilerParams(dimension_semantics=("arbitrary",),
                                             vmem_limit_bytes=VMEM_LIMIT),
        name="inproj",
    )(x2d, nw, *weights, wsmall, bias, carry0, convw, halo0)


def _fox_kernel(q_t_ref, k_ref, v_t_ref, kb_ref, kp_ref, v_tp_ref, kbp_ref,
                o_ref, m_ref, l_ref, acc_ref, t_ref, tmax_ref, *, tq, tk, pairs):
    pg = pl.program_id(1)
    i = pl.program_id(2)
    heads = range(2 * pairs)
    sub = lax.broadcasted_iota(jnp.int32, (LANES, 1), 0)
    rhs = []
    for hh in heads:
        pp, h = divmod(hh, 2)
        q_t = q_t_ref[pp * LANES:(pp + 1) * LANES, :]
        mine = sub < FOX_HEAD_DIM if h == 0 else sub >= FOX_HEAD_DIM
        head = 2 * (pg * pairs + pp) + h
        sel = jnp.where(jnp.logical_and(sub < 3 * FOX_HEADS, (sub % FOX_HEADS) == head),
                        1.0, 0.0).astype(BF16) * jnp.ones((1, tq), BF16)
        rhs.append(jnp.concatenate([jnp.where(mine, q_t, jnp.zeros_like(q_t)), sel], axis=0))

    def scores(kt, kbt, q_lo=0):
        lhs = [jnp.concatenate([kt[:, pp * LANES:(pp + 1) * LANES], kbt], axis=1)
               for pp in range(pairs)]
        return [_dot(lhs[hh // 2], rhs[hh][:, q_lo:]) for hh in heads]

    def column_max(ts):
        return [jnp.max(ts[hh], axis=0, keepdims=True) for hh in heads]

    def scores_into(slot, off, q_lo=0):
        ts = scores(k_ref[pl.ds(off, tk), :], kb_ref[pl.ds(off, tk), :], q_lo)
        t_max = column_max(ts)
        for hh in heads:
            t_ref[slot, hh, :, :tq - q_lo] = ts[hh]
            tmax_ref[slot, hh, :, :tq - q_lo] = t_max[hh]

    def update(read_t, t_max, v_aug, mask, first, q_lo=0):
        width = tq - q_lo

        def masked_t(hh):
            t = read_t(hh)
            return t if mask is None else jnp.where(mask, t, NEG_INF)

        if t_max is None:
            t_max = [jnp.max(masked_t(hh), axis=0, keepdims=True) for hh in heads]
        if first:
            m_new = t_max
        else:
            m_prev = [m_ref[hh, :, q_lo:] for hh in heads]
            m_new = [jnp.maximum(m_prev[hh], t_max[hh]) for hh in heads]
        pm = [jnp.exp2(masked_t(hh) - m_new[hh]).astype(BF16) for hh in heads]
        r = [_dot(v_aug[hh * V_ROWS:(hh + 1) * V_ROWS, :], pm[hh]) for hh in heads]
        pv = jnp.concatenate([r[hh][:FOX_HEAD_DIM] for hh in heads], axis=0)
        psum = [r[hh][FOX_HEAD_DIM:FOX_HEAD_DIM + 1] for hh in heads]
        if first:
            for hh in heads:
                l_ref[hh, :, q_lo:] = psum[hh]
            acc_ref[:, q_lo:] = pv
        else:
            alpha = [jnp.exp2(m_prev[hh] - m_new[hh]) for hh in heads]
            for hh in heads:
                l_ref[hh, :, q_lo:] = alpha[hh] * l_ref[hh, :, q_lo:] + psum[hh]
            alpha_rows = jnp.concatenate(
                [jnp.broadcast_to(alpha[hh], (FOX_HEAD_DIM, width)) for hh in heads], axis=0)
            acc_ref[:, q_lo:] = alpha_rows * acc_ref[:, q_lo:] + pv
        for hh in heads:
            m_ref[hh, :, q_lo:] = m_new[hh]

    ts_p = scores(kp_ref[...], kbp_ref[...])
    update(lambda hh: ts_p[hh], column_max(ts_p), v_tp_ref[...], None, True)

    def v_at(off):
        return v_t_ref[:, pl.ds(off, tk)]

    def slot(n, q_lo=0):
        return lambda hh: t_ref[n, hh, :, :tq - q_lo]

    def slot_max(n):
        return [tmax_ref[n, hh] for hh in heads]

    pairs_per_q = tq // (2 * tk)
    scores_into(0, 0)

    def body(jj, carry):
        off = pl.multiple_of(jj * 2 * tk, 2 * tk)
        scores_into(1, off + tk)
        update(slot(0), slot_max(0), v_at(off), None, False)
        scores_into(0, off + 2 * tk)
        update(slot(1), slot_max(1), v_at(off + tk), None, False)
        return carry

    lax.fori_loop(0, i * pairs_per_q, body, 0)

    def causal(q_lo):
        rr = lax.broadcasted_iota(jnp.int32, (tk, tq - q_lo), 0)
        cc = lax.broadcasted_iota(jnp.int32, (tk, tq - q_lo), 1)
        return rr <= cc

    for dp in range(pairs_per_q):
        off = pl.multiple_of(i * tq + dp * 2 * tk, 2 * tk)
        q_even, q_odd = dp * 2 * tk, (dp * 2 + 1) * tk
        scores_into(1, off + tk, q_odd)
        update(slot(0, q_even), None, v_at(off), causal(q_even), False, q_even)
        if dp + 1 < pairs_per_q:
            scores_into(0, off + 2 * tk, q_even + 2 * tk)
        update(slot(1, q_odd), None, v_at(off + tk), causal(q_odd), False, q_odd)

    l_rows = jnp.concatenate([jnp.broadcast_to(l_ref[hh], (FOX_HEAD_DIM, tq)) for hh in heads], axis=0)
    o_ref[...] = (acc_ref[...] / l_rows).astype(BF16)


def _fox(q_t, k, v_t, kbias, k_p, v_tp, kbias_p, *, tq, tk, pairs):
    b, s, _ = k.shape
    ngroups = FOX_HEADS // (2 * pairs)
    nq = s // tq
    wl = pairs * LANES
    return pl.pallas_call(
        functools.partial(_fox_kernel, tq=tq, tk=tk, pairs=pairs),
        grid=(b, ngroups, nq),
        in_specs=[
            pl.BlockSpec((wl, tq), lambda bi, pg, i: (pg, bi * nq + i)),
            pl.BlockSpec((None, s, wl), lambda bi, pg, i: (bi, 0, pg)),
            pl.BlockSpec((2 * pairs * V_ROWS, s), lambda bi, pg, i: (pg, bi)),
            pl.BlockSpec((None, s, LANES), lambda bi, pg, i: (bi, 0, 0)),
            pl.BlockSpec((N_META, wl), lambda bi, pg, i: (0, pg)),
            pl.BlockSpec((2 * pairs * V_ROWS, N_META), lambda bi, pg, i: (pg, 0)),
            pl.BlockSpec((N_META, LANES), lambda bi, pg, i: (0, 0)),
        ],
        out_specs=pl.BlockSpec((wl, tq), lambda bi, pg, i: (pg, bi * nq + i)),
        out_shape=jax.ShapeDtypeStruct((FOX_WIDTH, b * s), BF16),
        scratch_shapes=[pltpu.VMEM((2 * pairs, 1, tq), F32), pltpu.VMEM((2 * pairs, 1, tq), F32),
                        pltpu.VMEM((wl, tq), F32), pltpu.VMEM((2, 2 * pairs, tk, tq), F32),
                        pltpu.VMEM((2, 2 * pairs, 1, tq), F32)],
        compiler_params=pltpu.CompilerParams(
            dimension_semantics=("arbitrary", "arbitrary", "arbitrary"),
            vmem_limit_bytes=VMEM_LIMIT),
        name="fox_attention",
    )(q_t, k, v_t, kbias, k_p, v_tp, kbias_p)


def _dn_kernel(qkv_ref, scol_ref, srow_ref, dzs_ref, onw_ref, s0_ref, *refs, tt, n_cast):
    cast_in, (odn_ref, sfin_ref), cast_out = refs[:n_cast], refs[n_cast:n_cast + 2], refs[n_cast + 2:-1]
    state_ref = refs[-1]
    for src, dst in zip(cast_in, cast_out):
        dst[...] = src[...].astype(BF16)
    i = pl.program_id(1)
    nchunk = tt // DN_CHUNK

    @pl.when(i == 0)
    def _():
        for g in range(qkv_ref.shape[0]):
            state_ref[g] = s0_ref[...]

    bs = DN_BLOCK
    heads = range(DN_HEADS)
    groups = range(qkv_ref.shape[0])
    units = [(g, sb, h) for g in groups for sb in range(tt // bs) for h in heads]

    def slab(g, sb, c0):
        return qkv_ref[g, sb * bs:(sb + 1) * bs, c0:c0 + LANES]

    def small_col(g, sb, lane):
        return scol_ref[g, sb * bs:(sb + 1) * bs, lane:lane + 1]

    rr = lax.broadcasted_iota(jnp.int32, (bs, bs), 0)
    cc = lax.broadcasted_iota(jnp.int32, (bs, bs), 1)
    rc_xor = jnp.bitwise_xor(rr, cc)
    same = rc_xor < DN_CHUNK
    tril_m = jnp.logical_and(same, rr >= cc)
    strict_m = jnp.logical_and(same, rr > cc)
    eye = (rr == cc).astype(F32)

    qs = [slab(g, sb, h * LANES) for g, sb, h in units]
    ks = [slab(g, sb, DN_WIDTH + h * LANES) for g, sb, h in units]
    vs = [slab(g, sb, 2 * DN_WIDTH + h * LANES) for g, sb, h in units]
    states = {(g, h): state_ref[g, h] for g in groups for h in heads}
    betas = [small_col(g, sb, BETA_LANE + h) for g, sb, h in units]
    gcs = [small_col(g, sb, G_LANE + h) for g, sb, h in units]
    gls = [gcs[n] + small_col(g, sb, GSUF_LANE + h) - small_col(g, sb, GRAW_LANE + h)
           for n, (g, sb, h) in enumerate(units)]
    decays = [jnp.exp(jnp.where(
        tril_m, gcs[n] - srow_ref[g, G_LANE + h:G_LANE + h + 1, sb * bs:(sb + 1) * bs], NEG_INF))
        for n, (g, sb, h) in enumerate(units)]
    nu = range(len(units))
    egcs = [jnp.exp(g) for g in gcs]
    kbs = [ks[n] * betas[n] for n in nu]
    ks_b = [k.astype(BF16) for k in ks]
    lmats = [jnp.where(strict_m, _dot_nt(kbs[n].astype(BF16), ks_b[n]) * decays[n], 0.0) for n in nu]
    attns = [(_dot_nt(qs[n].astype(BF16), ks_b[n]) * decays[n]).astype(BF16) for n in nu]
    ainvs = [eye - jnp.where(rc_xor == 1, lm, 0.0) for lm in lmats]
    for lvl in range(1, 6):
        blk = 2 ** lvl
        lvl_m = jnp.logical_and(rc_xor >= blk, rc_xor < 2 * blk)
        ainvs_b = [a.astype(BF16) for a in ainvs]
        mids = [_dot(jnp.where(lvl_m, lmats[n], 0.0).astype(BF16), ainvs_b[n]).astype(BF16)
                for n in nu]
        ainvs = [ainvs[n] - _dot(ainvs_b[n], mids[n]) for n in nu]
    sols = [_dot(ainvs[n].astype(BF16),
                 jnp.concatenate([vs[n] * betas[n], kbs[n] * egcs[n]], axis=1).astype(BF16))
            for n in nu]
    uus = [sol[:, :DN_HEAD_DIM] for sol in sols]
    wws = [sol[:, DN_HEAD_DIM:] for sol in sols]
    qds = [qs[n] * egcs[n] for n in nu]
    kds = [(ks[n] * jnp.exp(gls[n] - gcs[n])).astype(BF16) for n in nu]

    vnews = [[] for _ in nu]
    qss = [[] for _ in nu]
    for c in range(nchunk):
        csb, lc = divmod(c, bs // DN_CHUNK)
        lo, hi = lc * DN_CHUNK, (lc + 1) * DN_CHUNK
        cur = [(n, (g, h)) for n, (g, sb, h) in enumerate(units) if sb == csb]
        rs = {n: _dot(jnp.concatenate([wws[n][lo:hi], qds[n][lo:hi]], axis=0).astype(BF16),
                      states[key].astype(BF16)) for n, key in cur}
        vns = {n: uus[n][lo:hi] - rs[n][:DN_CHUNK] for n, _ in cur}
        states.update({key: jnp.exp(gls[n][lo:lo + 1, :]) * states[key]
                       + _dot_tn(kds[n][lo:hi], vns[n].astype(BF16)) for n, key in cur})
        for n, _ in cur:
            qss[n].append(rs[n][DN_CHUNK:])
            vnews[n].append(vns[n])
    for (g, h), state in states.items():
        state_ref[g, h] = state
    for h in heads:
        sfin_ref[h] = states[(0, h)]
    for n, (g, sb, h) in enumerate(units):
        o = (jnp.concatenate(qss[n], axis=0)
             + _dot(attns[n], jnp.concatenate(vnews[n], axis=0).astype(BF16)))
        o = o * lax.rsqrt(jnp.mean(o * o, axis=-1, keepdims=True) + EPS) * onw_ref[...]
        o = o * dzs_ref[g, sb * bs:(sb + 1) * bs, h * LANES:(h + 1) * LANES]
        odn_ref[g, sb * bs:(sb + 1) * bs, h * LANES:(h + 1) * LANES] = o.astype(BF16)


def _cast_row_block(rows, nsteps):
    tile = 2 * SUBLANES
    for blk in range(tile, rows, tile):
        if rows % blk == 0 and rows // blk <= nsteps:
            return blk
    return rows


def _deltanet(qkv, scol, srow, dzs, onw, s0, *, group, tt, cast_weights=()):
    b, s, _ = qkv.shape
    nt = s // tt
    nsteps = (b // group) * nt
    cast_specs = []
    for w in cast_weights:
        blk = _cast_row_block(w.shape[0], nsteps)
        last = w.shape[0] // blk - 1
        cast_specs.append(pl.BlockSpec(
            (blk, w.shape[1]), lambda bi, i, last=last: (jnp.minimum(bi * nt + i, last), 0)))
    return pl.pallas_call(
        functools.partial(_dn_kernel, tt=tt, n_cast=len(cast_weights)),
        grid=(b // group, nt),
        in_specs=[
            pl.BlockSpec((group, tt, 3 * DN_WIDTH), lambda bi, i: (bi, i, 0)),
            pl.BlockSpec((group, tt, LANES), lambda bi, i: (bi, i, 0)),
            pl.BlockSpec((group, SROW, tt), lambda bi, i: (bi, 0, i)),
            pl.BlockSpec((group, tt, DN_WIDTH), lambda bi, i: (bi, i, 0)),
            pl.BlockSpec((1, DN_HEAD_DIM), lambda bi, i: (0, 0)),
            pl.BlockSpec((DN_HEADS, DN_HEAD_DIM, DN_HEAD_DIM), lambda bi, i: (0, 0, 0)),
        ] + cast_specs,
        out_specs=[pl.BlockSpec((group, tt, DN_WIDTH), lambda bi, i: (bi, i, 0)),
                   pl.BlockSpec((DN_HEADS, DN_HEAD_DIM, DN_HEAD_DIM), lambda bi, i: (0, 0, 0))]
        + cast_specs,
        out_shape=[jax.ShapeDtypeStruct((b, s, DN_WIDTH), BF16),
                   jax.ShapeDtypeStruct((DN_HEADS, DN_HEAD_DIM, DN_HEAD_DIM), F32)]
        + [jax.ShapeDtypeStruct(w.shape, BF16) for w in cast_weights],
        scratch_shapes=[pltpu.VMEM((group, DN_HEADS, DN_HEAD_DIM, DN_HEAD_DIM), F32)],
        compiler_params=pltpu.CompilerParams(dimension_semantics=("arbitrary", "arbitrary"),
                                             vmem_limit_bytes=VMEM_LIMIT),
        name="deltanet",
    )(qkv, scol, srow, dzs, onw, s0, *cast_weights)


def _tail_kernel(x_ref, ofox_ref, odn_ref, gates_ref, wbf_ref, wbd_ref, wout_ref, fnw_ref,
                 wg_ref, wu_ref, wd_ref, finw_ref, o_ref, *, ff_bounds):
    a = _dot_tn(ofox_ref[...], wbf_ref[...])
    bb = _dot(odn_ref[...], wbd_ref[...])
    y = gates_ref[:, :D_MODEL] * a + gates_ref[:, D_MODEL:] * bb
    h1 = x_ref[...] + _dot(y.astype(BF16), wout_ref[...])
    n = (h1 * lax.rsqrt(jnp.mean(h1 * h1, axis=-1, keepdims=True) + EPS) * fnw_ref[...]).astype(BF16)
    acc = h1
    for lo, hi in zip(ff_bounds[:-1], ff_bounds[1:]):
        gt = _dot(n, wg_ref[:, lo:hi])
        up = _dot(n, wu_ref[:, lo:hi])
        act = (gt * _sigmoid(gt) * up).astype(BF16)
        acc = acc + _dot(act, wd_ref[lo:hi, :])
    o_ref[...] = acc * lax.rsqrt(jnp.mean(acc * acc, axis=-1, keepdims=True) + EPS) * finw_ref[...]


def _tail(x2d, ofox, odn, gates, wbf, wbd, wout, fnw, wg, wu, wd, finw, *, tm, ff_bounds):
    m = x2d.shape[0]
    row = lambda w: pl.BlockSpec((tm, w), lambda i: (i, 0))
    return pl.pallas_call(
        functools.partial(_tail_kernel, ff_bounds=ff_bounds),
        grid=(m // tm,),
        in_specs=[row(D_MODEL), pl.BlockSpec((FOX_WIDTH, tm), lambda i: (0, i)), row(DN_WIDTH),
                  row(2 * D_MODEL),
                  _const_spec((FOX_WIDTH, D_MODEL)), _const_spec((DN_WIDTH, D_MODEL)),
                  _const_spec((D_MODEL, D_MODEL)), _const_spec((1, D_MODEL)),
                  _const_spec((D_MODEL, D_FF)), _const_spec((D_MODEL, D_FF)),
                  _const_spec((D_FF, D_MODEL)), _const_spec((1, D_MODEL))],
        out_specs=row(D_MODEL),
        out_shape=jax.ShapeDtypeStruct((m, D_MODEL), F32),
        compiler_params=pltpu.CompilerParams(dimension_semantics=("arbitrary",),
                                             vmem_limit_bytes=VMEM_LIMIT),
        name="merge_ffn",
    )(x2d, ofox, odn, gates, wbf, wbd, wout, fnw, wg, wu, wd, finw)


def _pick_tile(n, pref):
    t = min(pref, n)
    while n % t:
        t //= 2
    return t


def kernel(x, meta_tokens, mix_norm_w, w_in, fox_forget_bias, dn_conv_w, dn_a_log, dn_dt_bias,
           dn_out_norm_w, w_branch_fox, w_branch_dn, w_out, ffn_norm_w, w_ffn_gate, w_ffn_up,
           w_ffn_down, final_norm_w):
    b, s, _ = x.shape
    assert mix_norm_w.shape[0] == 1, "single layer only"
    assert s % PREFIX == 0
    m = b * s

    wi = w_in[0]
    o_small0 = 3 * FOX_WIDTH
    o_dn = o_small0 + FOX_HEADS
    o_small1 = o_dn + 3 * DN_WIDTH
    o_rest = o_small1 + 2 * DN_HEADS
    weights = (wi[:, :FOX_WIDTH].T.astype(BF16), wi[:, 2 * FOX_WIDTH:o_small0].T.astype(BF16),
               wi[:, FOX_WIDTH:2 * FOX_WIDTH].astype(BF16), wi[:, o_dn:o_small1].astype(BF16),
               wi[:, o_rest:].astype(BF16))
    w_alogit = wi[:, o_small1 + DN_HEADS:o_rest]
    wsmall = jnp.concatenate([wi[:, o_small0:o_dn], wi[:, o_small1:o_small1 + DN_HEADS],
                              w_alogit, w_alogit, w_alogit], axis=1)
    wsmall = jnp.pad(wsmall, ((0, 0), (0, LANES - N_SMALL))).astype(BF16)
    bias = jnp.zeros((SUBLANES, LANES), F32)
    bias = bias.at[0, LOGF_LANE:LOGF_LANE + FOX_HEADS].set(fox_forget_bias[0].astype(F32))
    bias = bias.at[0, G_LANE:N_SMALL].set(jnp.tile(dn_dt_bias[0].astype(F32), 3))
    bias = bias.at[1, G_LANE:N_SMALL].set(jnp.tile(dn_a_log[0].astype(F32), 3))
    nw = mix_norm_w[0].reshape(1, D_MODEL).astype(F32)

    x_p = jnp.concatenate([jnp.zeros((N_PAD, D_MODEL), F32), meta_tokens.astype(F32)], axis=0)
    convw = dn_conv_w[0].astype(F32)
    _, k_p, v_tp, kbias_p, qkv_p, dzs_p, _, scol_p, srow_p, conv_tail_p = _inproj(
        x_p, nw, weights, wsmall, bias, jnp.zeros((1, LANES), F32), convw,
        jnp.zeros((SUBLANES, 3 * DN_WIDTH), F32), tm=PREFIX, tiles_per_batch=1, n_pad=N_PAD)

    tm = _pick_tile(s, ROW_TILE)
    q_t, k, v_t, kbias, qkv, dzs, gates, scol, srow3, _ = _inproj(
        x.reshape(m, D_MODEL), nw, weights, wsmall, bias, scol_p[PREFIX - 1:PREFIX, :], convw,
        conv_tail_p, tm=tm, tiles_per_batch=s // tm, n_pad=0)

    scol3 = scol.reshape(b, s, LANES)
    tq = _pick_tile(s, ROW_TILE)
    ofox_t = _fox(q_t, k.reshape(b, s, FOX_WIDTH), v_t, kbias.reshape(b, s, LANES),
                  k_p[N_PAD:], v_tp[:, N_PAD:], kbias_p[N_PAD:], tq=tq, tk=min(MXU_DIM, tq // 2),
                  pairs=FOX_PAIRS_PER_STEP)

    onw = dn_out_norm_w[0].reshape(1, DN_HEAD_DIM).astype(F32)
    _, s_prefix = _deltanet(
        qkv_p[None], scol_p[None], srow_p, dzs_p[None], onw,
        jnp.zeros((DN_HEADS, DN_HEAD_DIM, DN_HEAD_DIM), F32), group=1, tt=PREFIX)
    group = _pick_tile(b, DN_UNITS // DN_HEADS)
    odn, _, wbf, wbd, wout, wg, wu, wd = _deltanet(
        qkv.reshape(b, s, 3 * DN_WIDTH), scol3, srow3, dzs.reshape(b, s, DN_WIDTH), onw, s_prefix,
        group=group, tt=_pick_tile(s, DN_UNITS // (DN_HEADS * group) * DN_BLOCK),
        cast_weights=(w_branch_fox[0], w_branch_dn[0], w_out[0], w_ffn_gate[0], w_ffn_up[0],
                      w_ffn_down[0]))

    out = _tail(
        x.reshape(m, D_MODEL), ofox_t, odn.reshape(m, DN_WIDTH), gates, wbf, wbd, wout,
        ffn_norm_w[0].reshape(1, D_MODEL).astype(F32), wg, wu, wd,
        final_norm_w.reshape(1, D_MODEL).astype(F32),
        tm=_pick_tile(m, ROW_TILE), ff_bounds=FF_BOUNDS)
    return out.reshape(b, s, D_MODEL)
```

```python
import functools

import jax
import jax.numpy as jnp
from jax import lax
from jax.experimental import pallas as pl
from jax.experimental.pallas import tpu as pltpu

F32 = jnp.float32
BF16 = jnp.bfloat16

LANES = 128
SUBLANES = 8
MXU_DIM = 256
ROW_TILE = 512
VMEM_LIMIT = 56 * 1024 * 1024

D_MODEL = 1024
N_META = 16
PREFIX = 128
N_PAD = PREFIX - N_META
FOX_HEADS = 8
FOX_HEAD_DIM = 64
FOX_WIDTH = FOX_HEADS * FOX_HEAD_DIM
V_ROWS = FOX_HEAD_DIM + SUBLANES
FOX_PAIRS_PER_STEP = 4
DN_HEADS = 4
DN_HEAD_DIM = 128
DN_WIDTH = DN_HEADS * DN_HEAD_DIM
DN_CHUNK = 64
DN_BLOCK = 2 * DN_CHUNK
DN_UNITS = 16
CONV_WIDTH = 4
D_FF = 2816
FF_BOUNDS = (0, (D_FF // MXU_DIM + 1) // 2 * MXU_DIM, D_FF)
EPS = 1e-6
NEG_INF = -1e30
LOG2E = 1.4426950408889634

COL_CHUNK = 512
LOGF_LANE = 0
BETA_LANE = FOX_HEADS
G_LANE = BETA_LANE + DN_HEADS
GSUF_LANE = G_LANE + DN_HEADS
GRAW_LANE = GSUF_LANE + DN_HEADS
N_SMALL = GRAW_LANE + DN_HEADS
SROW = -(-N_SMALL // SUBLANES) * SUBLANES


def _const_spec(shape):
    nd = len(shape)
    return pl.BlockSpec(shape, lambda *_: (0,) * nd, pipeline_mode=pl.Buffered(1))


def _sigmoid(x):
    return 0.5 * jnp.tanh(0.5 * x) + 0.5


def _dot(a, b, **kw):
    return jnp.dot(a, b, preferred_element_type=F32, **kw)


def _dot_nt(a, b, **kw):
    return lax.dot_general(a, b, (((1,), (1,)), ((), ())), preferred_element_type=F32, **kw)


def _dot_tn(a, b, **kw):
    return lax.dot_general(a, b, (((0,), (0,)), ((), ())), preferred_element_type=F32, **kw)


def _inproj_kernel(x_ref, nw_ref, wq_t_ref, wv_t_ref, wk_ref, wdn_ref, wrest_ref, wsmall_ref, bias_ref,
                   carry0_ref,
                   convw_ref, halo0_ref,
                   q_t_ref, k_ref, v_t_ref, kbias_ref, dqkv_ref, dzs_ref, gates_ref, scol_ref,
                   srow_ref, tail_ref, carry_ref, halo_ref, conv_ref, *, tm, tiles_per_batch, n_pad):
    i = pl.program_id(0)

    @pl.when(i % tiles_per_batch == 0)
    def _():
        carry_ref[...] = carry0_ref[...]
        halo_ref[...] = halo0_ref[...]

    x = x_ref[...]
    ms = jnp.mean(x * x, axis=-1, keepdims=True)
    hn = (x * lax.rsqrt(ms + EPS) * nw_ref[...]).astype(BF16)

    z = _dot(hn, wsmall_ref[...]) + bias_ref[0:1, :]
    lane = lax.broadcasted_iota(jnp.int32, (tm, LANES), 1)
    row = lax.broadcasted_iota(jnp.int32, (tm, LANES), 0)
    e = jnp.exp(-jnp.abs(z))
    l1p = jnp.log1p(e)
    logf = jnp.minimum(z, 0.0) - l1p
    softplus = jnp.maximum(z, 0.0) + l1p
    sig = jnp.where(z >= 0.0, 1.0, e) / (1.0 + e)
    g = -jnp.exp(bias_ref[1:2, :]) * softplus
    if n_pad:
        vm = (row + (i % tiles_per_batch) * tm >= n_pad).astype(F32)
        sig = sig * vm
        g = g * vm
    val = jnp.where(lane < BETA_LANE, logf,
                    jnp.where(lane < G_LANE, sig, jnp.where(lane < N_SMALL, g, 0.0)))
    in_chunk = row % DN_CHUNK
    is_logf = lane < BETA_LANE
    is_gpre = jnp.logical_and(lane >= G_LANE, lane < GSUF_LANE)
    is_gsuf = jnp.logical_and(lane >= GSUF_LANE, lane < GRAW_LANE)
    scan = val
    sh = 1
    while sh < tm:
        take_up = jnp.logical_and(is_logf, row >= sh)
        if sh < DN_CHUNK:
            take_up = jnp.logical_or(take_up, jnp.logical_and(is_gpre, in_chunk >= sh))
            take_down = jnp.logical_and(is_gsuf, in_chunk + sh < DN_CHUNK)
            below = jnp.where(take_down, pltpu.roll(scan, tm - sh, axis=0), 0.0)
        else:
            below = 0.0
        scan = scan + jnp.where(take_up, pltpu.roll(scan, sh, axis=0), below)
        sh *= 2
    out = scan + jnp.where(is_logf, carry_ref[...], 0.0)
    carry_ref[...] = out[tm - 1:tm, :]
    scol_ref[...] = out
    srow_ref[...] = out.T[:SROW, :]
    nc = jnp.where(is_logf, out * -LOG2E, 0.0)
    hi = nc.astype(BF16).astype(F32)
    mid = (nc - hi).astype(BF16).astype(F32)
    lo = nc - hi - mid
    kbias = hi + pltpu.roll(mid, FOX_HEADS, axis=1) + pltpu.roll(lo, 2 * FOX_HEADS, axis=1)
    kbias_ref[...] = kbias.astype(BF16)

    def main_chunk(c):
        if c == 0:
            return lambda: _dot(hn, wk_ref[...])
        ref, first = (wdn_ref, 1) if c < 4 else (wrest_ref, 4)
        return lambda: _dot(hn, ref[:, (c - first) * COL_CHUNK:(c - first + 1) * COL_CHUNK])

    def store_q_t(acc):
        q_t_ref[...] = (acc * (FOX_HEAD_DIM ** -0.5 * LOG2E)).astype(BF16)

    def store_v_t(acc):
        v_t = acc.astype(BF16)
        ones = jnp.ones((V_ROWS - FOX_HEAD_DIM, tm), BF16)
        v_t_ref[...] = jnp.concatenate(
            [piece for h in range(FOX_HEADS)
             for piece in (v_t[h * FOX_HEAD_DIM:(h + 1) * FOX_HEAD_DIM], ones)], axis=0)

    def store_k(acc):
        k_ref[...] = acc.astype(BF16)

    def store_dn(which):
        def epilogue(acc):
            cols = slice(which * COL_CHUNK, (which + 1) * COL_CHUNK)
            if n_pad:
                acc = acc * vm[:, :1]
            conv_ref[which, 0:SUBLANES, :] = halo_ref[:, cols]
            conv_ref[which, SUBLANES:, :] = acc
            halo_ref[:, cols] = acc[tm - SUBLANES:, :]
            tail_ref[:, cols] = acc[tm - SUBLANES:, :]
            a = convw_ref[CONV_WIDTH - 1:CONV_WIDTH, cols] * acc
            for t in range(CONV_WIDTH - 1):
                start = SUBLANES - (CONV_WIDTH - 1) + t
                a = a + convw_ref[t:t + 1, cols] * conv_ref[which, start:start + tm, :]
            a = a * _sigmoid(a)
            if which == 2:
                dqkv_ref[:, cols] = a
            else:
                scale = DN_HEAD_DIM ** -0.5 if which == 0 else 1.0
                for h in range(DN_HEADS):
                    ah = a[:, h * LANES:(h + 1) * LANES]
                    inv = lax.rsqrt(jnp.sum(ah * ah, axis=-1, keepdims=True) + EPS) * scale
                    lo_col = which * COL_CHUNK + h * LANES
                    dqkv_ref[:, lo_col:lo_col + LANES] = ah * inv
        return epilogue

    def store_dzs(acc):
        dzs_ref[...] = acc * _sigmoid(acc)

    def store_gates(j):
        def epilogue(acc):
            gates_ref[:, j * COL_CHUNK:(j + 1) * COL_CHUNK] = _sigmoid(acc)
        return epilogue

    store_q_t(_dot_nt(wq_t_ref[...], hn))
    store_v_t(_dot_nt(wv_t_ref[...], hn))
    store_k(main_chunk(0)())
    for which in range(3):
        store_dn(which)(main_chunk(1 + which)())
    store_dzs(main_chunk(4)())
    for j in range(2 * D_MODEL // COL_CHUNK):
        store_gates(j)(main_chunk(5 + j)())


def _inproj(x2d, nw, weights, wsmall, bias, carry0, convw, halo0, *, tm, tiles_per_batch, n_pad):
    m = x2d.shape[0]
    row = lambda w: pl.BlockSpec((tm, w), lambda i: (i, 0))
    col = lambda h: pl.BlockSpec((h, tm), lambda i: (0, i))
    return pl.pallas_call(
        functools.partial(_inproj_kernel, tm=tm, tiles_per_batch=tiles_per_batch, n_pad=n_pad),
        grid=(m // tm,),
        in_specs=[row(D_MODEL), _const_spec((1, D_MODEL))] + [_const_spec(w.shape) for w in weights] + [
                  _const_spec((D_MODEL, LANES)), _const_spec((SUBLANES, LANES)),
                  _const_spec((1, LANES)), _const_spec((CONV_WIDTH, 3 * DN_WIDTH)),
                  _const_spec((SUBLANES, 3 * DN_WIDTH))],
        out_specs=[col(FOX_WIDTH), row(FOX_WIDTH), col(FOX_HEADS * V_ROWS), row(LANES),
                   row(3 * DN_WIDTH), row(DN_WIDTH), row(2 * D_MODEL),
                   row(LANES),
                   pl.BlockSpec((None, SROW, tm),
                                lambda i: (i // tiles_per_batch, 0, i % tiles_per_batch)),
                   pl.BlockSpec((SUBLANES, 3 * DN_WIDTH), lambda i: (0, 0))],
        out_shape=[jax.ShapeDtypeStruct((FOX_WIDTH, m), BF16),
                   jax.ShapeDtypeStruct((m, FOX_WIDTH), BF16),
                   jax.ShapeDtypeStruct((FOX_HEADS * V_ROWS, m), BF16),
                   jax.ShapeDtypeStruct((m, LANES), BF16),
                   jax.ShapeDtypeStruct((m, 3 * DN_WIDTH), F32),
                   jax.ShapeDtypeStruct((m, DN_WIDTH), F32),
                   jax.ShapeDtypeStruct((m, 2 * D_MODEL), F32),
                   jax.ShapeDtypeStruct((m, LANES), F32),
                   jax.ShapeDtypeStruct((m // (tm * tiles_per_batch), SROW,
                                         tm * tiles_per_batch), F32),
                   jax.ShapeDtypeStruct((SUBLANES, 3 * DN_WIDTH), F32)],
        scratch_shapes=[pltpu.VMEM((1, LANES), F32), pltpu.VMEM((SUBLANES, 3 * DN_WIDTH), F32),
                        pltpu.VMEM((3, tm + SUBLANES, COL_CHUNK), F32)],
        compiler_params=pltpu.CompilerParams(dimension_semantics=("arbitrary",),
                                             vmem_limit_bytes=VMEM_LIMIT),
        name="inproj",
    )(x2d, nw, *weights, wsmall, bias, carry0, convw, halo0)


def _fox_kernel(q_t_ref, k_ref, v_t_ref, kb_ref, kp_ref, v_tp_ref, kbp_ref,
                o_ref, m_ref, l_ref, acc_ref, t_ref, tmax_ref, *, tq, tk, pairs):
    pg = pl.program_id(1)
    i = pl.program_id(2)
    heads = range(2 * pairs)
    sub = lax.broadcasted_iota(jnp.int32, (LANES, 1), 0)
    rhs = []
    for hh in heads:
        pp, h = divmod(hh, 2)
        q_t = q_t_ref[pp * LANES:(pp + 1) * LANES, :]
        mine = sub < FOX_HEAD_DIM if h == 0 else sub >= FOX_HEAD_DIM
        head = 2 * (pg * pairs + pp) + h
        sel = jnp.where(jnp.logical_and(sub < 3 * FOX_HEADS, (sub % FOX_HEADS) == head),
                        1.0, 0.0).astype(BF16) * jnp.ones((1, tq), BF16)
        rhs.append(jnp.concatenate([jnp.where(mine, q_t, jnp.zeros_like(q_t)), sel], axis=0))

    def scores(kt, kbt, q_lo=0):
        lhs = [jnp.concatenate([kt[:, pp * LANES:(pp + 1) * LANES], kbt], axis=1)
               for pp in range(pairs)]
        return [_dot(lhs[hh // 2], rhs[hh][:, q_lo:]) for hh in heads]

    def column_max(ts):
        return [jnp.max(ts[hh], axis=0, keepdims=True) for hh in heads]

    def scores_into(slot, off, q_lo=0):
        ts = scores(k_ref[pl.ds(off, tk), :], kb_ref[pl.ds(off, tk), :], q_lo)
        t_max = column_max(ts)
        for hh in heads:
            t_ref[slot, hh, :, :tq - q_lo] = ts[hh]
            tmax_ref[slot, hh, :, :tq - q_lo] = t_max[hh]

    def update(read_t, t_max, v_aug, mask, first, q_lo=0):
        width = tq - q_lo

        def masked_t(hh):
            t = read_t(hh)
            return t if mask is None else jnp.where(mask, t, NEG_INF)

        if t_max is None:
            t_max = [jnp.max(masked_t(hh), axis=0, keepdims=True) for hh in heads]
        if first:
            m_new = t_max
        else:
            m_prev = [m_ref[hh, :, q_lo:] for hh in heads]
            m_new = [jnp.maximum(m_prev[hh], t_max[hh]) for hh in heads]
        pm = [jnp.exp2(masked_t(hh) - m_new[hh]).astype(BF16) for hh in heads]
        r = [_dot(v_aug[hh * V_ROWS:(hh + 1) * V_ROWS, :], pm[hh]) for hh in heads]
        pv = jnp.concatenate([r[hh][:FOX_HEAD_DIM] for hh in heads], axis=0)
        psum = [r[hh][FOX_HEAD_DIM:FOX_HEAD_DIM + 1] for hh in heads]
        if first:
            for hh in heads:
                l_ref[hh, :, q_lo:] = psum[hh]
            acc_ref[:, q_lo:] = pv
        else:
            alpha = [jnp.exp2(m_prev[hh] - m_new[hh]) for hh in heads]
            for hh in heads:
                l_ref[hh, :, q_lo:] = alpha[hh] * l_ref[hh, :, q_lo:] + psum[hh]
            alpha_rows = jnp.concatenate(
                [jnp.broadcast_to(alpha[hh], (FOX_HEAD_DIM, width)) for hh in heads], axis=0)
            acc_ref[:, q_lo:] = alpha_rows * acc_ref[:, q_lo:] + pv
        for hh in heads:
            m_ref[hh, :, q_lo:] = m_new[hh]

    ts_p = scores(kp_ref[...], kbp_ref[...])
    update(lambda hh: ts_p[hh], column_max(ts_p), v_tp_ref[...], None, True)

    def v_at(off):
        return v_t_ref[:, pl.ds(off, tk)]

    def slot(n, q_lo=0):
        return lambda hh: t_ref[n, hh, :, :tq - q_lo]

    def slot_max(n):
        return [tmax_ref[n, hh] for hh in heads]

    pairs_per_q = tq // (2 * tk)
    scores_into(0, 0)

    def body(jj, carry):
        off = pl.multiple_of(jj * 2 * tk, 2 * tk)
        scores_into(1, off + tk)
        update(slot(0), slot_max(0), v_at(off), None, False)
        scores_into(0, off + 2 * tk)
        update(slot(1), slot_max(1), v_at(off + tk), None, False)
        return carry

    lax.fori_loop(0, i * pairs_per_q, body, 0)

    def causal(q_lo):
        rr = lax.broadcasted_iota(jnp.int32, (tk, tq - q_lo), 0)
        cc = lax.broadcasted_iota(jnp.int32, (tk, tq - q_lo), 1)
        return rr <= cc

    for dp in range(pairs_per_q):
        off = pl.multiple_of(i * tq + dp * 2 * tk, 2 * tk)
        q_even, q_odd = dp * 2 * tk, (dp * 2 + 1) * tk
        scores_into(1, off + tk, q_odd)
        update(slot(0, q_even), None, v_at(off), causal(q_even), False, q_even)
        if dp + 1 < pairs_per_q:
            scores_into(0, off + 2 * tk, q_even + 2 * tk)
        update(slot(1, q_odd), None, v_at(off + tk), causal(q_odd), False, q_odd)

    l_rows = jnp.concatenate([jnp.broadcast_to(l_ref[hh], (FOX_HEAD_DIM, tq)) for hh in heads], axis=0)
    o_ref[...] = (acc_ref[...] / l_rows).astype(BF16)


def _fox(q_t, k, v_t, kbias, k_p, v_tp, kbias_p, *, tq, tk, pairs):
    b, s, _ = k.shape
    ngroups = FOX_HEADS // (2 * pairs)
    nq = s // tq
    wl = pairs * LANES
    return pl.pallas_call(
        functools.partial(_fox_kernel, tq=tq, tk=tk, pairs=pairs),
        grid=(b, ngroups, nq),
        in_specs=[
            pl.BlockSpec((wl, tq), lambda bi, pg, i: (pg, bi * nq + i)),
            pl.BlockSpec((None, s, wl), lambda bi, pg, i: (bi, 0, pg)),
            pl.BlockSpec((2 * pairs * V_ROWS, s), lambda bi, pg, i: (pg, bi)),
            pl.BlockSpec((None, s, LANES), lambda bi, pg, i: (bi, 0, 0)),
            pl.BlockSpec((N_META, wl), lambda bi, pg, i: (0, pg)),
            pl.BlockSpec((2 * pairs * V_ROWS, N_META), lambda bi, pg, i: (pg, 0)),
            pl.BlockSpec((N_META, LANES), lambda bi, pg, i: (0, 0)),
        ],
        out_specs=pl.BlockSpec((wl, tq), lambda bi, pg, i: (pg, bi * nq + i)),
        out_shape=jax.ShapeDtypeStruct((FOX_WIDTH, b * s), BF16),
        scratch_shapes=[pltpu.VMEM((2 * pairs, 1, tq), F32), pltpu.VMEM((2 * pairs, 1, tq), F32),
                        pltpu.VMEM((wl, tq), F32), pltpu.VMEM((2, 2 * pairs, tk, tq), F32),
                        pltpu.VMEM((2, 2 * pairs, 1, tq), F32)],
        compiler_params=pltpu.CompilerParams(
            dimension_semantics=("arbitrary", "arbitrary", "arbitrary"),
            vmem_limit_bytes=VMEM_LIMIT),
        name="fox_attention",
    )(q_t, k, v_t, kbias, k_p, v_tp, kbias_p)


def _dn_kernel(qkv_ref, scol_ref, srow_ref, dzs_ref, onw_ref, s0_ref, *refs, tt, n_cast):
    cast_in, (odn_ref, sfin_ref), cast_out = refs[:n_cast], refs[n_cast:n_cast + 2], refs[n_cast + 2:-1]
    state_ref = refs[-1]
    for src, dst in zip(cast_in, cast_out):
        dst[...] = src[...].astype(BF16)
    i = pl.program_id(1)
    nchunk = tt // DN_CHUNK

    @pl.when(i == 0)
    def _():
        for g in range(qkv_ref.shape[0]):
            state_ref[g] = s0_ref[...]

    bs = DN_BLOCK
    heads = range(DN_HEADS)
    groups = range(qkv_ref.shape[0])
    units = [(g, sb, h) for g in groups for sb in range(tt // bs) for h in heads]

    def slab(g, sb, c0):
        return qkv_ref[g, sb * bs:(sb + 1) * bs, c0:c0 + LANES]

    def small_col(g, sb, lane):
        return scol_ref[g, sb * bs:(sb + 1) * bs, lane:lane + 1]

    rr = lax.broadcasted_iota(jnp.int32, (bs, bs), 0)
    cc = lax.broadcasted_iota(jnp.int32, (bs, bs), 1)
    rc_xor = jnp.bitwise_xor(rr, cc)
    same = rc_xor < DN_CHUNK
    tril_m = jnp.logical_and(same, rr >= cc)
    strict_m = jnp.logical_and(same, rr > cc)
    eye = (rr == cc).astype(F32)

    qs = [slab(g, sb, h * LANES) for g, sb, h in units]
    ks = [slab(g, sb, DN_WIDTH + h * LANES) for g, sb, h in units]
    vs = [slab(g, sb, 2 * DN_WIDTH + h * LANES) for g, sb, h in units]
    states = {(g, h): state_ref[g, h] for g in groups for h in heads}
    betas = [small_col(g, sb, BETA_LANE + h) for g, sb, h in units]
    gcs = [small_col(g, sb, G_LANE + h) for g, sb, h in units]
    gls = [gcs[n] + small_col(g, sb, GSUF_LANE + h) - small_col(g, sb, GRAW_LANE + h)
           for n, (g, sb, h) in enumerate(units)]
    decays = [jnp.exp(jnp.where(
        tril_m, gcs[n] - srow_ref[g, G_LANE + h:G_LANE + h + 1, sb * bs:(sb + 1) * bs], NEG_INF))
        for n, (g, sb, h) in enumerate(units)]
    nu = range(len(units))
    egcs = [jnp.exp(g) for g in gcs]
    kbs = [ks[n] * betas[n] for n in nu]
    ks_b = [k.astype(BF16) for k in ks]
    lmats = [jnp.where(strict_m, _dot_nt(kbs[n].astype(BF16), ks_b[n]) * decays[n], 0.0) for n in nu]
    attns = [(_dot_nt(qs[n].astype(BF16), ks_b[n]) * decays[n]).astype(BF16) for n in nu]
    ainvs = [eye - jnp.where(rc_xor == 1, lm, 0.0) for lm in lmats]
    for lvl in range(1, 6):
        blk = 2 ** lvl
        lvl_m = jnp.logical_and(rc_xor >= blk, rc_xor < 2 * blk)
        ainvs_b = [a.astype(BF16) for a in ainvs]
        mids = [_dot(jnp.where(lvl_m, lmats[n], 0.0).astype(BF16), ainvs_b[n]).astype(BF16)
                for n in nu]
        ainvs = [ainvs[n] - _dot(ainvs_b[n], mids[n]) for n in nu]
    sols = [_dot(ainvs[n].astype(BF16),
                 jnp.concatenate([vs[n] * betas[n], kbs[n] * egcs[n]], axis=1).astype(BF16))
            for n in nu]
    uus = [sol[:, :DN_HEAD_DIM] for sol in sols]
    wws = [sol[:, DN_HEAD_DIM:] for sol in sols]
    qds = [qs[n] * egcs[n] for n in nu]
    kds = [(ks[n] * jnp.exp(gls[n] - gcs[n])).astype(BF16) for n in nu]

    vnews = [[] for _ in nu]
    qss = [[] for _ in nu]
    for c in range(nchunk):
        csb, lc = divmod(c, bs // DN_CHUNK)
        lo, hi = lc * DN_CHUNK, (lc + 1) * DN_CHUNK
        cur = [(n, (g, h)) for n, (g, sb, h) in enumerate(units) if sb == csb]
        rs = {n: _dot(jnp.concatenate([wws[n][lo:hi], qds[n][lo:hi]], axis=0).astype(BF16),
                      states[key].astype(BF16)) for n, key in cur}
        vns = {n: uus[n][lo:hi] - rs[n][:DN_CHUNK] for n, _ in cur}
        states.update({key: jnp.exp(gls[n][lo:lo + 1, :]) * states[key]
                       + _dot_tn(kds[n][lo:hi], vns[n].astype(BF16)) for n, key in cur})
        for n, _ in cur:
            qss[n].append(rs[n][DN_CHUNK:])
            vnews[n].append(vns[n])
    for (g, h), state in states.items():
        state_ref[g, h] = state
    for h in heads:
        sfin_ref[h] = states[(0, h)]
    for n, (g, sb, h) in enumerate(units):
        o = (jnp.concatenate(qss[n], axis=0)
             + _dot(attns[n], jnp.concatenate(vnews[n], axis=0).astype(BF16)))
        o = o * lax.rsqrt(jnp.mean(o * o, axis=-1, keepdims=True) + EPS) * onw_ref[...]
        o = o * dzs_ref[g, sb * bs:(sb + 1) * bs, h * LANES:(h + 1) * LANES]
        odn_ref[g, sb * bs:(sb + 1) * bs, h * LANES:(h + 1) * LANES] = o.astype(BF16)


def _cast_row_block(rows, nsteps):
    tile = 2 * SUBLANES
    for blk in range(tile, rows, tile):
        if rows % blk == 0 and rows // blk <= nsteps:
            return blk
    return rows


def _deltanet(qkv, scol, srow, dzs, onw, s0, *, group, tt, cast_weights=()):
    b, s, _ = qkv.shape
    nt = s // tt
    nsteps = (b // group) * nt
    cast_specs = []
    for w in cast_weights:
        blk = _cast_row_block(w.shape[0], nsteps)
        last = w.shape[0] // blk - 1
        cast_specs.append(pl.BlockSpec(
            (blk, w.shape[1]), lambda bi, i, last=last: (jnp.minimum(bi * nt + i, last), 0)))
    return pl.pallas_call(
        functools.partial(_dn_kernel, tt=tt, n_cast=len(cast_weights)),
        grid=(b // group, nt),
        in_specs=[
            pl.BlockSpec((group, tt, 3 * DN_WIDTH), lambda bi, i: (bi, i, 0)),
            pl.BlockSpec((group, tt, LANES), lambda bi, i: (bi, i, 0)),
            pl.BlockSpec((group, SROW, tt), lambda bi, i: (bi, 0, i)),
            pl.BlockSpec((group, tt, DN_WIDTH), lambda bi, i: (bi, i, 0)),
            pl.BlockSpec((1, DN_HEAD_DIM), lambda bi, i: (0, 0)),
            pl.BlockSpec((DN_HEADS, DN_HEAD_DIM, DN_HEAD_DIM), lambda bi, i: (0, 0, 0)),
        ] + cast_specs,
        out_specs=[pl.BlockSpec((group, tt, DN_WIDTH), lambda bi, i: (bi, i, 0)),
                   pl.BlockSpec((DN_HEADS, DN_HEAD_DIM, DN_HEAD_DIM), lambda bi, i: (0, 0, 0))]
        + cast_specs,
        out_shape=[jax.ShapeDtypeStruct((b, s, DN_WIDTH), BF16),
                   jax.ShapeDtypeStruct((DN_HEADS, DN_HEAD_DIM, DN_HEAD_DIM), F32)]
        + [jax.ShapeDtypeStruct(w.shape, BF16) for w in cast_weights],
        scratch_shapes=[pltpu.VMEM((group, DN_HEADS, DN_HEAD_DIM, DN_HEAD_DIM), F32)],
        compiler_params=pltpu.CompilerParams(dimension_semantics=("arbitrary", "arbitrary"),
                                             vmem_limit_bytes=VMEM_LIMIT),
        name="deltanet",
    )(qkv, scol, srow, dzs, onw, s0, *cast_weights)


def _tail_kernel(x_ref, ofox_ref, odn_ref, gates_ref, wbf_ref, wbd_ref, wout_ref, fnw_ref,
                 wg_hbm, wu_hbm, wd_hbm, finw_ref, o_ref, wg_ref, wu_ref, wd_ref, sems, *, ff_bounds):
    first_step = pl.program_id(0) == 0

    def ffn_weight_copies():
        pairs = ((wg_hbm, wg_ref), (wu_hbm, wu_ref), (wd_hbm, wd_ref))
        return [pltpu.make_async_copy(src, dst, sems.at[n]) for n, (src, dst) in enumerate(pairs)]

    @pl.when(first_step)
    def _():
        for copy in ffn_weight_copies():
            copy.start()

    a = _dot_tn(ofox_ref[...], wbf_ref[...])
    bb = _dot(odn_ref[...], wbd_ref[...])
    y = gates_ref[:, :D_MODEL] * a + gates_ref[:, D_MODEL:] * bb
    h1 = x_ref[...] + _dot(y.astype(BF16), wout_ref[...])
    n = (h1 * lax.rsqrt(jnp.mean(h1 * h1, axis=-1, keepdims=True) + EPS) * fnw_ref[...]).astype(BF16)

    @pl.when(first_step)
    def _():
        for copy in ffn_weight_copies():
            copy.wait()

    acc = h1
    for lo, hi in zip(ff_bounds[:-1], ff_bounds[1:]):
        gt = _dot(n, wg_ref[:, lo:hi])
        up = _dot(n, wu_ref[:, lo:hi])
        act = (gt * _sigmoid(gt) * up).astype(BF16)
        acc = acc + _dot(act, wd_ref[lo:hi, :])
    o_ref[...] = acc * lax.rsqrt(jnp.mean(acc * acc, axis=-1, keepdims=True) + EPS) * finw_ref[...]


def _tail(x2d, ofox, odn, gates, wbf, wbd, wout, fnw, wg, wu, wd, finw, *, tm, ff_bounds):
    m = x2d.shape[0]
    row = lambda w: pl.BlockSpec((tm, w), lambda i: (i, 0))
    return pl.pallas_call(
        functools.partial(_tail_kernel, ff_bounds=ff_bounds),
        grid=(m // tm,),
        in_specs=[row(D_MODEL), pl.BlockSpec((FOX_WIDTH, tm), lambda i: (0, i)), row(DN_WIDTH),
                  row(2 * D_MODEL),
                  _const_spec((FOX_WIDTH, D_MODEL)), _const_spec((DN_WIDTH, D_MODEL)),
                  _const_spec((D_MODEL, D_MODEL)), _const_spec((1, D_MODEL)),
                  pl.BlockSpec(memory_space=pl.ANY), pl.BlockSpec(memory_space=pl.ANY),
                  pl.BlockSpec(memory_space=pl.ANY), _const_spec((1, D_MODEL))],
        out_specs=row(D_MODEL),
        out_shape=jax.ShapeDtypeStruct((m, D_MODEL), F32),
        scratch_shapes=[pltpu.VMEM((D_MODEL, D_FF), BF16), pltpu.VMEM((D_MODEL, D_FF), BF16),
                        pltpu.VMEM((D_FF, D_MODEL), BF16), pltpu.SemaphoreType.DMA((3,))],
        compiler_params=pltpu.CompilerParams(dimension_semantics=("arbitrary",),
                                             vmem_limit_bytes=VMEM_LIMIT),
        name="merge_ffn",
    )(x2d, ofox, odn, gates, wbf, wbd, wout, fnw, wg, wu, wd, finw)


def _pick_tile(n, pref):
    t = min(pref, n)
    while n % t:
        t //= 2
    return t


def kernel(x, meta_tokens, mix_norm_w, w_in, fox_forget_bias, dn_conv_w, dn_a_log, dn_dt_bias,
           dn_out_norm_w, w_branch_fox, w_branch_dn, w_out, ffn_norm_w, w_ffn_gate, w_ffn_up,
           w_ffn_down, final_norm_w):
    b, s, _ = x.shape
    assert mix_norm_w.shape[0] == 1, "single layer only"
    assert s % PREFIX == 0
    m = b * s

    wi = w_in[0]
    o_small0 = 3 * FOX_WIDTH
    o_dn = o_small0 + FOX_HEADS
    o_small1 = o_dn + 3 * DN_WIDTH
    o_rest = o_small1 + 2 * DN_HEADS
    weights = (wi[:, :FOX_WIDTH].T.astype(BF16), wi[:, 2 * FOX_WIDTH:o_small0].T.astype(BF16),
               wi[:, FOX_WIDTH:2 * FOX_WIDTH].astype(BF16), wi[:, o_dn:o_small1].astype(BF16),
               wi[:, o_rest:].astype(BF16))
    w_alogit = wi[:, o_small1 + DN_HEADS:o_rest]
    wsmall = jnp.concatenate([wi[:, o_small0:o_dn], wi[:, o_small1:o_small1 + DN_HEADS],
                              w_alogit, w_alogit, w_alogit], axis=1)
    wsmall = jnp.pad(wsmall, ((0, 0), (0, LANES - N_SMALL))).astype(BF16)
    bias = jnp.zeros((SUBLANES, LANES), F32)
    bias = bias.at[0, LOGF_LANE:LOGF_LANE + FOX_HEADS].set(fox_forget_bias[0].astype(F32))
    bias = bias.at[0, G_LANE:N_SMALL].set(jnp.tile(dn_dt_bias[0].astype(F32), 3))
    bias = bias.at[1, G_LANE:N_SMALL].set(jnp.tile(dn_a_log[0].astype(F32), 3))
    nw = mix_norm_w[0].reshape(1, D_MODEL).astype(F32)

    x_p = jnp.concatenate([jnp.zeros((N_PAD, D_MODEL), F32), meta_tokens.astype(F32)], axis=0)
    convw = dn_conv_w[0].astype(F32)
    _, k_p, v_tp, kbias_p, qkv_p, dzs_p, _, scol_p, srow_p, conv_tail_p = _inproj(
        x_p, nw, weights, wsmall, bias, jnp.zeros((1, LANES), F32), convw,
        jnp.zeros((SUBLANES, 3 * DN_WIDTH), F32), tm=PREFIX, tiles_per_batch=1, n_pad=N_PAD)

    tm = _pick_tile(s, ROW_TILE)
    q_t, k, v_t, kbias, qkv, dzs, gates, scol, srow3, _ = _inproj(
        x.reshape(m, D_MODEL), nw, weights, wsmall, bias, scol_p[PREFIX - 1:PREFIX, :], convw,
        conv_tail_p, tm=tm, tiles_per_batch=s // tm, n_pad=0)

    scol3 = scol.reshape(b, s, LANES)
    tq = _pick_tile(s, ROW_TILE)
    ofox_t = _fox(q_t, k.reshape(b, s, FOX_WIDTH), v_t, kbias.reshape(b, s, LANES),
                  k_p[N_PAD:], v_tp[:, N_PAD:], kbias_p[N_PAD:], tq=tq, tk=min(MXU_DIM, tq // 2),
                  pairs=FOX_PAIRS_PER_STEP)

    onw = dn_out_norm_w[0].reshape(1, DN_HEAD_DIM).astype(F32)
    _, s_prefix = _deltanet(
        qkv_p[None], scol_p[None], srow_p, dzs_p[None], onw,
        jnp.zeros((DN_HEADS, DN_HEAD_DIM, DN_HEAD_DIM), F32), group=1, tt=PREFIX)
    group = _pick_tile(b, DN_UNITS // DN_HEADS)
    odn, _, wbf, wbd, wout, wg, wu, wd = _deltanet(
        qkv.reshape(b, s, 3 * DN_WIDTH), scol3, srow3, dzs.reshape(b, s, DN_WIDTH), onw, s_prefix,
        group=group, tt=_pick_tile(s, DN_UNITS // (DN_HEADS * group) * DN_BLOCK),
        cast_weights=(w_branch_fox[0], w_branch_dn[0], w_out[0], w_ffn_gate[0], w_ffn_up[0],
                      w_ffn_down[0]))

    out = _tail(
        x.reshape(m, D_MODEL), ofox_t, odn.reshape(m, DN_WIDTH), gates, wbf, wbd, wout,
        ffn_norm_w[0].reshape(1, D_MODEL).astype(F32), wg, wu, wd,
        final_norm_w.reshape(1, D_MODEL).astype(F32),
        tm=_pick_tile(m, ROW_TILE), ff_bounds=FF_BOUNDS)
    return out.reshape(b, s, D_MODEL)
```

```python
import functools

import jax
import jax.numpy as jnp
from jax import lax
from jax.experimental import pallas as pl
from jax.experimental.pallas import tpu as pltpu

F32 = jnp.float32
BF16 = jnp.bfloat16

LANES = 128
SUBLANES = 8
MXU_DIM = 256
ROW_TILE = 512
VMEM_LIMIT = 56 * 1024 * 1024

D_MODEL = 1024
N_META = 16
PREFIX = 128
N_PAD = PREFIX - N_META
FOX_HEADS = 8
FOX_HEAD_DIM = 64
FOX_WIDTH = FOX_HEADS * FOX_HEAD_DIM
V_ROWS = FOX_HEAD_DIM + SUBLANES
FOX_PAIRS_PER_STEP = 4
DN_HEADS = 4
DN_HEAD_DIM = 128
DN_WIDTH = DN_HEADS * DN_HEAD_DIM
DN_CHUNK = 64
DN_BLOCK = 2 * DN_CHUNK
DN_UNITS = 16
CONV_WIDTH = 4
D_FF = 2816
FF_BOUNDS = (0, (D_FF // MXU_DIM + 1) // 2 * MXU_DIM, D_FF)
EPS = 1e-6
NEG_INF = -1e30
LOG2E = 1.4426950408889634

COL_CHUNK = 512
LOGF_LANE = 0
BETA_LANE = FOX_HEADS
G_LANE = BETA_LANE + DN_HEADS
GSUF_LANE = G_LANE + DN_HEADS
GRAW_LANE = GSUF_LANE + DN_HEADS
N_SMALL = GRAW_LANE + DN_HEADS
SROW = -(-N_SMALL // SUBLANES) * SUBLANES


def _const_spec(shape):
    nd = len(shape)
    return pl.BlockSpec(shape, lambda *_: (0,) * nd, pipeline_mode=pl.Buffered(1))


def _sigmoid(x):
    return 0.5 * jnp.tanh(0.5 * x) + 0.5


def _dot(a, b, **kw):
    return jnp.dot(a, b, preferred_element_type=F32, **kw)


def _dot_nt(a, b, **kw):
    return lax.dot_general(a, b, (((1,), (1,)), ((), ())), preferred_element_type=F32, **kw)


def _dot_tn(a, b, **kw):
    return lax.dot_general(a, b, (((0,), (0,)), ((), ())), preferred_element_type=F32, **kw)


def _inproj_kernel(x_ref, nw_ref, wq_t_ref, wv_t_ref, wk_ref, wdn_ref, wrest_ref, wsmall_ref, bias_ref,
                   carry0_ref,
                   convw_ref, halo0_ref,
                   q_t_ref, k_ref, v_t_ref, kbias_ref, dqkv_ref, dzs_ref, gates_ref, scol_ref,
                   srow_ref, tail_ref, carry_ref, halo_ref, conv_ref, *, tm, tiles_per_batch, n_pad):
    i = pl.program_id(0)

    @pl.when(i % tiles_per_batch == 0)
    def _():
        carry_ref[...] = carry0_ref[...]
        halo_ref[...] = halo0_ref[...]

    x = x_ref[...]
    ms = jnp.mean(x * x, axis=-1, keepdims=True)
    hn = (x * lax.rsqrt(ms + EPS) * nw_ref[...]).astype(BF16)

    z = _dot(hn, wsmall_ref[...]) + bias_ref[0:1, :]
    lane = lax.broadcasted_iota(jnp.int32, (tm, LANES), 1)
    row = lax.broadcasted_iota(jnp.int32, (tm, LANES), 0)
    e = jnp.exp(-jnp.abs(z))
    l1p = jnp.log1p(e)
    logf = jnp.minimum(z, 0.0) - l1p
    softplus = jnp.maximum(z, 0.0) + l1p
    sig = jnp.where(z >= 0.0, 1.0, e) / (1.0 + e)
    g = -jnp.exp(bias_ref[1:2, :]) * softplus
    if n_pad:
        vm = (row + (i % tiles_per_batch) * tm >= n_pad).astype(F32)
        sig = sig * vm
        g = g * vm
    val = jnp.where(lane < BETA_LANE, logf,
                    jnp.where(lane < G_LANE, sig, jnp.where(lane < N_SMALL, g, 0.0)))
    in_chunk = row % DN_CHUNK
    is_logf = lane < BETA_LANE
    is_gpre = jnp.logical_and(lane >= G_LANE, lane < GSUF_LANE)
    is_gsuf = jnp.logical_and(lane >= GSUF_LANE, lane < GRAW_LANE)
    scan = val
    sh = 1
    while sh < tm:
        take_up = jnp.logical_and(is_logf, row >= sh)
        if sh < DN_CHUNK:
            take_up = jnp.logical_or(take_up, jnp.logical_and(is_gpre, in_chunk >= sh))
            take_down = jnp.logical_and(is_gsuf, in_chunk + sh < DN_CHUNK)
            below = jnp.where(take_down, pltpu.roll(scan, tm - sh, axis=0), 0.0)
        else:
            below = 0.0
        scan = scan + jnp.where(take_up, pltpu.roll(scan, sh, axis=0), below)
        sh *= 2
    out = scan + jnp.where(is_logf, carry_ref[...], 0.0)
    carry_ref[...] = out[tm - 1:tm, :]
    scol_ref[...] = out
    srow_ref[...] = out.T[:SROW, :]
    nc = jnp.where(is_logf, out * -LOG2E, 0.0)
    hi = nc.astype(BF16).astype(F32)
    mid = (nc - hi).astype(BF16).astype(F32)
    lo = nc - hi - mid
    kbias = hi + pltpu.roll(mid, FOX_HEADS, axis=1) + pltpu.roll(lo, 2 * FOX_HEADS, axis=1)
    kbias_ref[...] = kbias.astype(BF16)

    def main_chunk(c):
        if c == 0:
            return lambda: _dot(hn, wk_ref[...])
        ref, first = (wdn_ref, 1) if c < 4 else (wrest_ref, 4)
        return lambda: _dot(hn, ref[:, (c - first) * COL_CHUNK:(c - first + 1) * COL_CHUNK])

    def store_q_t(acc):
        q_t_ref[...] = (acc * (FOX_HEAD_DIM ** -0.5 * LOG2E)).astype(BF16)

    def store_v_t(acc):
        v_t = acc.astype(BF16)
        ones = jnp.ones((V_ROWS - FOX_HEAD_DIM, tm), BF16)
        v_t_ref[...] = jnp.concatenate(
            [piece for h in range(FOX_HEADS)
             for piece in (v_t[h * FOX_HEAD_DIM:(h + 1) * FOX_HEAD_DIM], ones)], axis=0)

    def store_k(acc):
        k_ref[...] = acc.astype(BF16)

    def store_dn(which):
        def epilogue(acc):
            cols = slice(which * COL_CHUNK, (which + 1) * COL_CHUNK)
            if n_pad:
                acc = acc * vm[:, :1]
            conv_ref[which, 0:SUBLANES, :] = halo_ref[:, cols]
            conv_ref[which, SUBLANES:, :] = acc
            halo_ref[:, cols] = acc[tm - SUBLANES:, :]
            tail_ref[:, cols] = acc[tm - SUBLANES:, :]
            a = convw_ref[CONV_WIDTH - 1:CONV_WIDTH, cols] * acc
            for t in range(CONV_WIDTH - 1):
                start = SUBLANES - (CONV_WIDTH - 1) + t
                a = a + convw_ref[t:t + 1, cols] * conv_ref[which, start:start + tm, :]
            a = a * _sigmoid(a)
            if which == 2:
                dqkv_ref[:, cols] = a
            else:
                scale = DN_HEAD_DIM ** -0.5 if which == 0 else 1.0
                for h in range(DN_HEADS):
                    ah = a[:, h * LANES:(h + 1) * LANES]
                    inv = lax.rsqrt(jnp.sum(ah * ah, axis=-1, keepdims=True) + EPS) * scale
                    lo_col = which * COL_CHUNK + h * LANES
                    dqkv_ref[:, lo_col:lo_col + LANES] = ah * inv
        return epilogue

    def store_dzs(acc):
        dzs_ref[...] = acc * _sigmoid(acc)

    def store_gates(j):
        def epilogue(acc):
            gates_ref[:, j * COL_CHUNK:(j + 1) * COL_CHUNK] = _sigmoid(acc).astype(BF16)
        return epilogue

    store_q_t(_dot_nt(wq_t_ref[...], hn))
    store_v_t(_dot_nt(wv_t_ref[...], hn))
    store_k(main_chunk(0)())
    for which in range(3):
        store_dn(which)(main_chunk(1 + which)())
    store_dzs(main_chunk(4)())
    for j in range(2 * D_MODEL // COL_CHUNK):
        store_gates(j)(main_chunk(5 + j)())


def _inproj(x2d, nw, weights, wsmall, bias, carry0, convw, halo0, *, tm, tiles_per_batch, n_pad):
    m = x2d.shape[0]
    row = lambda w: pl.BlockSpec((tm, w), lambda i: (i, 0))
    col = lambda h: pl.BlockSpec((h, tm), lambda i: (0, i))
    return pl.pallas_call(
        functools.partial(_inproj_kernel, tm=tm, tiles_per_batch=tiles_per_batch, n_pad=n_pad),
        grid=(m // tm,),
        in_specs=[row(D_MODEL), _const_spec((1, D_MODEL))] + [_const_spec(w.shape) for w in weights] + [
                  _const_spec((D_MODEL, LANES)), _const_spec((SUBLANES, LANES)),
                  _const_spec((1, LANES)), _const_spec((CONV_WIDTH, 3 * DN_WIDTH)),
                  _const_spec((SUBLANES, 3 * DN_WIDTH))],
        out_specs=[col(FOX_WIDTH), row(FOX_WIDTH), col(FOX_HEADS * V_ROWS), row(LANES),
                   row(3 * DN_WIDTH), row(DN_WIDTH), row(2 * D_MODEL),
                   row(LANES),
                   pl.BlockSpec((None, SROW, tm),
                                lambda i: (i // tiles_per_batch, 0, i % tiles_per_batch)),
                   pl.BlockSpec((SUBLANES, 3 * DN_WIDTH), lambda i: (0, 0))],
        out_shape=[jax.ShapeDtypeStruct((FOX_WIDTH, m), BF16),
                   jax.ShapeDtypeStruct((m, FOX_WIDTH), BF16),
                   jax.ShapeDtypeStruct((FOX_HEADS * V_ROWS, m), BF16),
                   jax.ShapeDtypeStruct((m, LANES), BF16),
                   jax.ShapeDtypeStruct((m, 3 * DN_WIDTH), F32),
                   jax.ShapeDtypeStruct((m, DN_WIDTH), F32),
                   jax.ShapeDtypeStruct((m, 2 * D_MODEL), BF16),
                   jax.ShapeDtypeStruct((m, LANES), F32),
                   jax.ShapeDtypeStruct((m // (tm * tiles_per_batch), SROW,
                                         tm * tiles_per_batch), F32),
                   jax.ShapeDtypeStruct((SUBLANES, 3 * DN_WIDTH), F32)],
        scratch_shapes=[pltpu.VMEM((1, LANES), F32), pltpu.VMEM((SUBLANES, 3 * DN_WIDTH), F32),
                        pltpu.VMEM((3, tm + SUBLANES, COL_CHUNK), F32)],
        compiler_params=pltpu.CompilerParams(dimension_semantics=("arbitrary",),
                                             vmem_limit_bytes=VMEM_LIMIT),
        name="inproj",
    )(x2d, nw, *weights, wsmall, bias, carry0, convw, halo0)


def _fox_kernel(q_t_ref, k_ref, v_t_ref, kb_ref, kp_ref, v_tp_ref, kbp_ref,
                o_ref, m_ref, l_ref, acc_ref, t_ref, tmax_ref, *, tq, tk, pairs):
    pg = pl.program_id(1)
    i = pl.program_id(2)
    heads = range(2 * pairs)
    sub = lax.broadcasted_iota(jnp.int32, (LANES, 1), 0)
    rhs = []
    for hh in heads:
        pp, h = divmod(hh, 2)
        q_t = q_t_ref[pp * LANES:(pp + 1) * LANES, :]
        mine = sub < FOX_HEAD_DIM if h == 0 else sub >= FOX_HEAD_DIM
        head = 2 * (pg * pairs + pp) + h
        sel = jnp.where(jnp.logical_and(sub < 3 * FOX_HEADS, (sub % FOX_HEADS) == head),
                        1.0, 0.0).astype(BF16) * jnp.ones((1, tq), BF16)
        rhs.append(jnp.concatenate([jnp.where(mine, q_t, jnp.zeros_like(q_t)), sel], axis=0))

    def scores(kt, kbt, q_lo=0):
        lhs = [jnp.concatenate([kt[:, pp * LANES:(pp + 1) * LANES], kbt], axis=1)
               for pp in range(pairs)]
        return [_dot(lhs[hh // 2], rhs[hh][:, q_lo:]) for hh in heads]

    def column_max(ts):
        return [jnp.max(ts[hh], axis=0, keepdims=True) for hh in heads]

    def scores_into(slot, off, q_lo=0):
        ts = scores(k_ref[pl.ds(off, tk), :], kb_ref[pl.ds(off, tk), :], q_lo)
        t_max = column_max(ts)
        for hh in heads:
            t_ref[slot, hh, :, :tq - q_lo] = ts[hh]
            tmax_ref[slot, hh, :, :tq - q_lo] = t_max[hh]

    def update(read_t, t_max, v_aug, mask, first, q_lo=0):
        width = tq - q_lo

        def masked_t(hh):
            t = read_t(hh)
            return t if mask is None else jnp.where(mask, t, NEG_INF)

        if t_max is None:
            t_max = [jnp.max(masked_t(hh), axis=0, keepdims=True) for hh in heads]
        if first:
            m_new = t_max
        else:
            m_prev = [m_ref[hh, :, q_lo:] for hh in heads]
            m_new = [jnp.maximum(m_prev[hh], t_max[hh]) for hh in heads]
        pm = [jnp.exp2(masked_t(hh) - m_new[hh]).astype(BF16) for hh in heads]
        r = [_dot(v_aug[hh * V_ROWS:(hh + 1) * V_ROWS, :], pm[hh]) for hh in heads]
        pv = jnp.concatenate([r[hh][:FOX_HEAD_DIM] for hh in heads], axis=0)
        psum = [r[hh][FOX_HEAD_DIM:FOX_HEAD_DIM + 1] for hh in heads]
        if first:
            for hh in heads:
                l_ref[hh, :, q_lo:] = psum[hh]
            acc_ref[:, q_lo:] = pv
        else:
            alpha = [jnp.exp2(m_prev[hh] - m_new[hh]) for hh in heads]
            for hh in heads:
                l_ref[hh, :, q_lo:] = alpha[hh] * l_ref[hh, :, q_lo:] + psum[hh]
            alpha_rows = jnp.concatenate(
                [jnp.broadcast_to(alpha[hh], (FOX_HEAD_DIM, width)) for hh in heads], axis=0)
            acc_ref[:, q_lo:] = alpha_rows * acc_ref[:, q_lo:] + pv
        for hh in heads:
            m_ref[hh, :, q_lo:] = m_new[hh]

    ts_p = scores(kp_ref[...], kbp_ref[...])
    update(lambda hh: ts_p[hh], column_max(ts_p), v_tp_ref[...], None, True)

    def v_at(off):
        return v_t_ref[:, pl.ds(off, tk)]

    def slot(n, q_lo=0):
        return lambda hh: t_ref[n, hh, :, :tq - q_lo]

    def slot_max(n):
        return [tmax_ref[n, hh] for hh in heads]

    pairs_per_q = tq // (2 * tk)
    scores_into(0, 0)

    def body(jj, carry):
        off = pl.multiple_of(jj * 2 * tk, 2 * tk)
        scores_into(1, off + tk)
        update(slot(0), slot_max(0), v_at(off), None, False)
        scores_into(0, off + 2 * tk)
        update(slot(1), slot_max(1), v_at(off + tk), None, False)
        return carry

    lax.fori_loop(0, i * pairs_per_q, body, 0)

    def causal(q_lo):
        rr = lax.broadcasted_iota(jnp.int32, (tk, tq - q_lo), 0)
        cc = lax.broadcasted_iota(jnp.int32, (tk, tq - q_lo), 1)
        return rr <= cc

    for dp in range(pairs_per_q):
        off = pl.multiple_of(i * tq + dp * 2 * tk, 2 * tk)
        q_even, q_odd = dp * 2 * tk, (dp * 2 + 1) * tk
        scores_into(1, off + tk, q_odd)
        update(slot(0, q_even), None, v_at(off), causal(q_even), False, q_even)
        if dp + 1 < pairs_per_q:
            scores_into(0, off + 2 * tk, q_even + 2 * tk)
        update(slot(1, q_odd), None, v_at(off + tk), causal(q_odd), False, q_odd)

    l_rows = jnp.concatenate([jnp.broadcast_to(l_ref[hh], (FOX_HEAD_DIM, tq)) for hh in heads], axis=0)
    o_ref[...] = (acc_ref[...] / l_rows).astype(BF16)


def _fox(q_t, k, v_t, kbias, k_p, v_tp, kbias_p, *, tq, tk, pairs):
    b, s, _ = k.shape
    ngroups = FOX_HEADS // (2 * pairs)
    nq = s // tq
    wl = pairs * LANES
    return pl.pallas_call(
        functools.partial(_fox_kernel, tq=tq, tk=tk, pairs=pairs),
        grid=(b, ngroups, nq),
        in_specs=[
            pl.BlockSpec((wl, tq), lambda bi, pg, i: (pg, bi * nq + i)),
            pl.BlockSpec((None, s, wl), lambda bi, pg, i: (bi, 0, pg)),
            pl.BlockSpec((2 * pairs * V_ROWS, s), lambda bi, pg, i: (pg, bi)),
            pl.BlockSpec((None, s, LANES), lambda bi, pg, i: (bi, 0, 0)),
            pl.BlockSpec((N_META, wl), lambda bi, pg, i: (0, pg)),
            pl.BlockSpec((2 * pairs * V_ROWS, N_META), lambda bi, pg, i: (pg, 0)),
            pl.BlockSpec((N_META, LANES), lambda bi, pg, i: (0, 0)),
        ],
        out_specs=pl.BlockSpec((wl, tq), lambda bi, pg, i: (pg, bi * nq + i)),
        out_shape=jax.ShapeDtypeStruct((FOX_WIDTH, b * s), BF16),
        scratch_shapes=[pltpu.VMEM((2 * pairs, 1, tq), F32), pltpu.VMEM((2 * pairs, 1, tq), F32),
                        pltpu.VMEM((wl, tq), F32), pltpu.VMEM((2, 2 * pairs, tk, tq), F32),
                        pltpu.VMEM((2, 2 * pairs, 1, tq), F32)],
        compiler_params=pltpu.CompilerParams(
            dimension_semantics=("arbitrary", "arbitrary", "arbitrary"),
            vmem_limit_bytes=VMEM_LIMIT),
        name="fox_attention",
    )(q_t, k, v_t, kbias, k_p, v_tp, kbias_p)


def _dn_kernel(qkv_ref, scol_ref, srow_ref, dzs_ref, onw_ref, s0_ref, *refs, tt, n_cast):
    cast_in, (odn_ref, sfin_ref), cast_out = refs[:n_cast], refs[n_cast:n_cast + 2], refs[n_cast + 2:-1]
    state_ref = refs[-1]
    for src, dst in zip(cast_in, cast_out):
        dst[...] = src[...].astype(BF16)
    i = pl.program_id(1)
    nchunk = tt // DN_CHUNK

    @pl.when(i == 0)
    def _():
        for g in range(qkv_ref.shape[0]):
            state_ref[g] = s0_ref[...]

    bs = DN_BLOCK
    heads = range(DN_HEADS)
    groups = range(qkv_ref.shape[0])
    units = [(g, sb, h) for g in groups for sb in range(tt // bs) for h in heads]

    def slab(g, sb, c0):
        return qkv_ref[g, sb * bs:(sb + 1) * bs, c0:c0 + LANES]

    def small_col(g, sb, lane):
        return scol_ref[g, sb * bs:(sb + 1) * bs, lane:lane + 1]

    rr = lax.broadcasted_iota(jnp.int32, (bs, bs), 0)
    cc = lax.broadcasted_iota(jnp.int32, (bs, bs), 1)
    rc_xor = jnp.bitwise_xor(rr, cc)
    same = rc_xor < DN_CHUNK
    tril_m = jnp.logical_and(same, rr >= cc)
    strict_m = jnp.logical_and(same, rr > cc)
    eye = (rr == cc).astype(F32)

    qs = [slab(g, sb, h * LANES) for g, sb, h in units]
    ks = [slab(g, sb, DN_WIDTH + h * LANES) for g, sb, h in units]
    vs = [slab(g, sb, 2 * DN_WIDTH + h * LANES) for g, sb, h in units]
    states = {(g, h): state_ref[g, h] for g in groups for h in heads}
    betas = [small_col(g, sb, BETA_LANE + h) for g, sb, h in units]
    gcs = [small_col(g, sb, G_LANE + h) for g, sb, h in units]
    gls = [gcs[n] + small_col(g, sb, GSUF_LANE + h) - small_col(g, sb, GRAW_LANE + h)
           for n, (g, sb, h) in enumerate(units)]
    decays = [jnp.exp(jnp.where(
        tril_m, gcs[n] - srow_ref[g, G_LANE + h:G_LANE + h + 1, sb * bs:(sb + 1) * bs], NEG_INF))
        for n, (g, sb, h) in enumerate(units)]
    nu = range(len(units))
    egcs = [jnp.exp(g) for g in gcs]
    kbs = [ks[n] * betas[n] for n in nu]
    ks_b = [k.astype(BF16) for k in ks]
    lmats = [jnp.where(strict_m, _dot_nt(kbs[n].astype(BF16), ks_b[n]) * decays[n], 0.0) for n in nu]
    attns = [(_dot_nt(qs[n].astype(BF16), ks_b[n]) * decays[n]).astype(BF16) for n in nu]
    ainvs = [eye - jnp.where(rc_xor == 1, lm, 0.0) for lm in lmats]
    for lvl in range(1, 6):
        blk = 2 ** lvl
        lvl_m = jnp.logical_and(rc_xor >= blk, rc_xor < 2 * blk)
        ainvs_b = [a.astype(BF16) for a in ainvs]
        mids = [_dot(jnp.where(lvl_m, lmats[n], 0.0).astype(BF16), ainvs_b[n]).astype(BF16)
                for n in nu]
        ainvs = [ainvs[n] - _dot(ainvs_b[n], mids[n]) for n in nu]
    sols = [_dot(ainvs[n].astype(BF16),
                 jnp.concatenate([vs[n] * betas[n], kbs[n] * egcs[n]], axis=1).astype(BF16))
            for n in nu]
    uus = [sol[:, :DN_HEAD_DIM] for sol in sols]
    wws = [sol[:, DN_HEAD_DIM:] for sol in sols]
    qds = [qs[n] * egcs[n] for n in nu]
    kds = [(ks[n] * jnp.exp(gls[n] - gcs[n])).astype(BF16) for n in nu]

    vnews = [[] for _ in nu]
    qss = [[] for _ in nu]
    for c in range(nchunk):
        csb, lc = divmod(c, bs // DN_CHUNK)
        lo, hi = lc * DN_CHUNK, (lc + 1) * DN_CHUNK
        cur = [(n, (g, h)) for n, (g, sb, h) in enumerate(units) if sb == csb]
        rs = {n: _dot(jnp.concatenate([wws[n][lo:hi], qds[n][lo:hi]], axis=0).astype(BF16),
                      states[key].astype(BF16)) for n, key in cur}
        vns = {n: uus[n][lo:hi] - rs[n][:DN_CHUNK] for n, _ in cur}
        states.update({key: jnp.exp(gls[n][lo:lo + 1, :]) * states[key]
                       + _dot_tn(kds[n][lo:hi], vns[n].astype(BF16)) for n, key in cur})
        for n, _ in cur:
            qss[n].append(rs[n][DN_CHUNK:])
            vnews[n].append(vns[n])
    for (g, h), state in states.items():
        state_ref[g, h] = state
    for h in heads:
        sfin_ref[h] = states[(0, h)]
    for n, (g, sb, h) in enumerate(units):
        o = (jnp.concatenate(qss[n], axis=0)
             + _dot(attns[n], jnp.concatenate(vnews[n], axis=0).astype(BF16)))
        o = o * lax.rsqrt(jnp.mean(o * o, axis=-1, keepdims=True) + EPS) * onw_ref[...]
        o = o * dzs_ref[g, sb * bs:(sb + 1) * bs, h * LANES:(h + 1) * LANES]
        odn_ref[g, sb * bs:(sb + 1) * bs, h * LANES:(h + 1) * LANES] = o.astype(BF16)


def _cast_row_block(rows, nsteps):
    tile = 2 * SUBLANES
    for blk in range(tile, rows, tile):
        if rows % blk == 0 and rows // blk <= nsteps:
            return blk
    return rows


def _deltanet(qkv, scol, srow, dzs, onw, s0, *, group, tt, cast_weights=()):
    b, s, _ = qkv.shape
    nt = s // tt
    nsteps = (b // group) * nt
    cast_specs = []
    for w in cast_weights:
        blk = _cast_row_block(w.shape[0], nsteps)
        last = w.shape[0] // blk - 1
        cast_specs.append(pl.BlockSpec(
            (blk, w.shape[1]), lambda bi, i, last=last: (jnp.minimum(bi * nt + i, last), 0)))
    return pl.pallas_call(
        functools.partial(_dn_kernel, tt=tt, n_cast=len(cast_weights)),
        grid=(b // group, nt),
        in_specs=[
            pl.BlockSpec((group, tt, 3 * DN_WIDTH), lambda bi, i: (bi, i, 0)),
            pl.BlockSpec((group, tt, LANES), lambda bi, i: (bi, i, 0)),
            pl.BlockSpec((group, SROW, tt), lambda bi, i: (bi, 0, i)),
            pl.BlockSpec((group, tt, DN_WIDTH), lambda bi, i: (bi, i, 0)),
            pl.BlockSpec((1, DN_HEAD_DIM), lambda bi, i: (0, 0)),
            pl.BlockSpec((DN_HEADS, DN_HEAD_DIM, DN_HEAD_DIM), lambda bi, i: (0, 0, 0)),
        ] + cast_specs,
        out_specs=[pl.BlockSpec((group, tt, DN_WIDTH), lambda bi, i: (bi, i, 0)),
                   pl.BlockSpec((DN_HEADS, DN_HEAD_DIM, DN_HEAD_DIM), lambda bi, i: (0, 0, 0))]
        + cast_specs,
        out_shape=[jax.ShapeDtypeStruct((b, s, DN_WIDTH), BF16),
                   jax.ShapeDtypeStruct((DN_HEADS, DN_HEAD_DIM, DN_HEAD_DIM), F32)]
        + [jax.ShapeDtypeStruct(w.shape, BF16) for w in cast_weights],
        scratch_shapes=[pltpu.VMEM((group, DN_HEADS, DN_HEAD_DIM, DN_HEAD_DIM), F32)],
        compiler_params=pltpu.CompilerParams(dimension_semantics=("arbitrary", "arbitrary"),
                                             vmem_limit_bytes=VMEM_LIMIT),
        name="deltanet",
    )(qkv, scol, srow, dzs, onw, s0, *cast_weights)


def _tail_kernel(x_ref, ofox_ref, odn_ref, gates_ref, wbf_ref, wbd_ref, wout_ref, fnw_ref,
                 wg_ref, wu_ref, wd_ref, finw_ref, o_ref, *, ff_bounds):
    a = _dot_tn(ofox_ref[...], wbf_ref[...])
    bb = _dot(odn_ref[...], wbd_ref[...])
    y = gates_ref[:, :D_MODEL] * a + gates_ref[:, D_MODEL:] * bb
    h1 = x_ref[...] + _dot(y.astype(BF16), wout_ref[...])
    n = (h1 * lax.rsqrt(jnp.mean(h1 * h1, axis=-1, keepdims=True) + EPS) * fnw_ref[...]).astype(BF16)
    acc = h1
    for lo, hi in zip(ff_bounds[:-1], ff_bounds[1:]):
        gt = _dot(n, wg_ref[:, lo:hi])
        up = _dot(n, wu_ref[:, lo:hi])
        act = (gt * _sigmoid(gt) * up).astype(BF16)
        acc = acc + _dot(act, wd_ref[lo:hi, :])
    o_ref[...] = acc * lax.rsqrt(jnp.mean(acc * acc, axis=-1, keepdims=True) + EPS) * finw_ref[...]


def _tail(x2d, ofox, odn, gates, wbf, wbd, wout, fnw, wg, wu, wd, finw, *, tm, ff_bounds):
    m = x2d.shape[0]
    row = lambda w: pl.BlockSpec((tm, w), lambda i: (i, 0))
    return pl.pallas_call(
        functools.partial(_tail_kernel, ff_bounds=ff_bounds),
        grid=(m // tm,),
        in_specs=[row(D_MODEL), pl.BlockSpec((FOX_WIDTH, tm), lambda i: (0, i)), row(DN_WIDTH),
                  row(2 * D_MODEL),
                  _const_spec((FOX_WIDTH, D_MODEL)), _const_spec((DN_WIDTH, D_MODEL)),
                  _const_spec((D_MODEL, D_MODEL)), _const_spec((1, D_MODEL)),
                  _const_spec((D_MODEL, D_FF)), _const_spec((D_MODEL, D_FF)),
                  _const_spec((D_FF, D_MODEL)), _const_spec((1, D_MODEL))],
        out_specs=row(D_MODEL),
        out_shape=jax.ShapeDtypeStruct((m, D_MODEL), F32),
        compiler_params=pltpu.CompilerParams(dimension_semantics=("arbitrary",),
                                             vmem_limit_bytes=VMEM_LIMIT),
        name="merge_ffn",
    )(x2d, ofox, odn, gates, wbf, wbd, wout, fnw, wg, wu, wd, finw)


def _pick_tile(n, pref):
    t = min(pref, n)
    while n % t:
        t //= 2
    return t


def kernel(x, meta_tokens, mix_norm_w, w_in, fox_forget_bias, dn_conv_w, dn_a_log, dn_dt_bias,
           dn_out_norm_w, w_branch_fox, w_branch_dn, w_out, ffn_norm_w, w_ffn_gate, w_ffn_up,
           w_ffn_down, final_norm_w):
    b, s, _ = x.shape
    assert mix_norm_w.shape[0] == 1, "single layer only"
    assert s % PREFIX == 0
    m = b * s

    wi = w_in[0]
    o_small0 = 3 * FOX_WIDTH
    o_dn = o_small0 + FOX_HEADS
    o_small1 = o_dn + 3 * DN_WIDTH
    o_rest = o_small1 + 2 * DN_HEADS
    weights = (wi[:, :FOX_WIDTH].T.astype(BF16), wi[:, 2 * FOX_WIDTH:o_small0].T.astype(BF16),
               wi[:, FOX_WIDTH:2 * FOX_WIDTH].astype(BF16), wi[:, o_dn:o_small1].astype(BF16),
               wi[:, o_rest:].astype(BF16))
    w_alogit = wi[:, o_small1 + DN_HEADS:o_rest]
    wsmall = jnp.concatenate([wi[:, o_small0:o_dn], wi[:, o_small1:o_small1 + DN_HEADS],
                              w_alogit, w_alogit, w_alogit], axis=1)
    wsmall = jnp.pad(wsmall, ((0, 0), (0, LANES - N_SMALL))).astype(BF16)
    bias = jnp.zeros((SUBLANES, LANES), F32)
    bias = bias.at[0, LOGF_LANE:LOGF_LANE + FOX_HEADS].set(fox_forget_bias[0].astype(F32))
    bias = bias.at[0, G_LANE:N_SMALL].set(jnp.tile(dn_dt_bias[0].astype(F32), 3))
    bias = bias.at[1, G_LANE:N_SMALL].set(jnp.tile(dn_a_log[0].astype(F32), 3))
    nw = mix_norm_w[0].reshape(1, D_MODEL).astype(F32)

    x_p = jnp.concatenate([jnp.zeros((N_PAD, D_MODEL), F32), meta_tokens.astype(F32)], axis=0)
    convw = dn_conv_w[0].astype(F32)
    _, k_p, v_tp, kbias_p, qkv_p, dzs_p, _, scol_p, srow_p, conv_tail_p = _inproj(
        x_p, nw, weights, wsmall, bias, jnp.zeros((1, LANES), F32), convw,
        jnp.zeros((SUBLANES, 3 * DN_WIDTH), F32), tm=PREFIX, tiles_per_batch=1, n_pad=N_PAD)

    tm = _pick_tile(s, ROW_TILE)
    q_t, k, v_t, kbias, qkv, dzs, gates, scol, srow3, _ = _inproj(
        x.reshape(m, D_MODEL), nw, weights, wsmall, bias, scol_p[PREFIX - 1:PREFIX, :], convw,
        conv_tail_p, tm=tm, tiles_per_batch=s // tm, n_pad=0)

    scol3 = scol.reshape(b, s, LANES)
    tq = _pick_tile(s, ROW_TILE)
    ofox_t = _fox(q_t, k.reshape(b, s, FOX_WIDTH), v_t, kbias.reshape(b, s, LANES),
                  k_p[N_PAD:], v_tp[:, N_PAD:], kbias_p[N_PAD:], tq=tq, tk=min(MXU_DIM, tq // 2),
                  pairs=FOX_PAIRS_PER_STEP)

    onw = dn_out_norm_w[0].reshape(1, DN_HEAD_DIM).astype(F32)
    _, s_prefix = _deltanet(
        qkv_p[None], scol_p[None], srow_p, dzs_p[None], onw,
        jnp.zeros((DN_HEADS, DN_HEAD_DIM, DN_HEAD_DIM), F32), group=1, tt=PREFIX)
    group = _pick_tile(b, DN_UNITS // DN_HEADS)
    odn, _, wbf, wbd, wout, wg, wu, wd = _deltanet(
        qkv.reshape(b, s, 3 * DN_WIDTH), scol3, srow3, dzs.reshape(b, s, DN_WIDTH), onw, s_prefix,
        group=group, tt=_pick_tile(s, DN_UNITS // (DN_HEADS * group) * DN_BLOCK),
        cast_weights=(w_branch_fox[0], w_branch_dn[0], w_out[0], w_ffn_gate[0], w_ffn_up[0],
                      w_ffn_down[0]))

    out = _tail(
        x.reshape(m, D_MODEL), ofox_t, odn.reshape(m, DN_WIDTH), gates, wbf, wbd, wout,
        ffn_norm_w[0].reshape(1, D_MODEL).astype(F32), wg, wu, wd,
        final_norm_w.reshape(1, D_MODEL).astype(F32),
        tm=_pick_tile(m, ROW_TILE), ff_bounds=FF_BOUNDS)
    return out.reshape(b, s, D_MODEL)
```
